```python
import math
import jax
import jax.numpy as jnp
from jax import lax
import numpy as np

D_MODEL = 2048
BATCH = 8
SEQ = 8192
DEPTH = 2

MEM_LEN = 256
HEAD_DIM = 128
N_MIX_HEADS = D_MODEL // HEAD_DIM
MEM_HEADS = 4
MEM_W = MEM_HEADS * HEAD_DIM
ATTN_GROUPS = ((128, 1), (512, 4), (2048, 16))
ATTN_HEADS = N_MIX_HEADS - MEM_HEADS
HEADS_PER_GROUP = ATTN_HEADS // len(ATTN_GROUPS)
ATTN_W = ATTN_HEADS * HEAD_DIM
ATTN_OUT_W = HEADS_PER_GROUP * HEAD_DIM
BLK = 128
SGU_GROUPS = ATTN_HEADS
SGU_GROUP_DIM = HEAD_DIM
SGU_W = SGU_GROUPS * SGU_GROUP_DIM
SGU_CHUNK = 128
ROT_DIM = HEAD_DIM // 4
ROPE_THETA = 500000.0
FFN_HIDDEN = ((8 * D_MODEL // 3 + 255) // 256) * 256
NORM_EPS = 1e-6
LN_EPS = 1e-5
NEG_INF = -1e30

kernel_name = 'hybrid_dilated_attn_gmlp_memory_trunk'


def _rms_norm(x, g):
    xf = x.astype(jnp.float32)
    y = xf * lax.rsqrt(jnp.mean(xf * xf, axis=-1, keepdims=True) + NORM_EPS)
    return (y * g.astype(jnp.float32)).astype(x.dtype)


def _layer_norm(x, g, b):
    xf = x.astype(jnp.float32)
    mu = jnp.mean(xf, axis=-1, keepdims=True)
    var = jnp.mean(jnp.square(xf - mu), axis=-1, keepdims=True)
    y = (xf - mu) * lax.rsqrt(var + LN_EPS)
    return (y * g.astype(jnp.float32) + b.astype(jnp.float32)).astype(x.dtype)


def _partial_rotary(t, positions):
    half = ROT_DIM // 2
    inv_freq = ROPE_THETA ** (-jnp.arange(half, dtype=jnp.float32) / half)
    ang = positions.astype(jnp.float32)[:, :, None] * inv_freq
    cos = jnp.cos(ang)[:, :, None, :]
    sin = jnp.sin(ang)[:, :, None, :]
    tf = t.astype(jnp.float32)
    x1 = tf[..., :half]
    x2 = tf[..., half:ROT_DIM]
    rot = jnp.concatenate([x1 * cos - x2 * sin, x2 * cos + x1 * sin, tf[..., ROT_DIM:]], axis=-1)
    return rot.astype(t.dtype)


def _dilated_group(q, k, v, window, dilation):
    b, s, h, dh = q.shape
    n_back = window // dilation
    span = dilation * BLK
    sp = -(-s // span) * span
    length = sp // dilation
    nb = length // BLK

    def to_blocks(t):
        t = jnp.pad(t, ((0, 0), (0, sp - s), (0, 0), (0, 0)))
        t = t.reshape(b, length, dilation, h, dh).transpose(0, 2, 3, 1, 4)
        return t.reshape(b, dilation, h, nb, BLK, dh)

    def with_prev(t):
        prev = jnp.pad(t[:, :, :, :-1], ((0, 0), (0, 0), (0, 0), (1, 0), (0, 0), (0, 0)))
        return jnp.concatenate([prev, t], axis=4)

    qb = to_blocks(q)
    kb = with_prev(to_blocks(k))
    vb = with_prev(to_blocks(v))
    logits = jnp.einsum('brhnqd,brhnkd->brhnqk', qb, kb,
                        preferred_element_type=jnp.float32) * (dh ** -0.5)
    qi = jnp.arange(BLK)[:, None]
    ki = jnp.arange(2 * BLK)[None, :]
    dist = BLK + qi - ki
    band = (dist >= 0) & (dist <= n_back)
    first = (jnp.arange(nb) == 0)[:, None, None]
    mask = band[None] & (jnp.logical_not(first) | (ki >= BLK)[None])
    logits = jnp.where(mask, logits, NEG_INF)
    lse = jax.nn.logsumexp(logits, axis=-1)
    p = jnp.exp(logits - lse[..., None])
    out = jnp.einsum('brhnqk,brhnkd->brhnqd', p.astype(v.dtype), vb)
    out = out.reshape(b, dilation, h, length, dh).transpose(0, 3, 1, 2, 4).reshape(b, sp, h, dh)[:, :s]
    lse = lse.reshape(b, dilation, h, length).transpose(0, 3, 1, 2).reshape(b, sp, h)[:, :s]
    return out, lse


def _dilated_attention_mixer(h, positions, w_in):
    b, s, _ = h.shape
    proj = h @ w_in
    q, k, v, q_mem = jnp.split(proj, [ATTN_W, 2 * ATTN_W, 3 * ATTN_W], axis=-1)
    q = _partial_rotary(q.reshape(b, s, ATTN_HEADS, HEAD_DIM), positions)
    k = _partial_rotary(k.reshape(b, s, ATTN_HEADS, HEAD_DIM), positions)
    v = v.reshape(b, s, ATTN_HEADS, HEAD_DIM)
    outs, lses = [], []
    for g, (window, dilation) in enumerate(ATTN_GROUPS):
        sl = slice(g * HEADS_PER_GROUP, (g + 1) * HEADS_PER_GROUP)
        o, l = _dilated_group(q[:, :, sl], k[:, :, sl], v[:, :, sl], window, dilation)
        outs.append(o)
        lses.append(l)
    w = jax.nn.softmax(jnp.stack(lses, axis=0), axis=0)
    merged = jnp.einsum('gbsh,gbshd->bshd', w.astype(v.dtype), jnp.stack(outs, axis=0))
    return merged.reshape(b, s, ATTN_OUT_W), q_mem


def _spatial_gating_mixer(h, w_in, ln_g, ln_b, w_spatial, b_spatial):
    b, s, _ = h.shape
    proj = h @ w_in
    u, v, q_mem = jnp.split(proj, [SGU_W, 2 * SGU_W], axis=-1)
    u = jax.nn.gelu(u)
    v = _layer_norm(jax.nn.gelu(v), ln_g, ln_b)
    v = v.reshape(b, s // SGU_CHUNK, SGU_CHUNK, SGU_GROUPS, SGU_GROUP_DIM)
    causal = jnp.tril(jnp.ones((SGU_CHUNK, SGU_CHUNK), dtype=bool))
    w_s = jnp.where(causal[None], w_spatial, 0.0).astype(v.dtype)
    mixed = jnp.einsum('gts,bnsgc->bntgc', w_s, v) + b_spatial.T[None, None, :, :, None]
    return u * mixed.reshape(b, s, SGU_W), q_mem


def _memory_attention(q_mem, mem_n, w_mem_kv):
    b, s, _ = q_mem.shape
    kv = mem_n @ w_mem_kv
    k, v = jnp.split(kv, 2, axis=-1)
    q = q_mem.reshape(b, s, MEM_HEADS, HEAD_DIM)
    k = k.reshape(b, -1, MEM_HEADS, HEAD_DIM)
    v = v.reshape(b, -1, MEM_HEADS, HEAD_DIM)
    logits = jnp.einsum('bshd,bmhd->bhsm', q, k,
                        preferred_element_type=jnp.float32) * (HEAD_DIM ** -0.5)
    p = jax.nn.softmax(logits, axis=-1)
    out = jnp.einsum('bhsm,bmhd->bshd', p.astype(v.dtype), v)
    return out.reshape(b, s, MEM_W)


def _swiglu(h, w_gate, w_up, w_down):
    return (jax.nn.silu(h @ w_gate) * (h @ w_up)) @ w_down


def _fwd_setup_inputs(seed: int = 0) -> dict:
    key = jax.random.key(seed)
    ks = jax.random.split(key, 24)
    n_a = (DEPTH + 1) // 2
    n_b = DEPTH // 2

    def dense(k, shape, fan_in):
        return jax.random.normal(k, shape, jnp.float32) * (fan_in ** -0.5)

    def gain(k, shape):
        return 1.0 + 0.02 * jax.random.normal(k, shape, jnp.float32)

    def small(k, shape):
        return 0.02 * jax.random.normal(k, shape, jnp.float32)

    x = jax.random.normal(ks[0], (BATCH, SEQ, D_MODEL), jnp.float32)
    mem = jax.random.normal(ks[1], (BATCH, MEM_LEN, D_MODEL), jnp.float32)
    offset = jax.random.randint(ks[2], (BATCH, 1), 0, 4096, dtype=jnp.int32)
    positions = offset + jnp.arange(SEQ, dtype=jnp.int32)[None, :]
    return {
        'x': x,
        'mem': mem,
        'positions': positions,
        'mix_norm': gain(ks[3], (DEPTH, D_MODEL)),
        'mem_norm': gain(ks[4], (DEPTH, D_MODEL)),
        'w_mem_kv': dense(ks[5], (DEPTH, D_MODEL, 2 * MEM_W), D_MODEL),
        'ffn_norm': gain(ks[6], (DEPTH, D_MODEL)),
        'w_gate': dense(ks[7], (DEPTH, D_MODEL, FFN_HIDDEN), D_MODEL),
        'w_up': dense(ks[8], (DEPTH, D_MODEL, FFN_HIDDEN), D_MODEL),
        'w_down': dense(ks[9], (DEPTH, FFN_HIDDEN, D_MODEL), FFN_HIDDEN),
        'attn_w_in': dense(ks[10], (n_a, D_MODEL, 3 * ATTN_W + MEM_W), D_MODEL),
        'attn_w_out': dense(ks[11], (n_a, ATTN_OUT_W + MEM_W, D_MODEL), ATTN_OUT_W + MEM_W),
        'sgu_w_in': dense(ks[12], (n_b, D_MODEL, 2 * SGU_W + MEM_W), D_MODEL),
        'sgu_ln_g': gain(ks[13], (n_b, SGU_W)),
        'sgu_ln_b': small(ks[14], (n_b, SGU_W)),
        'sgu_w_spatial': dense(ks[15], (n_b, SGU_GROUPS, SGU_CHUNK, SGU_CHUNK), SGU_CHUNK),
        'sgu_b_spatial': gain(ks[16], (n_b, SGU_GROUPS, SGU_CHUNK)),
        'sgu_w_out': dense(ks[17], (n_b, SGU_W + MEM_W, D_MODEL), SGU_W + MEM_W),
        'final_norm': gain(ks[18], (D_MODEL,)),
    }


def _fwd_reference(x, mem, positions, mix_norm, mem_norm, w_mem_kv, ffn_norm, w_gate, w_up, w_down,
              attn_w_in, attn_w_out, sgu_w_in, sgu_ln_g, sgu_ln_b, sgu_w_spatial, sgu_b_spatial,
              sgu_w_out, final_norm):
    for i in range(DEPTH):
        j = i // 2
        h = _rms_norm(x, mix_norm[i])
        if i % 2 == 0:
            mix_out, q_mem = _dilated_attention_mixer(h, positions, attn_w_in[j])
            w_out = attn_w_out[j]
        else:
            mix_out, q_mem = _spatial_gating_mixer(h, sgu_w_in[j], sgu_ln_g[j], sgu_ln_b[j],
                                                   sgu_w_spatial[j], sgu_b_spatial[j])
            w_out = sgu_w_out[j]
        mem_out = _memory_attention(q_mem, _rms_norm(mem, mem_norm[i]), w_mem_kv[i])
        x = x + jnp.concatenate([mix_out, mem_out], axis=-1) @ w_out
        x = x + _swiglu(_rms_norm(x, ffn_norm[i]), w_gate[i], w_up[i], w_down[i])
    return _rms_norm(x, final_norm)


import jax as _jax
import jax.numpy as _jnp

TWIN_FORMAT = 'train_step'
FWD_PARAMS = ['x', 'mem', 'positions', 'mix_norm', 'mem_norm', 'w_mem_kv', 'ffn_norm', 'w_gate', 'w_up', 'w_down', 'attn_w_in', 'attn_w_out', 'sgu_w_in', 'sgu_ln_g', 'sgu_ln_b', 'sgu_w_spatial', 'sgu_b_spatial', 'sgu_w_out', 'final_norm']
TWIN_WEIGHTS = ['mix_norm', 'mem_norm', 'w_mem_kv', 'ffn_norm', 'w_gate', 'w_up', 'w_down', 'attn_w_in', 'attn_w_out', 'sgu_w_in', 'sgu_ln_g', 'sgu_ln_b', 'sgu_w_spatial', 'sgu_b_spatial', 'sgu_w_out', 'final_norm']
TWIN_DIFF_INPUT = 'x'
TWIN_INPUTS = ['x', 'mem', 'positions', 'mix_norm', 'mem_norm', 'w_mem_kv', 'ffn_norm', 'w_gate', 'w_up', 'w_down', 'attn_w_in', 'attn_w_out', 'sgu_w_in', 'sgu_ln_g', 'sgu_ln_b', 'sgu_w_spatial', 'sgu_b_spatial', 'sgu_w_out', 'final_norm', 'loss_target', 'm_mix_norm', 'm_mem_norm', 'm_w_mem_kv', 'm_ffn_norm', 'm_w_gate', 'm_w_up', 'm_w_down', 'm_attn_w_in', 'm_attn_w_out', 'm_sgu_w_in', 'm_sgu_ln_g', 'm_sgu_ln_b', 'm_sgu_w_spatial', 'm_sgu_b_spatial', 'm_sgu_w_out', 'm_final_norm', 'v_mix_norm', 'v_mem_norm', 'v_w_mem_kv', 'v_ffn_norm', 'v_w_gate', 'v_w_up', 'v_w_down', 'v_attn_w_in', 'v_attn_w_out', 'v_sgu_w_in', 'v_sgu_ln_g', 'v_sgu_ln_b', 'v_sgu_w_spatial', 'v_sgu_b_spatial', 'v_sgu_w_out', 'v_final_norm']
TWIN_OUTPUTS = ['loss', 'grad_x', 'grad_mix_norm', 'grad_mem_norm', 'grad_w_mem_kv', 'grad_ffn_norm', 'grad_w_gate', 'grad_w_up', 'grad_w_down', 'grad_attn_w_in', 'grad_attn_w_out', 'grad_sgu_w_in', 'grad_sgu_ln_g', 'grad_sgu_ln_b', 'grad_sgu_w_spatial', 'grad_sgu_b_spatial', 'grad_sgu_w_out', 'grad_final_norm', 'delta_mix_norm', 'delta_mem_norm', 'delta_w_mem_kv', 'delta_ffn_norm', 'delta_w_gate', 'delta_w_up', 'delta_w_down', 'delta_attn_w_in', 'delta_attn_w_out', 'delta_sgu_w_in', 'delta_sgu_ln_g', 'delta_sgu_ln_b', 'delta_sgu_w_spatial', 'delta_sgu_b_spatial', 'delta_sgu_w_out', 'delta_final_norm', 'new_m_mix_norm', 'new_m_mem_norm', 'new_m_w_mem_kv', 'new_m_ffn_norm', 'new_m_w_gate', 'new_m_w_up', 'new_m_w_down', 'new_m_attn_w_in', 'new_m_attn_w_out', 'new_m_sgu_w_in', 'new_m_sgu_ln_g', 'new_m_sgu_ln_b', 'new_m_sgu_w_spatial', 'new_m_sgu_b_spatial', 'new_m_sgu_w_out', 'new_m_final_norm', 'new_v_mix_norm', 'new_v_mem_norm', 'new_v_w_mem_kv', 'new_v_ffn_norm', 'new_v_w_gate', 'new_v_w_up', 'new_v_w_down', 'new_v_attn_w_in', 'new_v_attn_w_out', 'new_v_sgu_w_in', 'new_v_sgu_ln_g', 'new_v_sgu_ln_b', 'new_v_sgu_w_spatial', 'new_v_sgu_b_spatial', 'new_v_sgu_w_out', 'new_v_final_norm']
TWIN_LEAF_KINDS = {'loss': 'loss', 'grad_x': 'grad_x', 'grad_mix_norm': 'grad_w', 'grad_mem_norm': 'grad_w', 'grad_w_mem_kv': 'grad_w', 'grad_ffn_norm': 'grad_w', 'grad_w_gate': 'grad_w', 'grad_w_up': 'grad_w', 'grad_w_down': 'grad_w', 'grad_attn_w_in': 'grad_w', 'grad_attn_w_out': 'grad_w', 'grad_sgu_w_in': 'grad_w', 'grad_sgu_ln_g': 'grad_w', 'grad_sgu_ln_b': 'grad_w', 'grad_sgu_w_spatial': 'grad_w', 'grad_sgu_b_spatial': 'grad_w', 'grad_sgu_w_out': 'grad_w', 'grad_final_norm': 'grad_w', 'delta_mix_norm': 'delta_w', 'delta_mem_norm': 'delta_w', 'delta_w_mem_kv': 'delta_w', 'delta_ffn_norm': 'delta_w', 'delta_w_gate': 'delta_w', 'delta_w_up': 'delta_w', 'delta_w_down': 'delta_w', 'delta_attn_w_in': 'delta_w', 'delta_attn_w_out': 'delta_w', 'delta_sgu_w_in': 'delta_w', 'delta_sgu_ln_g': 'delta_w', 'delta_sgu_ln_b': 'delta_w', 'delta_sgu_w_spatial': 'delta_w', 'delta_sgu_b_spatial': 'delta_w', 'delta_sgu_w_out': 'delta_w', 'delta_final_norm': 'delta_w', 'new_m_mix_norm': 'new_m', 'new_m_mem_norm': 'new_m', 'new_m_w_mem_kv': 'new_m', 'new_m_ffn_norm': 'new_m', 'new_m_w_gate': 'new_m', 'new_m_w_up': 'new_m', 'new_m_w_down': 'new_m', 'new_m_attn_w_in': 'new_m', 'new_m_attn_w_out': 'new_m', 'new_m_sgu_w_in': 'new_m', 'new_m_sgu_ln_g': 'new_m', 'new_m_sgu_ln_b': 'new_m', 'new_m_sgu_w_spatial': 'new_m', 'new_m_sgu_b_spatial': 'new_m', 'new_m_sgu_w_out': 'new_m', 'new_m_final_norm': 'new_m', 'new_v_mix_norm': 'new_v', 'new_v_mem_norm': 'new_v', 'new_v_w_mem_kv': 'new_v', 'new_v_ffn_norm': 'new_v', 'new_v_w_gate': 'new_v', 'new_v_w_up': 'new_v', 'new_v_w_down': 'new_v', 'new_v_attn_w_in': 'new_v', 'new_v_attn_w_out': 'new_v', 'new_v_sgu_w_in': 'new_v', 'new_v_sgu_ln_g': 'new_v', 'new_v_sgu_ln_b': 'new_v', 'new_v_sgu_w_spatial': 'new_v', 'new_v_sgu_b_spatial': 'new_v', 'new_v_sgu_w_out': 'new_v', 'new_v_final_norm': 'new_v'}


def _forward(args):
    return _fwd_reference(*[args[k] for k in FWD_PARAMS])


def _output_shape():
    def fwd():
        inp = _fwd_setup_inputs(0)
        return _fwd_reference(*[inp[k] for k in FWD_PARAMS])
    out = _jax.eval_shape(fwd)
    return out.shape, out.dtype

N_MICROBATCH = 1
ADAM_LR = 0.001
ADAM_B1 = 0.9
ADAM_B2 = 0.999
ADAM_EPS = 1e-08
ADAM_WD = 0.01
ADAM_STEP = 10
PER_EXAMPLE_BATCH_AXIS = {'x': 0, 'mem': 0, 'positions': 0, 'loss_target': 0}
SHARED_INPUTS = []
_WEIGHT_DTYPES = {'mix_norm': _jnp.float32, 'mem_norm': _jnp.float32, 'w_mem_kv': _jnp.float32, 'ffn_norm': _jnp.float32, 'w_gate': _jnp.float32, 'w_up': _jnp.float32, 'w_down': _jnp.float32, 'attn_w_in': _jnp.float32, 'attn_w_out': _jnp.float32, 'sgu_w_in': _jnp.float32, 'sgu_ln_g': _jnp.float32, 'sgu_ln_b': _jnp.float32, 'sgu_w_spatial': _jnp.float32, 'sgu_b_spatial': _jnp.float32, 'sgu_w_out': _jnp.float32, 'final_norm': _jnp.float32}
MOMENT_SCALE = {'mix_norm': 6.473110e-02, 'mem_norm': 1.320529e-02, 'w_mem_kv': 1.853599e-02, 'ffn_norm': 8.992609e-02, 'w_gate': 3.947026e-02, 'w_up': 3.827033e-02, 'w_down': 6.346309e-02, 'attn_w_in': 2.303290e-02, 'attn_w_out': 2.435127e-02, 'sgu_w_in': 6.406559e-02, 'sgu_ln_g': 4.507141e-02, 'sgu_ln_b': 4.390972e-02, 'sgu_w_spatial': 4.561336e-02, 'sgu_b_spatial': 6.635413e-02, 'sgu_w_out': 7.178773e-02, 'final_norm': 3.204916e+01}


def _to_microbatches(a, axis):
    t = _jnp.moveaxis(a, axis, 0)
    t = t.reshape((N_MICROBATCH, t.shape[0] // N_MICROBATCH) + t.shape[1:])
    return _jnp.moveaxis(t, 1, axis + 1)


def setup_inputs(seed: int = 0) -> dict:
    inp = _fwd_setup_inputs(seed)
    key = _jax.random.fold_in(_jax.random.key(seed), 7919)
    shape, _ = _output_shape()
    out = dict(inp)
    out["loss_target"] = _jax.random.normal(_jax.random.fold_in(key, 0), shape, _jnp.float32)
    for i, name in enumerate(TWIN_WEIGHTS):
        w = inp[name].astype(_jnp.float32)
        if MOMENT_SCALE is None:
            s = _jnp.sqrt(_jnp.mean(_jnp.square(w)) + 1e-30)
        else:
            s = MOMENT_SCALE[name]
        km, kv = _jax.random.split(_jax.random.fold_in(key, i + 1))
        out[name] = w
        out["m_" + name] = s * _jax.random.normal(km, w.shape, _jnp.float32)
        out["v_" + name] = (s * s) * _jax.random.uniform(kv, w.shape, _jnp.float32, 0.5, 1.5)
    if N_MICROBATCH > 1:
        for name, axis in PER_EXAMPLE_BATCH_AXIS.items():
            out[name] = _to_microbatches(out[name], axis)
    return {'x': out['x'], 'mem': out['mem'], 'positions': out['positions'], 'mix_norm': out['mix_norm'], 'mem_norm': out['mem_norm'], 'w_mem_kv': out['w_mem_kv'], 'ffn_norm': out['ffn_norm'], 'w_gate': out['w_gate'], 'w_up': out['w_up'], 'w_down': out['w_down'], 'attn_w_in': out['attn_w_in'], 'attn_w_out': out['attn_w_out'], 'sgu_w_in': out['sgu_w_in'], 'sgu_ln_g': out['sgu_ln_g'], 'sgu_ln_b': out['sgu_ln_b'], 'sgu_w_spatial': out['sgu_w_spatial'], 'sgu_b_spatial': out['sgu_b_spatial'], 'sgu_w_out': out['sgu_w_out'], 'final_norm': out['final_norm'], 'loss_target': out['loss_target'], 'm_mix_norm': out['m_mix_norm'], 'm_mem_norm': out['m_mem_norm'], 'm_w_mem_kv': out['m_w_mem_kv'], 'm_ffn_norm': out['m_ffn_norm'], 'm_w_gate': out['m_w_gate'], 'm_w_up': out['m_w_up'], 'm_w_down': out['m_w_down'], 'm_attn_w_in': out['m_attn_w_in'], 'm_attn_w_out': out['m_attn_w_out'], 'm_sgu_w_in': out['m_sgu_w_in'], 'm_sgu_ln_g': out['m_sgu_ln_g'], 'm_sgu_ln_b': out['m_sgu_ln_b'], 'm_sgu_w_spatial': out['m_sgu_w_spatial'], 'm_sgu_b_spatial': out['m_sgu_b_spatial'], 'm_sgu_w_out': out['m_sgu_w_out'], 'm_final_norm': out['m_final_norm'], 'v_mix_norm': out['v_mix_norm'], 'v_mem_norm': out['v_mem_norm'], 'v_w_mem_kv': out['v_w_mem_kv'], 'v_ffn_norm': out['v_ffn_norm'], 'v_w_gate': out['v_w_gate'], 'v_w_up': out['v_w_up'], 'v_w_down': out['v_w_down'], 'v_attn_w_in': out['v_attn_w_in'], 'v_attn_w_out': out['v_attn_w_out'], 'v_sgu_w_in': out['v_sgu_w_in'], 'v_sgu_ln_g': out['v_sgu_ln_g'], 'v_sgu_ln_b': out['v_sgu_ln_b'], 'v_sgu_w_spatial': out['v_sgu_w_spatial'], 'v_sgu_b_spatial': out['v_sgu_b_spatial'], 'v_sgu_w_out': out['v_sgu_w_out'], 'v_final_norm': out['v_final_norm']}


def _loss(weights, diff, rest, loss_target):
    with _jax.named_scope("forward"):
        args = {**rest, TWIN_DIFF_INPUT: diff, **{k: w.astype(_WEIGHT_DTYPES[k]) for k, w in weights.items()}}
        y = _forward(args)
    with _jax.named_scope("loss_head"):
        err = _jnp.square(y.astype(_jnp.float32) - loss_target)
        return 0.5 * _jnp.sum(_jnp.mean(err, axis=-1)) if err.ndim else 0.5 * err


def _adamw(w, g, m, v):
    m = ADAM_B1 * m + (1.0 - ADAM_B1) * g
    v = ADAM_B2 * v + (1.0 - ADAM_B2) * _jnp.square(g)
    m_hat = m / (1.0 - ADAM_B1 ** ADAM_STEP)
    v_hat = v / (1.0 - ADAM_B2 ** ADAM_STEP)
    delta = -ADAM_LR * (m_hat / (_jnp.sqrt(v_hat) + ADAM_EPS) + ADAM_WD * w)
    return delta, m, v


def reference(x, mem, positions, mix_norm, mem_norm, w_mem_kv, ffn_norm, w_gate, w_up, w_down, attn_w_in, attn_w_out, sgu_w_in, sgu_ln_g, sgu_ln_b, sgu_w_spatial, sgu_b_spatial, sgu_w_out, final_norm, loss_target, m_mix_norm, m_mem_norm, m_w_mem_kv, m_ffn_norm, m_w_gate, m_w_up, m_w_down, m_attn_w_in, m_attn_w_out, m_sgu_w_in, m_sgu_ln_g, m_sgu_ln_b, m_sgu_w_spatial, m_sgu_b_spatial, m_sgu_w_out, m_final_norm, v_mix_norm, v_mem_norm, v_w_mem_kv, v_ffn_norm, v_w_gate, v_w_up, v_w_down, v_attn_w_in, v_attn_w_out, v_sgu_w_in, v_sgu_ln_g, v_sgu_ln_b, v_sgu_w_spatial, v_sgu_b_spatial, v_sgu_w_out, v_final_norm):
    given = dict(x=x, mem=mem, positions=positions, mix_norm=mix_norm, mem_norm=mem_norm, w_mem_kv=w_mem_kv, ffn_norm=ffn_norm, w_gate=w_gate, w_up=w_up, w_down=w_down, attn_w_in=attn_w_in, attn_w_out=attn_w_out, sgu_w_in=sgu_w_in, sgu_ln_g=sgu_ln_g, sgu_ln_b=sgu_ln_b, sgu_w_spatial=sgu_w_spatial, sgu_b_spatial=sgu_b_spatial, sgu_w_out=sgu_w_out, final_norm=final_norm, loss_target=loss_target, m_mix_norm=m_mix_norm, m_mem_norm=m_mem_norm, m_w_mem_kv=m_w_mem_kv, m_ffn_norm=m_ffn_norm, m_w_gate=m_w_gate, m_w_up=m_w_up, m_w_down=m_w_down, m_attn_w_in=m_attn_w_in, m_attn_w_out=m_attn_w_out, m_sgu_w_in=m_sgu_w_in, m_sgu_ln_g=m_sgu_ln_g, m_sgu_ln_b=m_sgu_ln_b, m_sgu_w_spatial=m_sgu_w_spatial, m_sgu_b_spatial=m_sgu_b_spatial, m_sgu_w_out=m_sgu_w_out, m_final_norm=m_final_norm, v_mix_norm=v_mix_norm, v_mem_norm=v_mem_norm, v_w_mem_kv=v_w_mem_kv, v_ffn_norm=v_ffn_norm, v_w_gate=v_w_gate, v_w_up=v_w_up, v_w_down=v_w_down, v_attn_w_in=v_attn_w_in, v_attn_w_out=v_attn_w_out, v_sgu_w_in=v_sgu_w_in, v_sgu_ln_g=v_sgu_ln_g, v_sgu_ln_b=v_sgu_ln_b, v_sgu_w_spatial=v_sgu_w_spatial, v_sgu_b_spatial=v_sgu_b_spatial, v_sgu_w_out=v_sgu_w_out, v_final_norm=v_final_norm)
    weights = {n: given[n] for n in TWIN_WEIGHTS}
    shared = {n: given[n] for n in SHARED_INPUTS}
    per_example = {n: given[n] for n in ['x', 'mem', 'positions']}
    grad_fn = _jax.value_and_grad(_loss, argnums=(0, 1))

    def one_microbatch(ex, loss_target):
        ex = dict(ex)
        diff = ex.pop(TWIN_DIFF_INPUT)
        return grad_fn(weights, diff, {**shared, **ex}, loss_target)

    if N_MICROBATCH == 1:
        loss, (grad_w, grad_x) = one_microbatch(per_example, given["loss_target"])
    else:
        def body(carry, xs):
            loss_sum, grad_sum = carry
            l_k, (gw_k, gx_k) = one_microbatch(xs[0], xs[1])
            with _jax.named_scope("update"):
                return (loss_sum + l_k, _jax.tree.map(_jnp.add, grad_sum, gw_k)), gx_k

        init = (_jnp.zeros((), _jnp.float32), _jax.tree.map(_jnp.zeros_like, weights))
        (loss, grad_w), grad_x = _jax.lax.scan(body, init, (per_example, given["loss_target"]))
    with _jax.named_scope("update"):
        delta_w, new_m, new_v = {}, {}, {}
        for n in TWIN_WEIGHTS:
            delta_w[n], new_m[n], new_v[n] = _adamw(weights[n], grad_w[n], given["m_" + n], given["v_" + n])
    return (loss, grad_x, *[grad_w[n] for n in TWIN_WEIGHTS], *[delta_w[n] for n in TWIN_WEIGHTS],
            *[new_m[n] for n in TWIN_WEIGHTS], *[new_v[n] for n in TWIN_WEIGHTS])
```

```python
import functools

import jax
import jax.numpy as jnp
from jax import lax
from jax.experimental import pallas as pl
from jax.experimental.pallas import tpu as pltpu

F32 = jnp.float32
BF16 = jnp.bfloat16

N_DEV = 8
HEAD = 128
HPG = 4
GROUP_W = HPG * HEAD
N_GROUPS = 3
DILATIONS = (1, 4, 16)
BLK = 128
SGU_GROUPS = 12
SGU_W = SGU_GROUPS * HEAD
ROT_HALF = 16
ROPE_THETA = 500000.0
NORM_EPS = 1e-6
LN_EPS = 1e-5
NEG_INF = -1e30
SCALE = HEAD ** -0.5

ADAM_LR = 0.001
ADAM_B1 = 0.9
ADAM_B2 = 0.999
ADAM_EPS = 1e-08
ADAM_WD = 0.01
ADAM_STEP = 10

VMEM_LIMIT_V7X = 56 * 1024 * 1024
MESH_AXES = ("x", "y", "c")

_DN = {
    "nn": (((1,), (0,)), ((), ())),
    "nt": (((1,), (1,)), ((), ())),
    "tn": (((0,), (0,)), ((), ())),
}


def _dot(a, b, kind="nn"):
    return lax.dot_general(a, b, _DN[kind], preferred_element_type=F32)


def _params():
    return pltpu.CompilerParams(vmem_limit_bytes=VMEM_LIMIT_V7X)


def _row_tile(rows, cap):
    if rows <= cap:
        return rows
    t = cap - cap % 16
    while t >= 16:
        if rows % t == 0:
            return t
        t -= 16
    return rows


def _gelu(x):
    c = 0.7978845608028654
    return 0.5 * x * (1.0 + jnp.tanh(c * (x + 0.044715 * x * x * x)))


def _gelu_grad(x):
    c = 0.7978845608028654
    t = jnp.tanh(c * (x + 0.044715 * x * x * x))
    return 0.5 * (1.0 + t) + 0.5 * x * (1.0 - t * t) * c * (1.0 + 3.0 * 0.044715 * x * x)


def _sigmoid(x):
    return 1.0 / (1.0 + jnp.exp(-x))


def _matmul(name, grid, terms, n_acc, acc_shape, extras, outs, epilogue):
    nk = grid[-1]
    nt, ne, no = len(terms), len(extras), len(outs)
    kinds = [(t[4], t[5]) for t in terms]

    def body(*refs):
        ab = refs[:2 * nt]
        ex = refs[2 * nt:2 * nt + ne]
        out = refs[2 * nt + ne:2 * nt + ne + no]
        accs = refs[2 * nt + ne + no:]
        parts = [None] * n_acc
        for t, (kind, ai) in enumerate(kinds):
            p = _dot(ab[2 * t][...].astype(BF16), ab[2 * t + 1][...].astype(BF16), kind)
            parts[ai] = p if parts[ai] is None else parts[ai] + p
        if nk == 1:
            epilogue(parts, ex, out)
        else:
            k = pl.program_id(len(grid) - 1)

            @pl.when(k == 0)
            def _():
                for ai in range(n_acc):
                    accs[ai][...] = parts[ai]

            @pl.when(k > 0)
            def _():
                for ai in range(n_acc):
                    accs[ai][...] += parts[ai]

            @pl.when(k == nk - 1)
            def _():
                epilogue([a[...] for a in accs], ex, out)

    in_specs, args = [], []
    for (a, a_spec, b, b_spec, _, _) in terms:
        in_specs += [a_spec, b_spec]
        args += [a, b]
    for (e, e_spec) in extras:
        in_specs.append(e_spec)
        args.append(e)
    scratch = [] if nk == 1 else [pltpu.VMEM(acc_shape, F32) for _ in range(n_acc)]
    res = pl.pallas_call(
        body, grid=grid, in_specs=in_specs,
        out_specs=[o[1] for o in outs], out_shape=[o[0] for o in outs],
        scratch_shapes=scratch, compiler_params=_params(), name=name,
    )(*args)
    return res


def _store_epilogue(parts, ex, out):
    out[0][...] = parts[0].astype(out[0].dtype)


def _residual_epilogue(parts, ex, out):
    out[0][...] = (parts[0] + ex[0][...]).astype(out[0].dtype)


def _rmsnorm_fwd(x, g, name):
    rows, d = x.shape
    tm = _row_tile(rows, 512)

    def body(x_ref, g_ref, o_ref):
        xf = x_ref[...]
        r = lax.rsqrt(jnp.mean(xf * xf, axis=-1, keepdims=True) + NORM_EPS)
        o_ref[...] = (xf * r * g_ref[...]).astype(o_ref.dtype)

    return pl.pallas_call(
        body, grid=(rows // tm,),
        in_specs=[pl.BlockSpec((tm, d), lambda i: (i, 0)), pl.BlockSpec((1, d), lambda i: (0, 0))],
        out_specs=pl.BlockSpec((tm, d), lambda i: (i, 0)),
        out_shape=jax.ShapeDtypeStruct((rows, d), BF16),
        compiler_params=_params(), name=name,
    )(x, g)


def _rmsnorm_bwd(x, g, dh, dres, name):
    rows, d = x.shape
    tm = _row_tile(rows, 256)
    has_res = dres is not None

    def body(*refs):
        if has_res:
            x_ref, g_ref, dh_ref, dres_ref, dx_ref, dxb_ref, dg_ref = refs
        else:
            x_ref, g_ref, dh_ref, dx_ref, dxb_ref, dg_ref = refs
        i = pl.program_id(0)
        xf = x_ref[...]
        r = lax.rsqrt(jnp.mean(xf * xf, axis=-1, keepdims=True) + NORM_EPS)
        xhat = xf * r
        dy = dh_ref[...]
        gdy = dy * g_ref[...]
        c = jnp.mean(gdy * xhat, axis=-1, keepdims=True)
        dx = r * (gdy - xhat * c)
        if has_res:
            dx = dx + dres_ref[...]
        dx_ref[...] = dx
        dxb_ref[...] = dx.astype(BF16)

        @pl.when(i == 0)
        def _():
            dg_ref[...] = jnp.zeros_like(dg_ref)

        dg_ref[...] += jnp.sum(dy * xhat, axis=0, keepdims=True)

    row_spec = pl.BlockSpec((tm, d), lambda i: (i, 0))
    vec_spec = pl.BlockSpec((1, d), lambda i: (0, 0))
    in_specs = [row_spec, vec_spec, row_spec] + ([row_spec] if has_res else [])
    args = [x, g, dh] + ([dres] if has_res else [])
    return pl.pallas_call(
        body, grid=(rows // tm,), in_specs=in_specs,
        out_specs=[row_spec, row_spec, vec_spec],
        out_shape=[jax.ShapeDtypeStruct((rows, d), F32), jax.ShapeDtypeStruct((rows, d), BF16),
                   jax.ShapeDtypeStruct((1, d), F32)],
        compiler_params=_params(), name=name,
    )(*args)


def _loss_head(x, g, target):
    rows, d = x.shape
    tm = _row_tile(rows, 256)

    def body(x_ref, g_ref, t_ref, loss_ref, dx_ref, dxb_ref, dg_ref):
        i = pl.program_id(0)
        xf = x_ref[...]
        gv = g_ref[...]
        r = lax.rsqrt(jnp.mean(xf * xf, axis=-1, keepdims=True) + NORM_EPS)
        xhat = xf * r
        err = xhat * gv - t_ref[...]
        row_loss = jnp.mean(err * err, axis=-1, keepdims=True)
        dy = err * (1.0 / d)
        gdy = dy * gv
        c = jnp.mean(gdy * xhat, axis=-1, keepdims=True)
        dx = r * (gdy - xhat * c)
        dx_ref[...] = dx
        dxb_ref[...] = dx.astype(BF16)

        @pl.when(i == 0)
        def _():
            dg_ref[...] = jnp.zeros_like(dg_ref)
            loss_ref[...] = jnp.zeros_like(loss_ref)

        dg_ref[...] += jnp.sum(dy * xhat, axis=0, keepdims=True)
        loss_ref[...] += 0.5 * jnp.sum(row_loss, axis=0, keepdims=True)

    row_spec = pl.BlockSpec((tm, d), lambda i: (i, 0))
    vec_spec = pl.BlockSpec((1, d), lambda i: (0, 0))
    return pl.pallas_call(
        body, grid=(rows // tm,), in_specs=[row_spec, vec_spec, row_spec],
        out_specs=[pl.BlockSpec((1, 1), lambda i: (0, 0)), row_spec, row_spec, vec_spec],
        out_shape=[jax.ShapeDtypeStruct((1, 1), F32), jax.ShapeDtypeStruct((rows, d), F32),
                   jax.ShapeDtypeStruct((rows, d), BF16), jax.ShapeDtypeStruct((1, d), F32)],
        compiler_params=_params(), name="loss_head",
    )(x, g, target)


def _rotary_tables(positions):
    inv_freq = ROPE_THETA ** (-jnp.arange(ROT_HALF, dtype=F32) / ROT_HALF)
    ang = positions.astype(F32)[:, None] * inv_freq
    cos, sin = jnp.cos(ang), jnp.sin(ang)
    s = positions.shape[0]
    z = jnp.zeros((s, HEAD - 2 * ROT_HALF), F32)
    z16 = jnp.zeros((s, ROT_HALF), F32)
    c = jnp.concatenate([cos, cos, jnp.ones_like(z)], axis=1)
    s1 = jnp.concatenate([z16, sin, z], axis=1)
    s2 = jnp.concatenate([-sin, z16, z], axis=1)
    return c, s1, s2


def _attn_in_proj(h, w_blk, tabs):
    s, d = h.shape
    nb, _, bw = w_blk.shape
    heads_per_blk = bw // HEAD
    n_rot_heads = 2 * N_GROUPS * HPG
    tm = _row_tile(s, 512)

    def epilogue(parts, ex, out):
        j = pl.program_id(0)
        acc = parts[0]
        c, s1, s2 = ex[0][...], ex[1][...], ex[2][...]
        for t in range(heads_per_blk):
            seg = acc[:, t * HEAD:(t + 1) * HEAD]
            rot = seg * c + pltpu.roll(seg, ROT_HALF, 1) * s1 + pltpu.roll(seg, HEAD - ROT_HALF, 1) * s2
            is_rot = (j * heads_per_blk + t) < n_rot_heads
            out[0][:, t * HEAD:(t + 1) * HEAD] = jnp.where(is_rot, rot, seg).astype(BF16)

    tab_spec = pl.BlockSpec((tm, HEAD), lambda j, m, k: (m, 0))
    return _matmul(
        "attn_in_proj", (nb, s // tm, 1),
        [(h, pl.BlockSpec((tm, d), lambda j, m, k: (m, 0)),
          w_blk, pl.BlockSpec((None, d, bw), lambda j, m, k: (j, 0, 0)), "nn", 0)],
        1, None, [(tabs[0], tab_spec), (tabs[1], tab_spec), (tabs[2], tab_spec)],
        [(jax.ShapeDtypeStruct((s, nb * bw), BF16), pl.BlockSpec((tm, bw), lambda j, m, k: (m, j)))],
        epilogue)[0]


ATT_TILE_BLOCKS = 4


def _att_blocks(seq_blocks):
    return min(ATT_TILE_BLOCKS, seq_blocks)


def _band_masks():
    qi = lax.broadcasted_iota(jnp.int32, (BLK, BLK), 0)
    ki = lax.broadcasted_iota(jnp.int32, (BLK, BLK), 1)
    return ki <= qi, ki >= qi


def _attn_fwd(q_arr, k_arr, v_arr, offs, seq_blocks, name):
    s = q_arr.shape[0]
    qo, ko, vo = offs
    nb = _att_blocks(seq_blocks)

    def body(q_ref, kc_ref, kp_ref, vc_ref, vp_ref, o_ref, lse_ref):
        n = pl.program_id(0)
        tile_starts_seq = (n * nb) % seq_blocks == 0
        mask_c, mask_p = _band_masks()
        for b in range(nb):
            r0 = b * BLK
            for h in range(HPG):
                c0 = h * HEAD
                q = q_ref[r0:r0 + BLK, c0:c0 + HEAD]
                kc = kc_ref[r0:r0 + BLK, c0:c0 + HEAD]
                vc = vc_ref[r0:r0 + BLK, c0:c0 + HEAD]
                if b == 0:
                    kp = kp_ref[:, c0:c0 + HEAD]
                    vp = vp_ref[:, c0:c0 + HEAD]
                else:
                    kp = kc_ref[r0 - BLK:r0, c0:c0 + HEAD]
                    vp = vc_ref[r0 - BLK:r0, c0:c0 + HEAD]
                s_c = jnp.where(mask_c, _dot(q, kc, "nt") * SCALE, NEG_INF)
                s_p = jnp.where(mask_p, _dot(q, kp, "nt") * SCALE, NEG_INF)
                if b == 0:
                    s_p = jnp.where(tile_starts_seq, NEG_INF, s_p)
                m = jnp.maximum(jnp.max(s_c, axis=-1, keepdims=True), jnp.max(s_p, axis=-1, keepdims=True))
                p_c = jnp.exp(s_c - m)
                p_p = jnp.exp(s_p - m)
                l = jnp.sum(p_c, axis=-1, keepdims=True) + jnp.sum(p_p, axis=-1, keepdims=True)
                inv = 1.0 / l
                o = _dot((p_c * inv).astype(BF16), vc) + _dot((p_p * inv).astype(BF16), vp)
                o_ref[r0:r0 + BLK, c0:c0 + HEAD] = o.astype(BF16)
                lse_ref[r0:r0 + BLK, c0:c0 + HEAD] = jnp.broadcast_to(m + jnp.log(l), (BLK, HEAD))

    def cur(off):
        return pl.BlockSpec((nb * BLK, GROUP_W), lambda n: (n, off))

    def prev(off):
        return pl.BlockSpec((BLK, GROUP_W), lambda n: (jnp.maximum(n * nb - 1, 0), off))

    return pl.pallas_call(
        body, grid=(s // (nb * BLK),),
        in_specs=[cur(qo), cur(ko), prev(ko), cur(vo), prev(vo)],
        out_specs=[cur(0), cur(0)],
        out_shape=[jax.ShapeDtypeStruct((s, GROUP_W), BF16), jax.ShapeDtypeStruct((s, GROUP_W), F32)],
        compiler_params=_params(), name=name,
    )(q_arr, k_arr, k_arr, v_arr, v_arr)


def _attn_bwd_dq(q_arr, k_arr, v_arr, offs, do, lse, delta, seq_blocks, name):
    s = q_arr.shape[0]
    qo, ko, vo = offs
    nb = _att_blocks(seq_blocks)

    def body(q_ref, kc_ref, kp_ref, vc_ref, vp_ref, do_ref, lse_ref, dl_ref, dq_ref):
        n = pl.program_id(0)
        tile_starts_seq = (n * nb) % seq_blocks == 0
        mask_c, mask_p = _band_masks()
        for b in range(nb):
            r0 = b * BLK
            for h in range(HPG):
                c0 = h * HEAD
                q = q_ref[r0:r0 + BLK, c0:c0 + HEAD]
                kc = kc_ref[r0:r0 + BLK, c0:c0 + HEAD]
                vc = vc_ref[r0:r0 + BLK, c0:c0 + HEAD]
                if b == 0:
                    kp = kp_ref[:, c0:c0 + HEAD]
                    vp = vp_ref[:, c0:c0 + HEAD]
                else:
                    kp = kc_ref[r0 - BLK:r0, c0:c0 + HEAD]
                    vp = vc_ref[r0 - BLK:r0, c0:c0 + HEAD]
                dob = do_ref[r0:r0 + BLK, c0:c0 + HEAD]
                lse_b = lse_ref[r0:r0 + BLK, c0:c0 + HEAD]
                dl_b = dl_ref[r0:r0 + BLK, c0:c0 + HEAD]
                s_c = jnp.where(mask_c, _dot(q, kc, "nt") * SCALE, NEG_INF)
                s_p = jnp.where(mask_p, _dot(q, kp, "nt") * SCALE, NEG_INF)
                if b == 0:
                    s_p = jnp.where(tile_starts_seq, NEG_INF, s_p)
                p_c = jnp.exp(s_c - lse_b)
                p_p = jnp.exp(s_p - lse_b)
                ds_c = p_c * (_dot(dob, vc, "nt") - dl_b) * SCALE
                ds_p = p_p * (_dot(dob, vp, "nt") - dl_b) * SCALE
                dq = _dot(ds_c.astype(BF16), kc) + _dot(ds_p.astype(BF16), kp)
                dq_ref[r0:r0 + BLK, c0:c0 + HEAD] = dq.astype(BF16)

    def cur(off):
        return pl.BlockSpec((nb * BLK, GROUP_W), lambda n: (n, off))

    def prev(off):
        return pl.BlockSpec((BLK, GROUP_W), lambda n: (jnp.maximum(n * nb - 1, 0), off))

    return pl.pallas_call(
        body, grid=(s // (nb * BLK),),
        in_specs=[cur(qo), cur(ko), prev(ko), cur(vo), prev(vo), cur(0), cur(0), cur(0)],
        out_specs=cur(0),
        out_shape=jax.ShapeDtypeStruct((s, GROUP_W), BF16),
        compiler_params=_params(), name=name,
    )(q_arr, k_arr, k_arr, v_arr, v_arr, do, lse, delta)


def _attn_bwd_dkv(q_arr, k_arr, v_arr, offs, do, lse, delta, seq_blocks, name):
    s = q_arr.shape[0]
    qo, ko, vo = offs
    nb = _att_blocks(seq_blocks)
    n_blocks = s // BLK

    def body(k_ref, v_ref, qc_ref, qn_ref, doc_ref, don_ref, lsec_ref, lsen_ref, dlc_ref, dln_ref,
             dk_ref, dv_ref):
        n = pl.program_id(0)
        next_in_seq = ((n + 1) * nb) % seq_blocks != 0
        mask_c, mask_p = _band_masks()
        for b in range(nb):
            r0 = b * BLK
            for h in range(HPG):
                c0 = h * HEAD
                k = k_ref[r0:r0 + BLK, c0:c0 + HEAD]
                v = v_ref[r0:r0 + BLK, c0:c0 + HEAD]
                q = qc_ref[r0:r0 + BLK, c0:c0 + HEAD]
                dob = doc_ref[r0:r0 + BLK, c0:c0 + HEAD]
                s_c = jnp.where(mask_c, _dot(q, k, "nt") * SCALE, NEG_INF)
                p_c = jnp.exp(s_c - lsec_ref[r0:r0 + BLK, c0:c0 + HEAD])
                ds_c = p_c * (_dot(dob, v, "nt") - dlc_ref[r0:r0 + BLK, c0:c0 + HEAD]) * SCALE
                dv = _dot(p_c.astype(BF16), dob, "tn")
                dk = _dot(ds_c.astype(BF16), q, "tn")
                if b < nb - 1:
                    r1 = r0 + BLK
                    q2 = qc_ref[r1:r1 + BLK, c0:c0 + HEAD]
                    do2 = doc_ref[r1:r1 + BLK, c0:c0 + HEAD]
                    lse2 = lsec_ref[r1:r1 + BLK, c0:c0 + HEAD]
                    dl2 = dlc_ref[r1:r1 + BLK, c0:c0 + HEAD]
                else:
                    q2 = qn_ref[:, c0:c0 + HEAD]
                    do2 = don_ref[:, c0:c0 + HEAD]
                    lse2 = lsen_ref[:, c0:c0 + HEAD]
                    dl2 = dln_ref[:, c0:c0 + HEAD]
                s_p = jnp.where(mask_p, _dot(q2, k, "nt") * SCALE, NEG_INF)
                if b == nb - 1:
                    s_p = jnp.where(next_in_seq, s_p, NEG_INF)
                p_p = jnp.exp(s_p - lse2)
                ds_p = p_p * (_dot(do2, v, "nt") - dl2) * SCALE
                dv = dv + _dot(p_p.astype(BF16), do2, "tn")
                dk = dk + _dot(ds_p.astype(BF16), q2, "tn")
                dk_ref[r0:r0 + BLK, c0:c0 + HEAD] = dk.astype(BF16)
                dv_ref[r0:r0 + BLK, c0:c0 + HEAD] = dv.astype(BF16)

    def cur(off):
        return pl.BlockSpec((nb * BLK, GROUP_W), lambda n: (n, off))

    def nxt(off):
        return pl.BlockSpec((BLK, GROUP_W), lambda n: (jnp.minimum((n + 1) * nb, n_blocks - 1), off))

    return pl.pallas_call(
        body, grid=(s // (nb * BLK),),
        in_specs=[cur(ko), cur(vo), cur(qo), nxt(qo), cur(0), nxt(0), cur(0), nxt(0), cur(0), nxt(0)],
        out_specs=[cur(0), cur(0)],
        out_shape=[jax.ShapeDtypeStruct((s, GROUP_W), BF16), jax.ShapeDtypeStruct((s, GROUP_W), BF16)],
        compiler_params=_params(), name=name,
    )(k_arr, v_arr, q_arr, q_arr, do, do, lse, lse, delta, delta)


def _merge_weights(lse_refs, c0):
    ls = [r[:, c0:c0 + HEAD] for r in lse_refs]
    m = jnp.maximum(jnp.maximum(ls[0], ls[1]), ls[2])
    es = [jnp.exp(l - m) for l in ls]
    inv = 1.0 / (es[0] + es[1] + es[2])
    return [e * inv for e in es]


def _merge_fwd(os_, lses):
    s = os_[0].shape[0]
    tm = _row_tile(s, 512)

    def body(o0, o1, o2, l0, l1, l2, out_ref):
        for h in range(HPG):
            c0 = h * HEAD
            w = _merge_weights((l0, l1, l2), c0)
            acc = None
            for wg, o in zip(w, (o0, o1, o2)):
                t = wg * o[:, c0:c0 + HEAD].astype(F32)
                acc = t if acc is None else acc + t
            out_ref[:, c0:c0 + HEAD] = acc.astype(BF16)

    spec = pl.BlockSpec((tm, GROUP_W), lambda i: (i, 0))
    return pl.pallas_call(
        body, grid=(s // tm,), in_specs=[spec] * 6, out_specs=spec,
        out_shape=jax.ShapeDtypeStruct((s, GROUP_W), BF16),
        compiler_params=_params(), name="merge_fwd",
    )(*os_, *lses)


def _merge_bwd(dcat, os_, lses):
    s = os_[0].shape[0]
    tm = _row_tile(s, 512)

    def body(d_ref, o0, o1, o2, l0, l1, l2, do0, do1, do2, dl0, dl1, dl2):
        for h in range(HPG):
            c0 = h * HEAD
            w = _merge_weights((l0, l1, l2), c0)
            dm = d_ref[:, c0:c0 + HEAD].astype(F32)
            merged = None
            for wg, o in zip(w, (o0, o1, o2)):
                t = wg * o[:, c0:c0 + HEAD].astype(F32)
                merged = t if merged is None else merged + t
            abar = jnp.sum(dm * merged, axis=-1, keepdims=True)
            for wg, do_ref, dl_ref in zip(w, (do0, do1, do2), (dl0, dl1, dl2)):
                do_ref[:, c0:c0 + HEAD] = (wg * dm).astype(BF16)
                dl_ref[:, c0:c0 + HEAD] = wg * abar

    spec = pl.BlockSpec((tm, GROUP_W), lambda i: (i, 0))
    return pl.pallas_call(
        body, grid=(s // tm,), in_specs=[spec] * 7, out_specs=[spec] * 6,
        out_shape=[jax.ShapeDtypeStruct((s, GROUP_W), BF16)] * 3 + [jax.ShapeDtypeStruct((s, GROUP_W), F32)] * 3,
        compiler_params=_params(), name="merge_bwd",
    )(dcat, *os_, *lses)


def _assemble_dproj(dqkv, dqm, tabs):
    s = dqm.shape[0]
    tm = _row_tile(s, 256)
    width = 3 * N_GROUPS * GROUP_W + GROUP_W

    def body(d0, d1, d2, dm_ref, c_ref, s1_ref, s2_ref, out_ref):
        c, s1, s2 = c_ref[...], s1_ref[...], s2_ref[...]
        for g, d_ref in enumerate((d0, d1, d2)):
            for part in range(3):
                for h in range(HPG):
                    src = part * GROUP_W + h * HEAD
                    dst = part * N_GROUPS * GROUP_W + g * GROUP_W + h * HEAD
                    seg = d_ref[:, src:src + HEAD]
                    if part < 2:
                        t = seg.astype(F32)
                        t = t * c - pltpu.roll(t, HEAD - ROT_HALF, 1) * s2 - pltpu.roll(t, ROT_HALF, 1) * s1
                        seg = t.astype(BF16)
                    out_ref[:, dst:dst + HEAD] = seg
        out_ref[:, 3 * N_GROUPS * GROUP_W:] = dm_ref[...]

    g_spec = pl.BlockSpec((tm, 3 * GROUP_W), lambda i: (i, 0))
    m_spec = pl.BlockSpec((tm, GROUP_W), lambda i: (i, 0))
    t_spec = pl.BlockSpec((tm, HEAD), lambda i: (i, 0))
    return pl.pallas_call(
        body, grid=(s // tm,), in_specs=[g_spec] * 3 + [m_spec] + [t_spec] * 3,
        out_specs=pl.BlockSpec((tm, width), lambda i: (i, 0)),
        out_shape=jax.ShapeDtypeStruct((s, width), BF16),
        compiler_params=_params(), name="assemble_dproj",
    )(*dqkv, dqm, *tabs)


def _mem_softmax(q, k):
    s = _dot(q, k, "nt") * SCALE
    m = jnp.max(s, axis=-1, keepdims=True)
    p = jnp.exp(s - m)
    return p * (1.0 / jnp.sum(p, axis=-1, keepdims=True))


def _memattn_fwd(q_arr, q_off, kv, name):
    s = q_arr.shape[0]
    mlen = kv.shape[0]
    tq = _row_tile(s, 512)

    def body(q_ref, kv_ref, o_ref):
        for h in range(HPG):
            c0 = h * HEAD
            p = _mem_softmax(q_ref[:, c0:c0 + HEAD], kv_ref[:, c0:c0 + HEAD])
            o = _dot(p.astype(BF16), kv_ref[:, GROUP_W + c0:GROUP_W + c0 + HEAD])
            o_ref[:, c0:c0 + HEAD] = o.astype(BF16)

    return pl.pallas_call(
        body, grid=(s // tq,),
        in_specs=[pl.BlockSpec((tq, GROUP_W), lambda i: (i, q_off)),
                  pl.BlockSpec((mlen, 2 * GROUP_W), lambda i: (0, 0))],
        out_specs=pl.BlockSpec((tq, GROUP_W), lambda i: (i, 0)),
        out_shape=jax.ShapeDtypeStruct((s, GROUP_W), BF16),
        compiler_params=_params(), name=name,
    )(q_arr, kv)


def _memattn_bwd(q_arr, q_off, kv, dcat, d_off, name):
    s = q_arr.shape[0]
    mlen = kv.shape[0]
    tq = _row_tile(s, 512)

    def body(q_ref, kv_ref, d_ref, dq_ref, dkv_ref):
        i = pl.program_id(0)

        @pl.when(i == 0)
        def _():
            dkv_ref[...] = jnp.zeros_like(dkv_ref)

        for h in range(HPG):
            c0 = h * HEAD
            q = q_ref[:, c0:c0 + HEAD]
            k = kv_ref[:, c0:c0 + HEAD]
            v = kv_ref[:, GROUP_W + c0:GROUP_W + c0 + HEAD]
            do = d_ref[:, c0:c0 + HEAD]
            p = _mem_softmax(q, k)
            dp = _dot(do, v, "nt")
            ds = p * (dp - jnp.sum(p * dp, axis=-1, keepdims=True)) * SCALE
            dsb = ds.astype(BF16)
            dq_ref[:, c0:c0 + HEAD] = _dot(dsb, k).astype(BF16)
            dkv_ref[:, c0:c0 + HEAD] += _dot(dsb, q, "tn")
            dkv_ref[:, GROUP_W + c0:GROUP_W + c0 + HEAD] += _dot(p.astype(BF16), do, "tn")

    return pl.pallas_call(
        body, grid=(s // tq,),
        in_specs=[pl.BlockSpec((tq, GROUP_W), lambda i: (i, q_off)),
                  pl.BlockSpec((mlen, 2 * GROUP_W), lambda i: (0, 0)),
                  pl.BlockSpec((tq, GROUP_W), lambda i: (i, d_off))],
        out_specs=[pl.BlockSpec((tq, GROUP_W), lambda i: (i, 0)),
                   pl.BlockSpec((mlen, 2 * GROUP_W), lambda i: (0, 0))],
        out_shape=[jax.ShapeDtypeStruct((s, GROUP_W), BF16), jax.ShapeDtypeStruct((mlen, 2 * GROUP_W), F32)],
        compiler_params=_params(), name=name,
    )(q_arr, kv, dcat)


SGU_TILE = 256


def _sgu_norm(v):
    vg = _gelu(v)
    mu = jnp.mean(vg, axis=-1, keepdims=True)
    xc = vg - mu
    var = jnp.mean(xc * xc, axis=-1, keepdims=True)
    rstd = lax.rsqrt(var + LN_EPS)
    return xc * rstd, rstd


def _tril_mask():
    r = lax.broadcasted_iota(jnp.int32, (BLK, BLK), 0)
    c = lax.broadcasted_iota(jnp.int32, (BLK, BLK), 1)
    return r >= c


def _sgu_fwd(proj, ln_g, ln_b, w_s, b_st):
    s = proj.shape[0]
    ts = _row_tile(s, SGU_TILE)

    def body(u_ref, v_ref, g_ref, b_ref, ws_ref, bst_ref, o_ref):
        ug = _gelu(u_ref[...].astype(F32))
        xhat, _ = _sgu_norm(v_ref[...].astype(F32))
        vn = (xhat * g_ref[...] + b_ref[...]).astype(BF16)
        tri = _tril_mask()
        for g in range(SGU_GROUPS):
            c0 = g * HEAD
            w = jnp.where(tri, ws_ref[g], 0.0).astype(BF16)
            bias = bst_ref[:, g:g + 1]
            for ch in range(ts // BLK):
                r0 = ch * BLK
                mixed = _dot(w, vn[r0:r0 + BLK, c0:c0 + HEAD]) + bias
                o_ref[r0:r0 + BLK, c0:c0 + HEAD] = (ug[r0:r0 + BLK, c0:c0 + HEAD] * mixed).astype(BF16)

    vec = pl.BlockSpec((1, SGU_W), lambda i: (0, 0))
    return pl.pallas_call(
        body, grid=(s // ts,),
        in_specs=[pl.BlockSpec((ts, SGU_W), lambda i: (i, 0)), pl.BlockSpec((ts, SGU_W), lambda i: (i, 1)),
                  vec, vec, pl.BlockSpec((SGU_GROUPS, BLK, BLK), lambda i: (0, 0, 0)),
                  pl.BlockSpec((BLK, SGU_GROUPS), lambda i: (0, 0))],
        out_specs=pl.BlockSpec((ts, SGU_W), lambda i: (i, 0)),
        out_shape=jax.ShapeDtypeStruct((s, SGU_W), BF16),
        compiler_params=_params(), name="sgu_fwd",
    )(proj, proj, ln_g, ln_b, w_s, b_st)


def _sgu_bwd(proj, dcat, ln_g, ln_b, w_s, b_st):
    s = proj.shape[0]
    ts = _row_tile(s, SGU_TILE)

    def body(u_ref, v_ref, d_ref, g_ref, b_ref, ws_ref, bst_ref,
             duv_ref, dws_ref, dbs_ref, dg_ref, db_ref, dvn_ref):
        i = pl.program_id(0)

        @pl.when(i == 0)
        def _():
            dws_ref[...] = jnp.zeros_like(dws_ref)
            dbs_ref[...] = jnp.zeros_like(dbs_ref)
            dg_ref[...] = jnp.zeros_like(dg_ref)
            db_ref[...] = jnp.zeros_like(db_ref)

        u = u_ref[...].astype(F32)
        v = v_ref[...].astype(F32)
        ug = _gelu(u)
        xhat, rstd = _sgu_norm(v)
        lng = g_ref[...]
        vn = (xhat * lng + b_ref[...]).astype(BF16)
        dout = d_ref[...].astype(F32)
        tri = _tril_mask()
        lane = lax.broadcasted_iota(jnp.int32, (BLK, BLK), 1)
        dbs = jnp.zeros((BLK, BLK), F32)
        for g in range(SGU_GROUPS):
            c0 = g * HEAD
            w = jnp.where(tri, ws_ref[g], 0.0).astype(BF16)
            bias = bst_ref[:, g:g + 1]
            dws = jnp.zeros((BLK, BLK), F32)
            for ch in range(ts // BLK):
                r0 = ch * BLK
                vn_gc = vn[r0:r0 + BLK, c0:c0 + HEAD]
                mixed = _dot(w, vn_gc) + bias
                do_gc = dout[r0:r0 + BLK, c0:c0 + HEAD]
                dmixed = do_gc * ug[r0:r0 + BLK, c0:c0 + HEAD]
                du = do_gc * mixed * _gelu_grad(u[r0:r0 + BLK, c0:c0 + HEAD])
                duv_ref[r0:r0 + BLK, c0:c0 + HEAD] = du.astype(BF16)
                dmb = dmixed.astype(BF16)
                dws = dws + _dot(dmb, vn_gc, "nt")
                dbs = dbs + jnp.where(lane == g, jnp.sum(dmixed, axis=-1, keepdims=True), 0.0)
                dvn_ref[r0:r0 + BLK, c0:c0 + HEAD] = _dot(w, dmb, "tn")
            dws_ref[g] += jnp.where(tri, dws, 0.0)
        dbs_ref[...] += dbs
        dvn = dvn_ref[...]
        gd = dvn * lng
        c1 = jnp.mean(gd, axis=-1, keepdims=True)
        c2 = jnp.mean(gd * xhat, axis=-1, keepdims=True)
        dvg = rstd * (gd - c1 - xhat * c2)
        duv_ref[:, SGU_W:] = (dvg * _gelu_grad(v)).astype(BF16)
        dg_ref[...] += jnp.sum(dvn * xhat, axis=0, keepdims=True)
        db_ref[...] += jnp.sum(dvn, axis=0, keepdims=True)

    vec = pl.BlockSpec((1, SGU_W), lambda i: (0, 0))
    ws_spec = pl.BlockSpec((SGU_GROUPS, BLK, BLK), lambda i: (0, 0, 0))
    return pl.pallas_call(
        body, grid=(s // ts,),
        in_specs=[pl.BlockSpec((ts, SGU_W), lambda i: (i, 0)), pl.BlockSpec((ts, SGU_W), lambda i: (i, 1)),
                  pl.BlockSpec((ts, SGU_W), lambda i: (i, 0)),
                  vec, vec, ws_spec, pl.BlockSpec((BLK, SGU_GROUPS), lambda i: (0, 0))],
        out_specs=[pl.BlockSpec((ts, 2 * SGU_W), lambda i: (i, 0)), ws_spec,
                   pl.BlockSpec((BLK, BLK), lambda i: (0, 0)), vec, vec],
        out_shape=[jax.ShapeDtypeStruct((s, 2 * SGU_W), BF16),
                   jax.ShapeDtypeStruct((SGU_GROUPS, BLK, BLK), F32),
                   jax.ShapeDtypeStruct((BLK, BLK), F32),
                   jax.ShapeDtypeStruct((1, SGU_W), F32), jax.ShapeDtypeStruct((1, SGU_W), F32)],
        scratch_shapes=[pltpu.VMEM((ts, SGU_W), F32)],
        compiler_params=_params(), name="sgu_bwd",
    )(proj, proj, dcat, ln_g, ln_b, w_s, b_st)


def _swiglu_fwd(h, wg, wu, layer, name):
    s, d = h.shape
    nb, _, _, fb = wg.shape
    tm = _row_tile(s, 512)

    def epilogue(parts, ex, out):
        g, u = parts
        out[0][...] = g.astype(BF16)
        out[1][...] = u.astype(BF16)
        out[2][...] = (g * _sigmoid(g) * u).astype(BF16)

    a_spec = pl.BlockSpec((tm, d), lambda j, m, k: (m, 0))
    w_spec = pl.BlockSpec((None, None, d, fb), lambda j, m, k: (j, layer, 0, 0))
    o_spec = pl.BlockSpec((None, tm, fb), lambda j, m, k: (j, m, 0))
    o_shape = jax.ShapeDtypeStruct((nb, s, fb), BF16)
    return _matmul(name, (nb, s // tm, 1),
                   [(h, a_spec, wg, w_spec, "nn", 0), (h, a_spec, wu, w_spec, "nn", 1)],
                   2, None, [], [(o_shape, o_spec)] * 3, epilogue)


def _swiglu_down(hid, wd, layer, res, name):
    nb, s, fb = hid.shape
    d = wd.shape[-1]
    tm = _row_tile(s, 512)
    return _matmul(
        name, (s // tm, nb),
        [(hid, pl.BlockSpec((None, tm, fb), lambda m, j: (j, m, 0)),
          wd, pl.BlockSpec((None, None, fb, d), lambda m, j: (j, layer, 0, 0)), "nn", 0)],
        1, (tm, d), [(res, pl.BlockSpec((tm, d), lambda m, j: (m, 0)))],
        [(jax.ShapeDtypeStruct((s, d), F32), pl.BlockSpec((tm, d), lambda m, j: (m, 0)))],
        _residual_epilogue)[0]


def _swiglu_bwd_hidden(dxb, wd, layer, gate, up, name):
    nb, s, fb = gate.shape
    d = dxb.shape[1]
    tm = _row_tile(s, 512)

    def epilogue(parts, ex, out):
        dh = parts[0]
        g = ex[0][...].astype(F32)
        u = ex[1][...].astype(F32)
        sg = _sigmoid(g)
        silu = g * sg
        out[0][...] = (dh * u * (sg + silu * (1.0 - sg))).astype(BF16)
        out[1][...] = (dh * silu).astype(BF16)

    blk = pl.BlockSpec((None, tm, fb), lambda j, m, k: (j, m, 0))
    o_shape = jax.ShapeDtypeStruct((nb, s, fb), BF16)
    return _matmul(
        name, (nb, s // tm, 1),
        [(dxb, pl.BlockSpec((tm, d), lambda j, m, k: (m, 0)),
          wd, pl.BlockSpec((None, None, fb, d), lambda j, m, k: (j, layer, 0, 0)), "nt", 0)],
        1, None, [(gate, blk), (up, blk)], [(o_shape, blk)] * 2, epilogue)


def _swiglu_bwd_input(dgate, dup, wg, wu, layer, name):
    nb, s, fb = dgate.shape
    d = wg.shape[2]
    tm = _row_tile(s, 512)
    a_spec = pl.BlockSpec((None, tm, fb), lambda m, j: (j, m, 0))
    w_spec = pl.BlockSpec((None, None, d, fb), lambda m, j: (j, layer, 0, 0))
    return _matmul(
        name, (s // tm, nb),
        [(dgate, a_spec, wg, w_spec, "nt", 0), (dup, a_spec, wu, w_spec, "nt", 0)],
        1, (tm, d), [],
        [(jax.ShapeDtypeStruct((s, d), F32), pl.BlockSpec((tm, d), lambda m, j: (m, 0)))],
        _store_epilogue)[0]


def _swiglu_bwd_w_in(h, dgate, dup, name):
    s, d = h.shape
    nb, _, fb = dgate.shape
    tk = _row_tile(s, 512)

    def epilogue(parts, ex, out):
        out[0][...] = parts[0].astype(BF16)
        out[1][...] = parts[1].astype(BF16)

    a_spec = pl.BlockSpec((tk, d), lambda j, k: (k, 0))
    b_spec = pl.BlockSpec((None, tk, fb), lambda j, k: (j, k, 0))
    o_spec = pl.BlockSpec((None, d, fb), lambda j, k: (j, 0, 0))
    o_shape = jax.ShapeDtypeStruct((nb, d, fb), BF16)
    return _matmul(name, (nb, s // tk),
                   [(h, a_spec, dgate, b_spec, "tn", 0), (h, a_spec, dup, b_spec, "tn", 1)],
                   2, (d, fb), [], [(o_shape, o_spec)] * 2, epilogue)


def _swiglu_bwd_w_down(hid, dxb, name):
    nb, s, fb = hid.shape
    d = dxb.shape[1]
    tk = _row_tile(s, 512)
    return _matmul(
        name, (nb, s // tk),
        [(hid, pl.BlockSpec((None, tk, fb), lambda j, k: (j, k, 0)),
          dxb, pl.BlockSpec((tk, d), lambda j, k: (k, 0)), "tn", 0)],
        1, (fb, d), [],
        [(jax.ShapeDtypeStruct((nb, fb, d), BF16), pl.BlockSpec((None, fb, d), lambda j, k: (j, 0, 0)))],
        _store_epilogue)[0]


def _mm_nn(a, b, name, tn, out_dtype=BF16, res=None):
    m, k = a.shape
    n = b.shape[1]
    tm = _row_tile(m, 512)
    extras = [] if res is None else [(res, pl.BlockSpec((tm, tn), lambda j, i, kk: (i, j)))]
    return _matmul(
        name, (n // tn, m // tm, 1),
        [(a, pl.BlockSpec((tm, k), lambda j, i, kk: (i, 0)),
          b, pl.BlockSpec((k, tn), lambda j, i, kk: (0, j)), "nn", 0)],
        1, None, extras,
        [(jax.ShapeDtypeStruct((m, n), out_dtype), pl.BlockSpec((tm, tn), lambda j, i, kk: (i, j)))],
        _store_epilogue if res is None else _residual_epilogue)[0]


def _mm_nn_colblocked(a, b_blk, name, res):
    m, k = a.shape
    nb, _, bw = b_blk.shape
    tm = _row_tile(m, 512)
    o_spec = pl.BlockSpec((tm, bw), lambda j, i, kk: (i, j))
    return _matmul(
        name, (nb, m // tm, 1),
        [(a, pl.BlockSpec((tm, k), lambda j, i, kk: (i, 0)),
          b_blk, pl.BlockSpec((None, k, bw), lambda j, i, kk: (j, 0, 0)), "nn", 0)],
        1, None, [(res, o_spec)],
        [(jax.ShapeDtypeStruct((m, nb * bw), F32), o_spec)], _residual_epilogue)[0]


def _mm_nt_colblocked(a, b_blk, name, out_dtype):
    m = a.shape[0]
    nb, n, bw = b_blk.shape
    tm = _row_tile(m, 512)
    return _matmul(
        name, (m // tm, nb),
        [(a, pl.BlockSpec((tm, bw), lambda i, j: (i, j)),
          b_blk, pl.BlockSpec((None, n, bw), lambda i, j: (j, 0, 0)), "nt", 0)],
        1, (tm, n), [],
        [(jax.ShapeDtypeStruct((m, n), out_dtype), pl.BlockSpec((tm, n), lambda i, j: (i, 0)))],
        _store_epilogue)[0]


def _mm_nt(a, b, name, tk, out_dtype):
    m, k = a.shape
    n = b.shape[0]
    tm = _row_tile(m, 512)
    return _matmul(
        name, (m // tm, k // tk),
        [(a, pl.BlockSpec((tm, tk), lambda i, kk: (i, kk)),
          b, pl.BlockSpec((n, tk), lambda i, kk: (0, kk)), "nt", 0)],
        1, (tm, n), [],
        [(jax.ShapeDtypeStruct((m, n), out_dtype), pl.BlockSpec((tm, n), lambda i, kk: (i, 0)))],
        _store_epilogue)[0]


def _mm_nt_rowblocked(a, b, name, tn, out_dtype):
    m, k = a.shape
    n = b.shape[0]
    tm = _row_tile(m, 512)
    return _matmul(
        name, (n // tn, m // tm, 1),
        [(a, pl.BlockSpec((tm, k), lambda j, i, kk: (i, 0)),
          b, pl.BlockSpec((tn, k), lambda j, i, kk: (j, 0)), "nt", 0)],
        1, None, [],
        [(jax.ShapeDtypeStruct((m, n), out_dtype), pl.BlockSpec((tm, tn), lambda j, i, kk: (i, j)))],
        _store_epilogue)[0]


def _mm_tn_colblocked(a, b, name, bw):
    s, m = a.shape
    nb = b.shape[1] // bw
    tk = _row_tile(s, 512)
    return _matmul(
        name, (nb, s // tk),
        [(a, pl.BlockSpec((tk, m), lambda j, k: (k, 0)),
          b, pl.BlockSpec((tk, bw), lambda j, k: (k, j)), "tn", 0)],
        1, (m, bw), [],
        [(jax.ShapeDtypeStruct((nb, m, bw), BF16), pl.BlockSpec((None, m, bw), lambda j, k: (j, 0, 0)))],
        _store_epilogue)[0]


def _mm_tn_rowblocked(a, b, name, bh):
    s, n = b.shape
    nb = a.shape[1] // bh
    tk = _row_tile(s, 512)
    return _matmul(
        name, (nb, s // tk),
        [(a, pl.BlockSpec((tk, bh), lambda j, k: (k, j)),
          b, pl.BlockSpec((tk, n), lambda j, k: (k, 0)), "tn", 0)],
        1, (bh, n), [],
        [(jax.ShapeDtypeStruct((nb, bh, n), BF16), pl.BlockSpec((None, bh, n), lambda j, k: (j, 0, 0)))],
        _store_epilogue)[0]


def _as2d(a):
    return a.reshape(-1, a.shape[-1])


def _cast_bf16(w, name):
    w2 = _as2d(w)
    rows, cols = w2.shape
    tr = _row_tile(rows, 256)

    def body(w_ref, o_ref):
        o_ref[...] = w_ref[...].astype(BF16)

    spec = pl.BlockSpec((tr, cols), lambda i: (i, 0))
    out = pl.pallas_call(
        body, grid=(rows // tr,), in_specs=[spec], out_specs=spec,
        out_shape=jax.ShapeDtypeStruct((rows, cols), BF16),
        compiler_params=_params(), name=name,
    )(w2)
    return out.reshape(w.shape)


def _reduce_adam(recvs, w, m, v, name):
    n_layers, rows, cols = w.shape
    n_slots = recvs[0].shape[0]
    tr = _row_tile(rows, max(16, (128 * 1024 // cols) // 16 * 16))
    nt = rows // tr
    c1 = 1.0 - ADAM_B1 ** ADAM_STEP
    c2 = 1.0 - ADAM_B2 ** ADAM_STEP

    def body(*refs):
        r_refs = refs[:n_layers]
        w_ref, m_ref, v_ref, g_out, d_out, m_out, v_out = refs[n_layers:]
        layer = pl.program_id(0)

        def update(r_ref):
            g = r_ref[0].astype(F32)
            for k in range(1, n_slots):
                g = g + r_ref[k].astype(F32)
            mm = ADAM_B1 * m_ref[...] + (1.0 - ADAM_B1) * g
            vv = ADAM_B2 * v_ref[...] + (1.0 - ADAM_B2) * (g * g)
            m_hat = mm / c1
            v_hat = vv / c2
            g_out[...] = g
            d_out[...] = -ADAM_LR * (m_hat / (jnp.sqrt(v_hat) + ADAM_EPS) + ADAM_WD * w_ref[...])
            m_out[...] = mm
            v_out[...] = vv

        for li in range(n_layers):
            if n_layers == 1:
                update(r_refs[li])
            else:
                pl.when(layer == li)(functools.partial(update, r_refs[li]))

    def recv_spec(li):
        def imap(layer, i):
            return (0, jnp.where(layer == li, i, jnp.where(layer < li, 0, nt - 1)), 0)
        return pl.BlockSpec((n_slots, tr, cols), imap)

    spec = pl.BlockSpec((None, tr, cols), lambda layer, i: (layer, i, 0))
    o_shape = jax.ShapeDtypeStruct(w.shape, F32)
    return pl.pallas_call(
        body, grid=(n_layers, nt),
        in_specs=[recv_spec(li) for li in range(n_layers)] + [spec] * 3,
        out_specs=[spec] * 4, out_shape=[o_shape] * 4,
        compiler_params=_params(), name=name,
    )(*recvs, w, m, v)


def _my_place():
    return lax.axis_index("x"), lax.axis_index("y"), lax.axis_index("c")


def _all_gather(block, name):
    out_shape = (N_DEV,) + block.shape

    def body(x_ref, out_ref, send_sems, recv_sems, local_sem):
        x, y, c = _my_place()
        me, sibling = (x, y, c), (x, y, 1 - c)
        chips = [(1 - x, y), (x, 1 - y), (1 - x, 1 - y)]

        def slot(px, py, pc):
            return out_ref.at[4 * px + 2 * py + pc]

        def copy(k, blk, to, src=None):
            return pltpu.make_async_remote_copy(
                src_ref=slot(*blk) if src is None else src, dst_ref=slot(*blk),
                send_sem=send_sems.at[k], recv_sem=recv_sems.at[k],
                device_id=to, device_id_type=pl.DeviceIdType.MESH)

        mine = pltpu.make_async_copy(x_ref, slot(*me), local_sem)
        mine.start()
        first = [copy(0, me, sibling, src=x_ref)]
        first += [copy(1 + j, me, (*chip, c), src=x_ref) for j, chip in enumerate(chips)]
        for cp in first:
            cp.start()
        passed = [copy(4 + j, (*chip, c), sibling) for j, chip in enumerate(chips)]
        for j, chip in enumerate(chips):
            copy(1 + j, (*chip, c), me).wait_recv()
            passed[j].start()
        copy(0, sibling, me).wait_recv()
        for j, chip in enumerate(chips):
            copy(4 + j, (*chip, 1 - c), me).wait_recv()
        for cp in first + passed:
            cp.wait_send()
        mine.wait()

    return pl.pallas_call(
        body, out_shape=jax.ShapeDtypeStruct(out_shape, block.dtype),
        in_specs=[pl.BlockSpec(memory_space=pltpu.HBM)],
        out_specs=pl.BlockSpec(memory_space=pltpu.HBM),
        scratch_shapes=[pltpu.SemaphoreType.DMA((7,)), pltpu.SemaphoreType.DMA((7,)),
                        pltpu.SemaphoreType.DMA],
        name=name,
    )(block)


def _all_to_all(blocked, name):
    n = len(blocked)

    def body(*refs):
        srcs = refs[:n]
        dsts = refs[n:2 * n]
        send_sems, recv_sems, local_sems = refs[2 * n:]
        x, y, c = _my_place()
        me = 4 * x + 2 * y + c
        local = [pltpu.make_async_copy(srcs[t].at[me], dsts[t].at[me], local_sems.at[t]) for t in range(n)]
        for cp in local:
            cp.start()
        copies = []
        for k in range(1, N_DEV):
            px, py, pc = x ^ (k >> 2), y ^ ((k >> 1) & 1), c ^ (k & 1)
            peer = 4 * px + 2 * py + pc
            for t in range(n):
                sem = (k - 1) * n + t
                copies.append((
                    pltpu.make_async_remote_copy(
                        src_ref=srcs[t].at[peer], dst_ref=dsts[t].at[me],
                        send_sem=send_sems.at[sem], recv_sem=recv_sems.at[sem],
                        device_id=(px, py, pc), device_id_type=pl.DeviceIdType.MESH),
                    pltpu.make_async_remote_copy(
                        src_ref=srcs[t].at[peer], dst_ref=dsts[t].at[peer],
                        send_sem=send_sems.at[sem], recv_sem=recv_sems.at[sem],
                        device_id=(px, py, pc), device_id_type=pl.DeviceIdType.MESH)))
        for send, _ in copies:
            send.start()
        for _, arrival in copies:
            arrival.wait_recv()
        for send, _ in copies:
            send.wait_send()
        for cp in local:
            cp.wait()

    hbm = pl.BlockSpec(memory_space=pltpu.HBM)
    return pl.pallas_call(
        body, out_shape=[jax.ShapeDtypeStruct(b.shape, b.dtype) for b in blocked],
        in_specs=[hbm] * n, out_specs=[hbm] * n,
        scratch_shapes=[pltpu.SemaphoreType.DMA((7 * n,)), pltpu.SemaphoreType.DMA((7 * n,)),
                        pltpu.SemaphoreType.DMA((n,))],
        name=name,
    )(*blocked)


def _to_residue_major(a, dilation):
    s, w = a.shape
    return a.reshape(s // dilation, dilation, w).transpose(1, 0, 2).reshape(s, w)


def _from_residue_major(a, dilation):
    s, w = a.shape
    return a.reshape(dilation, s // dilation, w).transpose(1, 0, 2).reshape(s, w)


def _mem_kv(mem, gain, wkv, layer, tag):
    mem_n = _rmsnorm_fwd(mem, gain, "mem_norm_" + tag)
    mlen, d = mem.shape
    nb, _, bh, n = wkv.shape
    kv = _matmul(
        "mem_kv_" + tag, (1, nb),
        [(mem_n, pl.BlockSpec((mlen, bh), lambda i, j: (0, j)),
          wkv, pl.BlockSpec((None, None, bh, n), lambda i, j: (j, layer, 0, 0)), "nn", 0)],
        1, (mlen, n), [],
        [(jax.ShapeDtypeStruct((mlen, n), BF16), pl.BlockSpec((mlen, n), lambda i, j: (0, 0)))],
        _store_epilogue)[0]
    return mem_n, kv


def _mem_kv_bwd(mem, gain, mem_n, wkv, layer, dkv, tag):
    mlen, d = mem.shape
    nb, _, bh, n = wkv.shape
    dkvb = dkv.astype(BF16)
    dw = _mm_tn_rowblocked(mem_n, dkvb, "mem_kv_dw_" + tag, bh)
    dmem_n = _matmul(
        "mem_kv_dx_" + tag, (nb, 1),
        [(dkvb, pl.BlockSpec((mlen, n), lambda j, k: (0, 0)),
          wkv, pl.BlockSpec((None, None, bh, n), lambda j, k: (j, layer, 0, 0)), "nt", 0)],
        1, None, [],
        [(jax.ShapeDtypeStruct((mlen, d), F32), pl.BlockSpec((mlen, bh), lambda j, k: (0, j)))],
        _store_epilogue)[0]
    _, _, dgain = _rmsnorm_bwd(mem, gain, dmem_n, None, "mem_norm_bwd_" + tag)
    return dw, dgain


def _ffn_fwd(x, gain, wts, layer, tag):
    hf = _rmsnorm_fwd(x, gain, "ffn_norm_" + tag)
    gate, up, hid = _swiglu_fwd(hf, wts["w_gate"], wts["w_up"], layer, "swiglu_fwd_" + tag)
    x_out = _swiglu_down(hid, wts["w_down"], layer, x, "swiglu_down_" + tag)
    return x_out, (hf, gate, up, hid)


def _ffn_bwd(x, gain, wts, layer, saved, dx, dxb, tag):
    hf, gate, up, hid = saved
    dgate, dup = _swiglu_bwd_hidden(dxb, wts["w_down"], layer, gate, up, "swiglu_bwd_hidden_" + tag)
    dwd = _swiglu_bwd_w_down(hid, dxb, "swiglu_bwd_wdown_" + tag)
    dwg, dwu = _swiglu_bwd_w_in(hf, dgate, dup, "swiglu_bwd_win_" + tag)
    dhf = _swiglu_bwd_input(dgate, dup, wts["w_gate"], wts["w_up"], layer, "swiglu_bwd_input_" + tag)
    dx_new, dxb_new, dgain = _rmsnorm_bwd(x, gain, dhf, dx, "ffn_norm_bwd_" + tag)
    return dx_new, dxb_new, dgain, dwg, dwu, dwd


def _local_step(x, mem, positions, target, wts, small):
    s, d = x.shape
    tabs = _rotary_tables(positions)
    mix_norm, mem_norm, ffn_norm = small["mix_norm"], small["mem_norm"], small["ffn_norm"]

    h0 = _rmsnorm_fwd(x, mix_norm[0:1], "mix_norm_0")
    proj0 = _attn_in_proj(h0, wts["attn_w_in"], tabs)
    qkv, offs, outs, lses = [], [], [], []
    for g, dil in enumerate(DILATIONS):
        if dil == 1:
            arr, off = proj0, (g, N_GROUPS + g, 2 * N_GROUPS + g)
        else:
            cols = [proj0[:, (p * N_GROUPS + g) * GROUP_W:(p * N_GROUPS + g + 1) * GROUP_W] for p in range(3)]
            arr, off = _to_residue_major(jnp.concatenate(cols, axis=1), dil), (0, 1, 2)
        o, lse = _attn_fwd(arr, arr, arr, off, s // dil // BLK, "attn_fwd_%d" % g)
        qkv.append(arr)
        offs.append(off)
        if dil > 1:
            o, lse = _from_residue_major(o, dil), _from_residue_major(lse, dil)
        outs.append(o)
        lses.append(lse)
    mix0 = _merge_fwd(outs, lses)
    qm_off0 = 3 * N_GROUPS
    mem_n0, kv0 = _mem_kv(mem, mem_norm[0:1], wts["w_mem_kv"], 0, "0")
    memo0 = _memattn_fwd(proj0, qm_off0, kv0, "memattn_fwd_0")
    cat0 = jnp.concatenate([mix0, memo0], axis=1)
    x1 = _mm_nn_colblocked(cat0, wts["attn_w_out"], "attn_out_proj", x)
    x2, ffn_saved0 = _ffn_fwd(x1, ffn_norm[0:1], wts, 0, "0")

    h1 = _rmsnorm_fwd(x2, mix_norm[1:2], "mix_norm_1")
    w_sgu_in = wts["sgu_w_in"]
    proj1 = _mm_nn(h1, w_sgu_in, "sgu_in_proj", w_sgu_in.shape[1] // 2)
    b_st = small["sgu_b_spatial"].T
    mix1 = _sgu_fwd(proj1, small["sgu_ln_g"], small["sgu_ln_b"], small["sgu_w_spatial"], b_st)
    qm_off1 = 2 * SGU_W // GROUP_W
    mem_n1, kv1 = _mem_kv(mem, mem_norm[1:2], wts["w_mem_kv"], 1, "1")
    memo1 = _memattn_fwd(proj1, qm_off1, kv1, "memattn_fwd_1")
    cat1 = jnp.concatenate([mix1, memo1], axis=1)
    w_sgu_out = wts["sgu_w_out"]
    x3 = _mm_nn(cat1, w_sgu_out, "sgu_out_proj", d // 2, out_dtype=F32, res=x2)
    x4, ffn_saved1 = _ffn_fwd(x3, ffn_norm[1:2], wts, 1, "1")

    loss, dx, dxb, d_final = _loss_head(x4, small["final_norm"], target)

    grads, sgrads = {}, {}
    dx, dxb, d_ffn1, dwg1, dwu1, dwd1 = _ffn_bwd(x3, ffn_norm[1:2], wts, 1, ffn_saved1, dx, dxb, "1")
    dcat1 = _mm_nt_rowblocked(dxb, w_sgu_out, "sgu_out_proj_dx", w_sgu_out.shape[0] // 2, BF16)
    grads["sgu_w_out"] = _mm_tn_rowblocked(cat1, dxb, "sgu_out_proj_dw", w_sgu_out.shape[0] // N_DEV)
    duv, dws, dbs, dlng, dlnb = _sgu_bwd(proj1, dcat1, small["sgu_ln_g"], small["sgu_ln_b"],
                                         small["sgu_w_spatial"], b_st)
    dqm1, dkv1 = _memattn_bwd(proj1, qm_off1, kv1, dcat1, SGU_W // GROUP_W, "memattn_bwd_1")
    dwkv1, d_memnorm1 = _mem_kv_bwd(mem, mem_norm[1:2], mem_n1, wts["w_mem_kv"], 1, dkv1, "1")
    dproj1 = jnp.concatenate([duv, dqm1], axis=1)
    dwsin = _mm_tn_colblocked(h1, dproj1, "sgu_in_proj_dw", dproj1.shape[1] // 7)
    dh1 = _mm_nt(dproj1, w_sgu_in, "sgu_in_proj_dx", dproj1.shape[1] // 7, F32)
    dx, dxb, d_mix1 = _rmsnorm_bwd(x2, mix_norm[1:2], dh1, dx, "mix_norm_bwd_1")

    dx, dxb, d_ffn0, dwg0, dwu0, dwd0 = _ffn_bwd(x1, ffn_norm[0:1], wts, 0, ffn_saved0, dx, dxb, "0")
    w_attn_out = wts["attn_w_out"]
    dcat0 = _mm_nt_colblocked(dxb, w_attn_out, "attn_out_proj_dx", BF16)
    grads["attn_w_out"] = _mm_tn_colblocked(cat0, dxb, "attn_out_proj_dw", w_attn_out.shape[2])
    dos_and_deltas = _merge_bwd(dcat0, outs, lses)
    dqkv = []
    for g, dil in enumerate(DILATIONS):
        do_g, dl_g = dos_and_deltas[g], dos_and_deltas[N_GROUPS + g]
        lse_g = lses[g]
        if dil > 1:
            do_g, dl_g, lse_g = (_to_residue_major(t, dil) for t in (do_g, dl_g, lse_g))
        args = (qkv[g], qkv[g], qkv[g], offs[g], do_g, lse_g, dl_g, s // dil // BLK)
        dq = _attn_bwd_dq(*args, "attn_bwd_dq_%d" % g)
        dk, dv = _attn_bwd_dkv(*args, "attn_bwd_dkv_%d" % g)
        t = jnp.concatenate([dq, dk, dv], axis=1)
        dqkv.append(_from_residue_major(t, dil) if dil > 1 else t)
    dqm0, dkv0 = _memattn_bwd(proj0, qm_off0, kv0, dcat0, 1, "memattn_bwd_0")
    dwkv0, d_memnorm0 = _mem_kv_bwd(mem, mem_norm[0:1], mem_n0, wts["w_mem_kv"], 0, dkv0, "0")
    dproj0 = _assemble_dproj(dqkv, dqm0, tabs)
    w_attn_in = wts["attn_w_in"]
    grads["attn_w_in"] = _mm_tn_colblocked(h0, dproj0, "attn_in_proj_dw", w_attn_in.shape[2])
    dh0 = _mm_nt_colblocked(dproj0, w_attn_in, "attn_in_proj_dx", F32)
    grad_x, _, d_mix0 = _rmsnorm_bwd(x, mix_norm[0:1], dh0, dx, "mix_norm_bwd_0")

    grads["w_gate"] = [dwg0, dwg1]
    grads["w_up"] = [dwu0, dwu1]
    grads["w_down"] = [dwd0, dwd1]
    grads["w_mem_kv"] = [dwkv0, dwkv1]
    grads["sgu_w_in_cols"] = dwsin
    sgrads["mix_norm"] = jnp.concatenate([d_mix0, d_mix1], axis=0)
    sgrads["mem_norm"] = jnp.concatenate([d_memnorm0, d_memnorm1], axis=0)
    sgrads["ffn_norm"] = jnp.concatenate([d_ffn0, d_ffn1], axis=0)
    sgrads["final_norm"] = d_final
    sgrads["sgu_w_spatial"] = dws
    sgrads["sgu_b_spatial"] = dbs[:, :SGU_GROUPS].T
    sgrads["sgu_ln_g"] = dlng
    sgrads["sgu_ln_b"] = dlnb
    return loss, grad_x, grads, sgrads


BIG = ("w_mem_kv", "w_gate", "w_up", "w_down", "attn_w_in", "attn_w_out", "sgu_w_in", "sgu_w_out")
SMALL_REPLICATED = ("mix_norm", "mem_norm", "ffn_norm", "final_norm", "sgu_w_spatial", "sgu_b_spatial")
SMALL_SHARDED = ("sgu_ln_g", "sgu_ln_b")
WEIGHT_ORDER = ("mix_norm", "mem_norm", "w_mem_kv", "ffn_norm", "w_gate", "w_up", "w_down", "attn_w_in",
                "attn_w_out", "sgu_w_in", "sgu_ln_g", "sgu_ln_b", "sgu_w_spatial", "sgu_b_spatial",
                "sgu_w_out", "final_norm")
PACK_LANES = 128


def _pack(parts):
    flat = [p.reshape(-1) for p in parts]
    sizes = [f.shape[0] for f in flat]
    total = sum(sizes)
    rows = -(-total // PACK_LANES)
    rows = -(-rows // 8) * 8
    pad = rows * PACK_LANES - total
    packed = jnp.concatenate(flat + [jnp.zeros((pad,), F32)]).reshape(rows, PACK_LANES)
    offs, o = [], 0
    for sz in sizes:
        offs.append(o)
        o += sz
    return packed, offs, sizes


def _unpack(packed, offs, sizes, shapes):
    flat = packed.reshape(-1)
    return [flat[o:o + sz].reshape(shp) for o, sz, shp in zip(offs, sizes, shapes)]


def kernel(x, mem, positions, mix_norm, mem_norm, w_mem_kv, ffn_norm, w_gate, w_up, w_down, attn_w_in, attn_w_out, sgu_w_in, sgu_ln_g, sgu_ln_b, sgu_w_spatial, sgu_b_spatial, sgu_w_out, final_norm, loss_target, m_mix_norm, m_mem_norm, m_w_mem_kv, m_ffn_norm, m_w_gate, m_w_up, m_w_down, m_attn_w_in, m_attn_w_out, m_sgu_w_in, m_sgu_ln_g, m_sgu_ln_b, m_sgu_w_spatial, m_sgu_b_spatial, m_sgu_w_out, m_final_norm, v_mix_norm, v_mem_norm, v_w_mem_kv, v_ffn_norm, v_w_gate, v_w_up, v_w_down, v_attn_w_in, v_attn_w_out, v_sgu_w_in, v_sgu_ln_g, v_sgu_ln_b, v_sgu_w_spatial, v_sgu_b_spatial, v_sgu_w_out, v_final_norm):
    w = dict(mix_norm=mix_norm, mem_norm=mem_norm, w_mem_kv=w_mem_kv, ffn_norm=ffn_norm, w_gate=w_gate,
             w_up=w_up, w_down=w_down, attn_w_in=attn_w_in, attn_w_out=attn_w_out, sgu_w_in=sgu_w_in,
             sgu_ln_g=sgu_ln_g, sgu_ln_b=sgu_ln_b, sgu_w_spatial=sgu_w_spatial, sgu_b_spatial=sgu_b_spatial,
             sgu_w_out=sgu_w_out, final_norm=final_norm)
    mo = dict(mix_norm=m_mix_norm, mem_norm=m_mem_norm, w_mem_kv=m_w_mem_kv, ffn_norm=m_ffn_norm,
              w_gate=m_w_gate, w_up=m_w_up, w_down=m_w_down, attn_w_in=m_attn_w_in, attn_w_out=m_attn_w_out,
              sgu_w_in=m_sgu_w_in, sgu_ln_g=m_sgu_ln_g, sgu_ln_b=m_sgu_ln_b, sgu_w_spatial=m_sgu_w_spatial,
              sgu_b_spatial=m_sgu_b_spatial, sgu_w_out=m_sgu_w_out, final_norm=m_final_norm)
    vo = dict(mix_norm=v_mix_norm, mem_norm=v_mem_norm, w_mem_kv=v_w_mem_kv, ffn_norm=v_ffn_norm,
              w_gate=v_w_gate, w_up=v_w_up, w_down=v_w_down, attn_w_in=v_attn_w_in, attn_w_out=v_attn_w_out,
              sgu_w_in=v_sgu_w_in, sgu_ln_g=v_sgu_ln_g, sgu_ln_b=v_sgu_ln_b, sgu_w_spatial=v_sgu_w_spatial,
              sgu_b_spatial=v_sgu_b_spatial, sgu_w_out=v_sgu_w_out, final_norm=v_final_norm)
    me = 4 * lax.axis_index("x") + 2 * lax.axis_index("y") + lax.axis_index("c")
    d_model = x.shape[-1]

    gathered = {n: _all_gather(_cast_bf16(w[n], "cast_" + n), "gather_" + n) for n in BIG}
    wts = {
        "w_mem_kv": gathered["w_mem_kv"],
        "w_gate": gathered["w_gate"],
        "w_up": gathered["w_up"],
        "w_down": gathered["w_down"],
        "attn_w_in": gathered["attn_w_in"][:, 0],
        "attn_w_out": gathered["attn_w_out"][:, 0],
        "sgu_w_in": gathered["sgu_w_in"][:, 0].transpose(1, 0, 2).reshape(d_model, -1),
        "sgu_w_out": gathered["sgu_w_out"][:, 0].reshape(-1, d_model),
    }
    ln_pack = jnp.concatenate([sgu_ln_g, sgu_ln_b], axis=0)
    ln_all = _all_gather(ln_pack, "gather_sgu_ln")
    ln_full = ln_all.transpose(1, 0, 2).reshape(2, 1, -1)
    small = dict(mix_norm=mix_norm, mem_norm=mem_norm, ffn_norm=ffn_norm, final_norm=final_norm.reshape(1, -1),
                 sgu_w_spatial=sgu_w_spatial[0], sgu_b_spatial=sgu_b_spatial[0],
                 sgu_ln_g=ln_full[0], sgu_ln_b=ln_full[1])

    loss, grad_x, grads, sgrads = _local_step(x[0], mem[0], positions[0], loss_target[0], wts, small)
    loss = lax.psum(loss[0, 0], MESH_AXES)

    dwsin = grads.pop("sgu_w_in_cols")
    n_cb, _, cbw = dwsin.shape
    dwsin = dwsin.transpose(1, 0, 2).reshape(d_model, n_cb * cbw)
    grads["sgu_w_in"] = dwsin.reshape(d_model, N_DEV, -1).transpose(1, 0, 2)
    out_g, out_d, out_m, out_v = {}, {}, {}, {}
    for n in BIG:
        blocked = grads[n] if isinstance(grads[n], list) else [grads[n]]
        recvs = _all_to_all(blocked, "exchange_" + n)
        shard = w[n]
        w3 = shard.reshape(shard.shape[0], -1, shard.shape[-1])
        recvs = [r.reshape(N_DEV, -1, shard.shape[-1]) for r in recvs]
        res = _reduce_adam(recvs, w3, mo[n].reshape(w3.shape), vo[n].reshape(w3.shape), "adam_" + n)
        out_g[n], out_d[n], out_m[n], out_v[n] = (r.reshape(shard.shape) for r in res)

    small_names = SMALL_REPLICATED + SMALL_SHARDED
    packed, offs, sizes = _pack([sgrads[n] for n in small_names])
    all_packs = _all_gather(packed, "gather_small_grads")
    rep_shapes = [w[n].shape for n in SMALL_REPLICATED]
    w_pack, w_offs, w_sizes = _pack([w[n] for n in SMALL_REPLICATED])
    m_pack, _, _ = _pack([mo[n] for n in SMALL_REPLICATED])
    v_pack, _, _ = _pack([vo[n] for n in SMALL_REPLICATED])
    n_rep_rows = w_pack.shape[0]
    res = _reduce_adam([all_packs[:, :n_rep_rows]], w_pack[None], m_pack[None], v_pack[None], "adam_small")
    for dst, r in zip((out_g, out_d, out_m, out_v), res):
        for n, val in zip(SMALL_REPLICATED, _unpack(r[0], w_offs, w_sizes, rep_shapes)):
            dst[n] = val
    ln_rows0 = offs[len(SMALL_REPLICATED)] // PACK_LANES
    ln_rows = 2 * SGU_W // PACK_LANES
    ln_sum = _reduce_adam([all_packs[:, ln_rows0:ln_rows0 + ln_rows]], jnp.zeros((1, ln_rows, PACK_LANES), F32),
                          jnp.zeros((1, ln_rows, PACK_LANES), F32), jnp.zeros((1, ln_rows, PACK_LANES), F32),
                          "sum_ln_grads")[0]
    ln_grads = ln_sum.reshape(2, N_DEV, -1)
    ln_mine = lax.dynamic_index_in_dim(ln_grads, me, axis=1, keepdims=False)
    w_ln = jnp.concatenate([sgu_ln_g, sgu_ln_b], axis=0)[None]
    m_ln = jnp.concatenate([m_sgu_ln_g, m_sgu_ln_b], axis=0)[None]
    v_ln = jnp.concatenate([v_sgu_ln_g, v_sgu_ln_b], axis=0)[None]
    res = _reduce_adam([ln_mine[None]], w_ln, m_ln, v_ln, "adam_ln")
    for dst, r in zip((out_g, out_d, out_m, out_v), res):
        dst["sgu_ln_g"], dst["sgu_ln_b"] = r[0, 0:1], r[0, 1:2]

    return (loss, grad_x[None], *[out_g[n] for n in WEIGHT_ORDER], *[out_d[n] for n in WEIGHT_ORDER],
            *[out_m[n] for n in WEIGHT_ORDER], *[out_v[n] for n in WEIGHT_ORDER])
```

```python
import functools

import jax
import jax.numpy as jnp
from jax import lax
from jax.experimental import pallas as pl
from jax.experimental.pallas import tpu as pltpu

F32 = jnp.float32
BF16 = jnp.bfloat16

N_DEV = 8
HEAD = 128
HPG = 4
GROUP_W = HPG * HEAD
N_GROUPS = 3
DILATIONS = (1, 4, 16)
BLK = 128
SGU_GROUPS = 12
SGU_W = SGU_GROUPS * HEAD
ROT_HALF = 16
ROPE_THETA = 500000.0
NORM_EPS = 1e-6
LN_EPS = 1e-5
NEG_INF = -1e30
SCALE = HEAD ** -0.5

ADAM_LR = 0.001
ADAM_B1 = 0.9
ADAM_B2 = 0.999
ADAM_EPS = 1e-08
ADAM_WD = 0.01
ADAM_STEP = 10

VMEM_LIMIT_V7X = 56 * 1024 * 1024
MESH_AXES = ("x", "y", "c")

_DN = {
    "nn": (((1,), (0,)), ((), ())),
    "nt": (((1,), (1,)), ((), ())),
    "tn": (((0,), (0,)), ((), ())),
}


def _dot(a, b, kind="nn"):
    return lax.dot_general(a, b, _DN[kind], preferred_element_type=F32)


def _params():
    return pltpu.CompilerParams(vmem_limit_bytes=VMEM_LIMIT_V7X)


def _row_tile(rows, cap):
    if rows <= cap:
        return rows
    t = cap - cap % 16
    while t >= 16:
        if rows % t == 0:
            return t
        t -= 16
    return rows


def _gelu(x):
    c = 0.7978845608028654
    return 0.5 * x * (1.0 + jnp.tanh(c * (x + 0.044715 * x * x * x)))


def _gelu_grad(x):
    c = 0.7978845608028654
    t = jnp.tanh(c * (x + 0.044715 * x * x * x))
    return 0.5 * (1.0 + t) + 0.5 * x * (1.0 - t * t) * c * (1.0 + 3.0 * 0.044715 * x * x)


def _sigmoid(x):
    return 1.0 / (1.0 + jnp.exp(-x))


def _matmul(name, grid, terms, n_acc, acc_shape, extras, outs, epilogue, side=None):
    nk = grid[-1]
    nt, ne, no = len(terms), len(extras), len(outs)
    kinds = [(t[4], t[5]) for t in terms]
    n_scratch_acc = 0 if nk == 1 else n_acc
    ns_in = len(side.ins) if side else 0
    ns_out = len(side.outs) if side else 0
    n_steps = 1
    for g in grid:
        n_steps *= g

    def body(*refs):
        pos = 0
        ab = refs[pos:pos + 2 * nt]
        pos += 2 * nt
        ex = refs[pos:pos + ne]
        pos += ne
        s_in = refs[pos:pos + ns_in]
        pos += ns_in
        out = refs[pos:pos + no]
        pos += no
        s_out = refs[pos:pos + ns_out]
        pos += ns_out
        accs = refs[pos:pos + n_scratch_acc]
        s_sems = refs[pos + n_scratch_acc:]
        if side:
            step = pl.program_id(0)
            for ax in range(1, len(grid)):
                step = step * grid[ax] + pl.program_id(ax)
            start, mid, finish = side.phases(s_in, s_out, *s_sems)
            pl.when(step == 0)(start)
        parts = [None] * n_acc
        for t, (kind, ai) in enumerate(kinds):
            p = _dot(ab[2 * t][...].astype(BF16), ab[2 * t + 1][...].astype(BF16), kind)
            parts[ai] = p if parts[ai] is None else parts[ai] + p
        if nk == 1:
            epilogue(parts, ex, out)
        else:
            k = pl.program_id(len(grid) - 1)

            @pl.when(k == 0)
            def _():
                for ai in range(n_acc):
                    accs[ai][...] = parts[ai]

            @pl.when(k > 0)
            def _():
                for ai in range(n_acc):
                    accs[ai][...] += parts[ai]

            @pl.when(k == nk - 1)
            def _():
                epilogue([a[...] for a in accs], ex, out)

        if side:
            pl.when(step == (3 * n_steps) // 4)(mid)
            pl.when(step == n_steps - 1)(finish)

    hbm = pl.BlockSpec(memory_space=pltpu.HBM)
    in_specs, args = [], []
    for (a, a_spec, b, b_spec, _, _) in terms:
        in_specs += [a_spec, b_spec]
        args += [a, b]
    for (e, e_spec) in extras:
        in_specs.append(e_spec)
        args.append(e)
    scratch = [pltpu.VMEM(acc_shape, F32) for _ in range(n_scratch_acc)]
    out_specs = [o[1] for o in outs]
    out_shape = [o[0] for o in outs]
    if side:
        in_specs += [hbm] * ns_in
        args += list(side.ins)
        out_specs += [hbm] * ns_out
        out_shape += list(side.outs)
        scratch += side.scratch()
    res = pl.pallas_call(
        body, grid=grid, in_specs=in_specs, out_specs=out_specs, out_shape=out_shape,
        scratch_shapes=scratch, compiler_params=_params(), name=name,
    )(*args)
    return res if side is None else (res[:no], res[no:])


def _one(res, side):
    return res[0] if side is None else (res[0][0], res[1])


def _store_epilogue(parts, ex, out):
    out[0][...] = parts[0].astype(out[0].dtype)


def _residual_epilogue(parts, ex, out):
    out[0][...] = (parts[0] + ex[0][...]).astype(out[0].dtype)


def _rmsnorm_fwd(x, g, name):
    rows, d = x.shape
    tm = _row_tile(rows, 512)

    def body(x_ref, g_ref, o_ref):
        xf = x_ref[...]
        r = lax.rsqrt(jnp.mean(xf * xf, axis=-1, keepdims=True) + NORM_EPS)
        o_ref[...] = (xf * r * g_ref[...]).astype(o_ref.dtype)

    return pl.pallas_call(
        body, grid=(rows // tm,),
        in_specs=[pl.BlockSpec((tm, d), lambda i: (i, 0)), pl.BlockSpec((1, d), lambda i: (0, 0))],
        out_specs=pl.BlockSpec((tm, d), lambda i: (i, 0)),
        out_shape=jax.ShapeDtypeStruct((rows, d), BF16),
        compiler_params=_params(), name=name,
    )(x, g)


def _rmsnorm_bwd(x, g, dh, dres, name):
    rows, d = x.shape
    tm = _row_tile(rows, 256)
    has_res = dres is not None

    def body(*refs):
        if has_res:
            x_ref, g_ref, dh_ref, dres_ref, dx_ref, dxb_ref, dg_ref = refs
        else:
            x_ref, g_ref, dh_ref, dx_ref, dxb_ref, dg_ref = refs
        i = pl.program_id(0)
        xf = x_ref[...]
        r = lax.rsqrt(jnp.mean(xf * xf, axis=-1, keepdims=True) + NORM_EPS)
        xhat = xf * r
        dy = dh_ref[...]
        gdy = dy * g_ref[...]
        c = jnp.mean(gdy * xhat, axis=-1, keepdims=True)
        dx = r * (gdy - xhat * c)
        if has_res:
            dx = dx + dres_ref[...]
        dx_ref[...] = dx
        dxb_ref[...] = dx.astype(BF16)

        @pl.when(i == 0)
        def _():
            dg_ref[...] = jnp.zeros_like(dg_ref)

        dg_ref[...] += jnp.sum(dy * xhat, axis=0, keepdims=True)

    row_spec = pl.BlockSpec((tm, d), lambda i: (i, 0))
    vec_spec = pl.BlockSpec((1, d), lambda i: (0, 0))
    in_specs = [row_spec, vec_spec, row_spec] + ([row_spec] if has_res else [])
    args = [x, g, dh] + ([dres] if has_res else [])
    return pl.pallas_call(
        body, grid=(rows // tm,), in_specs=in_specs,
        out_specs=[row_spec, row_spec, vec_spec],
        out_shape=[jax.ShapeDtypeStruct((rows, d), F32), jax.ShapeDtypeStruct((rows, d), BF16),
                   jax.ShapeDtypeStruct((1, d), F32)],
        compiler_params=_params(), name=name,
    )(*args)


def _loss_head(x, g, target):
    rows, d = x.shape
    tm = _row_tile(rows, 256)

    def body(x_ref, g_ref, t_ref, loss_ref, dx_ref, dxb_ref, dg_ref):
        i = pl.program_id(0)
        xf = x_ref[...]
        gv = g_ref[...]
        r = lax.rsqrt(jnp.mean(xf * xf, axis=-1, keepdims=True) + NORM_EPS)
        xhat = xf * r
        err = xhat * gv - t_ref[...]
        row_loss = jnp.mean(err * err, axis=-1, keepdims=True)
        dy = err * (1.0 / d)
        gdy = dy * gv
        c = jnp.mean(gdy * xhat, axis=-1, keepdims=True)
        dx = r * (gdy - xhat * c)
        dx_ref[...] = dx
        dxb_ref[...] = dx.astype(BF16)

        @pl.when(i == 0)
        def _():
            dg_ref[...] = jnp.zeros_like(dg_ref)
            loss_ref[...] = jnp.zeros_like(loss_ref)

        dg_ref[...] += jnp.sum(dy * xhat, axis=0, keepdims=True)
        loss_ref[...] += 0.5 * jnp.sum(row_loss, axis=0, keepdims=True)

    row_spec = pl.BlockSpec((tm, d), lambda i: (i, 0))
    vec_spec = pl.BlockSpec((1, d), lambda i: (0, 0))
    return pl.pallas_call(
        body, grid=(rows // tm,), in_specs=[row_spec, vec_spec, row_spec],
        out_specs=[pl.BlockSpec((1, 1), lambda i: (0, 0)), row_spec, row_spec, vec_spec],
        out_shape=[jax.ShapeDtypeStruct((1, 1), F32), jax.ShapeDtypeStruct((rows, d), F32),
                   jax.ShapeDtypeStruct((rows, d), BF16), jax.ShapeDtypeStruct((1, d), F32)],
        compiler_params=_params(), name="loss_head",
    )(x, g, target)


def _rotary_tables(positions):
    inv_freq = ROPE_THETA ** (-jnp.arange(ROT_HALF, dtype=F32) / ROT_HALF)
    ang = positions.astype(F32)[:, None] * inv_freq
    cos, sin = jnp.cos(ang), jnp.sin(ang)
    s = positions.shape[0]
    z = jnp.zeros((s, HEAD - 2 * ROT_HALF), F32)
    z16 = jnp.zeros((s, ROT_HALF), F32)
    c = jnp.concatenate([cos, cos, jnp.ones_like(z)], axis=1)
    s1 = jnp.concatenate([z16, sin, z], axis=1)
    s2 = jnp.concatenate([-sin, z16, z], axis=1)
    return c, s1, s2


def _attn_in_proj(h, w_blk, tabs, side=None):
    s, d = h.shape
    nb, _, bw = w_blk.shape
    heads_per_blk = bw // HEAD
    n_rot_heads = 2 * N_GROUPS * HPG
    tm = _row_tile(s, 512)

    def epilogue(parts, ex, out):
        j = pl.program_id(0)
        acc = parts[0]
        c, s1, s2 = ex[0][...], ex[1][...], ex[2][...]
        for t in range(heads_per_blk):
            seg = acc[:, t * HEAD:(t + 1) * HEAD]
            rot = seg * c + pltpu.roll(seg, ROT_HALF, 1) * s1 + pltpu.roll(seg, HEAD - ROT_HALF, 1) * s2
            is_rot = (j * heads_per_blk + t) < n_rot_heads
            out[0][:, t * HEAD:(t + 1) * HEAD] = jnp.where(is_rot, rot, seg).astype(BF16)

    tab_spec = pl.BlockSpec((tm, HEAD), lambda j, m, k: (m, 0))
    return _one(_matmul(
        "attn_in_proj", (nb, s // tm, 1),
        [(h, pl.BlockSpec((tm, d), lambda j, m, k: (m, 0)),
          w_blk, pl.BlockSpec((None, d, bw), lambda j, m, k: (j, 0, 0)), "nn", 0)],
        1, None, [(tabs[0], tab_spec), (tabs[1], tab_spec), (tabs[2], tab_spec)],
        [(jax.ShapeDtypeStruct((s, nb * bw), BF16), pl.BlockSpec((tm, bw), lambda j, m, k: (m, j)))],
        epilogue, side), side)


ATT_TILE_BLOCKS = 4


def _att_blocks(seq_blocks):
    return min(ATT_TILE_BLOCKS, seq_blocks)


def _band_masks():
    qi = lax.broadcasted_iota(jnp.int32, (BLK, BLK), 0)
    ki = lax.broadcasted_iota(jnp.int32, (BLK, BLK), 1)
    return ki <= qi, ki >= qi


def _attn_fwd(q_arr, k_arr, v_arr, offs, seq_blocks, name):
    s = q_arr.shape[0]
    qo, ko, vo = offs
    nb = _att_blocks(seq_blocks)

    def body(q_ref, kc_ref, kp_ref, vc_ref, vp_ref, o_ref, lse_ref):
        n = pl.program_id(0)
        tile_starts_seq = (n * nb) % seq_blocks == 0
        mask_c, mask_p = _band_masks()
        for b in range(nb):
            r0 = b * BLK
            for h in range(HPG):
                c0 = h * HEAD
                q = q_ref[r0:r0 + BLK, c0:c0 + HEAD]
                kc = kc_ref[r0:r0 + BLK, c0:c0 + HEAD]
                vc = vc_ref[r0:r0 + BLK, c0:c0 + HEAD]
                if b == 0:
                    kp = kp_ref[:, c0:c0 + HEAD]
                    vp = vp_ref[:, c0:c0 + HEAD]
                else:
                    kp = kc_ref[r0 - BLK:r0, c0:c0 + HEAD]
                    vp = vc_ref[r0 - BLK:r0, c0:c0 + HEAD]
                s_c = jnp.where(mask_c, _dot(q, kc, "nt") * SCALE, NEG_INF)
                s_p = jnp.where(mask_p, _dot(q, kp, "nt") * SCALE, NEG_INF)
                if b == 0:
                    s_p = jnp.where(tile_starts_seq, NEG_INF, s_p)
                m = jnp.maximum(jnp.max(s_c, axis=-1, keepdims=True), jnp.max(s_p, axis=-1, keepdims=True))
                p_c = jnp.exp(s_c - m)
                p_p = jnp.exp(s_p - m)
                l = jnp.sum(p_c, axis=-1, keepdims=True) + jnp.sum(p_p, axis=-1, keepdims=True)
                inv = 1.0 / l
                o = _dot((p_c * inv).astype(BF16), vc) + _dot((p_p * inv).astype(BF16), vp)
                o_ref[r0:r0 + BLK, c0:c0 + HEAD] = o.astype(BF16)
                lse_ref[r0:r0 + BLK, c0:c0 + HEAD] = jnp.broadcast_to(m + jnp.log(l), (BLK, HEAD))

    def cur(off):
        return pl.BlockSpec((nb * BLK, GROUP_W), lambda n: (n, off))

    def prev(off):
        return pl.BlockSpec((BLK, GROUP_W), lambda n: (jnp.maximum(n * nb - 1, 0), off))

    return pl.pallas_call(
        body, grid=(s // (nb * BLK),),
        in_specs=[cur(qo), cur(ko), prev(ko), cur(vo), prev(vo)],
        out_specs=[cur(0), cur(0)],
        out_shape=[jax.ShapeDtypeStruct((s, GROUP_W), BF16), jax.ShapeDtypeStruct((s, GROUP_W), F32)],
        compiler_params=_params(), name=name,
    )(q_arr, k_arr, k_arr, v_arr, v_arr)


def _attn_bwd_dq(q_arr, k_arr, v_arr, offs, do, lse, delta, seq_blocks, name):
    s = q_arr.shape[0]
    qo, ko, vo = offs
    nb = _att_blocks(seq_blocks)

    def body(q_ref, kc_ref, kp_ref, vc_ref, vp_ref, do_ref, lse_ref, dl_ref, dq_ref):
        n = pl.program_id(0)
        tile_starts_seq = (n * nb) % seq_blocks == 0
        mask_c, mask_p = _band_masks()
        for b in range(nb):
            r0 = b * BLK
            for h in range(HPG):
                c0 = h * HEAD
                q = q_ref[r0:r0 + BLK, c0:c0 + HEAD]
                kc = kc_ref[r0:r0 + BLK, c0:c0 + HEAD]
                vc = vc_ref[r0:r0 + BLK, c0:c0 + HEAD]
                if b == 0:
                    kp = kp_ref[:, c0:c0 + HEAD]
                    vp = vp_ref[:, c0:c0 + HEAD]
                else:
                    kp = kc_ref[r0 - BLK:r0, c0:c0 + HEAD]
                    vp = vc_ref[r0 - BLK:r0, c0:c0 + HEAD]
                dob = do_ref[r0:r0 + BLK, c0:c0 + HEAD]
                lse_b = lse_ref[r0:r0 + BLK, c0:c0 + HEAD]
                dl_b = dl_ref[r0:r0 + BLK, c0:c0 + HEAD]
                s_c = jnp.where(mask_c, _dot(q, kc, "nt") * SCALE, NEG_INF)
                s_p = jnp.where(mask_p, _dot(q, kp, "nt") * SCALE, NEG_INF)
                if b == 0:
                    s_p = jnp.where(tile_starts_seq, NEG_INF, s_p)
                p_c = jnp.exp(s_c - lse_b)
                p_p = jnp.exp(s_p - lse_b)
                ds_c = p_c * (_dot(dob, vc, "nt") - dl_b) * SCALE
                ds_p = p_p * (_dot(dob, vp, "nt") - dl_b) * SCALE
                dq = _dot(ds_c.astype(BF16), kc) + _dot(ds_p.astype(BF16), kp)
                dq_ref[r0:r0 + BLK, c0:c0 + HEAD] = dq.astype(BF16)

    def cur(off):
        return pl.BlockSpec((nb * BLK, GROUP_W), lambda n: (n, off))

    def prev(off):
        return pl.BlockSpec((BLK, GROUP_W), lambda n: (jnp.maximum(n * nb - 1, 0), off))

    return pl.pallas_call(
        body, grid=(s // (nb * BLK),),
        in_specs=[cur(qo), cur(ko), prev(ko), cur(vo), prev(vo), cur(0), cur(0), cur(0)],
        out_specs=cur(0),
        out_shape=jax.ShapeDtypeStruct((s, GROUP_W), BF16),
        compiler_params=_params(), name=name,
    )(q_arr, k_arr, k_arr, v_arr, v_arr, do, lse, delta)


def _attn_bwd_dkv(q_arr, k_arr, v_arr, offs, do, lse, delta, seq_blocks, name):
    s = q_arr.shape[0]
    qo, ko, vo = offs
    nb = _att_blocks(seq_blocks)
    n_blocks = s // BLK

    def body(k_ref, v_ref, qc_ref, qn_ref, doc_ref, don_ref, lsec_ref, lsen_ref, dlc_ref, dln_ref,
             dk_ref, dv_ref):
        n = pl.program_id(0)
        next_in_seq = ((n + 1) * nb) % seq_blocks != 0
        mask_c, mask_p = _band_masks()
        for b in range(nb):
            r0 = b * BLK
            for h in range(HPG):
                c0 = h * HEAD
                k = k_ref[r0:r0 + BLK, c0:c0 + HEAD]
                v = v_ref[r0:r0 + BLK, c0:c0 + HEAD]
                q = qc_ref[r0:r0 + BLK, c0:c0 + HEAD]
                dob = doc_ref[r0:r0 + BLK, c0:c0 + HEAD]
                s_c = jnp.where(mask_c, _dot(q, k, "nt") * SCALE, NEG_INF)
                p_c = jnp.exp(s_c - lsec_ref[r0:r0 + BLK, c0:c0 + HEAD])
                ds_c = p_c * (_dot(dob, v, "nt") - dlc_ref[r0:r0 + BLK, c0:c0 + HEAD]) * SCALE
                dv = _dot(p_c.astype(BF16), dob, "tn")
                dk = _dot(ds_c.astype(BF16), q, "tn")
                if b < nb - 1:
                    r1 = r0 + BLK
                    q2 = qc_ref[r1:r1 + BLK, c0:c0 + HEAD]
                    do2 = doc_ref[r1:r1 + BLK, c0:c0 + HEAD]
                    lse2 = lsec_ref[r1:r1 + BLK, c0:c0 + HEAD]
                    dl2 = dlc_ref[r1:r1 + BLK, c0:c0 + HEAD]
                else:
                    q2 = qn_ref[:, c0:c0 + HEAD]
                    do2 = don_ref[:, c0:c0 + HEAD]
                    lse2 = lsen_ref[:, c0:c0 + HEAD]
                    dl2 = dln_ref[:, c0:c0 + HEAD]
                s_p = jnp.where(mask_p, _dot(q2, k, "nt") * SCALE, NEG_INF)
                if b == nb - 1:
                    s_p = jnp.where(next_in_seq, s_p, NEG_INF)
                p_p = jnp.exp(s_p - lse2)
                ds_p = p_p * (_dot(do2, v, "nt") - dl2) * SCALE
                dv = dv + _dot(p_p.astype(BF16), do2, "tn")
                dk = dk + _dot(ds_p.astype(BF16), q2, "tn")
                dk_ref[r0:r0 + BLK, c0:c0 + HEAD] = dk.astype(BF16)
                dv_ref[r0:r0 + BLK, c0:c0 + HEAD] = dv.astype(BF16)

    def cur(off):
        return pl.BlockSpec((nb * BLK, GROUP_W), lambda n: (n, off))

    def nxt(off):
        return pl.BlockSpec((BLK, GROUP_W), lambda n: (jnp.minimum((n + 1) * nb, n_blocks - 1), off))

    return pl.pallas_call(
        body, grid=(s // (nb * BLK),),
        in_specs=[cur(ko), cur(vo), cur(qo), nxt(qo), cur(0), nxt(0), cur(0), nxt(0), cur(0), nxt(0)],
        out_specs=[cur(0), cur(0)],
        out_shape=[jax.ShapeDtypeStruct((s, GROUP_W), BF16), jax.ShapeDtypeStruct((s, GROUP_W), BF16)],
        compiler_params=_params(), name=name,
    )(k_arr, v_arr, q_arr, q_arr, do, do, lse, lse, delta, delta)


def _merge_weights(lse_refs, c0):
    ls = [r[:, c0:c0 + HEAD] for r in lse_refs]
    m = jnp.maximum(jnp.maximum(ls[0], ls[1]), ls[2])
    es = [jnp.exp(l - m) for l in ls]
    inv = 1.0 / (es[0] + es[1] + es[2])
    return [e * inv for e in es]


def _merge_fwd(os_, lses):
    s = os_[0].shape[0]
    tm = _row_tile(s, 512)

    def body(o0, o1, o2, l0, l1, l2, out_ref):
        for h in range(HPG):
            c0 = h * HEAD
            w = _merge_weights((l0, l1, l2), c0)
            acc = None
            for wg, o in zip(w, (o0, o1, o2)):
                t = wg * o[:, c0:c0 + HEAD].astype(F32)
                acc = t if acc is None else acc + t
            out_ref[:, c0:c0 + HEAD] = acc.astype(BF16)

    spec = pl.BlockSpec((tm, GROUP_W), lambda i: (i, 0))
    return pl.pallas_call(
        body, grid=(s // tm,), in_specs=[spec] * 6, out_specs=spec,
        out_shape=jax.ShapeDtypeStruct((s, GROUP_W), BF16),
        compiler_params=_params(), name="merge_fwd",
    )(*os_, *lses)


def _merge_bwd(dcat, os_, lses):
    s = os_[0].shape[0]
    tm = _row_tile(s, 512)

    def body(d_ref, o0, o1, o2, l0, l1, l2, do0, do1, do2, dl0, dl1, dl2):
        for h in range(HPG):
            c0 = h * HEAD
            w = _merge_weights((l0, l1, l2), c0)
            dm = d_ref[:, c0:c0 + HEAD].astype(F32)
            merged = None
            for wg, o in zip(w, (o0, o1, o2)):
                t = wg * o[:, c0:c0 + HEAD].astype(F32)
                merged = t if merged is None else merged + t
            abar = jnp.sum(dm * merged, axis=-1, keepdims=True)
            for wg, do_ref, dl_ref in zip(w, (do0, do1, do2), (dl0, dl1, dl2)):
                do_ref[:, c0:c0 + HEAD] = (wg * dm).astype(BF16)
                dl_ref[:, c0:c0 + HEAD] = wg * abar

    spec = pl.BlockSpec((tm, GROUP_W), lambda i: (i, 0))
    return pl.pallas_call(
        body, grid=(s // tm,), in_specs=[spec] * 7, out_specs=[spec] * 6,
        out_shape=[jax.ShapeDtypeStruct((s, GROUP_W), BF16)] * 3 + [jax.ShapeDtypeStruct((s, GROUP_W), F32)] * 3,
        compiler_params=_params(), name="merge_bwd",
    )(dcat, *os_, *lses)


def _assemble_dproj(dqkv, dqm, tabs):
    s = dqm.shape[0]
    tm = _row_tile(s, 256)
    width = 3 * N_GROUPS * GROUP_W + GROUP_W

    def body(d0, d1, d2, dm_ref, c_ref, s1_ref, s2_ref, out_ref):
        c, s1, s2 = c_ref[...], s1_ref[...], s2_ref[...]
        for g, d_ref in enumerate((d0, d1, d2)):
            for part in range(3):
                for h in range(HPG):
                    src = part * GROUP_W + h * HEAD
                    dst = part * N_GROUPS * GROUP_W + g * GROUP_W + h * HEAD
                    seg = d_ref[:, src:src + HEAD]
                    if part < 2:
                        t = seg.astype(F32)
                        t = t * c - pltpu.roll(t, HEAD - ROT_HALF, 1) * s2 - pltpu.roll(t, ROT_HALF, 1) * s1
                        seg = t.astype(BF16)
                    out_ref[:, dst:dst + HEAD] = seg
        out_ref[:, 3 * N_GROUPS * GROUP_W:] = dm_ref[...]

    g_spec = pl.BlockSpec((tm, 3 * GROUP_W), lambda i: (i, 0))
    m_spec = pl.BlockSpec((tm, GROUP_W), lambda i: (i, 0))
    t_spec = pl.BlockSpec((tm, HEAD), lambda i: (i, 0))
    return pl.pallas_call(
        body, grid=(s // tm,), in_specs=[g_spec] * 3 + [m_spec] + [t_spec] * 3,
        out_specs=pl.BlockSpec((tm, width), lambda i: (i, 0)),
        out_shape=jax.ShapeDtypeStruct((s, width), BF16),
        compiler_params=_params(), name="assemble_dproj",
    )(*dqkv, dqm, *tabs)


def _mem_softmax(q, k):
    s = _dot(q, k, "nt") * SCALE
    m = jnp.max(s, axis=-1, keepdims=True)
    p = jnp.exp(s - m)
    return p * (1.0 / jnp.sum(p, axis=-1, keepdims=True))


def _memattn_fwd(q_arr, q_off, kv, name):
    s = q_arr.shape[0]
    mlen = kv.shape[0]
    tq = _row_tile(s, 512)

    def body(q_ref, kv_ref, o_ref):
        for h in range(HPG):
            c0 = h * HEAD
            p = _mem_softmax(q_ref[:, c0:c0 + HEAD], kv_ref[:, c0:c0 + HEAD])
            o = _dot(p.astype(BF16), kv_ref[:, GROUP_W + c0:GROUP_W + c0 + HEAD])
            o_ref[:, c0:c0 + HEAD] = o.astype(BF16)

    return pl.pallas_call(
        body, grid=(s // tq,),
        in_specs=[pl.BlockSpec((tq, GROUP_W), lambda i: (i, q_off)),
                  pl.BlockSpec((mlen, 2 * GROUP_W), lambda i: (0, 0))],
        out_specs=pl.BlockSpec((tq, GROUP_W), lambda i: (i, 0)),
        out_shape=jax.ShapeDtypeStruct((s, GROUP_W), BF16),
        compiler_params=_params(), name=name,
    )(q_arr, kv)


def _memattn_bwd(q_arr, q_off, kv, dcat, d_off, name):
    s = q_arr.shape[0]
    mlen = kv.shape[0]
    tq = _row_tile(s, 512)

    def body(q_ref, kv_ref, d_ref, dq_ref, dkv_ref):
        i = pl.program_id(0)

        @pl.when(i == 0)
        def _():
            dkv_ref[...] = jnp.zeros_like(dkv_ref)

        for h in range(HPG):
            c0 = h * HEAD
            q = q_ref[:, c0:c0 + HEAD]
            k = kv_ref[:, c0:c0 + HEAD]
            v = kv_ref[:, GROUP_W + c0:GROUP_W + c0 + HEAD]
            do = d_ref[:, c0:c0 + HEAD]
            p = _mem_softmax(q, k)
            dp = _dot(do, v, "nt")
            ds = p * (dp - jnp.sum(p * dp, axis=-1, keepdims=True)) * SCALE
            dsb = ds.astype(BF16)
            dq_ref[:, c0:c0 + HEAD] = _dot(dsb, k).astype(BF16)
            dkv_ref[:, c0:c0 + HEAD] += _dot(dsb, q, "tn")
            dkv_ref[:, GROUP_W + c0:GROUP_W + c0 + HEAD] += _dot(p.astype(BF16), do, "tn")

    return pl.pallas_call(
        body, grid=(s // tq,),
        in_specs=[pl.BlockSpec((tq, GROUP_W), lambda i: (i, q_off)),
                  pl.BlockSpec((mlen, 2 * GROUP_W), lambda i: (0, 0)),
                  pl.BlockSpec((tq, GROUP_W), lambda i: (i, d_off))],
        out_specs=[pl.BlockSpec((tq, GROUP_W), lambda i: (i, 0)),
                   pl.BlockSpec((mlen, 2 * GROUP_W), lambda i: (0, 0))],
        out_shape=[jax.ShapeDtypeStruct((s, GROUP_W), BF16), jax.ShapeDtypeStruct((mlen, 2 * GROUP_W), F32)],
        compiler_params=_params(), name=name,
    )(q_arr, kv, dcat)


SGU_TILE = 256


def _sgu_norm(v):
    vg = _gelu(v)
    mu = jnp.mean(vg, axis=-1, keepdims=True)
    xc = vg - mu
    var = jnp.mean(xc * xc, axis=-1, keepdims=True)
    rstd = lax.rsqrt(var + LN_EPS)
    return xc * rstd, rstd


def _tril_mask():
    r = lax.broadcasted_iota(jnp.int32, (BLK, BLK), 0)
    c = lax.broadcasted_iota(jnp.int32, (BLK, BLK), 1)
    return r >= c


def _sgu_fwd(proj, ln_g, ln_b, w_s, b_st):
    s = proj.shape[0]
    ts = _row_tile(s, SGU_TILE)

    def body(u_ref, v_ref, g_ref, b_ref, ws_ref, bst_ref, o_ref):
        ug = _gelu(u_ref[...].astype(F32))
        xhat, _ = _sgu_norm(v_ref[...].astype(F32))
        vn = (xhat * g_ref[...] + b_ref[...]).astype(BF16)
        tri = _tril_mask()
        for g in range(SGU_GROUPS):
            c0 = g * HEAD
            w = jnp.where(tri, ws_ref[g], 0.0).astype(BF16)
            bias = bst_ref[:, g:g + 1]
            for ch in range(ts // BLK):
                r0 = ch * BLK
                mixed = _dot(w, vn[r0:r0 + BLK, c0:c0 + HEAD]) + bias
                o_ref[r0:r0 + BLK, c0:c0 + HEAD] = (ug[r0:r0 + BLK, c0:c0 + HEAD] * mixed).astype(BF16)

    vec = pl.BlockSpec((1, SGU_W), lambda i: (0, 0))
    return pl.pallas_call(
        body, grid=(s // ts,),
        in_specs=[pl.BlockSpec((ts, SGU_W), lambda i: (i, 0)), pl.BlockSpec((ts, SGU_W), lambda i: (i, 1)),
                  vec, vec, pl.BlockSpec((SGU_GROUPS, BLK, BLK), lambda i: (0, 0, 0)),
                  pl.BlockSpec((BLK, SGU_GROUPS), lambda i: (0, 0))],
        out_specs=pl.BlockSpec((ts, SGU_W), lambda i: (i, 0)),
        out_shape=jax.ShapeDtypeStruct((s, SGU_W), BF16),
        compiler_params=_params(), name="sgu_fwd",
    )(proj, proj, ln_g, ln_b, w_s, b_st)


def _sgu_bwd(proj, dcat, ln_g, ln_b, w_s, b_st):
    s = proj.shape[0]
    ts = _row_tile(s, SGU_TILE)

    def body(u_ref, v_ref, d_ref, g_ref, b_ref, ws_ref, bst_ref,
             duv_ref, dws_ref, dbs_ref, dg_ref, db_ref, dvn_ref):
        i = pl.program_id(0)

        @pl.when(i == 0)
        def _():
            dws_ref[...] = jnp.zeros_like(dws_ref)
            dbs_ref[...] = jnp.zeros_like(dbs_ref)
            dg_ref[...] = jnp.zeros_like(dg_ref)
            db_ref[...] = jnp.zeros_like(db_ref)

        u = u_ref[...].astype(F32)
        v = v_ref[...].astype(F32)
        ug = _gelu(u)
        xhat, rstd = _sgu_norm(v)
        lng = g_ref[...]
        vn = (xhat * lng + b_ref[...]).astype(BF16)
        dout = d_ref[...].astype(F32)
        tri = _tril_mask()
        lane = lax.broadcasted_iota(jnp.int32, (BLK, BLK), 1)
        dbs = jnp.zeros((BLK, BLK), F32)
        for g in range(SGU_GROUPS):
            c0 = g * HEAD
            w = jnp.where(tri, ws_ref[g], 0.0).astype(BF16)
            bias = bst_ref[:, g:g + 1]
            dws = jnp.zeros((BLK, BLK), F32)
            for ch in range(ts // BLK):
                r0 = ch * BLK
                vn_gc = vn[r0:r0 + BLK, c0:c0 + HEAD]
                mixed = _dot(w, vn_gc) + bias
                do_gc = dout[r0:r0 + BLK, c0:c0 + HEAD]
                dmixed = do_gc * ug[r0:r0 + BLK, c0:c0 + HEAD]
                du = do_gc * mixed * _gelu_grad(u[r0:r0 + BLK, c0:c0 + HEAD])
                duv_ref[r0:r0 + BLK, c0:c0 + HEAD] = du.astype(BF16)
                dmb = dmixed.astype(BF16)
                dws = dws + _dot(dmb, vn_gc, "nt")
                dbs = dbs + jnp.where(lane == g, jnp.sum(dmixed, axis=-1, keepdims=True), 0.0)
                dvn_ref[r0:r0 + BLK, c0:c0 + HEAD] = _dot(w, dmb, "tn")
            dws_ref[g] += jnp.where(tri, dws, 0.0)
        dbs_ref[...] += dbs
        dvn = dvn_ref[...]
        gd = dvn * lng
        c1 = jnp.mean(gd, axis=-1, keepdims=True)
        c2 = jnp.mean(gd * xhat, axis=-1, keepdims=True)
        dvg = rstd * (gd - c1 - xhat * c2)
        duv_ref[:, SGU_W:] = (dvg * _gelu_grad(v)).astype(BF16)
        dg_ref[...] += jnp.sum(dvn * xhat, axis=0, keepdims=True)
        db_ref[...] += jnp.sum(dvn, axis=0, keepdims=True)

    vec = pl.BlockSpec((1, SGU_W), lambda i: (0, 0))
    ws_spec = pl.BlockSpec((SGU_GROUPS, BLK, BLK), lambda i: (0, 0, 0))
    return pl.pallas_call(
        body, grid=(s // ts,),
        in_specs=[pl.BlockSpec((ts, SGU_W), lambda i: (i, 0)), pl.BlockSpec((ts, SGU_W), lambda i: (i, 1)),
                  pl.BlockSpec((ts, SGU_W), lambda i: (i, 0)),
                  vec, vec, ws_spec, pl.BlockSpec((BLK, SGU_GROUPS), lambda i: (0, 0))],
        out_specs=[pl.BlockSpec((ts, 2 * SGU_W), lambda i: (i, 0)), ws_spec,
                   pl.BlockSpec((BLK, BLK), lambda i: (0, 0)), vec, vec],
        out_shape=[jax.ShapeDtypeStruct((s, 2 * SGU_W), BF16),
                   jax.ShapeDtypeStruct((SGU_GROUPS, BLK, BLK), F32),
                   jax.ShapeDtypeStruct((BLK, BLK), F32),
                   jax.ShapeDtypeStruct((1, SGU_W), F32), jax.ShapeDtypeStruct((1, SGU_W), F32)],
        scratch_shapes=[pltpu.VMEM((ts, SGU_W), F32)],
        compiler_params=_params(), name="sgu_bwd",
    )(proj, proj, dcat, ln_g, ln_b, w_s, b_st)


def _swiglu_fwd(h, wg, wu, name, side=None):
    s, d = h.shape
    nb, _, fb = wg.shape
    tm = _row_tile(s, 512)

    def epilogue(parts, ex, out):
        g, u = parts
        out[0][...] = g.astype(BF16)
        out[1][...] = u.astype(BF16)
        out[2][...] = (g * _sigmoid(g) * u).astype(BF16)

    a_spec = pl.BlockSpec((tm, d), lambda j, m, k: (m, 0))
    w_spec = pl.BlockSpec((None, d, fb), lambda j, m, k: (j, 0, 0))
    o_spec = pl.BlockSpec((None, tm, fb), lambda j, m, k: (j, m, 0))
    o_shape = jax.ShapeDtypeStruct((nb, s, fb), BF16)
    return _matmul(name, (nb, s // tm, 1),
                   [(h, a_spec, wg, w_spec, "nn", 0), (h, a_spec, wu, w_spec, "nn", 1)],
                   2, None, [], [(o_shape, o_spec)] * 3, epilogue, side)


def _swiglu_down(hid, wd, res, name, side=None):
    nb, s, fb = hid.shape
    d = wd.shape[-1]
    tm = _row_tile(s, 512)
    return _one(_matmul(
        name, (s // tm, nb),
        [(hid, pl.BlockSpec((None, tm, fb), lambda m, j: (j, m, 0)),
          wd, pl.BlockSpec((None, fb, d), lambda m, j: (j, 0, 0)), "nn", 0)],
        1, (tm, d), [(res, pl.BlockSpec((tm, d), lambda m, j: (m, 0)))],
        [(jax.ShapeDtypeStruct((s, d), F32), pl.BlockSpec((tm, d), lambda m, j: (m, 0)))],
        _residual_epilogue, side), side)


def _swiglu_bwd_hidden(dxb, wd, gate, up, name):
    nb, s, fb = gate.shape
    d = dxb.shape[1]
    tm = _row_tile(s, 512)

    def epilogue(parts, ex, out):
        dh = parts[0]
        g = ex[0][...].astype(F32)
        u = ex[1][...].astype(F32)
        sg = _sigmoid(g)
        silu = g * sg
        out[0][...] = (dh * u * (sg + silu * (1.0 - sg))).astype(BF16)
        out[1][...] = (dh * silu).astype(BF16)

    blk = pl.BlockSpec((None, tm, fb), lambda j, m, k: (j, m, 0))
    o_shape = jax.ShapeDtypeStruct((nb, s, fb), BF16)
    return _matmul(
        name, (nb, s // tm, 1),
        [(dxb, pl.BlockSpec((tm, d), lambda j, m, k: (m, 0)),
          wd, pl.BlockSpec((None, fb, d), lambda j, m, k: (j, 0, 0)), "nt", 0)],
        1, None, [(gate, blk), (up, blk)], [(o_shape, blk)] * 2, epilogue)


def _swiglu_bwd_input(dgate, dup, wg, wu, name, side=None):
    nb, s, fb = dgate.shape
    d = wg.shape[1]
    tm = _row_tile(s, 512)
    a_spec = pl.BlockSpec((None, tm, fb), lambda m, j: (j, m, 0))
    w_spec = pl.BlockSpec((None, d, fb), lambda m, j: (j, 0, 0))
    return _one(_matmul(
        name, (s // tm, nb),
        [(dgate, a_spec, wg, w_spec, "nt", 0), (dup, a_spec, wu, w_spec, "nt", 0)],
        1, (tm, d), [],
        [(jax.ShapeDtypeStruct((s, d), F32), pl.BlockSpec((tm, d), lambda m, j: (m, 0)))],
        _store_epilogue, side), side)


def _swiglu_bwd_w_in(h, dgate, dup, name, side=None):
    s, d = h.shape
    nb, _, fb = dgate.shape
    tk = _row_tile(s, 512)

    def epilogue(parts, ex, out):
        out[0][...] = parts[0].astype(BF16)
        out[1][...] = parts[1].astype(BF16)

    a_spec = pl.BlockSpec((tk, d), lambda j, k: (k, 0))
    b_spec = pl.BlockSpec((None, tk, fb), lambda j, k: (j, k, 0))
    o_spec = pl.BlockSpec((None, d, fb), lambda j, k: (j, 0, 0))
    o_shape = jax.ShapeDtypeStruct((nb, d, fb), BF16)
    return _matmul(name, (nb, s // tk),
                   [(h, a_spec, dgate, b_spec, "tn", 0), (h, a_spec, dup, b_spec, "tn", 1)],
                   2, (d, fb), [], [(o_shape, o_spec)] * 2, epilogue, side)


def _swiglu_bwd_w_down(hid, dxb, name):
    nb, s, fb = hid.shape
    d = dxb.shape[1]
    tk = _row_tile(s, 512)
    return _matmul(
        name, (nb, s // tk),
        [(hid, pl.BlockSpec((None, tk, fb), lambda j, k: (j, k, 0)),
          dxb, pl.BlockSpec((tk, d), lambda j, k: (k, 0)), "tn", 0)],
        1, (fb, d), [],
        [(jax.ShapeDtypeStruct((nb, fb, d), BF16), pl.BlockSpec((None, fb, d), lambda j, k: (j, 0, 0)))],
        _store_epilogue)[0]


def _mm_nn(a, b, name, tn, out_dtype=BF16, res=None):
    m, k = a.shape
    n = b.shape[1]
    tm = _row_tile(m, 512)
    extras = [] if res is None else [(res, pl.BlockSpec((tm, tn), lambda j, i, kk: (i, j)))]
    return _matmul(
        name, (n // tn, m // tm, 1),
        [(a, pl.BlockSpec((tm, k), lambda j, i, kk: (i, 0)),
          b, pl.BlockSpec((k, tn), lambda j, i, kk: (0, j)), "nn", 0)],
        1, None, extras,
        [(jax.ShapeDtypeStruct((m, n), out_dtype), pl.BlockSpec((tm, tn), lambda j, i, kk: (i, j)))],
        _store_epilogue if res is None else _residual_epilogue)[0]


def _mm_nn_colblocked(a, b_blk, name, res, side=None):
    m, k = a.shape
    nb, _, bw = b_blk.shape
    tm = _row_tile(m, 512)
    o_spec = pl.BlockSpec((tm, bw), lambda j, i, kk: (i, j))
    return _one(_matmul(
        name, (nb, m // tm, 1),
        [(a, pl.BlockSpec((tm, k), lambda j, i, kk: (i, 0)),
          b_blk, pl.BlockSpec((None, k, bw), lambda j, i, kk: (j, 0, 0)), "nn", 0)],
        1, None, [(res, o_spec)],
        [(jax.ShapeDtypeStruct((m, nb * bw), F32), o_spec)], _residual_epilogue, side), side)


def _mm_nt_colblocked(a, b_blk, name, out_dtype, side=None):
    m = a.shape[0]
    nb, n, bw = b_blk.shape
    tm = _row_tile(m, 512)
    return _one(_matmul(
        name, (m // tm, nb),
        [(a, pl.BlockSpec((tm, bw), lambda i, j: (i, j)),
          b_blk, pl.BlockSpec((None, n, bw), lambda i, j: (j, 0, 0)), "nt", 0)],
        1, (tm, n), [],
        [(jax.ShapeDtypeStruct((m, n), out_dtype), pl.BlockSpec((tm, n), lambda i, j: (i, 0)))],
        _store_epilogue, side), side)


def _mm_nt(a, b, name, tk, out_dtype, side=None):
    m, k = a.shape
    n = b.shape[0]
    tm = _row_tile(m, 512)
    return _one(_matmul(
        name, (m // tm, k // tk),
        [(a, pl.BlockSpec((tm, tk), lambda i, kk: (i, kk)),
          b, pl.BlockSpec((n, tk), lambda i, kk: (0, kk)), "nt", 0)],
        1, (tm, n), [],
        [(jax.ShapeDtypeStruct((m, n), out_dtype), pl.BlockSpec((tm, n), lambda i, kk: (i, 0)))],
        _store_epilogue, side), side)


def _mm_nt_rowblocked(a, b, name, tn, out_dtype):
    m, k = a.shape
    n = b.shape[0]
    tm = _row_tile(m, 512)
    return _matmul(
        name, (n // tn, m // tm, 1),
        [(a, pl.BlockSpec((tm, k), lambda j, i, kk: (i, 0)),
          b, pl.BlockSpec((tn, k), lambda j, i, kk: (j, 0)), "nt", 0)],
        1, None, [],
        [(jax.ShapeDtypeStruct((m, n), out_dtype), pl.BlockSpec((tm, tn), lambda j, i, kk: (i, j)))],
        _store_epilogue)[0]


def _mm_tn_colblocked(a, b, name, bw, side=None):
    s, m = a.shape
    nb = b.shape[1] // bw
    tk = _row_tile(s, 512)
    return _one(_matmul(
        name, (nb, s // tk),
        [(a, pl.BlockSpec((tk, m), lambda j, k: (k, 0)),
          b, pl.BlockSpec((tk, bw), lambda j, k: (k, j)), "tn", 0)],
        1, (m, bw), [],
        [(jax.ShapeDtypeStruct((nb, m, bw), BF16), pl.BlockSpec((None, m, bw), lambda j, k: (j, 0, 0)))],
        _store_epilogue, side), side)


def _mm_tn_rowblocked(a, b, name, bh):
    s, n = b.shape
    nb = a.shape[1] // bh
    tk = _row_tile(s, 512)
    return _matmul(
        name, (nb, s // tk),
        [(a, pl.BlockSpec((tk, bh), lambda j, k: (k, j)),
          b, pl.BlockSpec((tk, n), lambda j, k: (k, 0)), "tn", 0)],
        1, (bh, n), [],
        [(jax.ShapeDtypeStruct((nb, bh, n), BF16), pl.BlockSpec((None, bh, n), lambda j, k: (j, 0, 0)))],
        _store_epilogue)[0]


def _as2d(a):
    return a.reshape(-1, a.shape[-1])


def _cast_bf16(w, name):
    w2 = _as2d(w)
    rows, cols = w2.shape
    tr = _row_tile(rows, 256)

    def body(w_ref, o_ref):
        o_ref[...] = w_ref[...].astype(BF16)

    spec = pl.BlockSpec((tr, cols), lambda i: (i, 0))
    out = pl.pallas_call(
        body, grid=(rows // tr,), in_specs=[spec], out_specs=spec,
        out_shape=jax.ShapeDtypeStruct((rows, cols), BF16),
        compiler_params=_params(), name=name,
    )(w2)
    return out.reshape(w.shape)


def _cast_bf16_layer(w, layer, name):
    _, rows, cols = w.shape
    tr = _row_tile(rows, 256)

    def body(w_ref, o_ref):
        o_ref[...] = w_ref[...].astype(BF16)

    return pl.pallas_call(
        body, grid=(rows // tr,),
        in_specs=[pl.BlockSpec((None, tr, cols), lambda i: (layer, i, 0))],
        out_specs=pl.BlockSpec((tr, cols), lambda i: (i, 0)),
        out_shape=jax.ShapeDtypeStruct((rows, cols), BF16),
        compiler_params=_params(), name=name,
    )(w)


def _reduce_adam(recvs, w, m, v, name):
    n_layers, rows, cols = w.shape
    n_slots = recvs[0].shape[0]
    tr = _row_tile(rows, max(16, (128 * 1024 // cols) // 16 * 16))
    nt = rows // tr
    c1 = 1.0 - ADAM_B1 ** ADAM_STEP
    c2 = 1.0 - ADAM_B2 ** ADAM_STEP

    def body(*refs):
        r_refs = refs[:n_layers]
        w_ref, m_ref, v_ref, g_out, d_out, m_out, v_out = refs[n_layers:]
        layer = pl.program_id(0)

        def update(r_ref):
            g = r_ref[0].astype(F32)
            for k in range(1, n_slots):
                g = g + r_ref[k].astype(F32)
            mm = ADAM_B1 * m_ref[...] + (1.0 - ADAM_B1) * g
            vv = ADAM_B2 * v_ref[...] + (1.0 - ADAM_B2) * (g * g)
            m_hat = mm / c1
            v_hat = vv / c2
            g_out[...] = g
            d_out[...] = -ADAM_LR * (m_hat / (jnp.sqrt(v_hat) + ADAM_EPS) + ADAM_WD * w_ref[...])
            m_out[...] = mm
            v_out[...] = vv

        for li in range(n_layers):
            if n_layers == 1:
                update(r_refs[li])
            else:
                pl.when(layer == li)(functools.partial(update, r_refs[li]))

    def recv_spec(li):
        def imap(layer, i):
            return (0, jnp.where(layer == li, i, jnp.where(layer < li, 0, nt - 1)), 0)
        return pl.BlockSpec((n_slots, tr, cols), imap)

    spec = pl.BlockSpec((None, tr, cols), lambda layer, i: (layer, i, 0))
    o_shape = jax.ShapeDtypeStruct(w.shape, F32)
    return pl.pallas_call(
        body, grid=(n_layers, nt),
        in_specs=[recv_spec(li) for li in range(n_layers)] + [spec] * 3,
        out_specs=[spec] * 4, out_shape=[o_shape] * 4,
        compiler_params=_params(), name=name,
    )(*recvs, w, m, v)


def _my_place():
    return lax.axis_index("x"), lax.axis_index("y"), lax.axis_index("c")


class _GatherSide:
    def __init__(self, blocks):
        self.ins = list(blocks)
        self.outs = [jax.ShapeDtypeStruct((N_DEV,) + b.shape, b.dtype) for b in blocks]

    def scratch(self):
        n = len(self.ins)
        return [pltpu.SemaphoreType.DMA((7 * n,)), pltpu.SemaphoreType.DMA((7 * n,)),
                pltpu.SemaphoreType.DMA((n,))]

    def phases(self, x_refs, out_refs, send_sems, recv_sems, local_sems):
        n = len(self.ins)
        x, y, c = _my_place()
        me, sibling = (x, y, c), (x, y, 1 - c)
        chips = [(1 - x, y), (x, 1 - y), (1 - x, 1 - y)]

        def slot(t, px, py, pc):
            return out_refs[t].at[4 * px + 2 * py + pc]

        def copy(t, k, blk, to, src=None):
            return pltpu.make_async_remote_copy(
                src_ref=slot(t, *blk) if src is None else src, dst_ref=slot(t, *blk),
                send_sem=send_sems.at[7 * t + k], recv_sem=recv_sems.at[7 * t + k],
                device_id=to, device_id_type=pl.DeviceIdType.MESH)

        def own(t):
            return pltpu.make_async_copy(x_refs[t], slot(t, *me), local_sems.at[t])

        def first(t):
            return [copy(t, 0, me, sibling, src=x_refs[t])] + [
                copy(t, 1 + j, me, (*chip, c), src=x_refs[t]) for j, chip in enumerate(chips)]

        def passed(t):
            return [copy(t, 4 + j, (*chip, c), sibling) for j, chip in enumerate(chips)]

        def start():
            for t in range(n):
                own(t).start()
                for cp in first(t):
                    cp.start()

        def mid():
            for t in range(n):
                fwd = passed(t)
                for j, chip in enumerate(chips):
                    copy(t, 1 + j, (*chip, c), me).wait_recv()
                    fwd[j].start()

        def finish():
            for t in range(n):
                copy(t, 0, sibling, me).wait_recv()
                for j, chip in enumerate(chips):
                    copy(t, 4 + j, (*chip, 1 - c), me).wait_recv()
                for cp in first(t) + passed(t):
                    cp.wait_send()
                own(t).wait()

        return start, mid, finish


class _ExchangeSide:
    def __init__(self, blocked):
        self.ins = list(blocked)
        self.outs = [jax.ShapeDtypeStruct(b.shape, b.dtype) for b in blocked]

    def scratch(self):
        n = len(self.ins)
        return [pltpu.SemaphoreType.DMA((7 * n,)), pltpu.SemaphoreType.DMA((7 * n,)),
                pltpu.SemaphoreType.DMA((n,))]

    def phases(self, srcs, dsts, send_sems, recv_sems, local_sems):
        n = len(self.ins)
        x, y, c = _my_place()
        me = 4 * x + 2 * y + c

        def own(t):
            return pltpu.make_async_copy(srcs[t].at[me], dsts[t].at[me], local_sems.at[t])

        def copies(t, arriving):
            res = []
            for k in range(1, N_DEV):
                px, py, pc = x ^ (k >> 2), y ^ ((k >> 1) & 1), c ^ (k & 1)
                peer = 4 * px + 2 * py + pc
                sem = 7 * t + k - 1
                res.append(pltpu.make_async_remote_copy(
                    src_ref=srcs[t].at[peer], dst_ref=dsts[t].at[peer if arriving else me],
                    send_sem=send_sems.at[sem], recv_sem=recv_sems.at[sem],
                    device_id=(px, py, pc), device_id_type=pl.DeviceIdType.MESH))
            return res

        def start():
            for t in range(n):
                own(t).start()
                for send in copies(t, False):
                    send.start()

        def mid():
            pass

        def finish():
            for t in range(n):
                for arrival in copies(t, True):
                    arrival.wait_recv()
                for send in copies(t, False):
                    send.wait_send()
                own(t).wait()

        return start, mid, finish


def _run_side(side, name):
    n_in, n_out = len(side.ins), len(side.outs)

    def body(*refs):
        start, mid, finish = side.phases(refs[:n_in], refs[n_in:n_in + n_out], *refs[n_in + n_out:])
        start()
        mid()
        finish()

    hbm = pl.BlockSpec(memory_space=pltpu.HBM)
    return pl.pallas_call(
        body, out_shape=list(side.outs), in_specs=[hbm] * n_in, out_specs=[hbm] * n_out,
        scratch_shapes=side.scratch(), name=name,
    )(*side.ins)


def _to_residue_major(a, dilation):
    s, w = a.shape
    return a.reshape(s // dilation, dilation, w).transpose(1, 0, 2).reshape(s, w)


def _from_residue_major(a, dilation):
    s, w = a.shape
    return a.reshape(dilation, s // dilation, w).transpose(1, 0, 2).reshape(s, w)


def _mem_kv(mem, gain, wkv, layer, tag):
    mem_n = _rmsnorm_fwd(mem, gain, "mem_norm_" + tag)
    mlen, d = mem.shape
    nb, _, bh, n = wkv.shape
    kv = _matmul(
        "mem_kv_" + tag, (1, nb),
        [(mem_n, pl.BlockSpec((mlen, bh), lambda i, j: (0, j)),
          wkv, pl.BlockSpec((None, None, bh, n), lambda i, j: (j, layer, 0, 0)), "nn", 0)],
        1, (mlen, n), [],
        [(jax.ShapeDtypeStruct((mlen, n), BF16), pl.BlockSpec((mlen, n), lambda i, j: (0, 0)))],
        _store_epilogue)[0]
    return mem_n, kv


def _mem_kv_bwd(mem, gain, mem_n, wkv, layer, dkv, tag):
    mlen, d = mem.shape
    nb, _, bh, n = wkv.shape
    dkvb = dkv.astype(BF16)
    dw = _mm_tn_rowblocked(mem_n, dkvb, "mem_kv_dw_" + tag, bh)
    dmem_n = _matmul(
        "mem_kv_dx_" + tag, (nb, 1),
        [(dkvb, pl.BlockSpec((mlen, n), lambda j, k: (0, 0)),
          wkv, pl.BlockSpec((None, None, bh, n), lambda j, k: (j, layer, 0, 0)), "nt", 0)],
        1, None, [],
        [(jax.ShapeDtypeStruct((mlen, d), F32), pl.BlockSpec((mlen, bh), lambda j, k: (0, j)))],
        _store_epilogue)[0]
    _, _, dgain = _rmsnorm_bwd(mem, gain, dmem_n, None, "mem_norm_bwd_" + tag)
    return dw, dgain


def _ffn_bwd(x, gain, w_gate, w_up, w_down, saved, dx, dxb, tag):
    hf, gate, up, hid = saved
    dgate, dup = _swiglu_bwd_hidden(dxb, w_down, gate, up, "swiglu_bwd_hidden_" + tag)
    dwd = _swiglu_bwd_w_down(hid, dxb, "swiglu_bwd_wdown_" + tag)
    (dwg, dwu), (r_wd,) = _swiglu_bwd_w_in(hf, dgate, dup, "swiglu_bwd_win_" + tag, _ExchangeSide([dwd]))
    dhf, (r_wg, r_wu) = _swiglu_bwd_input(dgate, dup, w_gate, w_up, "swiglu_bwd_input_" + tag,
                                          _ExchangeSide([dwg, dwu]))
    dx_new, dxb_new, dgain = _rmsnorm_bwd(x, gain, dhf, dx, "ffn_norm_bwd_" + tag)
    return dx_new, dxb_new, dgain, r_wg, r_wu, r_wd


def _local_step(x, mem, positions, target, w_attn_in, shards, small):
    s, d = x.shape
    tabs = _rotary_tables(positions)
    mix_norm, mem_norm, ffn_norm = small["mix_norm"], small["mem_norm"], small["ffn_norm"]

    h0 = _rmsnorm_fwd(x, mix_norm[0:1], "mix_norm_0")
    proj0, (w_mem_kv, w_attn_out, w_gate0) = _attn_in_proj(
        h0, w_attn_in, tabs, _GatherSide([shards["w_mem_kv"], shards["attn_w_out"], shards["w_gate"][0]]))
    qkv, offs, outs, lses = [], [], [], []
    for g, dil in enumerate(DILATIONS):
        if dil == 1:
            arr, off = proj0, (g, N_GROUPS + g, 2 * N_GROUPS + g)
        else:
            cols = [proj0[:, (p * N_GROUPS + g) * GROUP_W:(p * N_GROUPS + g + 1) * GROUP_W] for p in range(3)]
            arr, off = _to_residue_major(jnp.concatenate(cols, axis=1), dil), (0, 1, 2)
        o, lse = _attn_fwd(arr, arr, arr, off, s // dil // BLK, "attn_fwd_%d" % g)
        qkv.append(arr)
        offs.append(off)
        if dil > 1:
            o, lse = _from_residue_major(o, dil), _from_residue_major(lse, dil)
        outs.append(o)
        lses.append(lse)
    mix0 = _merge_fwd(outs, lses)
    qm_off0 = 3 * N_GROUPS
    mem_n0, kv0 = _mem_kv(mem, mem_norm[0:1], w_mem_kv, 0, "0")
    memo0 = _memattn_fwd(proj0, qm_off0, kv0, "memattn_fwd_0")
    cat0 = jnp.concatenate([mix0, memo0], axis=1)
    x1, (w_up0,) = _mm_nn_colblocked(cat0, w_attn_out, "attn_out_proj", x, _GatherSide([shards["w_up"][0]]))
    hf0 = _rmsnorm_fwd(x1, ffn_norm[0:1], "ffn_norm_0")
    (gate0, up0, hid0), (w_down0, w_sgu_in_blk, w_sgu_out_blk, w_gate1) = _swiglu_fwd(
        hf0, w_gate0, w_up0, "swiglu_fwd_0",
        _GatherSide([shards["w_down"][0], shards["sgu_w_in"], shards["sgu_w_out"], shards["w_gate"][1]]))
    x2, (w_up1, w_down1) = _swiglu_down(hid0, w_down0, x1, "swiglu_down_0",
                                        _GatherSide([shards["w_up"][1], shards["w_down"][1]]))
    ffn_saved0 = (hf0, gate0, up0, hid0)

    h1 = _rmsnorm_fwd(x2, mix_norm[1:2], "mix_norm_1")
    w_sgu_in = w_sgu_in_blk.transpose(1, 0, 2).reshape(d, -1)
    w_sgu_out = w_sgu_out_blk.reshape(-1, d)
    proj1 = _mm_nn(h1, w_sgu_in, "sgu_in_proj", w_sgu_in.shape[1] // 2)
    b_st = small["sgu_b_spatial"].T
    mix1 = _sgu_fwd(proj1, small["sgu_ln_g"], small["sgu_ln_b"], small["sgu_w_spatial"], b_st)
    qm_off1 = 2 * SGU_W // GROUP_W
    mem_n1, kv1 = _mem_kv(mem, mem_norm[1:2], w_mem_kv, 1, "1")
    memo1 = _memattn_fwd(proj1, qm_off1, kv1, "memattn_fwd_1")
    cat1 = jnp.concatenate([mix1, memo1], axis=1)
    x3 = _mm_nn(cat1, w_sgu_out, "sgu_out_proj", d // 2, out_dtype=F32, res=x2)
    hf1 = _rmsnorm_fwd(x3, ffn_norm[1:2], "ffn_norm_1")
    gate1, up1, hid1 = _swiglu_fwd(hf1, w_gate1, w_up1, "swiglu_fwd_1")
    x4 = _swiglu_down(hid1, w_down1, x3, "swiglu_down_1")
    ffn_saved1 = (hf1, gate1, up1, hid1)

    loss, dx, dxb, d_final = _loss_head(x4, small["final_norm"], target)

    recvs, sgrads = {}, {}
    dx, dxb, d_ffn1, r_wg1, r_wu1, r_wd1 = _ffn_bwd(x3, ffn_norm[1:2], w_gate1, w_up1, w_down1, ffn_saved1,
                                                    dx, dxb, "1")
    dcat1 = _mm_nt_rowblocked(dxb, w_sgu_out, "sgu_out_proj_dx", w_sgu_out.shape[0] // 2, BF16)
    dwsout = _mm_tn_rowblocked(cat1, dxb, "sgu_out_proj_dw", w_sgu_out.shape[0] // N_DEV)
    duv, dws, dbs, dlng, dlnb = _sgu_bwd(proj1, dcat1, small["sgu_ln_g"], small["sgu_ln_b"],
                                         small["sgu_w_spatial"], b_st)
    dqm1, dkv1 = _memattn_bwd(proj1, qm_off1, kv1, dcat1, SGU_W // GROUP_W, "memattn_bwd_1")
    dwkv1, d_memnorm1 = _mem_kv_bwd(mem, mem_norm[1:2], mem_n1, w_mem_kv, 1, dkv1, "1")
    dproj1 = jnp.concatenate([duv, dqm1], axis=1)
    n_cb = 7
    dwsin, (r_wsout, r_wkv1) = _mm_tn_colblocked(h1, dproj1, "sgu_in_proj_dw", dproj1.shape[1] // n_cb,
                                                 _ExchangeSide([dwsout, dwkv1]))
    dwsin = dwsin.transpose(1, 0, 2).reshape(d, N_DEV, -1).transpose(1, 0, 2)
    dh1, (r_wsin,) = _mm_nt(dproj1, w_sgu_in, "sgu_in_proj_dx", dproj1.shape[1] // n_cb, F32,
                            _ExchangeSide([dwsin]))
    dx, dxb, d_mix1 = _rmsnorm_bwd(x2, mix_norm[1:2], dh1, dx, "mix_norm_bwd_1")

    dx, dxb, d_ffn0, r_wg0, r_wu0, r_wd0 = _ffn_bwd(x1, ffn_norm[0:1], w_gate0, w_up0, w_down0, ffn_saved0,
                                                    dx, dxb, "0")
    dcat0 = _mm_nt_colblocked(dxb, w_attn_out, "attn_out_proj_dx", BF16)
    dwout0 = _mm_tn_colblocked(cat0, dxb, "attn_out_proj_dw", w_attn_out.shape[2])
    dos_and_deltas = _merge_bwd(dcat0, outs, lses)
    dqkv = []
    for g, dil in enumerate(DILATIONS):
        do_g, dl_g = dos_and_deltas[g], dos_and_deltas[N_GROUPS + g]
        lse_g = lses[g]
        if dil > 1:
            do_g, dl_g, lse_g = (_to_residue_major(t, dil) for t in (do_g, dl_g, lse_g))
        args = (qkv[g], qkv[g], qkv[g], offs[g], do_g, lse_g, dl_g, s // dil // BLK)
        dq = _attn_bwd_dq(*args, "attn_bwd_dq_%d" % g)
        dk, dv = _attn_bwd_dkv(*args, "attn_bwd_dkv_%d" % g)
        t = jnp.concatenate([dq, dk, dv], axis=1)
        dqkv.append(_from_residue_major(t, dil) if dil > 1 else t)
    dqm0, dkv0 = _memattn_bwd(proj0, qm_off0, kv0, dcat0, 1, "memattn_bwd_0")
    dwkv0, d_memnorm0 = _mem_kv_bwd(mem, mem_norm[0:1], mem_n0, w_mem_kv, 0, dkv0, "0")
    dproj0 = _assemble_dproj(dqkv, dqm0, tabs)
    dwin0, (r_wout0, r_wkv0) = _mm_tn_colblocked(h0, dproj0, "attn_in_proj_dw", w_attn_in.shape[2],
                                                 _ExchangeSide([dwout0, dwkv0]))
    dh0, (r_win0,) = _mm_nt_colblocked(dproj0, w_attn_in, "attn_in_proj_dx", F32, _ExchangeSide([dwin0]))
    grad_x, _, d_mix0 = _rmsnorm_bwd(x, mix_norm[0:1], dh0, dx, "mix_norm_bwd_0")

    recvs["w_gate"] = [r_wg0, r_wg1]
    recvs["w_up"] = [r_wu0, r_wu1]
    recvs["w_down"] = [r_wd0, r_wd1]
    recvs["w_mem_kv"] = [r_wkv0, r_wkv1]
    recvs["attn_w_in"] = [r_win0]
    recvs["attn_w_out"] = [r_wout0]
    recvs["sgu_w_in"] = [r_wsin]
    recvs["sgu_w_out"] = [r_wsout]
    sgrads["mix_norm"] = jnp.concatenate([d_mix0, d_mix1], axis=0)
    sgrads["mem_norm"] = jnp.concatenate([d_memnorm0, d_memnorm1], axis=0)
    sgrads["ffn_norm"] = jnp.concatenate([d_ffn0, d_ffn1], axis=0)
    sgrads["final_norm"] = d_final
    sgrads["sgu_w_spatial"] = dws
    sgrads["sgu_b_spatial"] = dbs[:, :SGU_GROUPS].T
    sgrads["sgu_ln_g"] = dlng
    sgrads["sgu_ln_b"] = dlnb
    return loss, grad_x, recvs, sgrads


BIG = ("w_mem_kv", "w_gate", "w_up", "w_down", "attn_w_in", "attn_w_out", "sgu_w_in", "sgu_w_out")
SMALL_REPLICATED = ("mix_norm", "mem_norm", "ffn_norm", "final_norm", "sgu_w_spatial", "sgu_b_spatial")
SMALL_SHARDED = ("sgu_ln_g", "sgu_ln_b")
WEIGHT_ORDER = ("mix_norm", "mem_norm", "w_mem_kv", "ffn_norm", "w_gate", "w_up", "w_down", "attn_w_in",
                "attn_w_out", "sgu_w_in", "sgu_ln_g", "sgu_ln_b", "sgu_w_spatial", "sgu_b_spatial",
                "sgu_w_out", "final_norm")
PACK_LANES = 128


def _pack(parts):
    flat = [p.reshape(-1) for p in parts]
    sizes = [f.shape[0] for f in flat]
    total = sum(sizes)
    rows = -(-total // PACK_LANES)
    rows = -(-rows // 8) * 8
    pad = rows * PACK_LANES - total
    packed = jnp.concatenate(flat + [jnp.zeros((pad,), F32)]).reshape(rows, PACK_LANES)
    offs, o = [], 0
    for sz in sizes:
        offs.append(o)
        o += sz
    return packed, offs, sizes


def _unpack(packed, offs, sizes, shapes):
    flat = packed.reshape(-1)
    return [flat[o:o + sz].reshape(shp) for o, sz, shp in zip(offs, sizes, shapes)]


def kernel(x, mem, positions, mix_norm, mem_norm, w_mem_kv, ffn_norm, w_gate, w_up, w_down, attn_w_in, attn_w_out, sgu_w_in, sgu_ln_g, sgu_ln_b, sgu_w_spatial, sgu_b_spatial, sgu_w_out, final_norm, loss_target, m_mix_norm, m_mem_norm, m_w_mem_kv, m_ffn_norm, m_w_gate, m_w_up, m_w_down, m_attn_w_in, m_attn_w_out, m_sgu_w_in, m_sgu_ln_g, m_sgu_ln_b, m_sgu_w_spatial, m_sgu_b_spatial, m_sgu_w_out, m_final_norm, v_mix_norm, v_mem_norm, v_w_mem_kv, v_ffn_norm, v_w_gate, v_w_up, v_w_down, v_attn_w_in, v_attn_w_out, v_sgu_w_in, v_sgu_ln_g, v_sgu_ln_b, v_sgu_w_spatial, v_sgu_b_spatial, v_sgu_w_out, v_final_norm):
    w = dict(mix_norm=mix_norm, mem_norm=mem_norm, w_mem_kv=w_mem_kv, ffn_norm=ffn_norm, w_gate=w_gate,
             w_up=w_up, w_down=w_down, attn_w_in=attn_w_in, attn_w_out=attn_w_out, sgu_w_in=sgu_w_in,
             sgu_ln_g=sgu_ln_g, sgu_ln_b=sgu_ln_b, sgu_w_spatial=sgu_w_spatial, sgu_b_spatial=sgu_b_spatial,
             sgu_w_out=sgu_w_out, final_norm=final_norm)
    mo = dict(mix_norm=m_mix_norm, mem_norm=m_mem_norm, w_mem_kv=m_w_mem_kv, ffn_norm=m_ffn_norm,
              w_gate=m_w_gate, w_up=m_w_up, w_down=m_w_down, attn_w_in=m_attn_w_in, attn_w_out=m_attn_w_out,
              sgu_w_in=m_sgu_w_in, sgu_ln_g=m_sgu_ln_g, sgu_ln_b=m_sgu_ln_b, sgu_w_spatial=m_sgu_w_spatial,
              sgu_b_spatial=m_sgu_b_spatial, sgu_w_out=m_sgu_w_out, final_norm=m_final_norm)
    vo = dict(mix_norm=v_mix_norm, mem_norm=v_mem_norm, w_mem_kv=v_w_mem_kv, ffn_norm=v_ffn_norm,
              w_gate=v_w_gate, w_up=v_w_up, w_down=v_w_down, attn_w_in=v_attn_w_in, attn_w_out=v_attn_w_out,
              sgu_w_in=v_sgu_w_in, sgu_ln_g=v_sgu_ln_g, sgu_ln_b=v_sgu_ln_b, sgu_w_spatial=v_sgu_w_spatial,
              sgu_b_spatial=v_sgu_b_spatial, sgu_w_out=v_sgu_w_out, final_norm=v_final_norm)
    me = 4 * lax.axis_index("x") + 2 * lax.axis_index("y") + lax.axis_index("c")
    d_model = x.shape[-1]

    shards = {
        "w_mem_kv": _cast_bf16(w_mem_kv, "cast_w_mem_kv"),
        "attn_w_out": _cast_bf16(attn_w_out[0], "cast_attn_w_out"),
        "sgu_w_in": _cast_bf16(sgu_w_in[0], "cast_sgu_w_in"),
        "sgu_w_out": _cast_bf16(sgu_w_out[0], "cast_sgu_w_out"),
    }
    for n in ("w_gate", "w_up", "w_down"):
        shards[n] = [_cast_bf16_layer(w[n], layer, "cast_%s_%d" % (n, layer)) for layer in range(w[n].shape[0])]
    ln_pack = jnp.concatenate([sgu_ln_g, sgu_ln_b], axis=0)
    w_attn_in, ln_all = _run_side(
        _GatherSide([_cast_bf16(attn_w_in[0], "cast_attn_w_in"), ln_pack]), "gather_attn_w_in")
    ln_full = ln_all.transpose(1, 0, 2).reshape(2, 1, -1)
    small = dict(mix_norm=mix_norm, mem_norm=mem_norm, ffn_norm=ffn_norm, final_norm=final_norm.reshape(1, -1),
                 sgu_w_spatial=sgu_w_spatial[0], sgu_b_spatial=sgu_b_spatial[0],
                 sgu_ln_g=ln_full[0], sgu_ln_b=ln_full[1])

    loss, grad_x, recvs, sgrads = _local_step(x[0], mem[0], positions[0], loss_target[0], w_attn_in, shards,
                                              small)
    loss = lax.psum(loss[0, 0], MESH_AXES)

    out_g, out_d, out_m, out_v = {}, {}, {}, {}
    for n in BIG:
        shard = w[n]
        w3 = shard.reshape(shard.shape[0], -1, shard.shape[-1])
        rs = [r.reshape(N_DEV, -1, shard.shape[-1]) for r in recvs[n]]
        res = _reduce_adam(rs, w3, mo[n].reshape(w3.shape), vo[n].reshape(w3.shape), "adam_" + n)
        out_g[n], out_d[n], out_m[n], out_v[n] = (r.reshape(shard.shape) for r in res)

    small_names = SMALL_REPLICATED + SMALL_SHARDED
    packed, offs, sizes = _pack([sgrads[n] for n in small_names])
    all_packs = _run_side(_GatherSide([packed]), "gather_small_grads")[0]
    rep_shapes = [w[n].shape for n in SMALL_REPLICATED]
    w_pack, w_offs, w_sizes = _pack([w[n] for n in SMALL_REPLICATED])
    m_pack, _, _ = _pack([mo[n] for n in SMALL_REPLICATED])
    v_pack, _, _ = _pack([vo[n] for n in SMALL_REPLICATED])
    n_rep_rows = w_pack.shape[0]
    res = _reduce_adam([all_packs[:, :n_rep_rows]], w_pack[None], m_pack[None], v_pack[None], "adam_small")
    for dst, r in zip((out_g, out_d, out_m, out_v), res):
        for n, val in zip(SMALL_REPLICATED, _unpack(r[0], w_offs, w_sizes, rep_shapes)):
            dst[n] = val
    ln_rows0 = offs[len(SMALL_REPLICATED)] // PACK_LANES
    ln_rows = 2 * SGU_W // PACK_LANES
    ln_sum = _reduce_adam([all_packs[:, ln_rows0:ln_rows0 + ln_rows]], jnp.zeros((1, ln_rows, PACK_LANES), F32),
                          jnp.zeros((1, ln_rows, PACK_LANES), F32), jnp.zeros((1, ln_rows, PACK_LANES), F32),
                          "sum_ln_grads")[0]
    ln_grads = ln_sum.reshape(2, N_DEV, -1)
    ln_mine = lax.dynamic_index_in_dim(ln_grads, me, axis=1, keepdims=False)
    w_ln = jnp.concatenate([sgu_ln_g, sgu_ln_b], axis=0)[None]
    m_ln = jnp.concatenate([m_sgu_ln_g, m_sgu_ln_b], axis=0)[None]
    v_ln = jnp.concatenate([v_sgu_ln_g, v_sgu_ln_b], axis=0)[None]
    res = _reduce_adam([ln_mine[None]], w_ln, m_ln, v_ln, "adam_ln")
    for dst, r in zip((out_g, out_d, out_m, out_v), res):
        dst["sgu_ln_g"], dst["sgu_ln_b"] = r[0, 0:1], r[0, 1:2]

    return (loss, grad_x[None], *[out_g[n] for n in WEIGHT_ORDER], *[out_d[n] for n in WEIGHT_ORDER],
            *[out_m[n] for n in WEIGHT_ORDER], *[out_v[n] for n in WEIGHT_ORDER])
```

```python
import functools

import jax
import jax.numpy as jnp
from jax import lax
from jax.experimental import pallas as pl
from jax.experimental.pallas import tpu as pltpu

F32 = jnp.float32
BF16 = jnp.bfloat16

N_DEV = 8
HEAD = 128
HPG = 4
GROUP_W = HPG * HEAD
N_GROUPS = 3
DILATIONS = (1, 4, 16)
BLK = 128
SGU_GROUPS = 12
SGU_W = SGU_GROUPS * HEAD
ROT_HALF = 16
ROPE_THETA = 500000.0
NORM_EPS = 1e-6
LN_EPS = 1e-5
NEG_INF = -1e30
SCALE = HEAD ** -0.5

ADAM_LR = 0.001
ADAM_B1 = 0.9
ADAM_B2 = 0.999
ADAM_EPS = 1e-08
ADAM_WD = 0.01
ADAM_STEP = 10

VMEM_LIMIT_V7X = 56 * 1024 * 1024
MESH_AXES = ("x", "y", "c")

_DN = {
    "nn": (((1,), (0,)), ((), ())),
    "nt": (((1,), (1,)), ((), ())),
    "tn": (((0,), (0,)), ((), ())),
}


def _dot(a, b, kind="nn"):
    return lax.dot_general(a, b, _DN[kind], preferred_element_type=F32)


def _params():
    return pltpu.CompilerParams(vmem_limit_bytes=VMEM_LIMIT_V7X)


def _row_tile(rows, cap):
    if rows <= cap:
        return rows
    t = cap - cap % 16
    while t >= 16:
        if rows % t == 0:
            return t
        t -= 16
    return rows


def _gelu(x):
    c = 0.7978845608028654
    return 0.5 * x * (1.0 + jnp.tanh(c * (x + 0.044715 * x * x * x)))


def _gelu_grad(x):
    c = 0.7978845608028654
    t = jnp.tanh(c * (x + 0.044715 * x * x * x))
    return 0.5 * (1.0 + t) + 0.5 * x * (1.0 - t * t) * c * (1.0 + 3.0 * 0.044715 * x * x)


def _sigmoid(x):
    return 1.0 / (1.0 + jnp.exp(-x))


def _matmul(name, grid, terms, n_acc, acc_shape, extras, outs, epilogue, side=None):
    nk = grid[-1]
    nt, ne, no = len(terms), len(extras), len(outs)
    kinds = [(t[4], t[5]) for t in terms]
    n_scratch_acc = 0 if nk == 1 else n_acc
    ns_in = len(side.ins) if side else 0
    ns_out = len(side.outs) if side else 0
    n_steps = 1
    for g in grid:
        n_steps *= g

    def body(*refs):
        pos = 0
        ab = refs[pos:pos + 2 * nt]
        pos += 2 * nt
        ex = refs[pos:pos + ne]
        pos += ne
        s_in = refs[pos:pos + ns_in]
        pos += ns_in
        out = refs[pos:pos + no]
        pos += no
        s_out = refs[pos:pos + ns_out]
        pos += ns_out
        accs = refs[pos:pos + n_scratch_acc]
        s_sems = refs[pos + n_scratch_acc:]
        if side:
            step = pl.program_id(0)
            for ax in range(1, len(grid)):
                step = step * grid[ax] + pl.program_id(ax)
            start, mid, finish = side.phases(s_in, s_out, *s_sems)
            pl.when(step == 0)(start)
        parts = [None] * n_acc
        for t, (kind, ai) in enumerate(kinds):
            a_ref, b_ref = ab[2 * t], ab[2 * t + 1]
            if len(b_ref.shape) == 2:
                pairs = [(a_ref[...], b_ref[...])]
            elif len(a_ref.shape) == 3:
                pairs = [(a_ref[q], b_ref[q]) for q in range(b_ref.shape[0])]
            else:
                bw = b_ref.shape[2]
                pairs = [(a_ref[:, q * bw:(q + 1) * bw], b_ref[q]) for q in range(b_ref.shape[0])]
            for a, b in pairs:
                p = _dot(a.astype(BF16), b.astype(BF16), kind)
                parts[ai] = p if parts[ai] is None else parts[ai] + p
        if nk == 1:
            epilogue(parts, ex, out)
        else:
            k = pl.program_id(len(grid) - 1)

            @pl.when(k == 0)
            def _():
                for ai in range(n_acc):
                    accs[ai][...] = parts[ai]

            @pl.when(k > 0)
            def _():
                for ai in range(n_acc):
                    accs[ai][...] += parts[ai]

            @pl.when(k == nk - 1)
            def _():
                epilogue([a[...] for a in accs], ex, out)

        if side:
            pl.when(step == (3 * n_steps) // 4)(mid)
            pl.when(step == n_steps - 1)(finish)

    hbm = pl.BlockSpec(memory_space=pltpu.HBM)
    in_specs, args = [], []
    for (a, a_spec, b, b_spec, _, _) in terms:
        in_specs += [a_spec, b_spec]
        args += [a, b]
    for (e, e_spec) in extras:
        in_specs.append(e_spec)
        args.append(e)
    scratch = [pltpu.VMEM(acc_shape, F32) for _ in range(n_scratch_acc)]
    out_specs = [o[1] for o in outs]
    out_shape = [o[0] for o in outs]
    if side:
        in_specs += [hbm] * ns_in
        args += list(side.ins)
        out_specs += [hbm] * ns_out
        out_shape += list(side.outs)
        scratch += side.scratch()
    res = pl.pallas_call(
        body, grid=grid, in_specs=in_specs, out_specs=out_specs, out_shape=out_shape,
        scratch_shapes=scratch, compiler_params=_params(), name=name,
    )(*args)
    return res if side is None else (res[:no], res[no:])


def _one(res, side):
    return res[0] if side is None else (res[0][0], res[1])


def _store_epilogue(parts, ex, out):
    out[0][...] = parts[0].astype(out[0].dtype)


def _residual_epilogue(parts, ex, out):
    out[0][...] = (parts[0] + ex[0][...]).astype(out[0].dtype)


def _rmsnorm_fwd(x, g, name):
    rows, d = x.shape
    tm = _row_tile(rows, 512)

    def body(x_ref, g_ref, o_ref):
        xf = x_ref[...]
        r = lax.rsqrt(jnp.mean(xf * xf, axis=-1, keepdims=True) + NORM_EPS)
        o_ref[...] = (xf * r * g_ref[...]).astype(o_ref.dtype)

    return pl.pallas_call(
        body, grid=(rows // tm,),
        in_specs=[pl.BlockSpec((tm, d), lambda i: (i, 0)), pl.BlockSpec((1, d), lambda i: (0, 0))],
        out_specs=pl.BlockSpec((tm, d), lambda i: (i, 0)),
        out_shape=jax.ShapeDtypeStruct((rows, d), BF16),
        compiler_params=_params(), name=name,
    )(x, g)


def _rmsnorm_bwd(x, g, dh, dres, name):
    rows, d = x.shape
    tm = _row_tile(rows, 256)
    has_res = dres is not None

    def body(*refs):
        if has_res:
            x_ref, g_ref, dh_ref, dres_ref, dx_ref, dxb_ref, dg_ref = refs
        else:
            x_ref, g_ref, dh_ref, dx_ref, dxb_ref, dg_ref = refs
        i = pl.program_id(0)
        xf = x_ref[...]
        r = lax.rsqrt(jnp.mean(xf * xf, axis=-1, keepdims=True) + NORM_EPS)
        xhat = xf * r
        dy = dh_ref[...]
        gdy = dy * g_ref[...]
        c = jnp.mean(gdy * xhat, axis=-1, keepdims=True)
        dx = r * (gdy - xhat * c)
        if has_res:
            dx = dx + dres_ref[...]
        dx_ref[...] = dx
        dxb_ref[...] = dx.astype(BF16)

        @pl.when(i == 0)
        def _():
            dg_ref[...] = jnp.zeros_like(dg_ref)

        dg_ref[...] += jnp.sum(dy * xhat, axis=0, keepdims=True)

    row_spec = pl.BlockSpec((tm, d), lambda i: (i, 0))
    vec_spec = pl.BlockSpec((1, d), lambda i: (0, 0))
    in_specs = [row_spec, vec_spec, row_spec] + ([row_spec] if has_res else [])
    args = [x, g, dh] + ([dres] if has_res else [])
    return pl.pallas_call(
        body, grid=(rows // tm,), in_specs=in_specs,
        out_specs=[row_spec, row_spec, vec_spec],
        out_shape=[jax.ShapeDtypeStruct((rows, d), F32), jax.ShapeDtypeStruct((rows, d), BF16),
                   jax.ShapeDtypeStruct((1, d), F32)],
        compiler_params=_params(), name=name,
    )(*args)


def _loss_head(x, g, target):
    rows, d = x.shape
    tm = _row_tile(rows, 256)

    def body(x_ref, g_ref, t_ref, loss_ref, dx_ref, dxb_ref, dg_ref):
        i = pl.program_id(0)
        xf = x_ref[...]
        gv = g_ref[...]
        r = lax.rsqrt(jnp.mean(xf * xf, axis=-1, keepdims=True) + NORM_EPS)
        xhat = xf * r
        err = xhat * gv - t_ref[...]
        row_loss = jnp.mean(err * err, axis=-1, keepdims=True)
        dy = err * (1.0 / d)
        gdy = dy * gv
        c = jnp.mean(gdy * xhat, axis=-1, keepdims=True)
        dx = r * (gdy - xhat * c)
        dx_ref[...] = dx
        dxb_ref[...] = dx.astype(BF16)

        @pl.when(i == 0)
        def _():
            dg_ref[...] = jnp.zeros_like(dg_ref)
            loss_ref[...] = jnp.zeros_like(loss_ref)

        dg_ref[...] += jnp.sum(dy * xhat, axis=0, keepdims=True)
        loss_ref[...] += 0.5 * jnp.sum(row_loss, axis=0, keepdims=True)

    row_spec = pl.BlockSpec((tm, d), lambda i: (i, 0))
    vec_spec = pl.BlockSpec((1, d), lambda i: (0, 0))
    return pl.pallas_call(
        body, grid=(rows // tm,), in_specs=[row_spec, vec_spec, row_spec],
        out_specs=[pl.BlockSpec((1, 1), lambda i: (0, 0)), row_spec, row_spec, vec_spec],
        out_shape=[jax.ShapeDtypeStruct((1, 1), F32), jax.ShapeDtypeStruct((rows, d), F32),
                   jax.ShapeDtypeStruct((rows, d), BF16), jax.ShapeDtypeStruct((1, d), F32)],
        compiler_params=_params(), name="loss_head",
    )(x, g, target)


def _rotary_tables(positions):
    inv_freq = ROPE_THETA ** (-jnp.arange(ROT_HALF, dtype=F32) / ROT_HALF)
    ang = positions.astype(F32)[:, None] * inv_freq
    cos, sin = jnp.cos(ang), jnp.sin(ang)
    s = positions.shape[0]
    z = jnp.zeros((s, HEAD - 2 * ROT_HALF), F32)
    z16 = jnp.zeros((s, ROT_HALF), F32)
    c = jnp.concatenate([cos, cos, jnp.ones_like(z)], axis=1)
    s1 = jnp.concatenate([z16, sin, z], axis=1)
    s2 = jnp.concatenate([-sin, z16, z], axis=1)
    return c, s1, s2


def _attn_in_proj(h, w_blk, tabs, side=None):
    s, d = h.shape
    nb, _, bw = w_blk.shape
    heads_per_blk = bw // HEAD
    n_rot_heads = 2 * N_GROUPS * HPG
    tm = _row_tile(s, 512)

    def epilogue(parts, ex, out):
        j = pl.program_id(0)
        acc = parts[0]
        c, s1, s2 = ex[0][...], ex[1][...], ex[2][...]
        for t in range(heads_per_blk):
            seg = acc[:, t * HEAD:(t + 1) * HEAD]
            rot = seg * c + pltpu.roll(seg, ROT_HALF, 1) * s1 + pltpu.roll(seg, HEAD - ROT_HALF, 1) * s2
            is_rot = (j * heads_per_blk + t) < n_rot_heads
            out[0][:, t * HEAD:(t + 1) * HEAD] = jnp.where(is_rot, rot, seg).astype(BF16)

    tab_spec = pl.BlockSpec((tm, HEAD), lambda j, m, k: (m, 0))
    return _one(_matmul(
        "attn_in_proj", (nb, s // tm, 1),
        [(h, pl.BlockSpec((tm, d), lambda j, m, k: (m, 0)),
          w_blk, pl.BlockSpec((None, d, bw), lambda j, m, k: (j, 0, 0)), "nn", 0)],
        1, None, [(tabs[0], tab_spec), (tabs[1], tab_spec), (tabs[2], tab_spec)],
        [(jax.ShapeDtypeStruct((s, nb * bw), BF16), pl.BlockSpec((tm, bw), lambda j, m, k: (m, j)))],
        epilogue, side), side)


ATT_TILE_BLOCKS = 4


def _att_blocks(seq_blocks):
    return min(ATT_TILE_BLOCKS, seq_blocks)


def _band_masks():
    qi = lax.broadcasted_iota(jnp.int32, (BLK, BLK), 0)
    ki = lax.broadcasted_iota(jnp.int32, (BLK, BLK), 1)
    return ki <= qi, ki >= qi


def _attn_fwd(q_arr, k_arr, v_arr, offs, seq_blocks, name):
    s = q_arr.shape[0]
    qo, ko, vo = offs
    nb = _att_blocks(seq_blocks)

    def body(q_ref, kc_ref, kp_ref, vc_ref, vp_ref, o_ref, lse_ref):
        n = pl.program_id(0)
        tile_starts_seq = (n * nb) % seq_blocks == 0
        mask_c, mask_p = _band_masks()
        for b in range(nb):
            r0 = b * BLK
            for h in range(HPG):
                c0 = h * HEAD
                q = q_ref[r0:r0 + BLK, c0:c0 + HEAD]
                kc = kc_ref[r0:r0 + BLK, c0:c0 + HEAD]
                vc = vc_ref[r0:r0 + BLK, c0:c0 + HEAD]
                if b == 0:
                    kp = kp_ref[:, c0:c0 + HEAD]
                    vp = vp_ref[:, c0:c0 + HEAD]
                else:
                    kp = kc_ref[r0 - BLK:r0, c0:c0 + HEAD]
                    vp = vc_ref[r0 - BLK:r0, c0:c0 + HEAD]
                s_c = jnp.where(mask_c, _dot(q, kc, "nt") * SCALE, NEG_INF)
                s_p = jnp.where(mask_p, _dot(q, kp, "nt") * SCALE, NEG_INF)
                if b == 0:
                    s_p = jnp.where(tile_starts_seq, NEG_INF, s_p)
                m = jnp.maximum(jnp.max(s_c, axis=-1, keepdims=True), jnp.max(s_p, axis=-1, keepdims=True))
                p_c = jnp.exp(s_c - m)
                p_p = jnp.exp(s_p - m)
                l = jnp.sum(p_c, axis=-1, keepdims=True) + jnp.sum(p_p, axis=-1, keepdims=True)
                inv = 1.0 / l
                o = _dot((p_c * inv).astype(BF16), vc) + _dot((p_p * inv).astype(BF16), vp)
                o_ref[r0:r0 + BLK, c0:c0 + HEAD] = o.astype(BF16)
                lse_ref[r0:r0 + BLK, c0:c0 + HEAD] = jnp.broadcast_to(m + jnp.log(l), (BLK, HEAD))

    def cur(off):
        return pl.BlockSpec((nb * BLK, GROUP_W), lambda n: (n, off))

    def prev(off):
        return pl.BlockSpec((BLK, GROUP_W), lambda n: (jnp.maximum(n * nb - 1, 0), off))

    return pl.pallas_call(
        body, grid=(s // (nb * BLK),),
        in_specs=[cur(qo), cur(ko), prev(ko), cur(vo), prev(vo)],
        out_specs=[cur(0), cur(0)],
        out_shape=[jax.ShapeDtypeStruct((s, GROUP_W), BF16), jax.ShapeDtypeStruct((s, GROUP_W), F32)],
        compiler_params=_params(), name=name,
    )(q_arr, k_arr, k_arr, v_arr, v_arr)


def _attn_bwd_dq(q_arr, k_arr, v_arr, offs, do, lse, delta, seq_blocks, name):
    s = q_arr.shape[0]
    qo, ko, vo = offs
    nb = _att_blocks(seq_blocks)

    def body(q_ref, kc_ref, kp_ref, vc_ref, vp_ref, do_ref, lse_ref, dl_ref, dq_ref):
        n = pl.program_id(0)
        tile_starts_seq = (n * nb) % seq_blocks == 0
        mask_c, mask_p = _band_masks()
        for b in range(nb):
            r0 = b * BLK
            for h in range(HPG):
                c0 = h * HEAD
                q = q_ref[r0:r0 + BLK, c0:c0 + HEAD]
                kc = kc_ref[r0:r0 + BLK, c0:c0 + HEAD]
                vc = vc_ref[r0:r0 + BLK, c0:c0 + HEAD]
                if b == 0:
                    kp = kp_ref[:, c0:c0 + HEAD]
                    vp = vp_ref[:, c0:c0 + HEAD]
                else:
                    kp = kc_ref[r0 - BLK:r0, c0:c0 + HEAD]
                    vp = vc_ref[r0 - BLK:r0, c0:c0 + HEAD]
                dob = do_ref[r0:r0 + BLK, c0:c0 + HEAD]
                lse_b = lse_ref[r0:r0 + BLK, c0:c0 + HEAD]
                dl_b = dl_ref[r0:r0 + BLK, c0:c0 + HEAD]
                s_c = jnp.where(mask_c, _dot(q, kc, "nt") * SCALE, NEG_INF)
                s_p = jnp.where(mask_p, _dot(q, kp, "nt") * SCALE, NEG_INF)
                if b == 0:
                    s_p = jnp.where(tile_starts_seq, NEG_INF, s_p)
                p_c = jnp.exp(s_c - lse_b)
                p_p = jnp.exp(s_p - lse_b)
                ds_c = p_c * (_dot(dob, vc, "nt") - dl_b) * SCALE
                ds_p = p_p * (_dot(dob, vp, "nt") - dl_b) * SCALE
                dq = _dot(ds_c.astype(BF16), kc) + _dot(ds_p.astype(BF16), kp)
                dq_ref[r0:r0 + BLK, c0:c0 + HEAD] = dq.astype(BF16)

    def cur(off):
        return pl.BlockSpec((nb * BLK, GROUP_W), lambda n: (n, off))

    def prev(off):
        return pl.BlockSpec((BLK, GROUP_W), lambda n: (jnp.maximum(n * nb - 1, 0), off))

    return pl.pallas_call(
        body, grid=(s // (nb * BLK),),
        in_specs=[cur(qo), cur(ko), prev(ko), cur(vo), prev(vo), cur(0), cur(0), cur(0)],
        out_specs=cur(0),
        out_shape=jax.ShapeDtypeStruct((s, GROUP_W), BF16),
        compiler_params=_params(), name=name,
    )(q_arr, k_arr, k_arr, v_arr, v_arr, do, lse, delta)


def _attn_bwd_dkv(q_arr, k_arr, v_arr, offs, do, lse, delta, seq_blocks, name):
    s = q_arr.shape[0]
    qo, ko, vo = offs
    nb = _att_blocks(seq_blocks)
    n_blocks = s // BLK

    def body(k_ref, v_ref, qc_ref, qn_ref, doc_ref, don_ref, lsec_ref, lsen_ref, dlc_ref, dln_ref,
             dk_ref, dv_ref):
        n = pl.program_id(0)
        next_in_seq = ((n + 1) * nb) % seq_blocks != 0
        mask_c, mask_p = _band_masks()
        for b in range(nb):
            r0 = b * BLK
            for h in range(HPG):
                c0 = h * HEAD
                k = k_ref[r0:r0 + BLK, c0:c0 + HEAD]
                v = v_ref[r0:r0 + BLK, c0:c0 + HEAD]
                q = qc_ref[r0:r0 + BLK, c0:c0 + HEAD]
                dob = doc_ref[r0:r0 + BLK, c0:c0 + HEAD]
                s_c = jnp.where(mask_c, _dot(q, k, "nt") * SCALE, NEG_INF)
                p_c = jnp.exp(s_c - lsec_ref[r0:r0 + BLK, c0:c0 + HEAD])
                ds_c = p_c * (_dot(dob, v, "nt") - dlc_ref[r0:r0 + BLK, c0:c0 + HEAD]) * SCALE
                dv = _dot(p_c.astype(BF16), dob, "tn")
                dk = _dot(ds_c.astype(BF16), q, "tn")
                if b < nb - 1:
                    r1 = r0 + BLK
                    q2 = qc_ref[r1:r1 + BLK, c0:c0 + HEAD]
                    do2 = doc_ref[r1:r1 + BLK, c0:c0 + HEAD]
                    lse2 = lsec_ref[r1:r1 + BLK, c0:c0 + HEAD]
                    dl2 = dlc_ref[r1:r1 + BLK, c0:c0 + HEAD]
                else:
                    q2 = qn_ref[:, c0:c0 + HEAD]
                    do2 = don_ref[:, c0:c0 + HEAD]
                    lse2 = lsen_ref[:, c0:c0 + HEAD]
                    dl2 = dln_ref[:, c0:c0 + HEAD]
                s_p = jnp.where(mask_p, _dot(q2, k, "nt") * SCALE, NEG_INF)
                if b == nb - 1:
                    s_p = jnp.where(next_in_seq, s_p, NEG_INF)
                p_p = jnp.exp(s_p - lse2)
                ds_p = p_p * (_dot(do2, v, "nt") - dl2) * SCALE
                dv = dv + _dot(p_p.astype(BF16), do2, "tn")
                dk = dk + _dot(ds_p.astype(BF16), q2, "tn")
                dk_ref[r0:r0 + BLK, c0:c0 + HEAD] = dk.astype(BF16)
                dv_ref[r0:r0 + BLK, c0:c0 + HEAD] = dv.astype(BF16)

    def cur(off):
        return pl.BlockSpec((nb * BLK, GROUP_W), lambda n: (n, off))

    def nxt(off):
        return pl.BlockSpec((BLK, GROUP_W), lambda n: (jnp.minimum((n + 1) * nb, n_blocks - 1), off))

    return pl.pallas_call(
        body, grid=(s // (nb * BLK),),
        in_specs=[cur(ko), cur(vo), cur(qo), nxt(qo), cur(0), nxt(0), cur(0), nxt(0), cur(0), nxt(0)],
        out_specs=[cur(0), cur(0)],
        out_shape=[jax.ShapeDtypeStruct((s, GROUP_W), BF16), jax.ShapeDtypeStruct((s, GROUP_W), BF16)],
        compiler_params=_params(), name=name,
    )(k_arr, v_arr, q_arr, q_arr, do, do, lse, lse, delta, delta)


def _merge_weights(lse_refs, c0):
    ls = [r[:, c0:c0 + HEAD] for r in lse_refs]
    m = jnp.maximum(jnp.maximum(ls[0], ls[1]), ls[2])
    es = [jnp.exp(l - m) for l in ls]
    inv = 1.0 / (es[0] + es[1] + es[2])
    return [e * inv for e in es]


def _merge_fwd(os_, lses):
    s = os_[0].shape[0]
    tm = _row_tile(s, 512)

    def body(o0, o1, o2, l0, l1, l2, out_ref):
        for h in range(HPG):
            c0 = h * HEAD
            w = _merge_weights((l0, l1, l2), c0)
            acc = None
            for wg, o in zip(w, (o0, o1, o2)):
                t = wg * o[:, c0:c0 + HEAD].astype(F32)
                acc = t if acc is None else acc + t
            out_ref[:, c0:c0 + HEAD] = acc.astype(BF16)

    spec = pl.BlockSpec((tm, GROUP_W), lambda i: (i, 0))
    return pl.pallas_call(
        body, grid=(s // tm,), in_specs=[spec] * 6, out_specs=spec,
        out_shape=jax.ShapeDtypeStruct((s, GROUP_W), BF16),
        compiler_params=_params(), name="merge_fwd",
    )(*os_, *lses)


def _merge_bwd(dcat, os_, lses):
    s = os_[0].shape[0]
    tm = _row_tile(s, 512)

    def body(d_ref, o0, o1, o2, l0, l1, l2, do0, do1, do2, dl0, dl1, dl2):
        for h in range(HPG):
            c0 = h * HEAD
            w = _merge_weights((l0, l1, l2), c0)
            dm = d_ref[:, c0:c0 + HEAD].astype(F32)
            merged = None
            for wg, o in zip(w, (o0, o1, o2)):
                t = wg * o[:, c0:c0 + HEAD].astype(F32)
                merged = t if merged is None else merged + t
            abar = jnp.sum(dm * merged, axis=-1, keepdims=True)
            for wg, do_ref, dl_ref in zip(w, (do0, do1, do2), (dl0, dl1, dl2)):
                do_ref[:, c0:c0 + HEAD] = (wg * dm).astype(BF16)
                dl_ref[:, c0:c0 + HEAD] = wg * abar

    spec = pl.BlockSpec((tm, GROUP_W), lambda i: (i, 0))
    return pl.pallas_call(
        body, grid=(s // tm,), in_specs=[spec] * 7, out_specs=[spec] * 6,
        out_shape=[jax.ShapeDtypeStruct((s, GROUP_W), BF16)] * 3 + [jax.ShapeDtypeStruct((s, GROUP_W), F32)] * 3,
        compiler_params=_params(), name="merge_bwd",
    )(dcat, *os_, *lses)


def _assemble_dproj(dqkv, dqm, tabs):
    s = dqm.shape[0]
    tm = _row_tile(s, 256)
    width = 3 * N_GROUPS * GROUP_W + GROUP_W

    def body(d0, d1, d2, dm_ref, c_ref, s1_ref, s2_ref, out_ref):
        c, s1, s2 = c_ref[...], s1_ref[...], s2_ref[...]
        for g, d_ref in enumerate((d0, d1, d2)):
            for part in range(3):
                for h in range(HPG):
                    src = part * GROUP_W + h * HEAD
                    dst = part * N_GROUPS * GROUP_W + g * GROUP_W + h * HEAD
                    seg = d_ref[:, src:src + HEAD]
                    if part < 2:
                        t = seg.astype(F32)
                        t = t * c - pltpu.roll(t, HEAD - ROT_HALF, 1) * s2 - pltpu.roll(t, ROT_HALF, 1) * s1
                        seg = t.astype(BF16)
                    out_ref[:, dst:dst + HEAD] = seg
        out_ref[:, 3 * N_GROUPS * GROUP_W:] = dm_ref[...]

    g_spec = pl.BlockSpec((tm, 3 * GROUP_W), lambda i: (i, 0))
    m_spec = pl.BlockSpec((tm, GROUP_W), lambda i: (i, 0))
    t_spec = pl.BlockSpec((tm, HEAD), lambda i: (i, 0))
    return pl.pallas_call(
        body, grid=(s // tm,), in_specs=[g_spec] * 3 + [m_spec] + [t_spec] * 3,
        out_specs=pl.BlockSpec((tm, width), lambda i: (i, 0)),
        out_shape=jax.ShapeDtypeStruct((s, width), BF16),
        compiler_params=_params(), name="assemble_dproj",
    )(*dqkv, dqm, *tabs)


def _mem_softmax(q, k):
    s = _dot(q, k, "nt") * SCALE
    m = jnp.max(s, axis=-1, keepdims=True)
    p = jnp.exp(s - m)
    return p * (1.0 / jnp.sum(p, axis=-1, keepdims=True))


def _memattn_fwd(q_arr, q_off, kv, name):
    s = q_arr.shape[0]
    mlen = kv.shape[0]
    tq = _row_tile(s, 512)

    def body(q_ref, kv_ref, o_ref):
        for h in range(HPG):
            c0 = h * HEAD
            p = _mem_softmax(q_ref[:, c0:c0 + HEAD], kv_ref[:, c0:c0 + HEAD])
            o = _dot(p.astype(BF16), kv_ref[:, GROUP_W + c0:GROUP_W + c0 + HEAD])
            o_ref[:, c0:c0 + HEAD] = o.astype(BF16)

    return pl.pallas_call(
        body, grid=(s // tq,),
        in_specs=[pl.BlockSpec((tq, GROUP_W), lambda i: (i, q_off)),
                  pl.BlockSpec((mlen, 2 * GROUP_W), lambda i: (0, 0))],
        out_specs=pl.BlockSpec((tq, GROUP_W), lambda i: (i, 0)),
        out_shape=jax.ShapeDtypeStruct((s, GROUP_W), BF16),
        compiler_params=_params(), name=name,
    )(q_arr, kv)


def _memattn_bwd(q_arr, q_off, kv, dcat, d_off, name):
    s = q_arr.shape[0]
    mlen = kv.shape[0]
    tq = _row_tile(s, 512)

    def body(q_ref, kv_ref, d_ref, dq_ref, dkv_ref):
        i = pl.program_id(0)

        @pl.when(i == 0)
        def _():
            dkv_ref[...] = jnp.zeros_like(dkv_ref)

        for h in range(HPG):
            c0 = h * HEAD
            q = q_ref[:, c0:c0 + HEAD]
            k = kv_ref[:, c0:c0 + HEAD]
            v = kv_ref[:, GROUP_W + c0:GROUP_W + c0 + HEAD]
            do = d_ref[:, c0:c0 + HEAD]
            p = _mem_softmax(q, k)
            dp = _dot(do, v, "nt")
            ds = p * (dp - jnp.sum(p * dp, axis=-1, keepdims=True)) * SCALE
            dsb = ds.astype(BF16)
            dq_ref[:, c0:c0 + HEAD] = _dot(dsb, k).astype(BF16)
            dkv_ref[:, c0:c0 + HEAD] += _dot(dsb, q, "tn")
            dkv_ref[:, GROUP_W + c0:GROUP_W + c0 + HEAD] += _dot(p.astype(BF16), do, "tn")

    return pl.pallas_call(
        body, grid=(s // tq,),
        in_specs=[pl.BlockSpec((tq, GROUP_W), lambda i: (i, q_off)),
                  pl.BlockSpec((mlen, 2 * GROUP_W), lambda i: (0, 0)),
                  pl.BlockSpec((tq, GROUP_W), lambda i: (i, d_off))],
        out_specs=[pl.BlockSpec((tq, GROUP_W), lambda i: (i, 0)),
                   pl.BlockSpec((mlen, 2 * GROUP_W), lambda i: (0, 0))],
        out_shape=[jax.ShapeDtypeStruct((s, GROUP_W), BF16), jax.ShapeDtypeStruct((mlen, 2 * GROUP_W), F32)],
        compiler_params=_params(), name=name,
    )(q_arr, kv, dcat)


SGU_TILE = 256


def _sgu_norm(v):
    vg = _gelu(v)
    mu = jnp.mean(vg, axis=-1, keepdims=True)
    xc = vg - mu
    var = jnp.mean(xc * xc, axis=-1, keepdims=True)
    rstd = lax.rsqrt(var + LN_EPS)
    return xc * rstd, rstd


def _tril_mask():
    r = lax.broadcasted_iota(jnp.int32, (BLK, BLK), 0)
    c = lax.broadcasted_iota(jnp.int32, (BLK, BLK), 1)
    return r >= c


def _sgu_fwd(proj, ln_g, ln_b, w_s, b_st):
    s = proj.shape[0]
    ts = _row_tile(s, SGU_TILE)

    def body(u_ref, v_ref, g_ref, b_ref, ws_ref, bst_ref, o_ref):
        ug = _gelu(u_ref[...].astype(F32))
        xhat, _ = _sgu_norm(v_ref[...].astype(F32))
        vn = (xhat * g_ref[...] + b_ref[...]).astype(BF16)
        tri = _tril_mask()
        for g in range(SGU_GROUPS):
            c0 = g * HEAD
            w = jnp.where(tri, ws_ref[g], 0.0).astype(BF16)
            bias = bst_ref[:, g:g + 1]
            for ch in range(ts // BLK):
                r0 = ch * BLK
                mixed = _dot(w, vn[r0:r0 + BLK, c0:c0 + HEAD]) + bias
                o_ref[r0:r0 + BLK, c0:c0 + HEAD] = (ug[r0:r0 + BLK, c0:c0 + HEAD] * mixed).astype(BF16)

    vec = pl.BlockSpec((1, SGU_W), lambda i: (0, 0))
    return pl.pallas_call(
        body, grid=(s // ts,),
        in_specs=[pl.BlockSpec((ts, SGU_W), lambda i: (i, 0)), pl.BlockSpec((ts, SGU_W), lambda i: (i, 1)),
                  vec, vec, pl.BlockSpec((SGU_GROUPS, BLK, BLK), lambda i: (0, 0, 0)),
                  pl.BlockSpec((BLK, SGU_GROUPS), lambda i: (0, 0))],
        out_specs=pl.BlockSpec((ts, SGU_W), lambda i: (i, 0)),
        out_shape=jax.ShapeDtypeStruct((s, SGU_W), BF16),
        compiler_params=_params(), name="sgu_fwd",
    )(proj, proj, ln_g, ln_b, w_s, b_st)


def _sgu_bwd(proj, dcat, ln_g, ln_b, w_s, b_st):
    s = proj.shape[0]
    ts = _row_tile(s, SGU_TILE)

    def body(u_ref, v_ref, d_ref, g_ref, b_ref, ws_ref, bst_ref,
             duv_ref, dws_ref, dbs_ref, dg_ref, db_ref, dvn_ref):
        i = pl.program_id(0)

        @pl.when(i == 0)
        def _():
            dws_ref[...] = jnp.zeros_like(dws_ref)
            dbs_ref[...] = jnp.zeros_like(dbs_ref)
            dg_ref[...] = jnp.zeros_like(dg_ref)
            db_ref[...] = jnp.zeros_like(db_ref)

        u = u_ref[...].astype(F32)
        v = v_ref[...].astype(F32)
        ug = _gelu(u)
        xhat, rstd = _sgu_norm(v)
        lng = g_ref[...]
        vn = (xhat * lng + b_ref[...]).astype(BF16)
        dout = d_ref[...].astype(F32)
        tri = _tril_mask()
        lane = lax.broadcasted_iota(jnp.int32, (BLK, BLK), 1)
        dbs = jnp.zeros((BLK, BLK), F32)
        for g in range(SGU_GROUPS):
            c0 = g * HEAD
            w = jnp.where(tri, ws_ref[g], 0.0).astype(BF16)
            bias = bst_ref[:, g:g + 1]
            dws = jnp.zeros((BLK, BLK), F32)
            for ch in range(ts // BLK):
                r0 = ch * BLK
                vn_gc = vn[r0:r0 + BLK, c0:c0 + HEAD]
                mixed = _dot(w, vn_gc) + bias
                do_gc = dout[r0:r0 + BLK, c0:c0 + HEAD]
                dmixed = do_gc * ug[r0:r0 + BLK, c0:c0 + HEAD]
                du = do_gc * mixed * _gelu_grad(u[r0:r0 + BLK, c0:c0 + HEAD])
                duv_ref[r0:r0 + BLK, c0:c0 + HEAD] = du.astype(BF16)
                dmb = dmixed.astype(BF16)
                dws = dws + _dot(dmb, vn_gc, "nt")
                dbs = dbs + jnp.where(lane == g, jnp.sum(dmixed, axis=-1, keepdims=True), 0.0)
                dvn_ref[r0:r0 + BLK, c0:c0 + HEAD] = _dot(w, dmb, "tn")
            dws_ref[g] += jnp.where(tri, dws, 0.0)
        dbs_ref[...] += dbs
        dvn = dvn_ref[...]
        gd = dvn * lng
        c1 = jnp.mean(gd, axis=-1, keepdims=True)
        c2 = jnp.mean(gd * xhat, axis=-1, keepdims=True)
        dvg = rstd * (gd - c1 - xhat * c2)
        duv_ref[:, SGU_W:] = (dvg * _gelu_grad(v)).astype(BF16)
        dg_ref[...] += jnp.sum(dvn * xhat, axis=0, keepdims=True)
        db_ref[...] += jnp.sum(dvn, axis=0, keepdims=True)

    vec = pl.BlockSpec((1, SGU_W), lambda i: (0, 0))
    ws_spec = pl.BlockSpec((SGU_GROUPS, BLK, BLK), lambda i: (0, 0, 0))
    return pl.pallas_call(
        body, grid=(s // ts,),
        in_specs=[pl.BlockSpec((ts, SGU_W), lambda i: (i, 0)), pl.BlockSpec((ts, SGU_W), lambda i: (i, 1)),
                  pl.BlockSpec((ts, SGU_W), lambda i: (i, 0)),
                  vec, vec, ws_spec, pl.BlockSpec((BLK, SGU_GROUPS), lambda i: (0, 0))],
        out_specs=[pl.BlockSpec((ts, 2 * SGU_W), lambda i: (i, 0)), ws_spec,
                   pl.BlockSpec((BLK, BLK), lambda i: (0, 0)), vec, vec],
        out_shape=[jax.ShapeDtypeStruct((s, 2 * SGU_W), BF16),
                   jax.ShapeDtypeStruct((SGU_GROUPS, BLK, BLK), F32),
                   jax.ShapeDtypeStruct((BLK, BLK), F32),
                   jax.ShapeDtypeStruct((1, SGU_W), F32), jax.ShapeDtypeStruct((1, SGU_W), F32)],
        scratch_shapes=[pltpu.VMEM((ts, SGU_W), F32)],
        compiler_params=_params(), name="sgu_bwd",
    )(proj, proj, dcat, ln_g, ln_b, w_s, b_st)


BLOCKS_PER_STEP = 2


def _swiglu_fwd(h, wg, wu, name, side=None):
    s, d = h.shape
    nb, _, fb = wg.shape
    tm = _row_tile(s, 512)

    def epilogue(parts, ex, out):
        g, u = parts
        out[0][...] = g.astype(BF16)
        out[1][...] = u.astype(BF16)
        out[2][...] = (g * _sigmoid(g) * u).astype(BF16)

    a_spec = pl.BlockSpec((tm, d), lambda j, m, k: (m, 0))
    w_spec = pl.BlockSpec((None, d, fb), lambda j, m, k: (j, 0, 0))
    o_spec = pl.BlockSpec((None, tm, fb), lambda j, m, k: (j, m, 0))
    o_shape = jax.ShapeDtypeStruct((nb, s, fb), BF16)
    return _matmul(name, (nb, s // tm, 1),
                   [(h, a_spec, wg, w_spec, "nn", 0), (h, a_spec, wu, w_spec, "nn", 1)],
                   2, None, [], [(o_shape, o_spec)] * 3, epilogue, side)


def _swiglu_down(hid, wd, res, name, side=None):
    nb, s, fb = hid.shape
    d = wd.shape[-1]
    tm = _row_tile(s, 512)
    jb = BLOCKS_PER_STEP
    return _one(_matmul(
        name, (s // tm, nb // jb),
        [(hid, pl.BlockSpec((jb, tm, fb), lambda m, j: (j, m, 0)),
          wd, pl.BlockSpec((jb, fb, d), lambda m, j: (j, 0, 0)), "nn", 0)],
        1, (tm, d), [(res, pl.BlockSpec((tm, d), lambda m, j: (m, 0)))],
        [(jax.ShapeDtypeStruct((s, d), F32), pl.BlockSpec((tm, d), lambda m, j: (m, 0)))],
        _residual_epilogue, side), side)


def _swiglu_bwd_hidden(dxb, wd, gate, up, name):
    nb, s, fb = gate.shape
    d = dxb.shape[1]
    tm = _row_tile(s, 512)

    def epilogue(parts, ex, out):
        dh = parts[0]
        g = ex[0][...].astype(F32)
        u = ex[1][...].astype(F32)
        sg = _sigmoid(g)
        silu = g * sg
        out[0][...] = (dh * u * (sg + silu * (1.0 - sg))).astype(BF16)
        out[1][...] = (dh * silu).astype(BF16)

    blk = pl.BlockSpec((None, tm, fb), lambda j, m, k: (j, m, 0))
    o_shape = jax.ShapeDtypeStruct((nb, s, fb), BF16)
    return _matmul(
        name, (nb, s // tm, 1),
        [(dxb, pl.BlockSpec((tm, d), lambda j, m, k: (m, 0)),
          wd, pl.BlockSpec((None, fb, d), lambda j, m, k: (j, 0, 0)), "nt", 0)],
        1, None, [(gate, blk), (up, blk)], [(o_shape, blk)] * 2, epilogue)


def _swiglu_bwd_input(dgate, dup, wg, wu, name, side=None):
    nb, s, fb = dgate.shape
    d = wg.shape[1]
    tm = _row_tile(s, 512)
    jb = BLOCKS_PER_STEP
    a_spec = pl.BlockSpec((jb, tm, fb), lambda m, j: (j, m, 0))
    w_spec = pl.BlockSpec((jb, d, fb), lambda m, j: (j, 0, 0))
    return _one(_matmul(
        name, (s // tm, nb // jb),
        [(dgate, a_spec, wg, w_spec, "nt", 0), (dup, a_spec, wu, w_spec, "nt", 0)],
        1, (tm, d), [],
        [(jax.ShapeDtypeStruct((s, d), F32), pl.BlockSpec((tm, d), lambda m, j: (m, 0)))],
        _store_epilogue, side), side)


def _swiglu_bwd_w_in(h, dact, name, side=None):
    s, d = h.shape
    nb, _, fb = dact.shape
    tmd = _row_tile(d, 512)
    return _one(_matmul(
        name, (nb, d // tmd, 1),
        [(h, pl.BlockSpec((s, tmd), lambda j, m, k: (0, m)),
          dact, pl.BlockSpec((None, s, fb), lambda j, m, k: (j, 0, 0)), "tn", 0)],
        1, None, [],
        [(jax.ShapeDtypeStruct((nb, d, fb), BF16), pl.BlockSpec((None, tmd, fb), lambda j, m, k: (j, m, 0)))],
        _store_epilogue, side), side)


def _swiglu_bwd_w_down(hid, dxb, name):
    nb, s, fb = hid.shape
    d = dxb.shape[1]
    tk = _row_tile(s, 2048)
    return _matmul(
        name, (nb, s // tk),
        [(hid, pl.BlockSpec((None, tk, fb), lambda j, k: (j, k, 0)),
          dxb, pl.BlockSpec((tk, d), lambda j, k: (k, 0)), "tn", 0)],
        1, (fb, d), [],
        [(jax.ShapeDtypeStruct((nb, fb, d), BF16), pl.BlockSpec((None, fb, d), lambda j, k: (j, 0, 0)))],
        _store_epilogue)[0]


def _mm_nn(a, b, name, tn, out_dtype=BF16, res=None):
    m, k = a.shape
    n = b.shape[1]
    tm = _row_tile(m, 512)
    extras = [] if res is None else [(res, pl.BlockSpec((tm, tn), lambda j, i, kk: (i, j)))]
    return _matmul(
        name, (n // tn, m // tm, 1),
        [(a, pl.BlockSpec((tm, k), lambda j, i, kk: (i, 0)),
          b, pl.BlockSpec((k, tn), lambda j, i, kk: (0, j)), "nn", 0)],
        1, None, extras,
        [(jax.ShapeDtypeStruct((m, n), out_dtype), pl.BlockSpec((tm, tn), lambda j, i, kk: (i, j)))],
        _store_epilogue if res is None else _residual_epilogue)[0]


def _mm_nn_colblocked(a, b_blk, name, res, side=None):
    m, k = a.shape
    nb, _, bw = b_blk.shape
    tm = _row_tile(m, 2048)
    o_spec = pl.BlockSpec((tm, bw), lambda j, i, kk: (i, j))
    return _one(_matmul(
        name, (nb, m // tm, 1),
        [(a, pl.BlockSpec((tm, k), lambda j, i, kk: (i, 0)),
          b_blk, pl.BlockSpec((None, k, bw), lambda j, i, kk: (j, 0, 0)), "nn", 0)],
        1, None, [(res, o_spec)],
        [(jax.ShapeDtypeStruct((m, nb * bw), F32), o_spec)], _residual_epilogue, side), side)


def _mm_nt_colblocked(a, b_blk, name, out_dtype, jb, side=None):
    m = a.shape[0]
    nb, n, bw = b_blk.shape
    tm = _row_tile(m, 512)
    return _one(_matmul(
        name, (m // tm, nb // jb),
        [(a, pl.BlockSpec((tm, jb * bw), lambda i, j: (i, j)),
          b_blk, pl.BlockSpec((jb, n, bw), lambda i, j: (j, 0, 0)), "nt", 0)],
        1, (tm, n), [],
        [(jax.ShapeDtypeStruct((m, n), out_dtype), pl.BlockSpec((tm, n), lambda i, j: (i, 0)))],
        _store_epilogue, side), side)


def _mm_nt(a, b, name, tk, out_dtype, side=None):
    m, k = a.shape
    n = b.shape[0]
    tm = _row_tile(m, 512)
    return _one(_matmul(
        name, (m // tm, k // tk),
        [(a, pl.BlockSpec((tm, tk), lambda i, kk: (i, kk)),
          b, pl.BlockSpec((n, tk), lambda i, kk: (0, kk)), "nt", 0)],
        1, (tm, n), [],
        [(jax.ShapeDtypeStruct((m, n), out_dtype), pl.BlockSpec((tm, n), lambda i, kk: (i, 0)))],
        _store_epilogue, side), side)


def _mm_nt_rowblocked(a, b, name, tn, out_dtype):
    m, k = a.shape
    n = b.shape[0]
    tm = _row_tile(m, 512)
    return _matmul(
        name, (n // tn, m // tm, 1),
        [(a, pl.BlockSpec((tm, k), lambda j, i, kk: (i, 0)),
          b, pl.BlockSpec((tn, k), lambda j, i, kk: (j, 0)), "nt", 0)],
        1, None, [],
        [(jax.ShapeDtypeStruct((m, n), out_dtype), pl.BlockSpec((tm, tn), lambda j, i, kk: (i, j)))],
        _store_epilogue)[0]


def _mm_tn_colblocked(a, b, name, bw, side=None):
    s, m = a.shape
    nb = b.shape[1] // bw
    tm = _row_tile(m, 512)
    return _one(_matmul(
        name, (nb, m // tm, 1),
        [(a, pl.BlockSpec((s, tm), lambda j, i, k: (0, i)),
          b, pl.BlockSpec((s, bw), lambda j, i, k: (0, j)), "tn", 0)],
        1, None, [],
        [(jax.ShapeDtypeStruct((nb, m, bw), BF16), pl.BlockSpec((None, tm, bw), lambda j, i, k: (j, i, 0)))],
        _store_epilogue, side), side)


def _mm_tn_rowblocked(a, b, name, bh):
    s, n = b.shape
    nb = a.shape[1] // bh
    tn = _row_tile(n, 512)
    return _matmul(
        name, (nb, n // tn, 1),
        [(a, pl.BlockSpec((s, bh), lambda j, i, k: (0, j)),
          b, pl.BlockSpec((s, tn), lambda j, i, k: (0, i)), "tn", 0)],
        1, None, [],
        [(jax.ShapeDtypeStruct((nb, bh, n), BF16), pl.BlockSpec((None, bh, tn), lambda j, i, k: (j, 0, i)))],
        _store_epilogue)[0]


def _as2d(a):
    return a.reshape(-1, a.shape[-1])


def _cast_bf16(w, name):
    w2 = _as2d(w)
    rows, cols = w2.shape
    tr = _row_tile(rows, 256)

    def body(w_ref, o_ref):
        o_ref[...] = w_ref[...].astype(BF16)

    spec = pl.BlockSpec((tr, cols), lambda i: (i, 0))
    out = pl.pallas_call(
        body, grid=(rows // tr,), in_specs=[spec], out_specs=spec,
        out_shape=jax.ShapeDtypeStruct((rows, cols), BF16),
        compiler_params=_params(), name=name,
    )(w2)
    return out.reshape(w.shape)


def _cast_bf16_layer(w, layer, name):
    _, rows, cols = w.shape
    tr = _row_tile(rows, 256)

    def body(w_ref, o_ref):
        o_ref[...] = w_ref[...].astype(BF16)

    return pl.pallas_call(
        body, grid=(rows // tr,),
        in_specs=[pl.BlockSpec((None, tr, cols), lambda i: (layer, i, 0))],
        out_specs=pl.BlockSpec((tr, cols), lambda i: (i, 0)),
        out_shape=jax.ShapeDtypeStruct((rows, cols), BF16),
        compiler_params=_params(), name=name,
    )(w)


def _reduce_adam(recvs, w, m, v, name):
    n_layers, rows, cols = w.shape
    n_slots = recvs[0].shape[0]
    tr = _row_tile(rows, max(16, (128 * 1024 // cols) // 16 * 16))
    nt = rows // tr
    c1 = 1.0 - ADAM_B1 ** ADAM_STEP
    c2 = 1.0 - ADAM_B2 ** ADAM_STEP

    def body(*refs):
        r_refs = refs[:n_layers]
        w_ref, m_ref, v_ref, g_out, d_out, m_out, v_out = refs[n_layers:]
        layer = pl.program_id(0)

        def update(r_ref):
            g = r_ref[0].astype(F32)
            for k in range(1, n_slots):
                g = g + r_ref[k].astype(F32)
            mm = ADAM_B1 * m_ref[...] + (1.0 - ADAM_B1) * g
            vv = ADAM_B2 * v_ref[...] + (1.0 - ADAM_B2) * (g * g)
            m_hat = mm / c1
            v_hat = vv / c2
            g_out[...] = g
            d_out[...] = -ADAM_LR * (m_hat / (jnp.sqrt(v_hat) + ADAM_EPS) + ADAM_WD * w_ref[...])
            m_out[...] = mm
            v_out[...] = vv

        for li in range(n_layers):
            if n_layers == 1:
                update(r_refs[li])
            else:
                pl.when(layer == li)(functools.partial(update, r_refs[li]))

    def recv_spec(li):
        def imap(layer, i):
            return (0, jnp.where(layer == li, i, jnp.where(layer < li, 0, nt - 1)), 0)
        return pl.BlockSpec((n_slots, tr, cols), imap)

    spec = pl.BlockSpec((None, tr, cols), lambda layer, i: (layer, i, 0))
    o_shape = jax.ShapeDtypeStruct(w.shape, F32)
    return pl.pallas_call(
        body, grid=(n_layers, nt),
        in_specs=[recv_spec(li) for li in range(n_layers)] + [spec] * 3,
        out_specs=[spec] * 4, out_shape=[o_shape] * 4,
        compiler_params=_params(), name=name,
    )(*recvs, w, m, v)


def _my_place():
    return lax.axis_index("x"), lax.axis_index("y"), lax.axis_index("c")


class _GatherSide:
    def __init__(self, blocks):
        self.ins = list(blocks)
        self.outs = [jax.ShapeDtypeStruct((N_DEV,) + b.shape, b.dtype) for b in blocks]

    def scratch(self):
        n = len(self.ins)
        return [pltpu.SemaphoreType.DMA((7 * n,)), pltpu.SemaphoreType.DMA((7 * n,)),
                pltpu.SemaphoreType.DMA((n,))]

    def phases(self, x_refs, out_refs, send_sems, recv_sems, local_sems):
        n = len(self.ins)
        x, y, c = _my_place()
        me, sibling = (x, y, c), (x, y, 1 - c)
        chips = [(1 - x, y), (x, 1 - y), (1 - x, 1 - y)]

        def slot(t, px, py, pc):
            return out_refs[t].at[4 * px + 2 * py + pc]

        def copy(t, k, blk, to, src=None):
            return pltpu.make_async_remote_copy(
                src_ref=slot(t, *blk) if src is None else src, dst_ref=slot(t, *blk),
                send_sem=send_sems.at[7 * t + k], recv_sem=recv_sems.at[7 * t + k],
                device_id=to, device_id_type=pl.DeviceIdType.MESH)

        def own(t):
            return pltpu.make_async_copy(x_refs[t], slot(t, *me), local_sems.at[t])

        def first(t):
            return [copy(t, 0, me, sibling, src=x_refs[t])] + [
                copy(t, 1 + j, me, (*chip, c), src=x_refs[t]) for j, chip in enumerate(chips)]

        def passed(t):
            return [copy(t, 4 + j, (*chip, c), sibling) for j, chip in enumerate(chips)]

        def start():
            for t in range(n):
                own(t).start()
                for cp in first(t):
                    cp.start()

        def mid():
            for t in range(n):
                fwd = passed(t)
                for j, chip in enumerate(chips):
                    copy(t, 1 + j, (*chip, c), me).wait_recv()
                    fwd[j].start()

        def finish():
            for t in range(n):
                copy(t, 0, sibling, me).wait_recv()
                for j, chip in enumerate(chips):
                    copy(t, 4 + j, (*chip, 1 - c), me).wait_recv()
                for cp in first(t) + passed(t):
                    cp.wait_send()
                own(t).wait()

        return start, mid, finish


class _ExchangeSide:
    def __init__(self, blocked):
        self.ins = list(blocked)
        self.outs = [jax.ShapeDtypeStruct(b.shape, b.dtype) for b in blocked]

    def scratch(self):
        n = len(self.ins)
        return [pltpu.SemaphoreType.DMA((7 * n,)), pltpu.SemaphoreType.DMA((7 * n,)),
                pltpu.SemaphoreType.DMA((n,))]

    def phases(self, srcs, dsts, send_sems, recv_sems, local_sems):
        n = len(self.ins)
        x, y, c = _my_place()
        me = 4 * x + 2 * y + c

        def own(t):
            return pltpu.make_async_copy(srcs[t].at[me], dsts[t].at[me], local_sems.at[t])

        def copies(t, arriving):
            res = []
            for k in range(1, N_DEV):
                px, py, pc = x ^ (k >> 2), y ^ ((k >> 1) & 1), c ^ (k & 1)
                peer = 4 * px + 2 * py + pc
                sem = 7 * t + k - 1
                res.append(pltpu.make_async_remote_copy(
                    src_ref=srcs[t].at[peer], dst_ref=dsts[t].at[peer if arriving else me],
                    send_sem=send_sems.at[sem], recv_sem=recv_sems.at[sem],
                    device_id=(px, py, pc), device_id_type=pl.DeviceIdType.MESH))
            return res

        def start():
            for t in range(n):
                own(t).start()
                for send in copies(t, False):
                    send.start()

        def mid():
            pass

        def finish():
            for t in range(n):
                for arrival in copies(t, True):
                    arrival.wait_recv()
                for send in copies(t, False):
                    send.wait_send()
                own(t).wait()

        return start, mid, finish


def _run_side(side, name):
    n_in, n_out = len(side.ins), len(side.outs)

    def body(*refs):
        start, mid, finish = side.phases(refs[:n_in], refs[n_in:n_in + n_out], *refs[n_in + n_out:])
        start()
        mid()
        finish()

    hbm = pl.BlockSpec(memory_space=pltpu.HBM)
    return pl.pallas_call(
        body, out_shape=list(side.outs), in_specs=[hbm] * n_in, out_specs=[hbm] * n_out,
        scratch_shapes=side.scratch(), name=name,
    )(*side.ins)


def _to_residue_major(a, dilation):
    s, w = a.shape
    return a.reshape(s // dilation, dilation, w).transpose(1, 0, 2).reshape(s, w)


def _from_residue_major(a, dilation):
    s, w = a.shape
    return a.reshape(dilation, s // dilation, w).transpose(1, 0, 2).reshape(s, w)


def _mem_kv(mem, gain, wkv, layer, tag):
    mem_n = _rmsnorm_fwd(mem, gain, "mem_norm_" + tag)
    mlen, d = mem.shape
    nb, _, bh, n = wkv.shape
    kv = _matmul(
        "mem_kv_" + tag, (1, nb),
        [(mem_n, pl.BlockSpec((mlen, bh), lambda i, j: (0, j)),
          wkv, pl.BlockSpec((None, None, bh, n), lambda i, j: (j, layer, 0, 0)), "nn", 0)],
        1, (mlen, n), [],
        [(jax.ShapeDtypeStruct((mlen, n), BF16), pl.BlockSpec((mlen, n), lambda i, j: (0, 0)))],
        _store_epilogue)[0]
    return mem_n, kv


def _mem_kv_bwd(mem, gain, mem_n, wkv, layer, dkv, tag):
    mlen, d = mem.shape
    nb, _, bh, n = wkv.shape
    dkvb = dkv.astype(BF16)
    dw = _mm_tn_rowblocked(mem_n, dkvb, "mem_kv_dw_" + tag, bh)
    dmem_n = _matmul(
        "mem_kv_dx_" + tag, (nb, 1),
        [(dkvb, pl.BlockSpec((mlen, n), lambda j, k: (0, 0)),
          wkv, pl.BlockSpec((None, None, bh, n), lambda j, k: (j, layer, 0, 0)), "nt", 0)],
        1, None, [],
        [(jax.ShapeDtypeStruct((mlen, d), F32), pl.BlockSpec((mlen, bh), lambda j, k: (0, j)))],
        _store_epilogue)[0]
    _, _, dgain = _rmsnorm_bwd(mem, gain, dmem_n, None, "mem_norm_bwd_" + tag)
    return dw, dgain


def _ffn_bwd(x, gain, w_gate, w_up, w_down, saved, dx, dxb, tag):
    hf, gate, up, hid = saved
    dgate, dup = _swiglu_bwd_hidden(dxb, w_down, gate, up, "swiglu_bwd_hidden_" + tag)
    dwd = _swiglu_bwd_w_down(hid, dxb, "swiglu_bwd_wdown_" + tag)
    dwg, (r_wd,) = _swiglu_bwd_w_in(hf, dgate, "swiglu_bwd_wgate_" + tag, _ExchangeSide([dwd]))
    dwu = _swiglu_bwd_w_in(hf, dup, "swiglu_bwd_wup_" + tag)
    dhf, (r_wg, r_wu) = _swiglu_bwd_input(dgate, dup, w_gate, w_up, "swiglu_bwd_input_" + tag,
                                          _ExchangeSide([dwg, dwu]))
    dx_new, dxb_new, dgain = _rmsnorm_bwd(x, gain, dhf, dx, "ffn_norm_bwd_" + tag)
    return dx_new, dxb_new, dgain, r_wg, r_wu, r_wd


def _local_step(x, mem, positions, target, w_attn_in, shards, small):
    s, d = x.shape
    tabs = _rotary_tables(positions)
    mix_norm, mem_norm, ffn_norm = small["mix_norm"], small["mem_norm"], small["ffn_norm"]

    h0 = _rmsnorm_fwd(x, mix_norm[0:1], "mix_norm_0")
    proj0, (w_mem_kv, w_attn_out, w_gate0) = _attn_in_proj(
        h0, w_attn_in, tabs, _GatherSide([shards["w_mem_kv"], shards["attn_w_out"], shards["w_gate"][0]]))
    qkv, offs, outs, lses = [], [], [], []
    for g, dil in enumerate(DILATIONS):
        if dil == 1:
            arr, off = proj0, (g, N_GROUPS + g, 2 * N_GROUPS + g)
        else:
            cols = [proj0[:, (p * N_GROUPS + g) * GROUP_W:(p * N_GROUPS + g + 1) * GROUP_W] for p in range(3)]
            arr, off = _to_residue_major(jnp.concatenate(cols, axis=1), dil), (0, 1, 2)
        o, lse = _attn_fwd(arr, arr, arr, off, s // dil // BLK, "attn_fwd_%d" % g)
        qkv.append(arr)
        offs.append(off)
        if dil > 1:
            o, lse = _from_residue_major(o, dil), _from_residue_major(lse, dil)
        outs.append(o)
        lses.append(lse)
    mix0 = _merge_fwd(outs, lses)
    qm_off0 = 3 * N_GROUPS
    mem_n0, kv0 = _mem_kv(mem, mem_norm[0:1], w_mem_kv, 0, "0")
    memo0 = _memattn_fwd(proj0, qm_off0, kv0, "memattn_fwd_0")
    cat0 = jnp.concatenate([mix0, memo0], axis=1)
    x1, (w_up0,) = _mm_nn_colblocked(cat0, w_attn_out, "attn_out_proj", x, _GatherSide([shards["w_up"][0]]))
    hf0 = _rmsnorm_fwd(x1, ffn_norm[0:1], "ffn_norm_0")
    (gate0, up0, hid0), (w_down0, w_sgu_in_blk, w_sgu_out_blk, w_gate1) = _swiglu_fwd(
        hf0, w_gate0, w_up0, "swiglu_fwd_0",
        _GatherSide([shards["w_down"][0], shards["sgu_w_in"], shards["sgu_w_out"], shards["w_gate"][1]]))
    x2, (w_up1, w_down1) = _swiglu_down(hid0, w_down0, x1, "swiglu_down_0",
                                        _GatherSide([shards["w_up"][1], shards["w_down"][1]]))
    ffn_saved0 = (hf0, gate0, up0, hid0)

    h1 = _rmsnorm_fwd(x2, mix_norm[1:2], "mix_norm_1")
    w_sgu_in = w_sgu_in_blk.transpose(1, 0, 2).reshape(d, -1)
    w_sgu_out = w_sgu_out_blk.reshape(-1, d)
    proj1 = _mm_nn(h1, w_sgu_in, "sgu_in_proj", w_sgu_in.shape[1] // 2)
    b_st = small["sgu_b_spatial"].T
    mix1 = _sgu_fwd(proj1, small["sgu_ln_g"], small["sgu_ln_b"], small["sgu_w_spatial"], b_st)
    qm_off1 = 2 * SGU_W // GROUP_W
    mem_n1, kv1 = _mem_kv(mem, mem_norm[1:2], w_mem_kv, 1, "1")
    memo1 = _memattn_fwd(proj1, qm_off1, kv1, "memattn_fwd_1")
    cat1 = jnp.concatenate([mix1, memo1], axis=1)
    x3 = _mm_nn(cat1, w_sgu_out, "sgu_out_proj", d // 2, out_dtype=F32, res=x2)
    hf1 = _rmsnorm_fwd(x3, ffn_norm[1:2], "ffn_norm_1")
    gate1, up1, hid1 = _swiglu_fwd(hf1, w_gate1, w_up1, "swiglu_fwd_1")
    x4 = _swiglu_down(hid1, w_down1, x3, "swiglu_down_1")
    ffn_saved1 = (hf1, gate1, up1, hid1)

    loss, dx, dxb, d_final = _loss_head(x4, small["final_norm"], target)

    recvs, sgrads = {}, {}
    dx, dxb, d_ffn1, r_wg1, r_wu1, r_wd1 = _ffn_bwd(x3, ffn_norm[1:2], w_gate1, w_up1, w_down1, ffn_saved1,
                                                    dx, dxb, "1")
    dcat1 = _mm_nt_rowblocked(dxb, w_sgu_out, "sgu_out_proj_dx", w_sgu_out.shape[0] // 2, BF16)
    dwsout = _mm_tn_rowblocked(cat1, dxb, "sgu_out_proj_dw", w_sgu_out.shape[0] // N_DEV)
    duv, dws, dbs, dlng, dlnb = _sgu_bwd(proj1, dcat1, small["sgu_ln_g"], small["sgu_ln_b"],
                                         small["sgu_w_spatial"], b_st)
    dqm1, dkv1 = _memattn_bwd(proj1, qm_off1, kv1, dcat1, SGU_W // GROUP_W, "memattn_bwd_1")
    dwkv1, d_memnorm1 = _mem_kv_bwd(mem, mem_norm[1:2], mem_n1, w_mem_kv, 1, dkv1, "1")
    dproj1 = jnp.concatenate([duv, dqm1], axis=1)
    n_cb = 7
    dwsin, (r_wsout, r_wkv1) = _mm_tn_colblocked(h1, dproj1, "sgu_in_proj_dw", dproj1.shape[1] // n_cb,
                                                 _ExchangeSide([dwsout, dwkv1]))
    dwsin = dwsin.transpose(1, 0, 2).reshape(d, N_DEV, -1).transpose(1, 0, 2)
    dh1, (r_wsin,) = _mm_nt(dproj1, w_sgu_in, "sgu_in_proj_dx", dproj1.shape[1] // 2, F32,
                            _ExchangeSide([dwsin]))
    dx, dxb, d_mix1 = _rmsnorm_bwd(x2, mix_norm[1:2], dh1, dx, "mix_norm_bwd_1")

    dx, dxb, d_ffn0, r_wg0, r_wu0, r_wd0 = _ffn_bwd(x1, ffn_norm[0:1], w_gate0, w_up0, w_down0, ffn_saved0,
                                                    dx, dxb, "0")
    dcat0 = _mm_nt_colblocked(dxb, w_attn_out, "attn_out_proj_dx", BF16, 4)
    dwout0 = _mm_tn_colblocked(cat0, dxb, "attn_out_proj_dw", w_attn_out.shape[2])
    dos_and_deltas = _merge_bwd(dcat0, outs, lses)
    dqkv = []
    for g, dil in enumerate(DILATIONS):
        do_g, dl_g = dos_and_deltas[g], dos_and_deltas[N_GROUPS + g]
        lse_g = lses[g]
        if dil > 1:
            do_g, dl_g, lse_g = (_to_residue_major(t, dil) for t in (do_g, dl_g, lse_g))
        args = (qkv[g], qkv[g], qkv[g], offs[g], do_g, lse_g, dl_g, s // dil // BLK)
        dq = _attn_bwd_dq(*args, "attn_bwd_dq_%d" % g)
        dk, dv = _attn_bwd_dkv(*args, "attn_bwd_dkv_%d" % g)
        t = jnp.concatenate([dq, dk, dv], axis=1)
        dqkv.append(_from_residue_major(t, dil) if dil > 1 else t)
    dqm0, dkv0 = _memattn_bwd(proj0, qm_off0, kv0, dcat0, 1, "memattn_bwd_0")
    dwkv0, d_memnorm0 = _mem_kv_bwd(mem, mem_norm[0:1], mem_n0, w_mem_kv, 0, dkv0, "0")
    dproj0 = _assemble_dproj(dqkv, dqm0, tabs)
    dwin0, (r_wout0, r_wkv0) = _mm_tn_colblocked(h0, dproj0, "attn_in_proj_dw", w_attn_in.shape[2],
                                                 _ExchangeSide([dwout0, dwkv0]))
    dh0, (r_win0,) = _mm_nt_colblocked(dproj0, w_attn_in, "attn_in_proj_dx", F32, 2, _ExchangeSide([dwin0]))
    grad_x, _, d_mix0 = _rmsnorm_bwd(x, mix_norm[0:1], dh0, dx, "mix_norm_bwd_0")

    recvs["w_gate"] = [r_wg0, r_wg1]
    recvs["w_up"] = [r_wu0, r_wu1]
    recvs["w_down"] = [r_wd0, r_wd1]
    recvs["w_mem_kv"] = [r_wkv0, r_wkv1]
    recvs["attn_w_in"] = [r_win0]
    recvs["attn_w_out"] = [r_wout0]
    recvs["sgu_w_in"] = [r_wsin]
    recvs["sgu_w_out"] = [r_wsout]
    sgrads["mix_norm"] = jnp.concatenate([d_mix0, d_mix1], axis=0)
    sgrads["mem_norm"] = jnp.concatenate([d_memnorm0, d_memnorm1], axis=0)
    sgrads["ffn_norm"] = jnp.concatenate([d_ffn0, d_ffn1], axis=0)
    sgrads["final_norm"] = d_final
    sgrads["sgu_w_spatial"] = dws
    sgrads["sgu_b_spatial"] = dbs[:, :SGU_GROUPS].T
    sgrads["sgu_ln_g"] = dlng
    sgrads["sgu_ln_b"] = dlnb
    return loss, grad_x, recvs, sgrads


BIG = ("w_mem_kv", "w_gate", "w_up", "w_down", "attn_w_in", "attn_w_out", "sgu_w_in", "sgu_w_out")
SMALL_REPLICATED = ("mix_norm", "mem_norm", "ffn_norm", "final_norm", "sgu_w_spatial", "sgu_b_spatial")
SMALL_SHARDED = ("sgu_ln_g", "sgu_ln_b")
WEIGHT_ORDER = ("mix_norm", "mem_norm", "w_mem_kv", "ffn_norm", "w_gate", "w_up", "w_down", "attn_w_in",
                "attn_w_out", "sgu_w_in", "sgu_ln_g", "sgu_ln_b", "sgu_w_spatial", "sgu_b_spatial",
                "sgu_w_out", "final_norm")
PACK_LANES = 128


def _pack(parts):
    flat = [p.reshape(-1) for p in parts]
    sizes = [f.shape[0] for f in flat]
    total = sum(sizes)
    rows = -(-total // PACK_LANES)
    rows = -(-rows // 8) * 8
    pad = rows * PACK_LANES - total
    packed = jnp.concatenate(flat + [jnp.zeros((pad,), F32)]).reshape(rows, PACK_LANES)
    offs, o = [], 0
    for sz in sizes:
        offs.append(o)
        o += sz
    return packed, offs, sizes


def _unpack(packed, offs, sizes, shapes):
    flat = packed.reshape(-1)
    return [flat[o:o + sz].reshape(shp) for o, sz, shp in zip(offs, sizes, shapes)]


def kernel(x, mem, positions, mix_norm, mem_norm, w_mem_kv, ffn_norm, w_gate, w_up, w_down, attn_w_in, attn_w_out, sgu_w_in, sgu_ln_g, sgu_ln_b, sgu_w_spatial, sgu_b_spatial, sgu_w_out, final_norm, loss_target, m_mix_norm, m_mem_norm, m_w_mem_kv, m_ffn_norm, m_w_gate, m_w_up, m_w_down, m_attn_w_in, m_attn_w_out, m_sgu_w_in, m_sgu_ln_g, m_sgu_ln_b, m_sgu_w_spatial, m_sgu_b_spatial, m_sgu_w_out, m_final_norm, v_mix_norm, v_mem_norm, v_w_mem_kv, v_ffn_norm, v_w_gate, v_w_up, v_w_down, v_attn_w_in, v_attn_w_out, v_sgu_w_in, v_sgu_ln_g, v_sgu_ln_b, v_sgu_w_spatial, v_sgu_b_spatial, v_sgu_w_out, v_final_norm):
    w = dict(mix_norm=mix_norm, mem_norm=mem_norm, w_mem_kv=w_mem_kv, ffn_norm=ffn_norm, w_gate=w_gate,
             w_up=w_up, w_down=w_down, attn_w_in=attn_w_in, attn_w_out=attn_w_out, sgu_w_in=sgu_w_in,
             sgu_ln_g=sgu_ln_g, sgu_ln_b=sgu_ln_b, sgu_w_spatial=sgu_w_spatial, sgu_b_spatial=sgu_b_spatial,
             sgu_w_out=sgu_w_out, final_norm=final_norm)
    mo = dict(mix_norm=m_mix_norm, mem_norm=m_mem_norm, w_mem_kv=m_w_mem_kv, ffn_norm=m_ffn_norm,
              w_gate=m_w_gate, w_up=m_w_up, w_down=m_w_down, attn_w_in=m_attn_w_in, attn_w_out=m_attn_w_out,
              sgu_w_in=m_sgu_w_in, sgu_ln_g=m_sgu_ln_g, sgu_ln_b=m_sgu_ln_b, sgu_w_spatial=m_sgu_w_spatial,
              sgu_b_spatial=m_sgu_b_spatial, sgu_w_out=m_sgu_w_out, final_norm=m_final_norm)
    vo = dict(mix_norm=v_mix_norm, mem_norm=v_mem_norm, w_mem_kv=v_w_mem_kv, ffn_norm=v_ffn_norm,
              w_gate=v_w_gate, w_up=v_w_up, w_down=v_w_down, attn_w_in=v_attn_w_in, attn_w_out=v_attn_w_out,
              sgu_w_in=v_sgu_w_in, sgu_ln_g=v_sgu_ln_g, sgu_ln_b=v_sgu_ln_b, sgu_w_spatial=v_sgu_w_spatial,
              sgu_b_spatial=v_sgu_b_spatial, sgu_w_out=v_sgu_w_out, final_norm=v_final_norm)
    me = 4 * lax.axis_index("x") + 2 * lax.axis_index("y") + lax.axis_index("c")
    d_model = x.shape[-1]

    shards = {
        "w_mem_kv": _cast_bf16(w_mem_kv, "cast_w_mem_kv"),
        "attn_w_out": _cast_bf16(attn_w_out[0], "cast_attn_w_out"),
        "sgu_w_in": _cast_bf16(sgu_w_in[0], "cast_sgu_w_in"),
        "sgu_w_out": _cast_bf16(sgu_w_out[0], "cast_sgu_w_out"),
    }
    for n in ("w_gate", "w_up", "w_down"):
        shards[n] = [_cast_bf16_layer(w[n], layer, "cast_%s_%d" % (n, layer)) for layer in range(w[n].shape[0])]
    ln_pack = jnp.concatenate([sgu_ln_g, sgu_ln_b], axis=0)
    w_attn_in, ln_all = _run_side(
        _GatherSide([_cast_bf16(attn_w_in[0], "cast_attn_w_in"), ln_pack]), "gather_attn_w_in")
    ln_full = ln_all.transpose(1, 0, 2).reshape(2, 1, -1)
    small = dict(mix_norm=mix_norm, mem_norm=mem_norm, ffn_norm=ffn_norm, final_norm=final_norm.reshape(1, -1),
                 sgu_w_spatial=sgu_w_spatial[0], sgu_b_spatial=sgu_b_spatial[0],
                 sgu_ln_g=ln_full[0], sgu_ln_b=ln_full[1])

    loss, grad_x, recvs, sgrads = _local_step(x[0], mem[0], positions[0], loss_target[0], w_attn_in, shards,
                                              small)
    loss = lax.psum(loss[0, 0], MESH_AXES)

    out_g, out_d, out_m, out_v = {}, {}, {}, {}
    for n in BIG:
        shard = w[n]
        w3 = shard.reshape(shard.shape[0], -1, shard.shape[-1])
        rs = [r.reshape(N_DEV, -1, shard.shape[-1]) for r in recvs[n]]
        res = _reduce_adam(rs, w3, mo[n].reshape(w3.shape), vo[n].reshape(w3.shape), "adam_" + n)
        out_g[n], out_d[n], out_m[n], out_v[n] = (r.reshape(shard.shape) for r in res)

    small_names = SMALL_REPLICATED + SMALL_SHARDED
    packed, offs, sizes = _pack([sgrads[n] for n in small_names])
    all_packs = _run_side(_GatherSide([packed]), "gather_small_grads")[0]
    rep_shapes = [w[n].shape for n in SMALL_REPLICATED]
    w_pack, w_offs, w_sizes = _pack([w[n] for n in SMALL_REPLICATED])
    m_pack, _, _ = _pack([mo[n] for n in SMALL_REPLICATED])
    v_pack, _, _ = _pack([vo[n] for n in SMALL_REPLICATED])
    n_rep_rows = w_pack.shape[0]
    res = _reduce_adam([all_packs[:, :n_rep_rows]], w_pack[None], m_pack[None], v_pack[None], "adam_small")
    for dst, r in zip((out_g, out_d, out_m, out_v), res):
        for n, val in zip(SMALL_REPLICATED, _unpack(r[0], w_offs, w_sizes, rep_shapes)):
            dst[n] = val
    ln_rows0 = offs[len(SMALL_REPLICATED)] // PACK_LANES
    ln_rows = 2 * SGU_W // PACK_LANES
    ln_sum = _reduce_adam([all_packs[:, ln_rows0:ln_rows0 + ln_rows]], jnp.zeros((1, ln_rows, PACK_LANES), F32),
                          jnp.zeros((1, ln_rows, PACK_LANES), F32), jnp.zeros((1, ln_rows, PACK_LANES), F32),
                          "sum_ln_grads")[0]
    ln_grads = ln_sum.reshape(2, N_DEV, -1)
    ln_mine = lax.dynamic_index_in_dim(ln_grads, me, axis=1, keepdims=False)
    w_ln = jnp.concatenate([sgu_ln_g, sgu_ln_b], axis=0)[None]
    m_ln = jnp.concatenate([m_sgu_ln_g, m_sgu_ln_b], axis=0)[None]
    v_ln = jnp.concatenate([v_sgu_ln_g, v_sgu_ln_b], axis=0)[None]
    res = _reduce_adam([ln_mine[None]], w_ln, m_ln, v_ln, "adam_ln")
    for dst, r in zip((out_g, out_d, out_m, out_v), res):
        dst["sgu_ln_g"], dst["sgu_ln_b"] = r[0, 0:1], r[0, 1:2]

    return (loss, grad_x[None], *[out_g[n] for n in WEIGHT_ORDER], *[out_d[n] for n in WEIGHT_ORDER],
            *[out_m[n] for n in WEIGHT_ORDER], *[out_v[n] for n in WEIGHT_ORDER])
```

```python
import functools

import jax
import jax.numpy as jnp
from jax import lax
from jax.experimental import pallas as pl
from jax.experimental.pallas import tpu as pltpu

F32 = jnp.float32
BF16 = jnp.bfloat16

N_DEV = 8
HEAD = 128
HPG = 4
GROUP_W = HPG * HEAD
N_GROUPS = 3
DILATIONS = (1, 4, 16)
BLK = 128
SGU_GROUPS = 12
SGU_W = SGU_GROUPS * HEAD
ROT_HALF = 16
ROPE_THETA = 500000.0
NORM_EPS = 1e-6
LN_EPS = 1e-5
NEG_INF = -1e30
SCALE = HEAD ** -0.5

ADAM_LR = 0.001
ADAM_B1 = 0.9
ADAM_B2 = 0.999
ADAM_EPS = 1e-08
ADAM_WD = 0.01
ADAM_STEP = 10

VMEM_LIMIT_V7X = 56 * 1024 * 1024
MESH_AXES = ("x", "y", "c")

_DN = {
    "nn": (((1,), (0,)), ((), ())),
    "nt": (((1,), (1,)), ((), ())),
    "tn": (((0,), (0,)), ((), ())),
}


def _dot(a, b, kind="nn"):
    return lax.dot_general(a, b, _DN[kind], preferred_element_type=F32)


def _params():
    return pltpu.CompilerParams(vmem_limit_bytes=VMEM_LIMIT_V7X)


def _row_tile(rows, cap):
    if rows <= cap:
        return rows
    t = cap - cap % 16
    while t >= 16:
        if rows % t == 0:
            return t
        t -= 16
    return rows


def _gelu(x):
    c = 0.7978845608028654
    return 0.5 * x * (1.0 + jnp.tanh(c * (x + 0.044715 * x * x * x)))


def _gelu_grad(x):
    c = 0.7978845608028654
    t = jnp.tanh(c * (x + 0.044715 * x * x * x))
    return 0.5 * (1.0 + t) + 0.5 * x * (1.0 - t * t) * c * (1.0 + 3.0 * 0.044715 * x * x)


def _sigmoid(x):
    return 1.0 / (1.0 + jnp.exp(-x))


def _matmul(name, grid, terms, n_acc, acc_shape, extras, outs, epilogue, side=None):
    nk = grid[-1]
    nt, ne, no = len(terms), len(extras), len(outs)
    kinds = [(t[4], t[5]) for t in terms]
    n_scratch_acc = 0 if nk == 1 else n_acc
    ns_in = len(side.ins) if side else 0
    ns_out = len(side.outs) if side else 0
    n_steps = 1
    for g in grid:
        n_steps *= g

    def body(*refs):
        pos = 0
        ab = refs[pos:pos + 2 * nt]
        pos += 2 * nt
        ex = refs[pos:pos + ne]
        pos += ne
        s_in = refs[pos:pos + ns_in]
        pos += ns_in
        out = refs[pos:pos + no]
        pos += no
        s_out = refs[pos:pos + ns_out]
        pos += ns_out
        accs = refs[pos:pos + n_scratch_acc]
        s_sems = refs[pos + n_scratch_acc:]
        if side:
            step = pl.program_id(0)
            for ax in range(1, len(grid)):
                step = step * grid[ax] + pl.program_id(ax)
            start, mid, finish = side.phases(s_in, s_out, *s_sems)
            pl.when(step == 0)(start)
        parts = [None] * n_acc
        for t, (kind, ai) in enumerate(kinds):
            a_ref, b_ref = ab[2 * t], ab[2 * t + 1]
            if len(b_ref.shape) == 2:
                pairs = [(a_ref[...], b_ref[...])]
            elif len(a_ref.shape) == 3:
                pairs = [(a_ref[q], b_ref[q]) for q in range(b_ref.shape[0])]
            else:
                bw = b_ref.shape[2]
                pairs = [(a_ref[:, q * bw:(q + 1) * bw], b_ref[q]) for q in range(b_ref.shape[0])]
            for a, b in pairs:
                p = _dot(a.astype(BF16), b.astype(BF16), kind)
                parts[ai] = p if parts[ai] is None else parts[ai] + p
        if nk == 1:
            epilogue(parts, ex, out)
        else:
            k = pl.program_id(len(grid) - 1)

            @pl.when(k == 0)
            def _():
                for ai in range(n_acc):
                    accs[ai][...] = parts[ai]

            @pl.when(k > 0)
            def _():
                for ai in range(n_acc):
                    accs[ai][...] += parts[ai]

            @pl.when(k == nk - 1)
            def _():
                epilogue([a[...] for a in accs], ex, out)

        if side:
            pl.when(step == (3 * n_steps) // 4)(mid)
            pl.when(step == n_steps - 1)(finish)

    hbm = pl.BlockSpec(memory_space=pltpu.HBM)
    in_specs, args = [], []
    for (a, a_spec, b, b_spec, _, _) in terms:
        in_specs += [a_spec, b_spec]
        args += [a, b]
    for (e, e_spec) in extras:
        in_specs.append(e_spec)
        args.append(e)
    scratch = [pltpu.VMEM(acc_shape, F32) for _ in range(n_scratch_acc)]
    out_specs = [o[1] for o in outs]
    out_shape = [o[0] for o in outs]
    if side:
        in_specs += [hbm] * ns_in
        args += list(side.ins)
        out_specs += [hbm] * ns_out
        out_shape += list(side.outs)
        scratch += side.scratch()
    res = pl.pallas_call(
        body, grid=grid, in_specs=in_specs, out_specs=out_specs, out_shape=out_shape,
        scratch_shapes=scratch, compiler_params=_params(), name=name,
    )(*args)
    return res if side is None else (res[:no], res[no:])


def _one(res, side):
    return res[0] if side is None else (res[0][0], res[1])


def _store_epilogue(parts, ex, out):
    out[0][...] = parts[0].astype(out[0].dtype)


def _residual_epilogue(parts, ex, out):
    out[0][...] = (parts[0] + ex[0][...]).astype(out[0].dtype)


def _rmsnorm_fwd(x, g, name):
    rows, d = x.shape
    tm = _row_tile(rows, 512)

    def body(x_ref, g_ref, o_ref):
        xf = x_ref[...]
        r = lax.rsqrt(jnp.mean(xf * xf, axis=-1, keepdims=True) + NORM_EPS)
        o_ref[...] = (xf * r * g_ref[...]).astype(o_ref.dtype)

    return pl.pallas_call(
        body, grid=(rows // tm,),
        in_specs=[pl.BlockSpec((tm, d), lambda i: (i, 0)), pl.BlockSpec((1, d), lambda i: (0, 0))],
        out_specs=pl.BlockSpec((tm, d), lambda i: (i, 0)),
        out_shape=jax.ShapeDtypeStruct((rows, d), BF16),
        compiler_params=_params(), name=name,
    )(x, g)


def _rmsnorm_bwd(x, g, dh, dres, name):
    rows, d = x.shape
    tm = _row_tile(rows, 256)
    has_res = dres is not None

    def body(*refs):
        if has_res:
            x_ref, g_ref, dh_ref, dres_ref, dx_ref, dxb_ref, dg_ref = refs
        else:
            x_ref, g_ref, dh_ref, dx_ref, dxb_ref, dg_ref = refs
        i = pl.program_id(0)
        xf = x_ref[...]
        r = lax.rsqrt(jnp.mean(xf * xf, axis=-1, keepdims=True) + NORM_EPS)
        xhat = xf * r
        dy = dh_ref[...]
        gdy = dy * g_ref[...]
        c = jnp.mean(gdy * xhat, axis=-1, keepdims=True)
        dx = r * (gdy - xhat * c)
        if has_res:
            dx = dx + dres_ref[...]
        dx_ref[...] = dx
        dxb_ref[...] = dx.astype(BF16)

        @pl.when(i == 0)
        def _():
            dg_ref[...] = jnp.zeros_like(dg_ref)

        dg_ref[...] += jnp.sum(dy * xhat, axis=0, keepdims=True)

    row_spec = pl.BlockSpec((tm, d), lambda i: (i, 0))
    vec_spec = pl.BlockSpec((1, d), lambda i: (0, 0))
    in_specs = [row_spec, vec_spec, row_spec] + ([row_spec] if has_res else [])
    args = [x, g, dh] + ([dres] if has_res else [])
    return pl.pallas_call(
        body, grid=(rows // tm,), in_specs=in_specs,
        out_specs=[row_spec, row_spec, vec_spec],
        out_shape=[jax.ShapeDtypeStruct((rows, d), F32), jax.ShapeDtypeStruct((rows, d), BF16),
                   jax.ShapeDtypeStruct((1, d), F32)],
        compiler_params=_params(), name=name,
    )(*args)


def _loss_head(x, g, target):
    rows, d = x.shape
    tm = _row_tile(rows, 256)

    def body(x_ref, g_ref, t_ref, loss_ref, dx_ref, dxb_ref, dg_ref):
        i = pl.program_id(0)
        xf = x_ref[...]
        gv = g_ref[...]
        r = lax.rsqrt(jnp.mean(xf * xf, axis=-1, keepdims=True) + NORM_EPS)
        xhat = xf * r
        err = xhat * gv - t_ref[...]
        row_loss = jnp.mean(err * err, axis=-1, keepdims=True)
        dy = err * (1.0 / d)
        gdy = dy * gv
        c = jnp.mean(gdy * xhat, axis=-1, keepdims=True)
        dx = r * (gdy - xhat * c)
        dx_ref[...] = dx
        dxb_ref[...] = dx.astype(BF16)

        @pl.when(i == 0)
        def _():
            dg_ref[...] = jnp.zeros_like(dg_ref)
            loss_ref[...] = jnp.zeros_like(loss_ref)

        dg_ref[...] += jnp.sum(dy * xhat, axis=0, keepdims=True)
        loss_ref[...] += 0.5 * jnp.sum(row_loss, axis=0, keepdims=True)

    row_spec = pl.BlockSpec((tm, d), lambda i: (i, 0))
    vec_spec = pl.BlockSpec((1, d), lambda i: (0, 0))
    return pl.pallas_call(
        body, grid=(rows // tm,), in_specs=[row_spec, vec_spec, row_spec],
        out_specs=[pl.BlockSpec((1, 1), lambda i: (0, 0)), row_spec, row_spec, vec_spec],
        out_shape=[jax.ShapeDtypeStruct((1, 1), F32), jax.ShapeDtypeStruct((rows, d), F32),
                   jax.ShapeDtypeStruct((rows, d), BF16), jax.ShapeDtypeStruct((1, d), F32)],
        compiler_params=_params(), name="loss_head",
    )(x, g, target)


def _rotary_tables(positions):
    inv_freq = ROPE_THETA ** (-jnp.arange(ROT_HALF, dtype=F32) / ROT_HALF)
    ang = positions.astype(F32)[:, None] * inv_freq
    cos, sin = jnp.cos(ang), jnp.sin(ang)
    s = positions.shape[0]
    z = jnp.zeros((s, HEAD - 2 * ROT_HALF), F32)
    z16 = jnp.zeros((s, ROT_HALF), F32)
    c = jnp.concatenate([cos, cos, jnp.ones_like(z)], axis=1)
    s1 = jnp.concatenate([z16, sin, z], axis=1)
    s2 = jnp.concatenate([-sin, z16, z], axis=1)
    return c, s1, s2


def _attn_in_proj(h, w_blk, tabs, side=None):
    s, d = h.shape
    nb, _, bw = w_blk.shape
    heads_per_blk = bw // HEAD
    n_rot_heads = 2 * N_GROUPS * HPG
    tm = _row_tile(s, 512)

    def epilogue(parts, ex, out):
        j = pl.program_id(0)
        acc = parts[0]
        c, s1, s2 = ex[0][...], ex[1][...], ex[2][...]
        for t in range(heads_per_blk):
            seg = acc[:, t * HEAD:(t + 1) * HEAD]
            rot = seg * c + pltpu.roll(seg, ROT_HALF, 1) * s1 + pltpu.roll(seg, HEAD - ROT_HALF, 1) * s2
            is_rot = (j * heads_per_blk + t) < n_rot_heads
            out[0][:, t * HEAD:(t + 1) * HEAD] = jnp.where(is_rot, rot, seg).astype(BF16)

    tab_spec = pl.BlockSpec((tm, HEAD), lambda j, m, k: (m, 0))
    return _one(_matmul(
        "attn_in_proj", (nb, s // tm, 1),
        [(h, pl.BlockSpec((tm, d), lambda j, m, k: (m, 0)),
          w_blk, pl.BlockSpec((None, d, bw), lambda j, m, k: (j, 0, 0)), "nn", 0)],
        1, None, [(tabs[0], tab_spec), (tabs[1], tab_spec), (tabs[2], tab_spec)],
        [(jax.ShapeDtypeStruct((s, nb * bw), BF16), pl.BlockSpec((tm, bw), lambda j, m, k: (m, j)))],
        epilogue, side), side)


ATT_TILE_BLOCKS = 4


def _att_blocks(seq_blocks):
    return min(ATT_TILE_BLOCKS, seq_blocks)


def _band_masks():
    qi = lax.broadcasted_iota(jnp.int32, (BLK, BLK), 0)
    ki = lax.broadcasted_iota(jnp.int32, (BLK, BLK), 1)
    return ki <= qi, ki >= qi


def _attn_fwd(q_arr, k_arr, v_arr, offs, seq_blocks, name):
    s = q_arr.shape[0]
    qo, ko, vo = offs
    nb = _att_blocks(seq_blocks)

    def body(q_ref, kc_ref, kp_ref, vc_ref, vp_ref, o_ref, lse_ref):
        n = pl.program_id(0)
        tile_starts_seq = (n * nb) % seq_blocks == 0
        mask_c, mask_p = _band_masks()
        for b in range(nb):
            r0 = b * BLK
            for h in range(HPG):
                c0 = h * HEAD
                q = q_ref[r0:r0 + BLK, c0:c0 + HEAD]
                kc = kc_ref[r0:r0 + BLK, c0:c0 + HEAD]
                vc = vc_ref[r0:r0 + BLK, c0:c0 + HEAD]
                if b == 0:
                    kp = kp_ref[:, c0:c0 + HEAD]
                    vp = vp_ref[:, c0:c0 + HEAD]
                else:
                    kp = kc_ref[r0 - BLK:r0, c0:c0 + HEAD]
                    vp = vc_ref[r0 - BLK:r0, c0:c0 + HEAD]
                s_c = jnp.where(mask_c, _dot(q, kc, "nt") * SCALE, NEG_INF)
                s_p = jnp.where(mask_p, _dot(q, kp, "nt") * SCALE, NEG_INF)
                if b == 0:
                    s_p = jnp.where(tile_starts_seq, NEG_INF, s_p)
                m = jnp.maximum(jnp.max(s_c, axis=-1, keepdims=True), jnp.max(s_p, axis=-1, keepdims=True))
                p_c = jnp.exp(s_c - m)
                p_p = jnp.exp(s_p - m)
                l = jnp.sum(p_c, axis=-1, keepdims=True) + jnp.sum(p_p, axis=-1, keepdims=True)
                inv = 1.0 / l
                o = _dot((p_c * inv).astype(BF16), vc) + _dot((p_p * inv).astype(BF16), vp)
                o_ref[r0:r0 + BLK, c0:c0 + HEAD] = o.astype(BF16)
                lse_ref[r0:r0 + BLK, c0:c0 + HEAD] = jnp.broadcast_to(m + jnp.log(l), (BLK, HEAD))

    def cur(off):
        return pl.BlockSpec((nb * BLK, GROUP_W), lambda n: (n, off))

    def prev(off):
        return pl.BlockSpec((BLK, GROUP_W), lambda n: (jnp.maximum(n * nb - 1, 0), off))

    return pl.pallas_call(
        body, grid=(s // (nb * BLK),),
        in_specs=[cur(qo), cur(ko), prev(ko), cur(vo), prev(vo)],
        out_specs=[cur(0), cur(0)],
        out_shape=[jax.ShapeDtypeStruct((s, GROUP_W), BF16), jax.ShapeDtypeStruct((s, GROUP_W), F32)],
        compiler_params=_params(), name=name,
    )(q_arr, k_arr, k_arr, v_arr, v_arr)


def _attn_bwd_dq(q_arr, k_arr, v_arr, offs, do, lse, delta, seq_blocks, name):
    s = q_arr.shape[0]
    qo, ko, vo = offs
    nb = _att_blocks(seq_blocks)

    def body(q_ref, kc_ref, kp_ref, vc_ref, vp_ref, do_ref, lse_ref, dl_ref, dq_ref):
        n = pl.program_id(0)
        tile_starts_seq = (n * nb) % seq_blocks == 0
        mask_c, mask_p = _band_masks()
        for b in range(nb):
            r0 = b * BLK
            for h in range(HPG):
                c0 = h * HEAD
                q = q_ref[r0:r0 + BLK, c0:c0 + HEAD]
                kc = kc_ref[r0:r0 + BLK, c0:c0 + HEAD]
                vc = vc_ref[r0:r0 + BLK, c0:c0 + HEAD]
                if b == 0:
                    kp = kp_ref[:, c0:c0 + HEAD]
                    vp = vp_ref[:, c0:c0 + HEAD]
                else:
                    kp = kc_ref[r0 - BLK:r0, c0:c0 + HEAD]
                    vp = vc_ref[r0 - BLK:r0, c0:c0 + HEAD]
                dob = do_ref[r0:r0 + BLK, c0:c0 + HEAD]
                lse_b = lse_ref[r0:r0 + BLK, c0:c0 + HEAD]
                dl_b = dl_ref[r0:r0 + BLK, c0:c0 + HEAD]
                s_c = jnp.where(mask_c, _dot(q, kc, "nt") * SCALE, NEG_INF)
                s_p = jnp.where(mask_p, _dot(q, kp, "nt") * SCALE, NEG_INF)
                if b == 0:
                    s_p = jnp.where(tile_starts_seq, NEG_INF, s_p)
                p_c = jnp.exp(s_c - lse_b)
                p_p = jnp.exp(s_p - lse_b)
                ds_c = p_c * (_dot(dob, vc, "nt") - dl_b) * SCALE
                ds_p = p_p * (_dot(dob, vp, "nt") - dl_b) * SCALE
                dq = _dot(ds_c.astype(BF16), kc) + _dot(ds_p.astype(BF16), kp)
                dq_ref[r0:r0 + BLK, c0:c0 + HEAD] = dq.astype(BF16)

    def cur(off):
        return pl.BlockSpec((nb * BLK, GROUP_W), lambda n: (n, off))

    def prev(off):
        return pl.BlockSpec((BLK, GROUP_W), lambda n: (jnp.maximum(n * nb - 1, 0), off))

    return pl.pallas_call(
        body, grid=(s // (nb * BLK),),
        in_specs=[cur(qo), cur(ko), prev(ko), cur(vo), prev(vo), cur(0), cur(0), cur(0)],
        out_specs=cur(0),
        out_shape=jax.ShapeDtypeStruct((s, GROUP_W), BF16),
        compiler_params=_params(), name=name,
    )(q_arr, k_arr, k_arr, v_arr, v_arr, do, lse, delta)


def _attn_bwd_dkv(q_arr, k_arr, v_arr, offs, do, lse, delta, seq_blocks, name):
    s = q_arr.shape[0]
    qo, ko, vo = offs
    nb = _att_blocks(seq_blocks)
    n_blocks = s // BLK

    def body(k_ref, v_ref, qc_ref, qn_ref, doc_ref, don_ref, lsec_ref, lsen_ref, dlc_ref, dln_ref,
             dk_ref, dv_ref):
        n = pl.program_id(0)
        next_in_seq = ((n + 1) * nb) % seq_blocks != 0
        mask_c, mask_p = _band_masks()
        for b in range(nb):
            r0 = b * BLK
            for h in range(HPG):
                c0 = h * HEAD
                k = k_ref[r0:r0 + BLK, c0:c0 + HEAD]
                v = v_ref[r0:r0 + BLK, c0:c0 + HEAD]
                q = qc_ref[r0:r0 + BLK, c0:c0 + HEAD]
                dob = doc_ref[r0:r0 + BLK, c0:c0 + HEAD]
                s_c = jnp.where(mask_c, _dot(q, k, "nt") * SCALE, NEG_INF)
                p_c = jnp.exp(s_c - lsec_ref[r0:r0 + BLK, c0:c0 + HEAD])
                ds_c = p_c * (_dot(dob, v, "nt") - dlc_ref[r0:r0 + BLK, c0:c0 + HEAD]) * SCALE
                dv = _dot(p_c.astype(BF16), dob, "tn")
                dk = _dot(ds_c.astype(BF16), q, "tn")
                if b < nb - 1:
                    r1 = r0 + BLK
                    q2 = qc_ref[r1:r1 + BLK, c0:c0 + HEAD]
                    do2 = doc_ref[r1:r1 + BLK, c0:c0 + HEAD]
                    lse2 = lsec_ref[r1:r1 + BLK, c0:c0 + HEAD]
                    dl2 = dlc_ref[r1:r1 + BLK, c0:c0 + HEAD]
                else:
                    q2 = qn_ref[:, c0:c0 + HEAD]
                    do2 = don_ref[:, c0:c0 + HEAD]
                    lse2 = lsen_ref[:, c0:c0 + HEAD]
                    dl2 = dln_ref[:, c0:c0 + HEAD]
                s_p = jnp.where(mask_p, _dot(q2, k, "nt") * SCALE, NEG_INF)
                if b == nb - 1:
                    s_p = jnp.where(next_in_seq, s_p, NEG_INF)
                p_p = jnp.exp(s_p - lse2)
                ds_p = p_p * (_dot(do2, v, "nt") - dl2) * SCALE
                dv = dv + _dot(p_p.astype(BF16), do2, "tn")
                dk = dk + _dot(ds_p.astype(BF16), q2, "tn")
                dk_ref[r0:r0 + BLK, c0:c0 + HEAD] = dk.astype(BF16)
                dv_ref[r0:r0 + BLK, c0:c0 + HEAD] = dv.astype(BF16)

    def cur(off):
        return pl.BlockSpec((nb * BLK, GROUP_W), lambda n: (n, off))

    def nxt(off):
        return pl.BlockSpec((BLK, GROUP_W), lambda n: (jnp.minimum((n + 1) * nb, n_blocks - 1), off))

    return pl.pallas_call(
        body, grid=(s // (nb * BLK),),
        in_specs=[cur(ko), cur(vo), cur(qo), nxt(qo), cur(0), nxt(0), cur(0), nxt(0), cur(0), nxt(0)],
        out_specs=[cur(0), cur(0)],
        out_shape=[jax.ShapeDtypeStruct((s, GROUP_W), BF16), jax.ShapeDtypeStruct((s, GROUP_W), BF16)],
        compiler_params=_params(), name=name,
    )(k_arr, v_arr, q_arr, q_arr, do, do, lse, lse, delta, delta)


def _merge_weights(lse_refs, c0):
    ls = [r[:, c0:c0 + HEAD] for r in lse_refs]
    m = jnp.maximum(jnp.maximum(ls[0], ls[1]), ls[2])
    es = [jnp.exp(l - m) for l in ls]
    inv = 1.0 / (es[0] + es[1] + es[2])
    return [e * inv for e in es]


def _merge_fwd(os_, lses):
    s = os_[0].shape[0]
    tm = _row_tile(s, 512)

    def body(o0, o1, o2, l0, l1, l2, out_ref):
        for h in range(HPG):
            c0 = h * HEAD
            w = _merge_weights((l0, l1, l2), c0)
            acc = None
            for wg, o in zip(w, (o0, o1, o2)):
                t = wg * o[:, c0:c0 + HEAD].astype(F32)
                acc = t if acc is None else acc + t
            out_ref[:, c0:c0 + HEAD] = acc.astype(BF16)

    spec = pl.BlockSpec((tm, GROUP_W), lambda i: (i, 0))
    return pl.pallas_call(
        body, grid=(s // tm,), in_specs=[spec] * 6, out_specs=spec,
        out_shape=jax.ShapeDtypeStruct((s, GROUP_W), BF16),
        compiler_params=_params(), name="merge_fwd",
    )(*os_, *lses)


def _merge_bwd(dcat, os_, lses):
    s = os_[0].shape[0]
    tm = _row_tile(s, 512)

    def body(d_ref, o0, o1, o2, l0, l1, l2, do0, do1, do2, dl0, dl1, dl2):
        for h in range(HPG):
            c0 = h * HEAD
            w = _merge_weights((l0, l1, l2), c0)
            dm = d_ref[:, c0:c0 + HEAD].astype(F32)
            merged = None
            for wg, o in zip(w, (o0, o1, o2)):
                t = wg * o[:, c0:c0 + HEAD].astype(F32)
                merged = t if merged is None else merged + t
            abar = jnp.sum(dm * merged, axis=-1, keepdims=True)
            for wg, do_ref, dl_ref in zip(w, (do0, do1, do2), (dl0, dl1, dl2)):
                do_ref[:, c0:c0 + HEAD] = (wg * dm).astype(BF16)
                dl_ref[:, c0:c0 + HEAD] = wg * abar

    spec = pl.BlockSpec((tm, GROUP_W), lambda i: (i, 0))
    return pl.pallas_call(
        body, grid=(s // tm,), in_specs=[spec] * 7, out_specs=[spec] * 6,
        out_shape=[jax.ShapeDtypeStruct((s, GROUP_W), BF16)] * 3 + [jax.ShapeDtypeStruct((s, GROUP_W), F32)] * 3,
        compiler_params=_params(), name="merge_bwd",
    )(dcat, *os_, *lses)


def _assemble_dproj(dqkv, dqm, tabs):
    s = dqm.shape[0]
    tm = _row_tile(s, 256)
    width = 3 * N_GROUPS * GROUP_W + GROUP_W

    def body(d0, d1, d2, dm_ref, c_ref, s1_ref, s2_ref, out_ref):
        c, s1, s2 = c_ref[...], s1_ref[...], s2_ref[...]
        for g, d_ref in enumerate((d0, d1, d2)):
            for part in range(3):
                for h in range(HPG):
                    src = part * GROUP_W + h * HEAD
                    dst = part * N_GROUPS * GROUP_W + g * GROUP_W + h * HEAD
                    seg = d_ref[:, src:src + HEAD]
                    if part < 2:
                        t = seg.astype(F32)
                        t = t * c - pltpu.roll(t, HEAD - ROT_HALF, 1) * s2 - pltpu.roll(t, ROT_HALF, 1) * s1
                        seg = t.astype(BF16)
                    out_ref[:, dst:dst + HEAD] = seg
        out_ref[:, 3 * N_GROUPS * GROUP_W:] = dm_ref[...]

    g_spec = pl.BlockSpec((tm, 3 * GROUP_W), lambda i: (i, 0))
    m_spec = pl.BlockSpec((tm, GROUP_W), lambda i: (i, 0))
    t_spec = pl.BlockSpec((tm, HEAD), lambda i: (i, 0))
    return pl.pallas_call(
        body, grid=(s // tm,), in_specs=[g_spec] * 3 + [m_spec] + [t_spec] * 3,
        out_specs=pl.BlockSpec((tm, width), lambda i: (i, 0)),
        out_shape=jax.ShapeDtypeStruct((s, width), BF16),
        compiler_params=_params(), name="assemble_dproj",
    )(*dqkv, dqm, *tabs)


def _mem_softmax(q, k):
    s = _dot(q, k, "nt") * SCALE
    m = jnp.max(s, axis=-1, keepdims=True)
    p = jnp.exp(s - m)
    return p * (1.0 / jnp.sum(p, axis=-1, keepdims=True))


def _memattn_fwd(q_arr, q_off, kv, name):
    s = q_arr.shape[0]
    mlen = kv.shape[0]
    tq = _row_tile(s, 512)

    def body(q_ref, kv_ref, o_ref):
        for h in range(HPG):
            c0 = h * HEAD
            p = _mem_softmax(q_ref[:, c0:c0 + HEAD], kv_ref[:, c0:c0 + HEAD])
            o = _dot(p.astype(BF16), kv_ref[:, GROUP_W + c0:GROUP_W + c0 + HEAD])
            o_ref[:, c0:c0 + HEAD] = o.astype(BF16)

    return pl.pallas_call(
        body, grid=(s // tq,),
        in_specs=[pl.BlockSpec((tq, GROUP_W), lambda i: (i, q_off)),
                  pl.BlockSpec((mlen, 2 * GROUP_W), lambda i: (0, 0))],
        out_specs=pl.BlockSpec((tq, GROUP_W), lambda i: (i, 0)),
        out_shape=jax.ShapeDtypeStruct((s, GROUP_W), BF16),
        compiler_params=_params(), name=name,
    )(q_arr, kv)


def _memattn_bwd(q_arr, q_off, kv, dcat, d_off, name):
    s = q_arr.shape[0]
    mlen = kv.shape[0]
    tq = _row_tile(s, 512)

    def body(q_ref, kv_ref, d_ref, dq_ref, dkv_ref):
        i = pl.program_id(0)

        @pl.when(i == 0)
        def _():
            dkv_ref[...] = jnp.zeros_like(dkv_ref)

        for h in range(HPG):
            c0 = h * HEAD
            q = q_ref[:, c0:c0 + HEAD]
            k = kv_ref[:, c0:c0 + HEAD]
            v = kv_ref[:, GROUP_W + c0:GROUP_W + c0 + HEAD]
            do = d_ref[:, c0:c0 + HEAD]
            p = _mem_softmax(q, k)
            dp = _dot(do, v, "nt")
            ds = p * (dp - jnp.sum(p * dp, axis=-1, keepdims=True)) * SCALE
            dsb = ds.astype(BF16)
            dq_ref[:, c0:c0 + HEAD] = _dot(dsb, k).astype(BF16)
            dkv_ref[:, c0:c0 + HEAD] += _dot(dsb, q, "tn")
            dkv_ref[:, GROUP_W + c0:GROUP_W + c0 + HEAD] += _dot(p.astype(BF16), do, "tn")

    return pl.pallas_call(
        body, grid=(s // tq,),
        in_specs=[pl.BlockSpec((tq, GROUP_W), lambda i: (i, q_off)),
                  pl.BlockSpec((mlen, 2 * GROUP_W), lambda i: (0, 0)),
                  pl.BlockSpec((tq, GROUP_W), lambda i: (i, d_off))],
        out_specs=[pl.BlockSpec((tq, GROUP_W), lambda i: (i, 0)),
                   pl.BlockSpec((mlen, 2 * GROUP_W), lambda i: (0, 0))],
        out_shape=[jax.ShapeDtypeStruct((s, GROUP_W), BF16), jax.ShapeDtypeStruct((mlen, 2 * GROUP_W), F32)],
        compiler_params=_params(), name=name,
    )(q_arr, kv, dcat)


SGU_TILE = 256


def _sgu_norm(v):
    vg = _gelu(v)
    mu = jnp.mean(vg, axis=-1, keepdims=True)
    xc = vg - mu
    var = jnp.mean(xc * xc, axis=-1, keepdims=True)
    rstd = lax.rsqrt(var + LN_EPS)
    return xc * rstd, rstd


def _tril_mask():
    r = lax.broadcasted_iota(jnp.int32, (BLK, BLK), 0)
    c = lax.broadcasted_iota(jnp.int32, (BLK, BLK), 1)
    return r >= c


def _sgu_fwd(proj, ln_g, ln_b, w_s, b_st):
    s = proj.shape[0]
    ts = _row_tile(s, SGU_TILE)

    def body(u_ref, v_ref, g_ref, b_ref, ws_ref, bst_ref, o_ref):
        ug = _gelu(u_ref[...].astype(F32))
        xhat, _ = _sgu_norm(v_ref[...].astype(F32))
        vn = (xhat * g_ref[...] + b_ref[...]).astype(BF16)
        tri = _tril_mask()
        for g in range(SGU_GROUPS):
            c0 = g * HEAD
            w = jnp.where(tri, ws_ref[g], 0.0).astype(BF16)
            bias = bst_ref[:, g:g + 1]
            for ch in range(ts // BLK):
                r0 = ch * BLK
                mixed = _dot(w, vn[r0:r0 + BLK, c0:c0 + HEAD]) + bias
                o_ref[r0:r0 + BLK, c0:c0 + HEAD] = (ug[r0:r0 + BLK, c0:c0 + HEAD] * mixed).astype(BF16)

    vec = pl.BlockSpec((1, SGU_W), lambda i: (0, 0))
    return pl.pallas_call(
        body, grid=(s // ts,),
        in_specs=[pl.BlockSpec((ts, SGU_W), lambda i: (i, 0)), pl.BlockSpec((ts, SGU_W), lambda i: (i, 1)),
                  vec, vec, pl.BlockSpec((SGU_GROUPS, BLK, BLK), lambda i: (0, 0, 0)),
                  pl.BlockSpec((BLK, SGU_GROUPS), lambda i: (0, 0))],
        out_specs=pl.BlockSpec((ts, SGU_W), lambda i: (i, 0)),
        out_shape=jax.ShapeDtypeStruct((s, SGU_W), BF16),
        compiler_params=_params(), name="sgu_fwd",
    )(proj, proj, ln_g, ln_b, w_s, b_st)


def _sgu_bwd(proj, dcat, ln_g, ln_b, w_s, b_st):
    s = proj.shape[0]
    ts = _row_tile(s, SGU_TILE)

    def body(u_ref, v_ref, d_ref, g_ref, b_ref, ws_ref, bst_ref,
             duv_ref, dws_ref, dbs_ref, dg_ref, db_ref, dvn_ref):
        i = pl.program_id(0)

        @pl.when(i == 0)
        def _():
            dws_ref[...] = jnp.zeros_like(dws_ref)
            dbs_ref[...] = jnp.zeros_like(dbs_ref)
            dg_ref[...] = jnp.zeros_like(dg_ref)
            db_ref[...] = jnp.zeros_like(db_ref)

        u = u_ref[...].astype(F32)
        v = v_ref[...].astype(F32)
        ug = _gelu(u)
        xhat, rstd = _sgu_norm(v)
        lng = g_ref[...]
        vn = (xhat * lng + b_ref[...]).astype(BF16)
        dout = d_ref[...].astype(F32)
        tri = _tril_mask()
        lane = lax.broadcasted_iota(jnp.int32, (BLK, BLK), 1)
        dbs = jnp.zeros((BLK, BLK), F32)
        for g in range(SGU_GROUPS):
            c0 = g * HEAD
            w = jnp.where(tri, ws_ref[g], 0.0).astype(BF16)
            bias = bst_ref[:, g:g + 1]
            dws = jnp.zeros((BLK, BLK), F32)
            for ch in range(ts // BLK):
                r0 = ch * BLK
                vn_gc = vn[r0:r0 + BLK, c0:c0 + HEAD]
                mixed = _dot(w, vn_gc) + bias
                do_gc = dout[r0:r0 + BLK, c0:c0 + HEAD]
                dmixed = do_gc * ug[r0:r0 + BLK, c0:c0 + HEAD]
                du = do_gc * mixed * _gelu_grad(u[r0:r0 + BLK, c0:c0 + HEAD])
                duv_ref[r0:r0 + BLK, c0:c0 + HEAD] = du.astype(BF16)
                dmb = dmixed.astype(BF16)
                dws = dws + _dot(dmb, vn_gc, "nt")
                dbs = dbs + jnp.where(lane == g, jnp.sum(dmixed, axis=-1, keepdims=True), 0.0)
                dvn_ref[r0:r0 + BLK, c0:c0 + HEAD] = _dot(w, dmb, "tn")
            dws_ref[g] += jnp.where(tri, dws, 0.0)
        dbs_ref[...] += dbs
        dvn = dvn_ref[...]
        gd = dvn * lng
        c1 = jnp.mean(gd, axis=-1, keepdims=True)
        c2 = jnp.mean(gd * xhat, axis=-1, keepdims=True)
        dvg = rstd * (gd - c1 - xhat * c2)
        duv_ref[:, SGU_W:] = (dvg * _gelu_grad(v)).astype(BF16)
        dg_ref[...] += jnp.sum(dvn * xhat, axis=0, keepdims=True)
        db_ref[...] += jnp.sum(dvn, axis=0, keepdims=True)

    vec = pl.BlockSpec((1, SGU_W), lambda i: (0, 0))
    ws_spec = pl.BlockSpec((SGU_GROUPS, BLK, BLK), lambda i: (0, 0, 0))
    return pl.pallas_call(
        body, grid=(s // ts,),
        in_specs=[pl.BlockSpec((ts, SGU_W), lambda i: (i, 0)), pl.BlockSpec((ts, SGU_W), lambda i: (i, 1)),
                  pl.BlockSpec((ts, SGU_W), lambda i: (i, 0)),
                  vec, vec, ws_spec, pl.BlockSpec((BLK, SGU_GROUPS), lambda i: (0, 0))],
        out_specs=[pl.BlockSpec((ts, 2 * SGU_W), lambda i: (i, 0)), ws_spec,
                   pl.BlockSpec((BLK, BLK), lambda i: (0, 0)), vec, vec],
        out_shape=[jax.ShapeDtypeStruct((s, 2 * SGU_W), BF16),
                   jax.ShapeDtypeStruct((SGU_GROUPS, BLK, BLK), F32),
                   jax.ShapeDtypeStruct((BLK, BLK), F32),
                   jax.ShapeDtypeStruct((1, SGU_W), F32), jax.ShapeDtypeStruct((1, SGU_W), F32)],
        scratch_shapes=[pltpu.VMEM((ts, SGU_W), F32)],
        compiler_params=_params(), name="sgu_bwd",
    )(proj, proj, dcat, ln_g, ln_b, w_s, b_st)


def _swiglu_fwd(h, wg_t, wu_t, name, side=None):
    s, d = h.shape
    f = wg_t.shape[0]
    tm, tn = _row_tile(s, 1024), _row_tile(f, 512)

    def epilogue(parts, ex, out):
        g, u = parts
        out[0][...] = g.astype(BF16)
        out[1][...] = u.astype(BF16)
        out[2][...] = (g * _sigmoid(g) * u).astype(BF16)

    a_spec = pl.BlockSpec((tm, d), lambda n, m, k: (m, 0))
    w_spec = pl.BlockSpec((tn, d), lambda n, m, k: (n, 0))
    o_spec = pl.BlockSpec((tm, tn), lambda n, m, k: (m, n))
    o_shape = jax.ShapeDtypeStruct((s, f), BF16)
    return _matmul(name, (f // tn, s // tm, 1),
                   [(h, a_spec, wg_t, w_spec, "nt", 0), (h, a_spec, wu_t, w_spec, "nt", 1)],
                   2, None, [], [(o_shape, o_spec)] * 3, epilogue, side)


def _swiglu_down(hid, wd, res, name, side=None):
    s, f = hid.shape
    d = wd.shape[1]
    tm, tn = _row_tile(s, 1024), _row_tile(d, 512)
    o_spec = pl.BlockSpec((tm, tn), lambda n, m, k: (m, n))
    return _one(_matmul(
        name, (d // tn, s // tm, 1),
        [(hid, pl.BlockSpec((tm, f), lambda n, m, k: (m, 0)),
          wd, pl.BlockSpec((f, tn), lambda n, m, k: (0, n)), "nn", 0)],
        1, None, [(res, o_spec)], [(jax.ShapeDtypeStruct((s, d), F32), o_spec)],
        _residual_epilogue, side), side)


def _swiglu_bwd_hidden(dxb, wd, gate, up, name):
    s, f = gate.shape
    d = dxb.shape[1]
    tm, tn = _row_tile(s, 1024), _row_tile(f, 512)

    def epilogue(parts, ex, out):
        dh = parts[0]
        g = ex[0][...].astype(F32)
        u = ex[1][...].astype(F32)
        sg = _sigmoid(g)
        silu = g * sg
        out[0][...] = (dh * u * (sg + silu * (1.0 - sg))).astype(BF16)
        out[1][...] = (dh * silu).astype(BF16)

    blk = pl.BlockSpec((tm, tn), lambda n, m, k: (m, n))
    o_shape = jax.ShapeDtypeStruct((s, f), BF16)
    return _matmul(
        name, (f // tn, s // tm, 1),
        [(dxb, pl.BlockSpec((tm, d), lambda n, m, k: (m, 0)),
          wd, pl.BlockSpec((tn, d), lambda n, m, k: (n, 0)), "nt", 0)],
        1, None, [(gate, blk), (up, blk)], [(o_shape, blk)] * 2, epilogue)


def _swiglu_bwd_input(dgate, dup, wg_t, wu_t, name, side=None):
    s, f = dgate.shape
    d = wg_t.shape[1]
    tm, tn, tk = _row_tile(s, 1024), _row_tile(d, 512), f // 2
    a_spec = pl.BlockSpec((tm, tk), lambda n, m, k: (m, k))
    w_spec = pl.BlockSpec((tk, tn), lambda n, m, k: (k, n))
    return _one(_matmul(
        name, (d // tn, s // tm, f // tk),
        [(dgate, a_spec, wg_t, w_spec, "nn", 0), (dup, a_spec, wu_t, w_spec, "nn", 0)],
        1, (tm, tn), [],
        [(jax.ShapeDtypeStruct((s, d), F32), pl.BlockSpec((tm, tn), lambda n, m, k: (m, n)))],
        _store_epilogue, side), side)


def _mm_tn_full(a, b, name, side=None):
    s, m = a.shape
    n = b.shape[1]
    tm, tn = _row_tile(m, 512), _row_tile(n, 512)
    return _one(_matmul(
        name, (m // tm, n // tn, 1),
        [(a, pl.BlockSpec((s, tm), lambda i, j, k: (0, i)),
          b, pl.BlockSpec((s, tn), lambda i, j, k: (0, j)), "tn", 0)],
        1, None, [],
        [(jax.ShapeDtypeStruct((m, n), BF16), pl.BlockSpec((tm, tn), lambda i, j, k: (i, j)))],
        _store_epilogue, side), side)


def _mm_nn(a, b, name, tn, out_dtype=BF16, res=None, side=None):
    m, k = a.shape
    n = b.shape[1]
    tm = _row_tile(m, 1024)
    extras = [] if res is None else [(res, pl.BlockSpec((tm, tn), lambda j, i, kk: (i, j)))]
    return _one(_matmul(
        name, (n // tn, m // tm, 1),
        [(a, pl.BlockSpec((tm, k), lambda j, i, kk: (i, 0)),
          b, pl.BlockSpec((k, tn), lambda j, i, kk: (0, j)), "nn", 0)],
        1, None, extras,
        [(jax.ShapeDtypeStruct((m, n), out_dtype), pl.BlockSpec((tm, tn), lambda j, i, kk: (i, j)))],
        _store_epilogue if res is None else _residual_epilogue, side), side)


def _mm_nn_colblocked(a, b_blk, name, res, side=None):
    m, k = a.shape
    nb, _, bw = b_blk.shape
    tm = _row_tile(m, 2048)
    o_spec = pl.BlockSpec((tm, bw), lambda j, i, kk: (i, j))
    return _one(_matmul(
        name, (nb, m // tm, 1),
        [(a, pl.BlockSpec((tm, k), lambda j, i, kk: (i, 0)),
          b_blk, pl.BlockSpec((None, k, bw), lambda j, i, kk: (j, 0, 0)), "nn", 0)],
        1, None, [(res, o_spec)],
        [(jax.ShapeDtypeStruct((m, nb * bw), F32), o_spec)], _residual_epilogue, side), side)


def _mm_nt_colblocked(a, b_blk, name, out_dtype, jb, side=None):
    m = a.shape[0]
    nb, n, bw = b_blk.shape
    tm = _row_tile(m, 512)
    return _one(_matmul(
        name, (m // tm, nb // jb),
        [(a, pl.BlockSpec((tm, jb * bw), lambda i, j: (i, j)),
          b_blk, pl.BlockSpec((jb, n, bw), lambda i, j: (j, 0, 0)), "nt", 0)],
        1, (tm, n), [],
        [(jax.ShapeDtypeStruct((m, n), out_dtype), pl.BlockSpec((tm, n), lambda i, j: (i, 0)))],
        _store_epilogue, side), side)


def _mm_nt(a, b, name, tk, out_dtype, side=None):
    m, k = a.shape
    n = b.shape[0]
    tm = _row_tile(m, 512)
    return _one(_matmul(
        name, (m // tm, k // tk),
        [(a, pl.BlockSpec((tm, tk), lambda i, kk: (i, kk)),
          b, pl.BlockSpec((n, tk), lambda i, kk: (0, kk)), "nt", 0)],
        1, (tm, n), [],
        [(jax.ShapeDtypeStruct((m, n), out_dtype), pl.BlockSpec((tm, n), lambda i, kk: (i, 0)))],
        _store_epilogue, side), side)


def _mm_nt_rowblocked(a, b, name, tn, out_dtype):
    m, k = a.shape
    n = b.shape[0]
    tm = _row_tile(m, 1024)
    return _matmul(
        name, (n // tn, m // tm, 1),
        [(a, pl.BlockSpec((tm, k), lambda j, i, kk: (i, 0)),
          b, pl.BlockSpec((tn, k), lambda j, i, kk: (j, 0)), "nt", 0)],
        1, None, [],
        [(jax.ShapeDtypeStruct((m, n), out_dtype), pl.BlockSpec((tm, tn), lambda j, i, kk: (i, j)))],
        _store_epilogue)[0]


def _mm_tn_colblocked(a, b, name, bw, side=None):
    s, m = a.shape
    nb = b.shape[1] // bw
    tm = _row_tile(m, 512)
    return _one(_matmul(
        name, (nb, m // tm, 1),
        [(a, pl.BlockSpec((s, tm), lambda j, i, k: (0, i)),
          b, pl.BlockSpec((s, bw), lambda j, i, k: (0, j)), "tn", 0)],
        1, None, [],
        [(jax.ShapeDtypeStruct((nb, m, bw), BF16), pl.BlockSpec((None, tm, bw), lambda j, i, k: (j, i, 0)))],
        _store_epilogue, side), side)


def _mm_tn_rowblocked(a, b, name, bh):
    s, n = b.shape
    nb = a.shape[1] // bh
    tn = _row_tile(n, 512)
    return _matmul(
        name, (nb, n // tn, 1),
        [(a, pl.BlockSpec((s, bh), lambda j, i, k: (0, j)),
          b, pl.BlockSpec((s, tn), lambda j, i, k: (0, i)), "tn", 0)],
        1, None, [],
        [(jax.ShapeDtypeStruct((nb, bh, n), BF16), pl.BlockSpec((None, bh, tn), lambda j, i, k: (j, 0, i)))],
        _store_epilogue)[0]


def _as2d(a):
    return a.reshape(-1, a.shape[-1])


def _cast_bf16(w, name):
    w2 = _as2d(w)
    rows, cols = w2.shape
    tr = _row_tile(rows, 256)

    def body(w_ref, o_ref):
        o_ref[...] = w_ref[...].astype(BF16)

    spec = pl.BlockSpec((tr, cols), lambda i: (i, 0))
    out = pl.pallas_call(
        body, grid=(rows // tr,), in_specs=[spec], out_specs=spec,
        out_shape=jax.ShapeDtypeStruct((rows, cols), BF16),
        compiler_params=_params(), name=name,
    )(w2)
    return out.reshape(w.shape)


def _cast_bf16_layer(w, layer, name):
    _, rows, cols = w.shape
    tr = _row_tile(rows, 256)

    def body(w_ref, o_ref):
        o_ref[...] = w_ref[...].astype(BF16)

    return pl.pallas_call(
        body, grid=(rows // tr,),
        in_specs=[pl.BlockSpec((None, tr, cols), lambda i: (layer, i, 0))],
        out_specs=pl.BlockSpec((tr, cols), lambda i: (i, 0)),
        out_shape=jax.ShapeDtypeStruct((rows, cols), BF16),
        compiler_params=_params(), name=name,
    )(w)


def _reduce_adam(recvs, w, m, v, name):
    n_layers, rows, cols = w.shape
    n_slots = recvs[0].shape[0]
    tr = _row_tile(rows, max(16, (128 * 1024 // cols) // 16 * 16))
    nt = rows // tr
    c1 = 1.0 - ADAM_B1 ** ADAM_STEP
    c2 = 1.0 - ADAM_B2 ** ADAM_STEP

    def body(*refs):
        r_refs = refs[:n_layers]
        w_ref, m_ref, v_ref, g_out, d_out, m_out, v_out = refs[n_layers:]
        layer = pl.program_id(0)

        def update(r_ref):
            g = r_ref[0].astype(F32)
            for k in range(1, n_slots):
                g = g + r_ref[k].astype(F32)
            mm = ADAM_B1 * m_ref[...] + (1.0 - ADAM_B1) * g
            vv = ADAM_B2 * v_ref[...] + (1.0 - ADAM_B2) * (g * g)
            m_hat = mm / c1
            v_hat = vv / c2
            g_out[...] = g
            d_out[...] = -ADAM_LR * (m_hat / (jnp.sqrt(v_hat) + ADAM_EPS) + ADAM_WD * w_ref[...])
            m_out[...] = mm
            v_out[...] = vv

        for li in range(n_layers):
            if n_layers == 1:
                update(r_refs[li])
            else:
                pl.when(layer == li)(functools.partial(update, r_refs[li]))

    def recv_spec(li):
        def imap(layer, i):
            return (0, jnp.where(layer == li, i, jnp.where(layer < li, 0, nt - 1)), 0)
        return pl.BlockSpec((n_slots, tr, cols), imap)

    spec = pl.BlockSpec((None, tr, cols), lambda layer, i: (layer, i, 0))
    o_shape = jax.ShapeDtypeStruct(w.shape, F32)
    return pl.pallas_call(
        body, grid=(n_layers, nt),
        in_specs=[recv_spec(li) for li in range(n_layers)] + [spec] * 3,
        out_specs=[spec] * 4, out_shape=[o_shape] * 4,
        compiler_params=_params(), name=name,
    )(*recvs, w, m, v)


def _my_place():
    return lax.axis_index("x"), lax.axis_index("y"), lax.axis_index("c")


class _GatherSide:
    def __init__(self, blocks):
        self.ins = list(blocks)
        self.outs = [jax.ShapeDtypeStruct((N_DEV,) + b.shape, b.dtype) for b in blocks]

    def scratch(self):
        n = len(self.ins)
        return [pltpu.SemaphoreType.DMA((7 * n,)), pltpu.SemaphoreType.DMA((7 * n,)),
                pltpu.SemaphoreType.DMA((n,))]

    def phases(self, x_refs, out_refs, send_sems, recv_sems, local_sems):
        n = len(self.ins)
        x, y, c = _my_place()
        me, sibling = (x, y, c), (x, y, 1 - c)
        chips = [(1 - x, y), (x, 1 - y), (1 - x, 1 - y)]

        def slot(t, px, py, pc):
            return out_refs[t].at[4 * px + 2 * py + pc]

        def copy(t, k, blk, to, src=None):
            return pltpu.make_async_remote_copy(
                src_ref=slot(t, *blk) if src is None else src, dst_ref=slot(t, *blk),
                send_sem=send_sems.at[7 * t + k], recv_sem=recv_sems.at[7 * t + k],
                device_id=to, device_id_type=pl.DeviceIdType.MESH)

        def own(t):
            return pltpu.make_async_copy(x_refs[t], slot(t, *me), local_sems.at[t])

        def first(t):
            return [copy(t, 0, me, sibling, src=x_refs[t])] + [
                copy(t, 1 + j, me, (*chip, c), src=x_refs[t]) for j, chip in enumerate(chips)]

        def passed(t):
            return [copy(t, 4 + j, (*chip, c), sibling) for j, chip in enumerate(chips)]

        def start():
            for t in range(n):
                own(t).start()
                for cp in first(t):
                    cp.start()

        def mid():
            for t in range(n):
                fwd = passed(t)
                for j, chip in enumerate(chips):
                    copy(t, 1 + j, (*chip, c), me).wait_recv()
                    fwd[j].start()

        def finish():
            for t in range(n):
                copy(t, 0, sibling, me).wait_recv()
                for j, chip in enumerate(chips):
                    copy(t, 4 + j, (*chip, 1 - c), me).wait_recv()
                for cp in first(t) + passed(t):
                    cp.wait_send()
                own(t).wait()

        return start, mid, finish


class _ExchangeSide:
    def __init__(self, blocked):
        self.ins = list(blocked)
        self.outs = [jax.ShapeDtypeStruct(b.shape, b.dtype) for b in blocked]

    def scratch(self):
        n = len(self.ins)
        return [pltpu.SemaphoreType.DMA((7 * n,)), pltpu.SemaphoreType.DMA((7 * n,)),
                pltpu.SemaphoreType.DMA((n,))]

    def phases(self, srcs, dsts, send_sems, recv_sems, local_sems):
        n = len(self.ins)
        x, y, c = _my_place()
        me = 4 * x + 2 * y + c

        def own(t):
            return pltpu.make_async_copy(srcs[t].at[me], dsts[t].at[me], local_sems.at[t])

        def copies(t, arriving):
            res = []
            for k in range(1, N_DEV):
                px, py, pc = x ^ (k >> 2), y ^ ((k >> 1) & 1), c ^ (k & 1)
                peer = 4 * px + 2 * py + pc
                sem = 7 * t + k - 1
                res.append(pltpu.make_async_remote_copy(
                    src_ref=srcs[t].at[peer], dst_ref=dsts[t].at[peer if arriving else me],
                    send_sem=send_sems.at[sem], recv_sem=recv_sems.at[sem],
                    device_id=(px, py, pc), device_id_type=pl.DeviceIdType.MESH))
            return res

        def start():
            for t in range(n):
                own(t).start()
                for send in copies(t, False):
                    send.start()

        def mid():
            pass

        def finish():
            for t in range(n):
                for arrival in copies(t, True):
                    arrival.wait_recv()
                for send in copies(t, False):
                    send.wait_send()
                own(t).wait()

        return start, mid, finish


def _run_side(side, name):
    n_in, n_out = len(side.ins), len(side.outs)

    def body(*refs):
        start, mid, finish = side.phases(refs[:n_in], refs[n_in:n_in + n_out], *refs[n_in + n_out:])
        start()
        mid()
        finish()

    hbm = pl.BlockSpec(memory_space=pltpu.HBM)
    return pl.pallas_call(
        body, out_shape=list(side.outs), in_specs=[hbm] * n_in, out_specs=[hbm] * n_out,
        scratch_shapes=side.scratch(), name=name,
    )(*side.ins)


def _to_residue_major(a, dilation):
    s, w = a.shape
    return a.reshape(s // dilation, dilation, w).transpose(1, 0, 2).reshape(s, w)


def _from_residue_major(a, dilation):
    s, w = a.shape
    return a.reshape(dilation, s // dilation, w).transpose(1, 0, 2).reshape(s, w)


def _mem_kv(mem, gain, wkv, layer, tag):
    mem_n = _rmsnorm_fwd(mem, gain, "mem_norm_" + tag)
    mlen, d = mem.shape
    nb, _, bh, n = wkv.shape
    kv = _matmul(
        "mem_kv_" + tag, (1, nb),
        [(mem_n, pl.BlockSpec((mlen, bh), lambda i, j: (0, j)),
          wkv, pl.BlockSpec((None, None, bh, n), lambda i, j: (j, layer, 0, 0)), "nn", 0)],
        1, (mlen, n), [],
        [(jax.ShapeDtypeStruct((mlen, n), BF16), pl.BlockSpec((mlen, n), lambda i, j: (0, 0)))],
        _store_epilogue)[0]
    return mem_n, kv


def _mem_kv_bwd(mem, gain, mem_n, wkv, layer, dkv, tag):
    mlen, d = mem.shape
    nb, _, bh, n = wkv.shape
    dkvb = dkv.astype(BF16)
    dw = _mm_tn_rowblocked(mem_n, dkvb, "mem_kv_dw_" + tag, bh)
    dmem_n = _matmul(
        "mem_kv_dx_" + tag, (nb, 1),
        [(dkvb, pl.BlockSpec((mlen, n), lambda j, k: (0, 0)),
          wkv, pl.BlockSpec((None, None, bh, n), lambda j, k: (j, layer, 0, 0)), "nt", 0)],
        1, None, [],
        [(jax.ShapeDtypeStruct((mlen, d), F32), pl.BlockSpec((mlen, bh), lambda j, k: (0, j)))],
        _store_epilogue)[0]
    _, _, dgain = _rmsnorm_bwd(mem, gain, dmem_n, None, "mem_norm_bwd_" + tag)
    return dw, dgain


def _row_blocks(a):
    return a.reshape(N_DEV, -1, a.shape[-1])


def _rows(a):
    return a.reshape(-1, a.shape[-1])


def _ffn_bwd(x, gain, w_gate, w_up, w_down, saved, dx, dxb, tag):
    hf, gate, up, hid = saved
    dgate, dup = _swiglu_bwd_hidden(dxb, w_down, gate, up, "swiglu_bwd_hidden_" + tag)
    dwd = _mm_tn_full(hid, dxb, "swiglu_bwd_wdown_" + tag)
    dwg, (r_wd,) = _mm_tn_full(dgate, hf, "swiglu_bwd_wgate_" + tag, _ExchangeSide([_row_blocks(dwd)]))
    dwu = _mm_tn_full(dup, hf, "swiglu_bwd_wup_" + tag)
    dhf, (r_wg, r_wu) = _swiglu_bwd_input(dgate, dup, w_gate, w_up, "swiglu_bwd_input_" + tag,
                                          _ExchangeSide([_row_blocks(dwg), _row_blocks(dwu)]))
    dx_new, dxb_new, dgain = _rmsnorm_bwd(x, gain, dhf, dx, "ffn_norm_bwd_" + tag)
    return dx_new, dxb_new, dgain, r_wg, r_wu, r_wd


def _local_step(x, mem, positions, target, w_attn_in, shards, small):
    s, d = x.shape
    tabs = _rotary_tables(positions)
    mix_norm, mem_norm, ffn_norm = small["mix_norm"], small["mem_norm"], small["ffn_norm"]

    h0 = _rmsnorm_fwd(x, mix_norm[0:1], "mix_norm_0")
    proj0, (w_mem_kv, w_attn_out, w_gate0) = _attn_in_proj(
        h0, w_attn_in, tabs, _GatherSide([shards["w_mem_kv"], shards["attn_w_out"], shards["w_gate"][0]]))
    qkv, offs, outs, lses = [], [], [], []
    for g, dil in enumerate(DILATIONS):
        if dil == 1:
            arr, off = proj0, (g, N_GROUPS + g, 2 * N_GROUPS + g)
        else:
            cols = [proj0[:, (p * N_GROUPS + g) * GROUP_W:(p * N_GROUPS + g + 1) * GROUP_W] for p in range(3)]
            arr, off = _to_residue_major(jnp.concatenate(cols, axis=1), dil), (0, 1, 2)
        o, lse = _attn_fwd(arr, arr, arr, off, s // dil // BLK, "attn_fwd_%d" % g)
        qkv.append(arr)
        offs.append(off)
        if dil > 1:
            o, lse = _from_residue_major(o, dil), _from_residue_major(lse, dil)
        outs.append(o)
        lses.append(lse)
    mix0 = _merge_fwd(outs, lses)
    qm_off0 = 3 * N_GROUPS
    mem_n0, kv0 = _mem_kv(mem, mem_norm[0:1], w_mem_kv, 0, "0")
    memo0 = _memattn_fwd(proj0, qm_off0, kv0, "memattn_fwd_0")
    cat0 = jnp.concatenate([mix0, memo0], axis=1)
    x1, (w_up0,) = _mm_nn_colblocked(cat0, w_attn_out, "attn_out_proj", x, _GatherSide([shards["w_up"][0]]))
    hf0 = _rmsnorm_fwd(x1, ffn_norm[0:1], "ffn_norm_0")
    w_gate0, w_up0 = _rows(w_gate0), _rows(w_up0)
    (gate0, up0, hid0), (w_down0, w_sgu_in, w_sgu_out, w_gate1) = _swiglu_fwd(
        hf0, w_gate0, w_up0, "swiglu_fwd_0",
        _GatherSide([shards["w_down"][0], shards["sgu_w_in"], shards["sgu_w_out"], shards["w_gate"][1]]))
    w_down0 = _rows(w_down0)
    x2, (w_up1, w_down1) = _swiglu_down(hid0, w_down0, x1, "swiglu_down_0",
                                        _GatherSide([shards["w_up"][1], shards["w_down"][1]]))
    w_gate1, w_up1, w_down1 = _rows(w_gate1), _rows(w_up1), _rows(w_down1)
    ffn_saved0 = (hf0, gate0, up0, hid0)

    h1 = _rmsnorm_fwd(x2, mix_norm[1:2], "mix_norm_1")
    w_sgu_in = _rows(w_sgu_in)
    w_sgu_out = _rows(w_sgu_out)
    proj1 = _mm_nt_rowblocked(h1, w_sgu_in, "sgu_in_proj", w_sgu_in.shape[0] // 7, BF16)
    b_st = small["sgu_b_spatial"].T
    mix1 = _sgu_fwd(proj1, small["sgu_ln_g"], small["sgu_ln_b"], small["sgu_w_spatial"], b_st)
    qm_off1 = 2 * SGU_W // GROUP_W
    mem_n1, kv1 = _mem_kv(mem, mem_norm[1:2], w_mem_kv, 1, "1")
    memo1 = _memattn_fwd(proj1, qm_off1, kv1, "memattn_fwd_1")
    cat1 = jnp.concatenate([mix1, memo1], axis=1)
    x3 = _mm_nn(cat1, w_sgu_out, "sgu_out_proj", d // 2, out_dtype=F32, res=x2)
    hf1 = _rmsnorm_fwd(x3, ffn_norm[1:2], "ffn_norm_1")
    gate1, up1, hid1 = _swiglu_fwd(hf1, w_gate1, w_up1, "swiglu_fwd_1")
    x4 = _swiglu_down(hid1, w_down1, x3, "swiglu_down_1")
    ffn_saved1 = (hf1, gate1, up1, hid1)

    loss, dx, dxb, d_final = _loss_head(x4, small["final_norm"], target)

    recvs, sgrads = {}, {}
    dx, dxb, d_ffn1, r_wg1, r_wu1, r_wd1 = _ffn_bwd(x3, ffn_norm[1:2], w_gate1, w_up1, w_down1, ffn_saved1,
                                                    dx, dxb, "1")
    dcat1 = _mm_nt_rowblocked(dxb, w_sgu_out, "sgu_out_proj_dx", w_sgu_out.shape[0] // 2, BF16)
    dwsout = _mm_tn_rowblocked(cat1, dxb, "sgu_out_proj_dw", w_sgu_out.shape[0] // N_DEV)
    duv, dws, dbs, dlng, dlnb = _sgu_bwd(proj1, dcat1, small["sgu_ln_g"], small["sgu_ln_b"],
                                         small["sgu_w_spatial"], b_st)
    dqm1, dkv1 = _memattn_bwd(proj1, qm_off1, kv1, dcat1, SGU_W // GROUP_W, "memattn_bwd_1")
    dwkv1, d_memnorm1 = _mem_kv_bwd(mem, mem_norm[1:2], mem_n1, w_mem_kv, 1, dkv1, "1")
    dproj1 = jnp.concatenate([duv, dqm1], axis=1)
    dwsin, (r_wsout, r_wkv1) = _mm_tn_full(dproj1, h1, "sgu_in_proj_dw", _ExchangeSide([dwsout, dwkv1]))
    dh1, (r_wsin,) = _mm_nn(dproj1, w_sgu_in, "sgu_in_proj_dx", d // 4, out_dtype=F32,
                            side=_ExchangeSide([_row_blocks(dwsin)]))
    dx, dxb, d_mix1 = _rmsnorm_bwd(x2, mix_norm[1:2], dh1, dx, "mix_norm_bwd_1")

    dx, dxb, d_ffn0, r_wg0, r_wu0, r_wd0 = _ffn_bwd(x1, ffn_norm[0:1], w_gate0, w_up0, w_down0, ffn_saved0,
                                                    dx, dxb, "0")
    dcat0 = _mm_nt_colblocked(dxb, w_attn_out, "attn_out_proj_dx", BF16, 4)
    dwout0 = _mm_tn_colblocked(cat0, dxb, "attn_out_proj_dw", w_attn_out.shape[2])
    dos_and_deltas = _merge_bwd(dcat0, outs, lses)
    dqkv = []
    for g, dil in enumerate(DILATIONS):
        do_g, dl_g = dos_and_deltas[g], dos_and_deltas[N_GROUPS + g]
        lse_g = lses[g]
        if dil > 1:
            do_g, dl_g, lse_g = (_to_residue_major(t, dil) for t in (do_g, dl_g, lse_g))
        args = (qkv[g], qkv[g], qkv[g], offs[g], do_g, lse_g, dl_g, s // dil // BLK)
        dq = _attn_bwd_dq(*args, "attn_bwd_dq_%d" % g)
        dk, dv = _attn_bwd_dkv(*args, "attn_bwd_dkv_%d" % g)
        t = jnp.concatenate([dq, dk, dv], axis=1)
        dqkv.append(_from_residue_major(t, dil) if dil > 1 else t)
    dqm0, dkv0 = _memattn_bwd(proj0, qm_off0, kv0, dcat0, 1, "memattn_bwd_0")
    dwkv0, d_memnorm0 = _mem_kv_bwd(mem, mem_norm[0:1], mem_n0, w_mem_kv, 0, dkv0, "0")
    dproj0 = _assemble_dproj(dqkv, dqm0, tabs)
    dwin0, (r_wout0, r_wkv0) = _mm_tn_colblocked(h0, dproj0, "attn_in_proj_dw", w_attn_in.shape[2],
                                                 _ExchangeSide([dwout0, dwkv0]))
    dh0, (r_win0,) = _mm_nt_colblocked(dproj0, w_attn_in, "attn_in_proj_dx", F32, 2, _ExchangeSide([dwin0]))
    grad_x, _, d_mix0 = _rmsnorm_bwd(x, mix_norm[0:1], dh0, dx, "mix_norm_bwd_0")

    recvs["w_gate"] = [r_wg0, r_wg1]
    recvs["w_up"] = [r_wu0, r_wu1]
    recvs["w_down"] = [r_wd0, r_wd1]
    recvs["w_mem_kv"] = [r_wkv0, r_wkv1]
    recvs["attn_w_in"] = [r_win0]
    recvs["attn_w_out"] = [r_wout0]
    recvs["sgu_w_in"] = [r_wsin]
    recvs["sgu_w_out"] = [r_wsout]
    sgrads["mix_norm"] = jnp.concatenate([d_mix0, d_mix1], axis=0)
    sgrads["mem_norm"] = jnp.concatenate([d_memnorm0, d_memnorm1], axis=0)
    sgrads["ffn_norm"] = jnp.concatenate([d_ffn0, d_ffn1], axis=0)
    sgrads["final_norm"] = d_final
    sgrads["sgu_w_spatial"] = dws
    sgrads["sgu_b_spatial"] = dbs[:, :SGU_GROUPS].T
    sgrads["sgu_ln_g"] = dlng
    sgrads["sgu_ln_b"] = dlnb
    return loss, grad_x, recvs, sgrads


BIG = ("w_mem_kv", "w_gate", "w_up", "w_down", "attn_w_in", "attn_w_out", "sgu_w_in", "sgu_w_out")
TRANSPOSED = ("w_gate", "w_up", "sgu_w_in")
SMALL_REPLICATED = ("mix_norm", "mem_norm", "ffn_norm", "final_norm", "sgu_w_spatial", "sgu_b_spatial")
SMALL_SHARDED = ("sgu_ln_g", "sgu_ln_b")
WEIGHT_ORDER = ("mix_norm", "mem_norm", "w_mem_kv", "ffn_norm", "w_gate", "w_up", "w_down", "attn_w_in",
                "attn_w_out", "sgu_w_in", "sgu_ln_g", "sgu_ln_b", "sgu_w_spatial", "sgu_b_spatial",
                "sgu_w_out", "final_norm")
PACK_LANES = 128


def _pack(parts):
    flat = [p.reshape(-1) for p in parts]
    sizes = [f.shape[0] for f in flat]
    total = sum(sizes)
    rows = -(-total // PACK_LANES)
    rows = -(-rows // 8) * 8
    pad = rows * PACK_LANES - total
    packed = jnp.concatenate(flat + [jnp.zeros((pad,), F32)]).reshape(rows, PACK_LANES)
    offs, o = [], 0
    for sz in sizes:
        offs.append(o)
        o += sz
    return packed, offs, sizes


def _unpack(packed, offs, sizes, shapes):
    flat = packed.reshape(-1)
    return [flat[o:o + sz].reshape(shp) for o, sz, shp in zip(offs, sizes, shapes)]


def kernel(x, mem, positions, mix_norm, mem_norm, w_mem_kv, ffn_norm, w_gate, w_up, w_down, attn_w_in, attn_w_out, sgu_w_in, sgu_ln_g, sgu_ln_b, sgu_w_spatial, sgu_b_spatial, sgu_w_out, final_norm, loss_target, m_mix_norm, m_mem_norm, m_w_mem_kv, m_ffn_norm, m_w_gate, m_w_up, m_w_down, m_attn_w_in, m_attn_w_out, m_sgu_w_in, m_sgu_ln_g, m_sgu_ln_b, m_sgu_w_spatial, m_sgu_b_spatial, m_sgu_w_out, m_final_norm, v_mix_norm, v_mem_norm, v_w_mem_kv, v_ffn_norm, v_w_gate, v_w_up, v_w_down, v_attn_w_in, v_attn_w_out, v_sgu_w_in, v_sgu_ln_g, v_sgu_ln_b, v_sgu_w_spatial, v_sgu_b_spatial, v_sgu_w_out, v_final_norm):
    w = dict(mix_norm=mix_norm, mem_norm=mem_norm, w_mem_kv=w_mem_kv, ffn_norm=ffn_norm, w_gate=w_gate,
             w_up=w_up, w_down=w_down, attn_w_in=attn_w_in, attn_w_out=attn_w_out, sgu_w_in=sgu_w_in,
             sgu_ln_g=sgu_ln_g, sgu_ln_b=sgu_ln_b, sgu_w_spatial=sgu_w_spatial, sgu_b_spatial=sgu_b_spatial,
             sgu_w_out=sgu_w_out, final_norm=final_norm)
    mo = dict(mix_norm=m_mix_norm, mem_norm=m_mem_norm, w_mem_kv=m_w_mem_kv, ffn_norm=m_ffn_norm,
              w_gate=m_w_gate, w_up=m_w_up, w_down=m_w_down, attn_w_in=m_attn_w_in, attn_w_out=m_attn_w_out,
              sgu_w_in=m_sgu_w_in, sgu_ln_g=m_sgu_ln_g, sgu_ln_b=m_sgu_ln_b, sgu_w_spatial=m_sgu_w_spatial,
              sgu_b_spatial=m_sgu_b_spatial, sgu_w_out=m_sgu_w_out, final_norm=m_final_norm)
    vo = dict(mix_norm=v_mix_norm, mem_norm=v_mem_norm, w_mem_kv=v_w_mem_kv, ffn_norm=v_ffn_norm,
              w_gate=v_w_gate, w_up=v_w_up, w_down=v_w_down, attn_w_in=v_attn_w_in, attn_w_out=v_attn_w_out,
              sgu_w_in=v_sgu_w_in, sgu_ln_g=v_sgu_ln_g, sgu_ln_b=v_sgu_ln_b, sgu_w_spatial=v_sgu_w_spatial,
              sgu_b_spatial=v_sgu_b_spatial, sgu_w_out=v_sgu_w_out, final_norm=v_final_norm)
    me = 4 * lax.axis_index("x") + 2 * lax.axis_index("y") + lax.axis_index("c")
    d_model = x.shape[-1]

    for n in TRANSPOSED:
        w[n], mo[n], vo[n] = (jnp.swapaxes(t, 1, 2) for t in (w[n], mo[n], vo[n]))

    shards = {
        "w_mem_kv": _cast_bf16(w_mem_kv, "cast_w_mem_kv"),
        "attn_w_out": _cast_bf16(attn_w_out[0], "cast_attn_w_out"),
        "sgu_w_in": _cast_bf16(w["sgu_w_in"][0], "cast_sgu_w_in"),
        "sgu_w_out": _cast_bf16(sgu_w_out[0], "cast_sgu_w_out"),
    }
    for n in ("w_gate", "w_up", "w_down"):
        shards[n] = [_cast_bf16_layer(w[n], layer, "cast_%s_%d" % (n, layer)) for layer in range(w[n].shape[0])]
    ln_pack = jnp.concatenate([sgu_ln_g, sgu_ln_b], axis=0)
    w_attn_in, ln_all = _run_side(
        _GatherSide([_cast_bf16(attn_w_in[0], "cast_attn_w_in"), ln_pack]), "gather_attn_w_in")
    ln_full = ln_all.transpose(1, 0, 2).reshape(2, 1, -1)
    small = dict(mix_norm=mix_norm, mem_norm=mem_norm, ffn_norm=ffn_norm, final_norm=final_norm.reshape(1, -1),
                 sgu_w_spatial=sgu_w_spatial[0], sgu_b_spatial=sgu_b_spatial[0],
                 sgu_ln_g=ln_full[0], sgu_ln_b=ln_full[1])

    loss, grad_x, recvs, sgrads = _local_step(x[0], mem[0], positions[0], loss_target[0], w_attn_in, shards,
                                              small)
    loss = lax.psum(loss[0, 0], MESH_AXES)

    out_g, out_d, out_m, out_v = {}, {}, {}, {}
    for n in BIG:
        shard = w[n]
        w3 = shard.reshape(shard.shape[0], -1, shard.shape[-1])
        rs = [r.reshape(N_DEV, -1, shard.shape[-1]) for r in recvs[n]]
        res = _reduce_adam(rs, w3, mo[n].reshape(w3.shape), vo[n].reshape(w3.shape), "adam_" + n)
        res = [r.reshape(shard.shape) for r in res]
        if n in TRANSPOSED:
            res = [jnp.swapaxes(r, 1, 2) for r in res]
        out_g[n], out_d[n], out_m[n], out_v[n] = res

    small_names = SMALL_REPLICATED + SMALL_SHARDED
    packed, offs, sizes = _pack([sgrads[n] for n in small_names])
    all_packs = _run_side(_GatherSide([packed]), "gather_small_grads")[0]
    rep_shapes = [w[n].shape for n in SMALL_REPLICATED]
    w_pack, w_offs, w_sizes = _pack([w[n] for n in SMALL_REPLICATED])
    m_pack, _, _ = _pack([mo[n] for n in SMALL_REPLICATED])
    v_pack, _, _ = _pack([vo[n] for n in SMALL_REPLICATED])
    n_rep_rows = w_pack.shape[0]
    res = _reduce_adam([all_packs[:, :n_rep_rows]], w_pack[None], m_pack[None], v_pack[None], "adam_small")
    for dst, r in zip((out_g, out_d, out_m, out_v), res):
        for n, val in zip(SMALL_REPLICATED, _unpack(r[0], w_offs, w_sizes, rep_shapes)):
            dst[n] = val
    ln_rows0 = offs[len(SMALL_REPLICATED)] // PACK_LANES
    ln_rows = 2 * SGU_W // PACK_LANES
    ln_sum = _reduce_adam([all_packs[:, ln_rows0:ln_rows0 + ln_rows]], jnp.zeros((1, ln_rows, PACK_LANES), F32),
                          jnp.zeros((1, ln_rows, PACK_LANES), F32), jnp.zeros((1, ln_rows, PACK_LANES), F32),
                          "sum_ln_grads")[0]
    ln_grads = ln_sum.reshape(2, N_DEV, -1)
    ln_mine = lax.dynamic_index_in_dim(ln_grads, me, axis=1, keepdims=False)
    w_ln = jnp.concatenate([sgu_ln_g, sgu_ln_b], axis=0)[None]
    m_ln = jnp.concatenate([m_sgu_ln_g, m_sgu_ln_b], axis=0)[None]
    v_ln = jnp.concatenate([v_sgu_ln_g, v_sgu_ln_b], axis=0)[None]
    res = _reduce_adam([ln_mine[None]], w_ln, m_ln, v_ln, "adam_ln")
    for dst, r in zip((out_g, out_d, out_m, out_v), res):
        dst["sgu_ln_g"], dst["sgu_ln_b"] = r[0, 0:1], r[0, 1:2]

    return (loss, grad_x[None], *[out_g[n] for n in WEIGHT_ORDER], *[out_d[n] for n in WEIGHT_ORDER],
            *[out_m[n] for n in WEIGHT_ORDER], *[out_v[n] for n in WEIGHT_ORDER])
```

```python
import functools

import jax
import jax.numpy as jnp
from jax import lax
from jax.experimental import pallas as pl
from jax.experimental.pallas import tpu as pltpu

F32 = jnp.float32
BF16 = jnp.bfloat16

N_DEV = 8
HEAD = 128
HPG = 4
GROUP_W = HPG * HEAD
N_GROUPS = 3
DILATIONS = (1, 4, 16)
BLK = 128
SGU_GROUPS = 12
SGU_W = SGU_GROUPS * HEAD
ROT_HALF = 16
ROPE_THETA = 500000.0
NORM_EPS = 1e-6
LN_EPS = 1e-5
NEG_INF = -1e30
SCALE = HEAD ** -0.5

ADAM_LR = 0.001
ADAM_B1 = 0.9
ADAM_B2 = 0.999
ADAM_EPS = 1e-08
ADAM_WD = 0.01
ADAM_STEP = 10

VMEM_LIMIT_V7X = 56 * 1024 * 1024
MESH_AXES = ("x", "y", "c")

_DN = {
    "nn": (((1,), (0,)), ((), ())),
    "nt": (((1,), (1,)), ((), ())),
    "tn": (((0,), (0,)), ((), ())),
}


def _dot(a, b, kind="nn"):
    return lax.dot_general(a, b, _DN[kind], preferred_element_type=F32)


def _params():
    return pltpu.CompilerParams(vmem_limit_bytes=VMEM_LIMIT_V7X)


def _row_tile(rows, cap):
    if rows <= cap:
        return rows
    t = cap - cap % 16
    while t >= 16:
        if rows % t == 0:
            return t
        t -= 16
    return rows


def _gelu(x):
    c = 0.7978845608028654
    return 0.5 * x * (1.0 + jnp.tanh(c * (x + 0.044715 * x * x * x)))


def _gelu_grad(x):
    c = 0.7978845608028654
    t = jnp.tanh(c * (x + 0.044715 * x * x * x))
    return 0.5 * (1.0 + t) + 0.5 * x * (1.0 - t * t) * c * (1.0 + 3.0 * 0.044715 * x * x)


def _sigmoid(x):
    return 1.0 / (1.0 + jnp.exp(-x))


def _matmul(name, grid, terms, n_acc, acc_shape, extras, outs, epilogue, side=None):
    nk = grid[-1]
    nt, ne, no = len(terms), len(extras), len(outs)
    kinds = [(t[4], t[5]) for t in terms]
    n_scratch_acc = 0 if nk == 1 else n_acc
    ns_in = len(side.ins) if side else 0
    ns_out = len(side.outs) if side else 0
    n_steps = 1
    for g in grid:
        n_steps *= g

    def body(*refs):
        pos = 0
        ab = refs[pos:pos + 2 * nt]
        pos += 2 * nt
        ex = refs[pos:pos + ne]
        pos += ne
        s_in = refs[pos:pos + ns_in]
        pos += ns_in
        out = refs[pos:pos + no]
        pos += no
        s_out = refs[pos:pos + ns_out]
        pos += ns_out
        accs = refs[pos:pos + n_scratch_acc]
        s_sems = refs[pos + n_scratch_acc:]
        if side:
            step = pl.program_id(0)
            for ax in range(1, len(grid)):
                step = step * grid[ax] + pl.program_id(ax)
            start, mid, finish = side.phases(s_in, s_out, *s_sems)
            pl.when(step == 0)(start)
        parts = [None] * n_acc
        for t, (kind, ai) in enumerate(kinds):
            a_ref, b_ref = ab[2 * t], ab[2 * t + 1]
            if len(b_ref.shape) == 2:
                pairs = [(a_ref[...], b_ref[...])]
            elif len(a_ref.shape) == 3:
                pairs = [(a_ref[q], b_ref[q]) for q in range(b_ref.shape[0])]
            else:
                bw = b_ref.shape[2]
                pairs = [(a_ref[:, q * bw:(q + 1) * bw], b_ref[q]) for q in range(b_ref.shape[0])]
            for a, b in pairs:
                p = _dot(a.astype(BF16), b.astype(BF16), kind)
                parts[ai] = p if parts[ai] is None else parts[ai] + p
        if nk == 1:
            epilogue(parts, ex, out)
        else:
            k = pl.program_id(len(grid) - 1)

            @pl.when(k == 0)
            def _():
                for ai in range(n_acc):
                    accs[ai][...] = parts[ai]

            @pl.when(k > 0)
            def _():
                for ai in range(n_acc):
                    accs[ai][...] += parts[ai]

            @pl.when(k == nk - 1)
            def _():
                epilogue([a[...] for a in accs], ex, out)

        if side:
            pl.when(step == (3 * n_steps) // 4)(mid)
            pl.when(step == n_steps - 1)(finish)

    hbm = pl.BlockSpec(memory_space=pltpu.HBM)
    in_specs, args = [], []
    for (a, a_spec, b, b_spec, _, _) in terms:
        in_specs += [a_spec, b_spec]
        args += [a, b]
    for (e, e_spec) in extras:
        in_specs.append(e_spec)
        args.append(e)
    scratch = [pltpu.VMEM(acc_shape, F32) for _ in range(n_scratch_acc)]
    out_specs = [o[1] for o in outs]
    out_shape = [o[0] for o in outs]
    if side:
        in_specs += [hbm] * ns_in
        args += list(side.ins)
        out_specs += [hbm] * ns_out
        out_shape += list(side.outs)
        scratch += side.scratch()
    res = pl.pallas_call(
        body, grid=grid, in_specs=in_specs, out_specs=out_specs, out_shape=out_shape,
        scratch_shapes=scratch, compiler_params=_params(), name=name,
    )(*args)
    return res if side is None else (res[:no], res[no:])


def _one(res, side):
    return res[0] if side is None else (res[0][0], res[1])


def _store_epilogue(parts, ex, out):
    out[0][...] = parts[0].astype(out[0].dtype)


def _residual_epilogue(parts, ex, out):
    out[0][...] = (parts[0] + ex[0][...]).astype(out[0].dtype)


def _rmsnorm_fwd(x, g, name):
    rows, d = x.shape
    tm = _row_tile(rows, 512)

    def body(x_ref, g_ref, o_ref):
        xf = x_ref[...]
        r = lax.rsqrt(jnp.mean(xf * xf, axis=-1, keepdims=True) + NORM_EPS)
        o_ref[...] = (xf * r * g_ref[...]).astype(o_ref.dtype)

    return pl.pallas_call(
        body, grid=(rows // tm,),
        in_specs=[pl.BlockSpec((tm, d), lambda i: (i, 0)), pl.BlockSpec((1, d), lambda i: (0, 0))],
        out_specs=pl.BlockSpec((tm, d), lambda i: (i, 0)),
        out_shape=jax.ShapeDtypeStruct((rows, d), BF16),
        compiler_params=_params(), name=name,
    )(x, g)


def _rmsnorm_bwd(x, g, dh, dres, name):
    rows, d = x.shape
    tm = _row_tile(rows, 256)
    has_res = dres is not None

    def body(*refs):
        if has_res:
            x_ref, g_ref, dh_ref, dres_ref, dx_ref, dxb_ref, dg_ref = refs
        else:
            x_ref, g_ref, dh_ref, dx_ref, dxb_ref, dg_ref = refs
        i = pl.program_id(0)
        xf = x_ref[...]
        r = lax.rsqrt(jnp.mean(xf * xf, axis=-1, keepdims=True) + NORM_EPS)
        xhat = xf * r
        dy = dh_ref[...]
        gdy = dy * g_ref[...]
        c = jnp.mean(gdy * xhat, axis=-1, keepdims=True)
        dx = r * (gdy - xhat * c)
        if has_res:
            dx = dx + dres_ref[...]
        dx_ref[...] = dx
        dxb_ref[...] = dx.astype(BF16)

        @pl.when(i == 0)
        def _():
            dg_ref[...] = jnp.zeros_like(dg_ref)

        dg_ref[...] += jnp.sum(dy * xhat, axis=0, keepdims=True)

    row_spec = pl.BlockSpec((tm, d), lambda i: (i, 0))
    vec_spec = pl.BlockSpec((1, d), lambda i: (0, 0))
    in_specs = [row_spec, vec_spec, row_spec] + ([row_spec] if has_res else [])
    args = [x, g, dh] + ([dres] if has_res else [])
    return pl.pallas_call(
        body, grid=(rows // tm,), in_specs=in_specs,
        out_specs=[row_spec, row_spec, vec_spec],
        out_shape=[jax.ShapeDtypeStruct((rows, d), F32), jax.ShapeDtypeStruct((rows, d), BF16),
                   jax.ShapeDtypeStruct((1, d), F32)],
        compiler_params=_params(), name=name,
    )(*args)


def _loss_head(x, g, target):
    rows, d = x.shape
    tm = _row_tile(rows, 256)

    def body(x_ref, g_ref, t_ref, loss_ref, dx_ref, dxb_ref, dg_ref):
        i = pl.program_id(0)
        xf = x_ref[...]
        gv = g_ref[...]
        r = lax.rsqrt(jnp.mean(xf * xf, axis=-1, keepdims=True) + NORM_EPS)
        xhat = xf * r
        err = xhat * gv - t_ref[...]
        row_loss = jnp.mean(err * err, axis=-1, keepdims=True)
        dy = err * (1.0 / d)
        gdy = dy * gv
        c = jnp.mean(gdy * xhat, axis=-1, keepdims=True)
        dx = r * (gdy - xhat * c)
        dx_ref[...] = dx
        dxb_ref[...] = dx.astype(BF16)

        @pl.when(i == 0)
        def _():
            dg_ref[...] = jnp.zeros_like(dg_ref)
            loss_ref[...] = jnp.zeros_like(loss_ref)

        dg_ref[...] += jnp.sum(dy * xhat, axis=0, keepdims=True)
        loss_ref[...] += 0.5 * jnp.sum(row_loss, axis=0, keepdims=True)

    row_spec = pl.BlockSpec((tm, d), lambda i: (i, 0))
    vec_spec = pl.BlockSpec((1, d), lambda i: (0, 0))
    return pl.pallas_call(
        body, grid=(rows // tm,), in_specs=[row_spec, vec_spec, row_spec],
        out_specs=[pl.BlockSpec((1, 1), lambda i: (0, 0)), row_spec, row_spec, vec_spec],
        out_shape=[jax.ShapeDtypeStruct((1, 1), F32), jax.ShapeDtypeStruct((rows, d), F32),
                   jax.ShapeDtypeStruct((rows, d), BF16), jax.ShapeDtypeStruct((1, d), F32)],
        compiler_params=_params(), name="loss_head",
    )(x, g, target)


def _rotary_tables(positions):
    inv_freq = ROPE_THETA ** (-jnp.arange(ROT_HALF, dtype=F32) / ROT_HALF)
    ang = positions.astype(F32)[:, None] * inv_freq
    cos, sin = jnp.cos(ang), jnp.sin(ang)
    s = positions.shape[0]
    z = jnp.zeros((s, HEAD - 2 * ROT_HALF), F32)
    z16 = jnp.zeros((s, ROT_HALF), F32)
    c = jnp.concatenate([cos, cos, jnp.ones_like(z)], axis=1)
    s1 = jnp.concatenate([z16, sin, z], axis=1)
    s2 = jnp.concatenate([-sin, z16, z], axis=1)
    return c, s1, s2


def _attn_in_proj(h, w_blk, tabs, side=None):
    s, d = h.shape
    nb, _, bw = w_blk.shape
    heads_per_blk = bw // HEAD
    n_rot_heads = 2 * N_GROUPS * HPG
    tm = _row_tile(s, 1024)

    def epilogue(parts, ex, out):
        j = pl.program_id(0)
        acc = parts[0]
        c, s1, s2 = ex[0][...], ex[1][...], ex[2][...]
        for t in range(heads_per_blk):
            seg = acc[:, t * HEAD:(t + 1) * HEAD]
            rot = seg * c + pltpu.roll(seg, ROT_HALF, 1) * s1 + pltpu.roll(seg, HEAD - ROT_HALF, 1) * s2
            is_rot = (j * heads_per_blk + t) < n_rot_heads
            out[0][:, t * HEAD:(t + 1) * HEAD] = jnp.where(is_rot, rot, seg).astype(BF16)

    tab_spec = pl.BlockSpec((tm, HEAD), lambda j, m, k: (m, 0))
    return _one(_matmul(
        "attn_in_proj", (nb, s // tm, 1),
        [(h, pl.BlockSpec((tm, d), lambda j, m, k: (m, 0)),
          w_blk, pl.BlockSpec((None, d, bw), lambda j, m, k: (j, 0, 0)), "nn", 0)],
        1, None, [(tabs[0], tab_spec), (tabs[1], tab_spec), (tabs[2], tab_spec)],
        [(jax.ShapeDtypeStruct((s, nb * bw), BF16), pl.BlockSpec((tm, bw), lambda j, m, k: (m, j)))],
        epilogue, side), side)


ATT_TILE_BLOCKS = 4


def _att_blocks(seq_blocks):
    return min(ATT_TILE_BLOCKS, seq_blocks)


def _band_masks():
    qi = lax.broadcasted_iota(jnp.int32, (BLK, BLK), 0)
    ki = lax.broadcasted_iota(jnp.int32, (BLK, BLK), 1)
    return ki <= qi, ki >= qi


def _attn_fwd(q_arr, k_arr, v_arr, offs, seq_blocks, name):
    s = q_arr.shape[0]
    qo, ko, vo = offs
    nb = _att_blocks(seq_blocks)

    def body(q_ref, kc_ref, kp_ref, vc_ref, vp_ref, o_ref, lse_ref):
        n = pl.program_id(0)
        tile_starts_seq = (n * nb) % seq_blocks == 0
        mask_c, mask_p = _band_masks()
        pairs = [(b * BLK, h * HEAD) for b in range(nb) for h in range(HPG)]

        def keys_prev(ref_c, ref_p, r0, c0):
            return ref_p[:, c0:c0 + HEAD] if r0 == 0 else ref_c[r0 - BLK:r0, c0:c0 + HEAD]

        s_c, s_p = [], []
        for r0, c0 in pairs:
            q = q_ref[r0:r0 + BLK, c0:c0 + HEAD]
            s_c.append(jnp.where(mask_c, _dot(q, kc_ref[r0:r0 + BLK, c0:c0 + HEAD], "nt") * SCALE, NEG_INF))
            sp = jnp.where(mask_p, _dot(q, keys_prev(kc_ref, kp_ref, r0, c0), "nt") * SCALE, NEG_INF)
            s_p.append(jnp.where(tile_starts_seq, NEG_INF, sp) if r0 == 0 else sp)
        m = [jnp.maximum(jnp.max(a, axis=-1, keepdims=True), jnp.max(b, axis=-1, keepdims=True))
             for a, b in zip(s_c, s_p)]
        p_c = [jnp.exp(a - mm) for a, mm in zip(s_c, m)]
        p_p = [jnp.exp(a - mm) for a, mm in zip(s_p, m)]
        l = [jnp.sum(a, axis=-1, keepdims=True) + jnp.sum(b, axis=-1, keepdims=True) for a, b in zip(p_c, p_p)]
        inv = [1.0 / ll for ll in l]
        for i, (r0, c0) in enumerate(pairs):
            o = (_dot((p_c[i] * inv[i]).astype(BF16), vc_ref[r0:r0 + BLK, c0:c0 + HEAD])
                 + _dot((p_p[i] * inv[i]).astype(BF16), keys_prev(vc_ref, vp_ref, r0, c0)))
            o_ref[r0:r0 + BLK, c0:c0 + HEAD] = o.astype(BF16)
            lse_ref[r0:r0 + BLK, c0:c0 + HEAD] = jnp.broadcast_to(m[i] + jnp.log(l[i]), (BLK, HEAD))

    def cur(off):
        return pl.BlockSpec((nb * BLK, GROUP_W), lambda n: (n, off))

    def prev(off):
        return pl.BlockSpec((BLK, GROUP_W), lambda n: (jnp.maximum(n * nb - 1, 0), off))

    return pl.pallas_call(
        body, grid=(s // (nb * BLK),),
        in_specs=[cur(qo), cur(ko), prev(ko), cur(vo), prev(vo)],
        out_specs=[cur(0), cur(0)],
        out_shape=[jax.ShapeDtypeStruct((s, GROUP_W), BF16), jax.ShapeDtypeStruct((s, GROUP_W), F32)],
        compiler_params=_params(), name=name,
    )(q_arr, k_arr, k_arr, v_arr, v_arr)


def _attn_bwd_dq(q_arr, k_arr, v_arr, offs, do, lse, delta, seq_blocks, name):
    s = q_arr.shape[0]
    qo, ko, vo = offs
    nb = _att_blocks(seq_blocks)

    def body(q_ref, kc_ref, kp_ref, vc_ref, vp_ref, do_ref, lse_ref, dl_ref, dq_ref):
        n = pl.program_id(0)
        tile_starts_seq = (n * nb) % seq_blocks == 0
        mask_c, mask_p = _band_masks()
        pairs = [(b * BLK, h * HEAD) for b in range(nb) for h in range(HPG)]

        def prev_blk(ref_c, ref_p, r0, c0):
            return ref_p[:, c0:c0 + HEAD] if r0 == 0 else ref_c[r0 - BLK:r0, c0:c0 + HEAD]

        def own_blk(ref, r0, c0):
            return ref[r0:r0 + BLK, c0:c0 + HEAD]

        s_c, s_p, dp_c, dp_p = [], [], [], []
        for r0, c0 in pairs:
            q, dob = own_blk(q_ref, r0, c0), own_blk(do_ref, r0, c0)
            s_c.append(jnp.where(mask_c, _dot(q, own_blk(kc_ref, r0, c0), "nt") * SCALE, NEG_INF))
            sp = jnp.where(mask_p, _dot(q, prev_blk(kc_ref, kp_ref, r0, c0), "nt") * SCALE, NEG_INF)
            s_p.append(jnp.where(tile_starts_seq, NEG_INF, sp) if r0 == 0 else sp)
            dp_c.append(_dot(dob, own_blk(vc_ref, r0, c0), "nt"))
            dp_p.append(_dot(dob, prev_blk(vc_ref, vp_ref, r0, c0), "nt"))
        ds_c, ds_p = [], []
        for i, (r0, c0) in enumerate(pairs):
            lse_b, dl_b = own_blk(lse_ref, r0, c0), own_blk(dl_ref, r0, c0)
            ds_c.append((jnp.exp(s_c[i] - lse_b) * (dp_c[i] - dl_b) * SCALE).astype(BF16))
            ds_p.append((jnp.exp(s_p[i] - lse_b) * (dp_p[i] - dl_b) * SCALE).astype(BF16))
        for i, (r0, c0) in enumerate(pairs):
            dq = _dot(ds_c[i], own_blk(kc_ref, r0, c0)) + _dot(ds_p[i], prev_blk(kc_ref, kp_ref, r0, c0))
            dq_ref[r0:r0 + BLK, c0:c0 + HEAD] = dq.astype(BF16)

    def cur(off):
        return pl.BlockSpec((nb * BLK, GROUP_W), lambda n: (n, off))

    def prev(off):
        return pl.BlockSpec((BLK, GROUP_W), lambda n: (jnp.maximum(n * nb - 1, 0), off))

    return pl.pallas_call(
        body, grid=(s // (nb * BLK),),
        in_specs=[cur(qo), cur(ko), prev(ko), cur(vo), prev(vo), cur(0), cur(0), cur(0)],
        out_specs=cur(0),
        out_shape=jax.ShapeDtypeStruct((s, GROUP_W), BF16),
        compiler_params=_params(), name=name,
    )(q_arr, k_arr, k_arr, v_arr, v_arr, do, lse, delta)


def _attn_bwd_dkv(q_arr, k_arr, v_arr, offs, do, lse, delta, seq_blocks, name):
    s = q_arr.shape[0]
    qo, ko, vo = offs
    nb = _att_blocks(seq_blocks)
    n_blocks = s // BLK

    def body(k_ref, v_ref, qc_ref, qn_ref, doc_ref, don_ref, lsec_ref, lsen_ref, dlc_ref, dln_ref,
             dk_ref, dv_ref):
        n = pl.program_id(0)
        next_in_seq = ((n + 1) * nb) % seq_blocks != 0
        mask_c, mask_p = _band_masks()
        pairs = [(b * BLK, h * HEAD) for b in range(nb) for h in range(HPG)]
        last = (nb - 1) * BLK

        def own_blk(ref, r0, c0):
            return ref[r0:r0 + BLK, c0:c0 + HEAD]

        def next_blk(ref_c, ref_n, r0, c0):
            return ref_n[:, c0:c0 + HEAD] if r0 == last else ref_c[r0 + BLK:r0 + 2 * BLK, c0:c0 + HEAD]

        s_c, s_p, dp_c, dp_p = [], [], [], []
        for r0, c0 in pairs:
            k, v = own_blk(k_ref, r0, c0), own_blk(v_ref, r0, c0)
            s_c.append(jnp.where(mask_c, _dot(own_blk(qc_ref, r0, c0), k, "nt") * SCALE, NEG_INF))
            sp = jnp.where(mask_p, _dot(next_blk(qc_ref, qn_ref, r0, c0), k, "nt") * SCALE, NEG_INF)
            s_p.append(jnp.where(next_in_seq, sp, NEG_INF) if r0 == last else sp)
            dp_c.append(_dot(own_blk(doc_ref, r0, c0), v, "nt"))
            dp_p.append(_dot(next_blk(doc_ref, don_ref, r0, c0), v, "nt"))
        p_c, p_p, ds_c, ds_p = [], [], [], []
        for i, (r0, c0) in enumerate(pairs):
            pc = jnp.exp(s_c[i] - own_blk(lsec_ref, r0, c0))
            pp = jnp.exp(s_p[i] - next_blk(lsec_ref, lsen_ref, r0, c0))
            ds_c.append((pc * (dp_c[i] - own_blk(dlc_ref, r0, c0)) * SCALE).astype(BF16))
            ds_p.append((pp * (dp_p[i] - next_blk(dlc_ref, dln_ref, r0, c0)) * SCALE).astype(BF16))
            p_c.append(pc.astype(BF16))
            p_p.append(pp.astype(BF16))
        for i, (r0, c0) in enumerate(pairs):
            dv = (_dot(p_c[i], own_blk(doc_ref, r0, c0), "tn")
                  + _dot(p_p[i], next_blk(doc_ref, don_ref, r0, c0), "tn"))
            dk = (_dot(ds_c[i], own_blk(qc_ref, r0, c0), "tn")
                  + _dot(ds_p[i], next_blk(qc_ref, qn_ref, r0, c0), "tn"))
            dk_ref[r0:r0 + BLK, c0:c0 + HEAD] = dk.astype(BF16)
            dv_ref[r0:r0 + BLK, c0:c0 + HEAD] = dv.astype(BF16)

    def cur(off):
        return pl.BlockSpec((nb * BLK, GROUP_W), lambda n: (n, off))

    def nxt(off):
        return pl.BlockSpec((BLK, GROUP_W), lambda n: (jnp.minimum((n + 1) * nb, n_blocks - 1), off))

    return pl.pallas_call(
        body, grid=(s // (nb * BLK),),
        in_specs=[cur(ko), cur(vo), cur(qo), nxt(qo), cur(0), nxt(0), cur(0), nxt(0), cur(0), nxt(0)],
        out_specs=[cur(0), cur(0)],
        out_shape=[jax.ShapeDtypeStruct((s, GROUP_W), BF16), jax.ShapeDtypeStruct((s, GROUP_W), BF16)],
        compiler_params=_params(), name=name,
    )(k_arr, v_arr, q_arr, q_arr, do, do, lse, lse, delta, delta)


def _merge_weights(lse_refs, c0):
    ls = [r[:, c0:c0 + HEAD] for r in lse_refs]
    m = jnp.maximum(jnp.maximum(ls[0], ls[1]), ls[2])
    es = [jnp.exp(l - m) for l in ls]
    inv = 1.0 / (es[0] + es[1] + es[2])
    return [e * inv for e in es]


def _merge_fwd(os_, lses):
    s = os_[0].shape[0]
    tm = _row_tile(s, 512)

    def body(o0, o1, o2, l0, l1, l2, out_ref):
        for h in range(HPG):
            c0 = h * HEAD
            w = _merge_weights((l0, l1, l2), c0)
            acc = None
            for wg, o in zip(w, (o0, o1, o2)):
                t = wg * o[:, c0:c0 + HEAD].astype(F32)
                acc = t if acc is None else acc + t
            out_ref[:, c0:c0 + HEAD] = acc.astype(BF16)

    spec = pl.BlockSpec((tm, GROUP_W), lambda i: (i, 0))
    return pl.pallas_call(
        body, grid=(s // tm,), in_specs=[spec] * 6, out_specs=spec,
        out_shape=jax.ShapeDtypeStruct((s, GROUP_W), BF16),
        compiler_params=_params(), name="merge_fwd",
    )(*os_, *lses)


def _merge_bwd(dcat, os_, lses):
    s = os_[0].shape[0]
    tm = _row_tile(s, 512)

    def body(d_ref, o0, o1, o2, l0, l1, l2, do0, do1, do2, dl0, dl1, dl2):
        for h in range(HPG):
            c0 = h * HEAD
            w = _merge_weights((l0, l1, l2), c0)
            dm = d_ref[:, c0:c0 + HEAD].astype(F32)
            merged = None
            for wg, o in zip(w, (o0, o1, o2)):
                t = wg * o[:, c0:c0 + HEAD].astype(F32)
                merged = t if merged is None else merged + t
            abar = jnp.sum(dm * merged, axis=-1, keepdims=True)
            for wg, do_ref, dl_ref in zip(w, (do0, do1, do2), (dl0, dl1, dl2)):
                do_ref[:, c0:c0 + HEAD] = (wg * dm).astype(BF16)
                dl_ref[:, c0:c0 + HEAD] = wg * abar

    spec = pl.BlockSpec((tm, GROUP_W), lambda i: (i, 0))
    return pl.pallas_call(
        body, grid=(s // tm,), in_specs=[spec] * 7, out_specs=[spec] * 6,
        out_shape=[jax.ShapeDtypeStruct((s, GROUP_W), BF16)] * 3 + [jax.ShapeDtypeStruct((s, GROUP_W), F32)] * 3,
        compiler_params=_params(), name="merge_bwd",
    )(dcat, *os_, *lses)


def _assemble_dproj(dqkv, dqm, tabs):
    s = dqm.shape[0]
    tm = _row_tile(s, 256)
    width = 3 * N_GROUPS * GROUP_W + GROUP_W

    def body(d0, d1, d2, dm_ref, c_ref, s1_ref, s2_ref, out_ref):
        c, s1, s2 = c_ref[...], s1_ref[...], s2_ref[...]
        for g, d_ref in enumerate((d0, d1, d2)):
            for part in range(3):
                for h in range(HPG):
                    src = part * GROUP_W + h * HEAD
                    dst = part * N_GROUPS * GROUP_W + g * GROUP_W + h * HEAD
                    seg = d_ref[:, src:src + HEAD]
                    if part < 2:
                        t = seg.astype(F32)
                        t = t * c - pltpu.roll(t, HEAD - ROT_HALF, 1) * s2 - pltpu.roll(t, ROT_HALF, 1) * s1
                        seg = t.astype(BF16)
                    out_ref[:, dst:dst + HEAD] = seg
        out_ref[:, 3 * N_GROUPS * GROUP_W:] = dm_ref[...]

    g_spec = pl.BlockSpec((tm, 3 * GROUP_W), lambda i: (i, 0))
    m_spec = pl.BlockSpec((tm, GROUP_W), lambda i: (i, 0))
    t_spec = pl.BlockSpec((tm, HEAD), lambda i: (i, 0))
    return pl.pallas_call(
        body, grid=(s // tm,), in_specs=[g_spec] * 3 + [m_spec] + [t_spec] * 3,
        out_specs=pl.BlockSpec((tm, width), lambda i: (i, 0)),
        out_shape=jax.ShapeDtypeStruct((s, width), BF16),
        compiler_params=_params(), name="assemble_dproj",
    )(*dqkv, dqm, *tabs)


def _mem_softmax(q, k):
    s = _dot(q, k, "nt") * SCALE
    m = jnp.max(s, axis=-1, keepdims=True)
    p = jnp.exp(s - m)
    return p * (1.0 / jnp.sum(p, axis=-1, keepdims=True))


def _memattn_fwd(q_arr, q_off, kv, name):
    s = q_arr.shape[0]
    mlen = kv.shape[0]
    tq = _row_tile(s, 512)

    def body(q_ref, kv_ref, o_ref):
        for h in range(HPG):
            c0 = h * HEAD
            p = _mem_softmax(q_ref[:, c0:c0 + HEAD], kv_ref[:, c0:c0 + HEAD])
            o = _dot(p.astype(BF16), kv_ref[:, GROUP_W + c0:GROUP_W + c0 + HEAD])
            o_ref[:, c0:c0 + HEAD] = o.astype(BF16)

    return pl.pallas_call(
        body, grid=(s // tq,),
        in_specs=[pl.BlockSpec((tq, GROUP_W), lambda i: (i, q_off)),
                  pl.BlockSpec((mlen, 2 * GROUP_W), lambda i: (0, 0))],
        out_specs=pl.BlockSpec((tq, GROUP_W), lambda i: (i, 0)),
        out_shape=jax.ShapeDtypeStruct((s, GROUP_W), BF16),
        compiler_params=_params(), name=name,
    )(q_arr, kv)


def _memattn_bwd(q_arr, q_off, kv, dcat, d_off, name):
    s = q_arr.shape[0]
    mlen = kv.shape[0]
    tq = _row_tile(s, 512)

    def body(q_ref, kv_ref, d_ref, dq_ref, dkv_ref):
        i = pl.program_id(0)

        @pl.when(i == 0)
        def _():
            dkv_ref[...] = jnp.zeros_like(dkv_ref)

        for h in range(HPG):
            c0 = h * HEAD
            q = q_ref[:, c0:c0 + HEAD]
            k = kv_ref[:, c0:c0 + HEAD]
            v = kv_ref[:, GROUP_W + c0:GROUP_W + c0 + HEAD]
            do = d_ref[:, c0:c0 + HEAD]
            p = _mem_softmax(q, k)
            dp = _dot(do, v, "nt")
            ds = p * (dp - jnp.sum(p * dp, axis=-1, keepdims=True)) * SCALE
            dsb = ds.astype(BF16)
            dq_ref[:, c0:c0 + HEAD] = _dot(dsb, k).astype(BF16)
            dkv_ref[:, c0:c0 + HEAD] += _dot(dsb, q, "tn")
            dkv_ref[:, GROUP_W + c0:GROUP_W + c0 + HEAD] += _dot(p.astype(BF16), do, "tn")

    return pl.pallas_call(
        body, grid=(s // tq,),
        in_specs=[pl.BlockSpec((tq, GROUP_W), lambda i: (i, q_off)),
                  pl.BlockSpec((mlen, 2 * GROUP_W), lambda i: (0, 0)),
                  pl.BlockSpec((tq, GROUP_W), lambda i: (i, d_off))],
        out_specs=[pl.BlockSpec((tq, GROUP_W), lambda i: (i, 0)),
                   pl.BlockSpec((mlen, 2 * GROUP_W), lambda i: (0, 0))],
        out_shape=[jax.ShapeDtypeStruct((s, GROUP_W), BF16), jax.ShapeDtypeStruct((mlen, 2 * GROUP_W), F32)],
        compiler_params=_params(), name=name,
    )(q_arr, kv, dcat)


SGU_TILE = 256


def _sgu_norm(v):
    vg = _gelu(v)
    mu = jnp.mean(vg, axis=-1, keepdims=True)
    xc = vg - mu
    var = jnp.mean(xc * xc, axis=-1, keepdims=True)
    rstd = lax.rsqrt(var + LN_EPS)
    return xc * rstd, rstd


def _tril_mask():
    r = lax.broadcasted_iota(jnp.int32, (BLK, BLK), 0)
    c = lax.broadcasted_iota(jnp.int32, (BLK, BLK), 1)
    return r >= c


def _sgu_fwd(proj, ln_g, ln_b, w_s, b_st):
    s = proj.shape[0]
    ts = _row_tile(s, SGU_TILE)

    def body(u_ref, v_ref, g_ref, b_ref, ws_ref, bst_ref, o_ref):
        ug = _gelu(u_ref[...].astype(F32))
        xhat, _ = _sgu_norm(v_ref[...].astype(F32))
        vn = (xhat * g_ref[...] + b_ref[...]).astype(BF16)
        tri = _tril_mask()
        for g in range(SGU_GROUPS):
            c0 = g * HEAD
            w = jnp.where(tri, ws_ref[g], 0.0).astype(BF16)
            bias = bst_ref[:, g:g + 1]
            for ch in range(ts // BLK):
                r0 = ch * BLK
                mixed = _dot(w, vn[r0:r0 + BLK, c0:c0 + HEAD]) + bias
                o_ref[r0:r0 + BLK, c0:c0 + HEAD] = (ug[r0:r0 + BLK, c0:c0 + HEAD] * mixed).astype(BF16)

    vec = pl.BlockSpec((1, SGU_W), lambda i: (0, 0))
    return pl.pallas_call(
        body, grid=(s // ts,),
        in_specs=[pl.BlockSpec((ts, SGU_W), lambda i: (i, 0)), pl.BlockSpec((ts, SGU_W), lambda i: (i, 1)),
                  vec, vec, pl.BlockSpec((SGU_GROUPS, BLK, BLK), lambda i: (0, 0, 0)),
                  pl.BlockSpec((BLK, SGU_GROUPS), lambda i: (0, 0))],
        out_specs=pl.BlockSpec((ts, SGU_W), lambda i: (i, 0)),
        out_shape=jax.ShapeDtypeStruct((s, SGU_W), BF16),
        compiler_params=_params(), name="sgu_fwd",
    )(proj, proj, ln_g, ln_b, w_s, b_st)


def _sgu_bwd(proj, dcat, ln_g, ln_b, w_s, b_st):
    s = proj.shape[0]
    ts = _row_tile(s, SGU_TILE)

    def body(u_ref, v_ref, d_ref, g_ref, b_ref, ws_ref, bst_ref,
             duv_ref, dws_ref, dbs_ref, dg_ref, db_ref, dvn_ref):
        i = pl.program_id(0)

        @pl.when(i == 0)
        def _():
            dws_ref[...] = jnp.zeros_like(dws_ref)
            dbs_ref[...] = jnp.zeros_like(dbs_ref)
            dg_ref[...] = jnp.zeros_like(dg_ref)
            db_ref[...] = jnp.zeros_like(db_ref)

        u = u_ref[...].astype(F32)
        v = v_ref[...].astype(F32)
        ug = _gelu(u)
        xhat, rstd = _sgu_norm(v)
        lng = g_ref[...]
        vn = (xhat * lng + b_ref[...]).astype(BF16)
        dout = d_ref[...].astype(F32)
        tri = _tril_mask()
        lane = lax.broadcasted_iota(jnp.int32, (BLK, BLK), 1)
        dbs = jnp.zeros((BLK, BLK), F32)
        for g in range(SGU_GROUPS):
            c0 = g * HEAD
            w = jnp.where(tri, ws_ref[g], 0.0).astype(BF16)
            bias = bst_ref[:, g:g + 1]
            dws = jnp.zeros((BLK, BLK), F32)
            for ch in range(ts // BLK):
                r0 = ch * BLK
                vn_gc = vn[r0:r0 + BLK, c0:c0 + HEAD]
                mixed = _dot(w, vn_gc) + bias
                do_gc = dout[r0:r0 + BLK, c0:c0 + HEAD]
                dmixed = do_gc * ug[r0:r0 + BLK, c0:c0 + HEAD]
                du = do_gc * mixed * _gelu_grad(u[r0:r0 + BLK, c0:c0 + HEAD])
                duv_ref[r0:r0 + BLK, c0:c0 + HEAD] = du.astype(BF16)
                dmb = dmixed.astype(BF16)
                dws = dws + _dot(dmb, vn_gc, "nt")
                dbs = dbs + jnp.where(lane == g, jnp.sum(dmixed, axis=-1, keepdims=True), 0.0)
                dvn_ref[r0:r0 + BLK, c0:c0 + HEAD] = _dot(w, dmb, "tn")
            dws_ref[g] += jnp.where(tri, dws, 0.0)
        dbs_ref[...] += dbs
        dvn = dvn_ref[...]
        gd = dvn * lng
        c1 = jnp.mean(gd, axis=-1, keepdims=True)
        c2 = jnp.mean(gd * xhat, axis=-1, keepdims=True)
        dvg = rstd * (gd - c1 - xhat * c2)
        duv_ref[:, SGU_W:] = (dvg * _gelu_grad(v)).astype(BF16)
        dg_ref[...] += jnp.sum(dvn * xhat, axis=0, keepdims=True)
        db_ref[...] += jnp.sum(dvn, axis=0, keepdims=True)

    vec = pl.BlockSpec((1, SGU_W), lambda i: (0, 0))
    ws_spec = pl.BlockSpec((SGU_GROUPS, BLK, BLK), lambda i: (0, 0, 0))
    return pl.pallas_call(
        body, grid=(s // ts,),
        in_specs=[pl.BlockSpec((ts, SGU_W), lambda i: (i, 0)), pl.BlockSpec((ts, SGU_W), lambda i: (i, 1)),
                  pl.BlockSpec((ts, SGU_W), lambda i: (i, 0)),
                  vec, vec, ws_spec, pl.BlockSpec((BLK, SGU_GROUPS), lambda i: (0, 0))],
        out_specs=[pl.BlockSpec((ts, 2 * SGU_W), lambda i: (i, 0)), ws_spec,
                   pl.BlockSpec((BLK, BLK), lambda i: (0, 0)), vec, vec],
        out_shape=[jax.ShapeDtypeStruct((s, 2 * SGU_W), BF16),
                   jax.ShapeDtypeStruct((SGU_GROUPS, BLK, BLK), F32),
                   jax.ShapeDtypeStruct((BLK, BLK), F32),
                   jax.ShapeDtypeStruct((1, SGU_W), F32), jax.ShapeDtypeStruct((1, SGU_W), F32)],
        scratch_shapes=[pltpu.VMEM((ts, SGU_W), F32)],
        compiler_params=_params(), name="sgu_bwd",
    )(proj, proj, dcat, ln_g, ln_b, w_s, b_st)


def _swiglu_fwd(h, wg_t, wu_t, name, side=None):
    s, d = h.shape
    f = wg_t.shape[0]
    tm, tn = _row_tile(s, 1024), _row_tile(f, 512)

    def epilogue(parts, ex, out):
        g, u = parts
        sg = _sigmoid(g)
        silu = g * sg
        out[0][...] = silu.astype(BF16)
        out[1][...] = (u * (sg + silu * (1.0 - sg))).astype(BF16)
        out[2][...] = (silu * u).astype(BF16)

    a_spec = pl.BlockSpec((tm, d), lambda n, m, k: (m, 0))
    w_spec = pl.BlockSpec((tn, d), lambda n, m, k: (n, 0))
    o_spec = pl.BlockSpec((tm, tn), lambda n, m, k: (m, n))
    o_shape = jax.ShapeDtypeStruct((s, f), BF16)
    return _matmul(name, (f // tn, s // tm, 1),
                   [(h, a_spec, wg_t, w_spec, "nt", 0), (h, a_spec, wu_t, w_spec, "nt", 1)],
                   2, None, [], [(o_shape, o_spec)] * 3, epilogue, side)


def _swiglu_down(hid, wd, res, name, side=None):
    s, f = hid.shape
    d = wd.shape[1]
    tm, tn = _row_tile(s, 1024), _row_tile(d, 512)
    o_spec = pl.BlockSpec((tm, tn), lambda n, m, k: (m, n))
    return _one(_matmul(
        name, (d // tn, s // tm, 1),
        [(hid, pl.BlockSpec((tm, f), lambda n, m, k: (m, 0)),
          wd, pl.BlockSpec((f, tn), lambda n, m, k: (0, n)), "nn", 0)],
        1, None, [(res, o_spec)], [(jax.ShapeDtypeStruct((s, d), F32), o_spec)],
        _residual_epilogue, side), side)


def _swiglu_bwd_hidden(dxb, wd, silu, up_dsilu, name):
    s, f = silu.shape
    d = dxb.shape[1]
    tm, tn = _row_tile(s, 1024), _row_tile(f, 512)

    def epilogue(parts, ex, out):
        dh = parts[0]
        out[0][...] = (dh * ex[1][...].astype(F32)).astype(BF16)
        out[1][...] = (dh * ex[0][...].astype(F32)).astype(BF16)

    blk = pl.BlockSpec((tm, tn), lambda n, m, k: (m, n))
    o_shape = jax.ShapeDtypeStruct((s, f), BF16)
    return _matmul(
        name, (f // tn, s // tm, 1),
        [(dxb, pl.BlockSpec((tm, d), lambda n, m, k: (m, 0)),
          wd, pl.BlockSpec((tn, d), lambda n, m, k: (n, 0)), "nt", 0)],
        1, None, [(silu, blk), (up_dsilu, blk)], [(o_shape, blk)] * 2, epilogue)


def _swiglu_bwd_input(dgate, dup, wg_t, wu_t, name, side=None):
    s, f = dgate.shape
    d = wg_t.shape[1]
    tm, tn, tk = _row_tile(s, 1024), _row_tile(d, 512), f // 2
    a_spec = pl.BlockSpec((tm, tk), lambda n, m, k: (m, k))
    w_spec = pl.BlockSpec((tk, tn), lambda n, m, k: (k, n))
    return _one(_matmul(
        name, (d // tn, s // tm, f // tk),
        [(dgate, a_spec, wg_t, w_spec, "nn", 0), (dup, a_spec, wu_t, w_spec, "nn", 0)],
        1, (tm, tn), [],
        [(jax.ShapeDtypeStruct((s, d), F32), pl.BlockSpec((tm, tn), lambda n, m, k: (m, n)))],
        _store_epilogue, side), side)


def _mm_tn_full(a, b, name, side=None):
    s, m = a.shape
    n = b.shape[1]
    tm, tn = _row_tile(m, 512), _row_tile(n, 512)
    return _one(_matmul(
        name, (m // tm, n // tn, 1),
        [(a, pl.BlockSpec((s, tm), lambda i, j, k: (0, i)),
          b, pl.BlockSpec((s, tn), lambda i, j, k: (0, j)), "tn", 0)],
        1, None, [],
        [(jax.ShapeDtypeStruct((m, n), BF16), pl.BlockSpec((tm, tn), lambda i, j, k: (i, j)))],
        _store_epilogue, side), side)


def _mm_nn(a, b, name, tn, out_dtype=BF16, res=None, side=None):
    m, k = a.shape
    n = b.shape[1]
    tm = _row_tile(m, 1024)
    extras = [] if res is None else [(res, pl.BlockSpec((tm, tn), lambda j, i, kk: (i, j)))]
    return _one(_matmul(
        name, (n // tn, m // tm, 1),
        [(a, pl.BlockSpec((tm, k), lambda j, i, kk: (i, 0)),
          b, pl.BlockSpec((k, tn), lambda j, i, kk: (0, j)), "nn", 0)],
        1, None, extras,
        [(jax.ShapeDtypeStruct((m, n), out_dtype), pl.BlockSpec((tm, tn), lambda j, i, kk: (i, j)))],
        _store_epilogue if res is None else _residual_epilogue, side), side)


def _mm_nn_colblocked(a, b_blk, name, res, side=None):
    m, k = a.shape
    nb, _, bw = b_blk.shape
    tm = _row_tile(m, 2048)
    o_spec = pl.BlockSpec((tm, bw), lambda j, i, kk: (i, j))
    return _one(_matmul(
        name, (nb, m // tm, 1),
        [(a, pl.BlockSpec((tm, k), lambda j, i, kk: (i, 0)),
          b_blk, pl.BlockSpec((None, k, bw), lambda j, i, kk: (j, 0, 0)), "nn", 0)],
        1, None, [(res, o_spec)],
        [(jax.ShapeDtypeStruct((m, nb * bw), F32), o_spec)], _residual_epilogue, side), side)


def _mm_nt_colblocked(a, b_blk, name, out_dtype, jb, side=None):
    m = a.shape[0]
    nb, n, bw = b_blk.shape
    tm = _row_tile(m, 512)
    return _one(_matmul(
        name, (m // tm, nb // jb),
        [(a, pl.BlockSpec((tm, jb * bw), lambda i, j: (i, j)),
          b_blk, pl.BlockSpec((jb, n, bw), lambda i, j: (j, 0, 0)), "nt", 0)],
        1, (tm, n), [],
        [(jax.ShapeDtypeStruct((m, n), out_dtype), pl.BlockSpec((tm, n), lambda i, j: (i, 0)))],
        _store_epilogue, side), side)


def _mm_nt_rowblocked(a, b, name, tn, out_dtype, side=None):
    m, k = a.shape
    n = b.shape[0]
    tm = _row_tile(m, 1024)
    return _one(_matmul(
        name, (n // tn, m // tm, 1),
        [(a, pl.BlockSpec((tm, k), lambda j, i, kk: (i, 0)),
          b, pl.BlockSpec((tn, k), lambda j, i, kk: (j, 0)), "nt", 0)],
        1, None, [],
        [(jax.ShapeDtypeStruct((m, n), out_dtype), pl.BlockSpec((tm, tn), lambda j, i, kk: (i, j)))],
        _store_epilogue, side), side)


def _mm_tn_colblocked(a, b, name, bw, side=None):
    s, m = a.shape
    nb = b.shape[1] // bw
    tm = _row_tile(m, 512)
    return _one(_matmul(
        name, (nb, m // tm, 1),
        [(a, pl.BlockSpec((s, tm), lambda j, i, k: (0, i)),
          b, pl.BlockSpec((s, bw), lambda j, i, k: (0, j)), "tn", 0)],
        1, None, [],
        [(jax.ShapeDtypeStruct((nb, m, bw), BF16), pl.BlockSpec((None, tm, bw), lambda j, i, k: (j, i, 0)))],
        _store_epilogue, side), side)


def _mm_tn_rowblocked(a, b, name, bh):
    s, n = b.shape
    nb = a.shape[1] // bh
    tn = _row_tile(n, 512)
    return _matmul(
        name, (nb, n // tn, 1),
        [(a, pl.BlockSpec((s, bh), lambda j, i, k: (0, j)),
          b, pl.BlockSpec((s, tn), lambda j, i, k: (0, i)), "tn", 0)],
        1, None, [],
        [(jax.ShapeDtypeStruct((nb, bh, n), BF16), pl.BlockSpec((None, bh, tn), lambda j, i, k: (j, 0, i)))],
        _store_epilogue)[0]


def _as2d(a):
    return a.reshape(-1, a.shape[-1])


def _cast_bf16(w, name):
    w2 = _as2d(w)
    rows, cols = w2.shape
    tr = _row_tile(rows, 256)

    def body(w_ref, o_ref):
        o_ref[...] = w_ref[...].astype(BF16)

    spec = pl.BlockSpec((tr, cols), lambda i: (i, 0))
    out = pl.pallas_call(
        body, grid=(rows // tr,), in_specs=[spec], out_specs=spec,
        out_shape=jax.ShapeDtypeStruct((rows, cols), BF16),
        compiler_params=_params(), name=name,
    )(w2)
    return out.reshape(w.shape)


def _cast_bf16_layer(w, layer, name):
    _, rows, cols = w.shape
    tr = _row_tile(rows, 256)

    def body(w_ref, o_ref):
        o_ref[...] = w_ref[...].astype(BF16)

    return pl.pallas_call(
        body, grid=(rows // tr,),
        in_specs=[pl.BlockSpec((None, tr, cols), lambda i: (layer, i, 0))],
        out_specs=pl.BlockSpec((tr, cols), lambda i: (i, 0)),
        out_shape=jax.ShapeDtypeStruct((rows, cols), BF16),
        compiler_params=_params(), name=name,
    )(w)


def _reduce_adam(recvs, w, m, v, name):
    n_layers, rows, cols = w.shape
    n_slots = recvs[0].shape[0]
    tr = _row_tile(rows, max(16, (128 * 1024 // cols) // 16 * 16))
    nt = rows // tr
    c1 = 1.0 - ADAM_B1 ** ADAM_STEP
    c2 = 1.0 - ADAM_B2 ** ADAM_STEP

    def body(*refs):
        r_refs = refs[:n_layers]
        w_ref, m_ref, v_ref, g_out, d_out, m_out, v_out = refs[n_layers:]
        layer = pl.program_id(0)

        def update(r_ref):
            g = r_ref[0].astype(F32)
            for k in range(1, n_slots):
                g = g + r_ref[k].astype(F32)
            mm = ADAM_B1 * m_ref[...] + (1.0 - ADAM_B1) * g
            vv = ADAM_B2 * v_ref[...] + (1.0 - ADAM_B2) * (g * g)
            m_hat = mm / c1
            v_hat = vv / c2
            g_out[...] = g
            d_out[...] = -ADAM_LR * (m_hat / (jnp.sqrt(v_hat) + ADAM_EPS) + ADAM_WD * w_ref[...])
            m_out[...] = mm
            v_out[...] = vv

        for li in range(n_layers):
            if n_layers == 1:
                update(r_refs[li])
            else:
                pl.when(layer == li)(functools.partial(update, r_refs[li]))

    def recv_spec(li):
        def imap(layer, i):
            return (0, jnp.where(layer == li, i, jnp.where(layer < li, 0, nt - 1)), 0)
        return pl.BlockSpec((n_slots, tr, cols), imap)

    spec = pl.BlockSpec((None, tr, cols), lambda layer, i: (layer, i, 0))
    o_shape = jax.ShapeDtypeStruct(w.shape, F32)
    return pl.pallas_call(
        body, grid=(n_layers, nt),
        in_specs=[recv_spec(li) for li in range(n_layers)] + [spec] * 3,
        out_specs=[spec] * 4, out_shape=[o_shape] * 4,
        compiler_params=_params(), name=name,
    )(*recvs, w, m, v)


def _my_place():
    return lax.axis_index("x"), lax.axis_index("y"), lax.axis_index("c")


class _GatherSide:
    def __init__(self, blocks):
        self.ins = list(blocks)
        self.outs = [jax.ShapeDtypeStruct((N_DEV,) + b.shape, b.dtype) for b in blocks]

    def scratch(self):
        n = len(self.ins)
        return [pltpu.SemaphoreType.DMA((7 * n,)), pltpu.SemaphoreType.DMA((7 * n,)),
                pltpu.SemaphoreType.DMA((n,))]

    def phases(self, x_refs, out_refs, send_sems, recv_sems, local_sems):
        n = len(self.ins)
        x, y, c = _my_place()
        me, sibling = (x, y, c), (x, y, 1 - c)
        chips = [(1 - x, y), (x, 1 - y), (1 - x, 1 - y)]

        def slot(t, px, py, pc):
            return out_refs[t].at[4 * px + 2 * py + pc]

        def copy(t, k, blk, to, src=None):
            return pltpu.make_async_remote_copy(
                src_ref=slot(t, *blk) if src is None else src, dst_ref=slot(t, *blk),
                send_sem=send_sems.at[7 * t + k], recv_sem=recv_sems.at[7 * t + k],
                device_id=to, device_id_type=pl.DeviceIdType.MESH)

        def own(t):
            return pltpu.make_async_copy(x_refs[t], slot(t, *me), local_sems.at[t])

        def first(t):
            return [copy(t, 0, me, sibling, src=x_refs[t])] + [
                copy(t, 1 + j, me, (*chip, c), src=x_refs[t]) for j, chip in enumerate(chips)]

        def passed(t):
            return [copy(t, 4 + j, (*chip, c), sibling) for j, chip in enumerate(chips)]

        def start():
            for t in range(n):
                own(t).start()
                for cp in first(t):
                    cp.start()

        def mid():
            for t in range(n):
                fwd = passed(t)
                for j, chip in enumerate(chips):
                    copy(t, 1 + j, (*chip, c), me).wait_recv()
                    fwd[j].start()

        def finish():
            for t in range(n):
                copy(t, 0, sibling, me).wait_recv()
                for j, chip in enumerate(chips):
                    copy(t, 4 + j, (*chip, 1 - c), me).wait_recv()
                for cp in first(t) + passed(t):
                    cp.wait_send()
                own(t).wait()

        return start, mid, finish


class _ExchangeSide:
    def __init__(self, blocked):
        self.ins = list(blocked)
        self.outs = [jax.ShapeDtypeStruct(b.shape, b.dtype) for b in blocked]

    def scratch(self):
        n = len(self.ins)
        return [pltpu.SemaphoreType.DMA((7 * n,)), pltpu.SemaphoreType.DMA((7 * n,)),
                pltpu.SemaphoreType.DMA((n,))]

    def phases(self, srcs, dsts, send_sems, recv_sems, local_sems):
        n = len(self.ins)
        x, y, c = _my_place()
        me = 4 * x + 2 * y + c

        def own(t):
            return pltpu.make_async_copy(srcs[t].at[me], dsts[t].at[me], local_sems.at[t])

        def copies(t, arriving):
            res = []
            for k in range(1, N_DEV):
                px, py, pc = x ^ (k >> 2), y ^ ((k >> 1) & 1), c ^ (k & 1)
                peer = 4 * px + 2 * py + pc
                sem = 7 * t + k - 1
                res.append(pltpu.make_async_remote_copy(
                    src_ref=srcs[t].at[peer], dst_ref=dsts[t].at[peer if arriving else me],
                    send_sem=send_sems.at[sem], recv_sem=recv_sems.at[sem],
                    device_id=(px, py, pc), device_id_type=pl.DeviceIdType.MESH))
            return res

        def start():
            for t in range(n):
                own(t).start()
                for send in copies(t, False):
                    send.start()

        def mid():
            pass

        def finish():
            for t in range(n):
                for arrival in copies(t, True):
                    arrival.wait_recv()
                for send in copies(t, False):
                    send.wait_send()
                own(t).wait()

        return start, mid, finish


def _run_side(side, name):
    n_in, n_out = len(side.ins), len(side.outs)

    def body(*refs):
        start, mid, finish = side.phases(refs[:n_in], refs[n_in:n_in + n_out], *refs[n_in + n_out:])
        start()
        mid()
        finish()

    hbm = pl.BlockSpec(memory_space=pltpu.HBM)
    return pl.pallas_call(
        body, out_shape=list(side.outs), in_specs=[hbm] * n_in, out_specs=[hbm] * n_out,
        scratch_shapes=side.scratch(), name=name,
    )(*side.ins)


def _to_residue_major(a, dilation):
    s, w = a.shape
    return a.reshape(s // dilation, dilation, w).transpose(1, 0, 2).reshape(s, w)


def _from_residue_major(a, dilation):
    s, w = a.shape
    return a.reshape(dilation, s // dilation, w).transpose(1, 0, 2).reshape(s, w)


def _mem_kv(mem, gain, wkv, layer, tag):
    mem_n = _rmsnorm_fwd(mem, gain, "mem_norm_" + tag)
    mlen, d = mem.shape
    nb, _, bh, n = wkv.shape
    kv = _matmul(
        "mem_kv_" + tag, (1, nb),
        [(mem_n, pl.BlockSpec((mlen, bh), lambda i, j: (0, j)),
          wkv, pl.BlockSpec((None, None, bh, n), lambda i, j: (j, layer, 0, 0)), "nn", 0)],
        1, (mlen, n), [],
        [(jax.ShapeDtypeStruct((mlen, n), BF16), pl.BlockSpec((mlen, n), lambda i, j: (0, 0)))],
        _store_epilogue)[0]
    return mem_n, kv


def _mem_kv_bwd(mem, gain, mem_n, wkv, layer, dkv, tag):
    mlen, d = mem.shape
    nb, _, bh, n = wkv.shape
    dkvb = dkv.astype(BF16)
    dw = _mm_tn_rowblocked(mem_n, dkvb, "mem_kv_dw_" + tag, bh)
    dmem_n = _matmul(
        "mem_kv_dx_" + tag, (nb, 1),
        [(dkvb, pl.BlockSpec((mlen, n), lambda j, k: (0, 0)),
          wkv, pl.BlockSpec((None, None, bh, n), lambda j, k: (j, layer, 0, 0)), "nt", 0)],
        1, None, [],
        [(jax.ShapeDtypeStruct((mlen, d), F32), pl.BlockSpec((mlen, bh), lambda j, k: (0, j)))],
        _store_epilogue)[0]
    _, _, dgain = _rmsnorm_bwd(mem, gain, dmem_n, None, "mem_norm_bwd_" + tag)
    return dw, dgain


def _row_blocks(a):
    return a.reshape(N_DEV, -1, a.shape[-1])


def _rows(a):
    return a.reshape(-1, a.shape[-1])


def _ffn_bwd(x, gain, w_gate, w_up, w_down, saved, dx, dxb, tag):
    hf, silu, up_dsilu, hid = saved
    dgate, dup = _swiglu_bwd_hidden(dxb, w_down, silu, up_dsilu, "swiglu_bwd_hidden_" + tag)
    dwd = _mm_tn_full(hid, dxb, "swiglu_bwd_wdown_" + tag)
    dwg, (r_wd,) = _mm_tn_full(dgate, hf, "swiglu_bwd_wgate_" + tag, _ExchangeSide([_row_blocks(dwd)]))
    dwu, (r_wg,) = _mm_tn_full(dup, hf, "swiglu_bwd_wup_" + tag, _ExchangeSide([_row_blocks(dwg)]))
    dhf, (r_wu,) = _swiglu_bwd_input(dgate, dup, w_gate, w_up, "swiglu_bwd_input_" + tag,
                                     _ExchangeSide([_row_blocks(dwu)]))
    dx_new, dxb_new, dgain = _rmsnorm_bwd(x, gain, dhf, dx, "ffn_norm_bwd_" + tag)
    return dx_new, dxb_new, dgain, r_wg, r_wu, r_wd


def _local_step(x, mem, positions, target, w_attn_in, shards, small):
    s, d = x.shape
    tabs = _rotary_tables(positions)
    mix_norm, mem_norm, ffn_norm = small["mix_norm"], small["mem_norm"], small["ffn_norm"]

    h0 = _rmsnorm_fwd(x, mix_norm[0:1], "mix_norm_0")
    proj0, (w_mem_kv, w_attn_out, w_gate0) = _attn_in_proj(
        h0, w_attn_in, tabs, _GatherSide([shards["w_mem_kv"], shards["attn_w_out"], shards["w_gate"][0]]))
    qkv, offs, outs, lses = [], [], [], []
    for g, dil in enumerate(DILATIONS):
        if dil == 1:
            arr, off = proj0, (g, N_GROUPS + g, 2 * N_GROUPS + g)
        else:
            cols = [proj0[:, (p * N_GROUPS + g) * GROUP_W:(p * N_GROUPS + g + 1) * GROUP_W] for p in range(3)]
            arr, off = _to_residue_major(jnp.concatenate(cols, axis=1), dil), (0, 1, 2)
        o, lse = _attn_fwd(arr, arr, arr, off, s // dil // BLK, "attn_fwd_%d" % g)
        qkv.append(arr)
        offs.append(off)
        if dil > 1:
            o, lse = _from_residue_major(o, dil), _from_residue_major(lse, dil)
        outs.append(o)
        lses.append(lse)
    mix0 = _merge_fwd(outs, lses)
    qm_off0 = 3 * N_GROUPS
    mem_n0, kv0 = _mem_kv(mem, mem_norm[0:1], w_mem_kv, 0, "0")
    memo0 = _memattn_fwd(proj0, qm_off0, kv0, "memattn_fwd_0")
    cat0 = jnp.concatenate([mix0, memo0], axis=1)
    x1, (w_up0,) = _mm_nn_colblocked(cat0, w_attn_out, "attn_out_proj", x, _GatherSide([shards["w_up"][0]]))
    hf0 = _rmsnorm_fwd(x1, ffn_norm[0:1], "ffn_norm_0")
    w_gate0, w_up0 = _rows(w_gate0), _rows(w_up0)
    (silu0, ud0, hid0), (w_down0, w_sgu_in, w_sgu_out) = _swiglu_fwd(
        hf0, w_gate0, w_up0, "swiglu_fwd_0",
        _GatherSide([shards["w_down"][0], shards["sgu_w_in"], shards["sgu_w_out"]]))
    w_down0 = _rows(w_down0)
    x2, (w_gate1,) = _swiglu_down(hid0, w_down0, x1, "swiglu_down_0", _GatherSide([shards["w_gate"][1]]))
    ffn_saved0 = (hf0, silu0, ud0, hid0)

    h1 = _rmsnorm_fwd(x2, mix_norm[1:2], "mix_norm_1")
    w_sgu_in = _rows(w_sgu_in)
    w_sgu_out = _rows(w_sgu_out)
    proj1, (w_up1,) = _mm_nt_rowblocked(h1, w_sgu_in, "sgu_in_proj", w_sgu_in.shape[0] // 7, BF16,
                                        _GatherSide([shards["w_up"][1]]))
    w_gate1, w_up1 = _rows(w_gate1), _rows(w_up1)
    b_st = small["sgu_b_spatial"].T
    mix1 = _sgu_fwd(proj1, small["sgu_ln_g"], small["sgu_ln_b"], small["sgu_w_spatial"], b_st)
    qm_off1 = 2 * SGU_W // GROUP_W
    mem_n1, kv1 = _mem_kv(mem, mem_norm[1:2], w_mem_kv, 1, "1")
    memo1 = _memattn_fwd(proj1, qm_off1, kv1, "memattn_fwd_1")
    cat1 = jnp.concatenate([mix1, memo1], axis=1)
    x3 = _mm_nn(cat1, w_sgu_out, "sgu_out_proj", d // 2, out_dtype=F32, res=x2)
    hf1 = _rmsnorm_fwd(x3, ffn_norm[1:2], "ffn_norm_1")
    (silu1, ud1, hid1), (w_down1,) = _swiglu_fwd(hf1, w_gate1, w_up1, "swiglu_fwd_1",
                                                 _GatherSide([shards["w_down"][1]]))
    w_down1 = _rows(w_down1)
    x4 = _swiglu_down(hid1, w_down1, x3, "swiglu_down_1")
    ffn_saved1 = (hf1, silu1, ud1, hid1)

    loss, dx, dxb, d_final = _loss_head(x4, small["final_norm"], target)

    recvs, sgrads = {}, {}
    dx, dxb, d_ffn1, r_wg1, r_wu1, r_wd1 = _ffn_bwd(x3, ffn_norm[1:2], w_gate1, w_up1, w_down1, ffn_saved1,
                                                    dx, dxb, "1")
    dcat1 = _mm_nt_rowblocked(dxb, w_sgu_out, "sgu_out_proj_dx", w_sgu_out.shape[0] // 2, BF16)
    dwsout = _mm_tn_rowblocked(cat1, dxb, "sgu_out_proj_dw", w_sgu_out.shape[0] // N_DEV)
    duv, dws, dbs, dlng, dlnb = _sgu_bwd(proj1, dcat1, small["sgu_ln_g"], small["sgu_ln_b"],
                                         small["sgu_w_spatial"], b_st)
    dqm1, dkv1 = _memattn_bwd(proj1, qm_off1, kv1, dcat1, SGU_W // GROUP_W, "memattn_bwd_1")
    dwkv1, d_memnorm1 = _mem_kv_bwd(mem, mem_norm[1:2], mem_n1, w_mem_kv, 1, dkv1, "1")
    dproj1 = jnp.concatenate([duv, dqm1], axis=1)
    dwsin, (r_wsout, r_wkv1) = _mm_tn_full(dproj1, h1, "sgu_in_proj_dw", _ExchangeSide([dwsout, dwkv1]))
    dh1, (r_wsin,) = _mm_nn(dproj1, w_sgu_in, "sgu_in_proj_dx", d // 4, out_dtype=F32,
                            side=_ExchangeSide([_row_blocks(dwsin)]))
    dx, dxb, d_mix1 = _rmsnorm_bwd(x2, mix_norm[1:2], dh1, dx, "mix_norm_bwd_1")

    dx, dxb, d_ffn0, r_wg0, r_wu0, r_wd0 = _ffn_bwd(x1, ffn_norm[0:1], w_gate0, w_up0, w_down0, ffn_saved0,
                                                    dx, dxb, "0")
    dcat0 = _mm_nt_colblocked(dxb, w_attn_out, "attn_out_proj_dx", BF16, 4)
    dwout0 = _mm_tn_colblocked(cat0, dxb, "attn_out_proj_dw", w_attn_out.shape[2])
    dos_and_deltas = _merge_bwd(dcat0, outs, lses)
    dqkv = []
    for g, dil in enumerate(DILATIONS):
        do_g, dl_g = dos_and_deltas[g], dos_and_deltas[N_GROUPS + g]
        lse_g = lses[g]
        if dil > 1:
            do_g, dl_g, lse_g = (_to_residue_major(t, dil) for t in (do_g, dl_g, lse_g))
        args = (qkv[g], qkv[g], qkv[g], offs[g], do_g, lse_g, dl_g, s // dil // BLK)
        dq = _attn_bwd_dq(*args, "attn_bwd_dq_%d" % g)
        dk, dv = _attn_bwd_dkv(*args, "attn_bwd_dkv_%d" % g)
        t = jnp.concatenate([dq, dk, dv], axis=1)
        dqkv.append(_from_residue_major(t, dil) if dil > 1 else t)
    dqm0, dkv0 = _memattn_bwd(proj0, qm_off0, kv0, dcat0, 1, "memattn_bwd_0")
    dwkv0, d_memnorm0 = _mem_kv_bwd(mem, mem_norm[0:1], mem_n0, w_mem_kv, 0, dkv0, "0")
    dproj0 = _assemble_dproj(dqkv, dqm0, tabs)
    dwin0, (r_wout0, r_wkv0) = _mm_tn_colblocked(h0, dproj0, "attn_in_proj_dw", w_attn_in.shape[2],
                                                 _ExchangeSide([dwout0, dwkv0]))
    dh0, (r_win0,) = _mm_nt_colblocked(dproj0, w_attn_in, "attn_in_proj_dx", F32, 2, _ExchangeSide([dwin0]))
    grad_x, _, d_mix0 = _rmsnorm_bwd(x, mix_norm[0:1], dh0, dx, "mix_norm_bwd_0")

    recvs["w_gate"] = [r_wg0, r_wg1]
    recvs["w_up"] = [r_wu0, r_wu1]
    recvs["w_down"] = [r_wd0, r_wd1]
    recvs["w_mem_kv"] = [r_wkv0, r_wkv1]
    recvs["attn_w_in"] = [r_win0]
    recvs["attn_w_out"] = [r_wout0]
    recvs["sgu_w_in"] = [r_wsin]
    recvs["sgu_w_out"] = [r_wsout]
    sgrads["mix_norm"] = jnp.concatenate([d_mix0, d_mix1], axis=0)
    sgrads["mem_norm"] = jnp.concatenate([d_memnorm0, d_memnorm1], axis=0)
    sgrads["ffn_norm"] = jnp.concatenate([d_ffn0, d_ffn1], axis=0)
    sgrads["final_norm"] = d_final
    sgrads["sgu_w_spatial"] = dws
    sgrads["sgu_b_spatial"] = dbs[:, :SGU_GROUPS].T
    sgrads["sgu_ln_g"] = dlng
    sgrads["sgu_ln_b"] = dlnb
    return loss, grad_x, recvs, sgrads


BIG = ("w_mem_kv", "w_gate", "w_up", "w_down", "attn_w_in", "attn_w_out", "sgu_w_in", "sgu_w_out")
TRANSPOSED = ("w_gate", "w_up", "sgu_w_in")
SMALL_REPLICATED = ("mix_norm", "mem_norm", "ffn_norm", "final_norm", "sgu_w_spatial", "sgu_b_spatial")
SMALL_SHARDED = ("sgu_ln_g", "sgu_ln_b")
WEIGHT_ORDER = ("mix_norm", "mem_norm", "w_mem_kv", "ffn_norm", "w_gate", "w_up", "w_down", "attn_w_in",
                "attn_w_out", "sgu_w_in", "sgu_ln_g", "sgu_ln_b", "sgu_w_spatial", "sgu_b_spatial",
                "sgu_w_out", "final_norm")
PACK_LANES = 128


def _pack(parts):
    flat = [p.reshape(-1) for p in parts]
    sizes = [f.shape[0] for f in flat]
    total = sum(sizes)
    rows = -(-total // PACK_LANES)
    rows = -(-rows // 8) * 8
    pad = rows * PACK_LANES - total
    packed = jnp.concatenate(flat + [jnp.zeros((pad,), F32)]).reshape(rows, PACK_LANES)
    offs, o = [], 0
    for sz in sizes:
        offs.append(o)
        o += sz
    return packed, offs, sizes


def _unpack(packed, offs, sizes, shapes):
    flat = packed.reshape(-1)
    return [flat[o:o + sz].reshape(shp) for o, sz, shp in zip(offs, sizes, shapes)]


def kernel(x, mem, positions, mix_norm, mem_norm, w_mem_kv, ffn_norm, w_gate, w_up, w_down, attn_w_in, attn_w_out, sgu_w_in, sgu_ln_g, sgu_ln_b, sgu_w_spatial, sgu_b_spatial, sgu_w_out, final_norm, loss_target, m_mix_norm, m_mem_norm, m_w_mem_kv, m_ffn_norm, m_w_gate, m_w_up, m_w_down, m_attn_w_in, m_attn_w_out, m_sgu_w_in, m_sgu_ln_g, m_sgu_ln_b, m_sgu_w_spatial, m_sgu_b_spatial, m_sgu_w_out, m_final_norm, v_mix_norm, v_mem_norm, v_w_mem_kv, v_ffn_norm, v_w_gate, v_w_up, v_w_down, v_attn_w_in, v_attn_w_out, v_sgu_w_in, v_sgu_ln_g, v_sgu_ln_b, v_sgu_w_spatial, v_sgu_b_spatial, v_sgu_w_out, v_final_norm):
    w = dict(mix_norm=mix_norm, mem_norm=mem_norm, w_mem_kv=w_mem_kv, ffn_norm=ffn_norm, w_gate=w_gate,
             w_up=w_up, w_down=w_down, attn_w_in=attn_w_in, attn_w_out=attn_w_out, sgu_w_in=sgu_w_in,
             sgu_ln_g=sgu_ln_g, sgu_ln_b=sgu_ln_b, sgu_w_spatial=sgu_w_spatial, sgu_b_spatial=sgu_b_spatial,
             sgu_w_out=sgu_w_out, final_norm=final_norm)
    mo = dict(mix_norm=m_mix_norm, mem_norm=m_mem_norm, w_mem_kv=m_w_mem_kv, ffn_norm=m_ffn_norm,
              w_gate=m_w_gate, w_up=m_w_up, w_down=m_w_down, attn_w_in=m_attn_w_in, attn_w_out=m_attn_w_out,
              sgu_w_in=m_sgu_w_in, sgu_ln_g=m_sgu_ln_g, sgu_ln_b=m_sgu_ln_b, sgu_w_spatial=m_sgu_w_spatial,
              sgu_b_spatial=m_sgu_b_spatial, sgu_w_out=m_sgu_w_out, final_norm=m_final_norm)
    vo = dict(mix_norm=v_mix_norm, mem_norm=v_mem_norm, w_mem_kv=v_w_mem_kv, ffn_norm=v_ffn_norm,
              w_gate=v_w_gate, w_up=v_w_up, w_down=v_w_down, attn_w_in=v_attn_w_in, attn_w_out=v_attn_w_out,
              sgu_w_in=v_sgu_w_in, sgu_ln_g=v_sgu_ln_g, sgu_ln_b=v_sgu_ln_b, sgu_w_spatial=v_sgu_w_spatial,
              sgu_b_spatial=v_sgu_b_spatial, sgu_w_out=v_sgu_w_out, final_norm=v_final_norm)
    me = 4 * lax.axis_index("x") + 2 * lax.axis_index("y") + lax.axis_index("c")
    d_model = x.shape[-1]

    for n in TRANSPOSED:
        w[n], mo[n], vo[n] = (jnp.swapaxes(t, 1, 2) for t in (w[n], mo[n], vo[n]))

    shards = {
        "w_mem_kv": _cast_bf16(w_mem_kv, "cast_w_mem_kv"),
        "attn_w_out": _cast_bf16(attn_w_out[0], "cast_attn_w_out"),
        "sgu_w_in": _cast_bf16(w["sgu_w_in"][0], "cast_sgu_w_in"),
        "sgu_w_out": _cast_bf16(sgu_w_out[0], "cast_sgu_w_out"),
    }
    for n in ("w_gate", "w_up", "w_down"):
        shards[n] = [_cast_bf16_layer(w[n], layer, "cast_%s_%d" % (n, layer)) for layer in range(w[n].shape[0])]
    ln_pack = jnp.concatenate([sgu_ln_g, sgu_ln_b], axis=0)
    w_attn_in, ln_all = _run_side(
        _GatherSide([_cast_bf16(attn_w_in[0], "cast_attn_w_in"), ln_pack]), "gather_attn_w_in")
    ln_full = ln_all.transpose(1, 0, 2).reshape(2, 1, -1)
    small = dict(mix_norm=mix_norm, mem_norm=mem_norm, ffn_norm=ffn_norm, final_norm=final_norm.reshape(1, -1),
                 sgu_w_spatial=sgu_w_spatial[0], sgu_b_spatial=sgu_b_spatial[0],
                 sgu_ln_g=ln_full[0], sgu_ln_b=ln_full[1])

    loss, grad_x, recvs, sgrads = _local_step(x[0], mem[0], positions[0], loss_target[0], w_attn_in, shards,
                                              small)
    loss = lax.psum(loss[0, 0], MESH_AXES)

    out_g, out_d, out_m, out_v = {}, {}, {}, {}
    for n in BIG:
        shard = w[n]
        w3 = shard.reshape(shard.shape[0], -1, shard.shape[-1])
        rs = [r.reshape(N_DEV, -1, shard.shape[-1]) for r in recvs[n]]
        res = _reduce_adam(rs, w3, mo[n].reshape(w3.shape), vo[n].reshape(w3.shape), "adam_" + n)
        res = [r.reshape(shard.shape) for r in res]
        if n in TRANSPOSED:
            res = [jnp.swapaxes(r, 1, 2) for r in res]
        out_g[n], out_d[n], out_m[n], out_v[n] = res

    small_names = SMALL_REPLICATED + SMALL_SHARDED
    packed, offs, sizes = _pack([sgrads[n] for n in small_names])
    all_packs = _run_side(_GatherSide([packed]), "gather_small_grads")[0]
    rep_shapes = [w[n].shape for n in SMALL_REPLICATED]
    w_pack, w_offs, w_sizes = _pack([w[n] for n in SMALL_REPLICATED])
    m_pack, _, _ = _pack([mo[n] for n in SMALL_REPLICATED])
    v_pack, _, _ = _pack([vo[n] for n in SMALL_REPLICATED])
    n_rep_rows = w_pack.shape[0]
    res = _reduce_adam([all_packs[:, :n_rep_rows]], w_pack[None], m_pack[None], v_pack[None], "adam_small")
    for dst, r in zip((out_g, out_d, out_m, out_v), res):
        for n, val in zip(SMALL_REPLICATED, _unpack(r[0], w_offs, w_sizes, rep_shapes)):
            dst[n] = val
    ln_rows0 = offs[len(SMALL_REPLICATED)] // PACK_LANES
    ln_rows = 2 * SGU_W // PACK_LANES
    ln_sum = _reduce_adam([all_packs[:, ln_rows0:ln_rows0 + ln_rows]], jnp.zeros((1, ln_rows, PACK_LANES), F32),
                          jnp.zeros((1, ln_rows, PACK_LANES), F32), jnp.zeros((1, ln_rows, PACK_LANES), F32),
                          "sum_ln_grads")[0]
    ln_grads = ln_sum.reshape(2, N_DEV, -1)
    ln_mine = lax.dynamic_index_in_dim(ln_grads, me, axis=1, keepdims=False)
    w_ln = jnp.concatenate([sgu_ln_g, sgu_ln_b], axis=0)[None]
    m_ln = jnp.concatenate([m_sgu_ln_g, m_sgu_ln_b], axis=0)[None]
    v_ln = jnp.concatenate([v_sgu_ln_g, v_sgu_ln_b], axis=0)[None]
    res = _reduce_adam([ln_mine[None]], w_ln, m_ln, v_ln, "adam_ln")
    for dst, r in zip((out_g, out_d, out_m, out_v), res):
        dst["sgu_ln_g"], dst["sgu_ln_b"] = r[0, 0:1], r[0, 1:2]

    return (loss, grad_x[None], *[out_g[n] for n in WEIGHT_ORDER], *[out_d[n] for n in WEIGHT_ORDER],
            *[out_m[n] for n in WEIGHT_ORDER], *[out_v[n] for n in WEIGHT_ORDER])
```

```python
import functools

import jax
import jax.numpy as jnp
from jax import lax
from jax.experimental import pallas as pl
from jax.experimental.pallas import tpu as pltpu

F32 = jnp.float32
BF16 = jnp.bfloat16

N_DEV = 8
HEAD = 128
HPG = 4
GROUP_W = HPG * HEAD
N_GROUPS = 3
DILATIONS = (1, 4, 16)
BLK = 128
SGU_GROUPS = 12
SGU_W = SGU_GROUPS * HEAD
ROT_HALF = 16
ROPE_THETA = 500000.0
NORM_EPS = 1e-6
LN_EPS = 1e-5
NEG_INF = -1e30
SCALE = HEAD ** -0.5

ADAM_LR = 0.001
ADAM_B1 = 0.9
ADAM_B2 = 0.999
ADAM_EPS = 1e-08
ADAM_WD = 0.01
ADAM_STEP = 10

VMEM_LIMIT_V7X = 56 * 1024 * 1024
MESH_AXES = ("x", "y", "c")

_DN = {
    "nn": (((1,), (0,)), ((), ())),
    "nt": (((1,), (1,)), ((), ())),
    "tn": (((0,), (0,)), ((), ())),
}


def _dot(a, b, kind="nn"):
    return lax.dot_general(a, b, _DN[kind], preferred_element_type=F32)


def _params():
    return pltpu.CompilerParams(vmem_limit_bytes=VMEM_LIMIT_V7X)


def _row_tile(rows, cap):
    if rows <= cap:
        return rows
    t = cap - cap % 16
    while t >= 16:
        if rows % t == 0:
            return t
        t -= 16
    return rows


def _gelu(x):
    c = 0.7978845608028654
    return 0.5 * x * (1.0 + jnp.tanh(c * (x + 0.044715 * x * x * x)))


def _gelu_and_grad(x):
    c = 0.7978845608028654
    x2 = x * x
    t = jnp.tanh(c * x * (1.0 + 0.044715 * x2))
    half = 0.5 * (1.0 + t)
    return x * half, half + 0.5 * x * (1.0 - t * t) * c * (1.0 + 3.0 * 0.044715 * x2)


def _sigmoid(x):
    return 1.0 / (1.0 + jnp.exp(-x))


def _matmul(name, grid, terms, n_acc, acc_shape, extras, outs, epilogue, side=None):
    nk = grid[-1]
    nt, ne, no = len(terms), len(extras), len(outs)
    kinds = [(t[4], t[5]) for t in terms]
    n_scratch_acc = 0 if nk == 1 else n_acc
    ns_in = len(side.ins) if side else 0
    ns_out = len(side.outs) if side else 0
    n_steps = 1
    for g in grid:
        n_steps *= g

    def body(*refs):
        pos = 0
        ab = refs[pos:pos + 2 * nt]
        pos += 2 * nt
        ex = refs[pos:pos + ne]
        pos += ne
        s_in = refs[pos:pos + ns_in]
        pos += ns_in
        out = refs[pos:pos + no]
        pos += no
        s_out = refs[pos:pos + ns_out]
        pos += ns_out
        accs = refs[pos:pos + n_scratch_acc]
        s_sems = refs[pos + n_scratch_acc:]
        if side:
            step = pl.program_id(0)
            for ax in range(1, len(grid)):
                step = step * grid[ax] + pl.program_id(ax)
            start, mid, finish = side.phases(s_in, s_out, *s_sems)
            pl.when(step == 0)(start)
        parts = [None] * n_acc
        for t, (kind, ai) in enumerate(kinds):
            a_ref, b_ref = ab[2 * t], ab[2 * t + 1]
            if len(b_ref.shape) == 2:
                pairs = [(a_ref[...], b_ref[...])]
            elif len(a_ref.shape) == 3:
                pairs = [(a_ref[q], b_ref[q]) for q in range(b_ref.shape[0])]
            else:
                bw = b_ref.shape[2]
                pairs = [(a_ref[:, q * bw:(q + 1) * bw], b_ref[q]) for q in range(b_ref.shape[0])]
            for a, b in pairs:
                p = _dot(a.astype(BF16), b.astype(BF16), kind)
                parts[ai] = p if parts[ai] is None else parts[ai] + p
        if nk == 1:
            epilogue(parts, ex, out)
        else:
            k = pl.program_id(len(grid) - 1)

            @pl.when(k == 0)
            def _():
                for ai in range(n_acc):
                    accs[ai][...] = parts[ai]

            @pl.when(k > 0)
            def _():
                for ai in range(n_acc):
                    accs[ai][...] += parts[ai]

            @pl.when(k == nk - 1)
            def _():
                epilogue([a[...] for a in accs], ex, out)

        if side:
            pl.when(step == (3 * n_steps) // 4)(mid)
            pl.when(step == n_steps - 1)(finish)

    hbm = pl.BlockSpec(memory_space=pltpu.HBM)
    in_specs, args = [], []
    for (a, a_spec, b, b_spec, _, _) in terms:
        in_specs += [a_spec, b_spec]
        args += [a, b]
    for (e, e_spec) in extras:
        in_specs.append(e_spec)
        args.append(e)
    scratch = [pltpu.VMEM(acc_shape, F32) for _ in range(n_scratch_acc)]
    out_specs = [o[1] for o in outs]
    out_shape = [o[0] for o in outs]
    if side:
        in_specs += [hbm] * ns_in
        args += list(side.ins)
        out_specs += [hbm] * ns_out
        out_shape += list(side.outs)
        scratch += side.scratch()
    res = pl.pallas_call(
        body, grid=grid, in_specs=in_specs, out_specs=out_specs, out_shape=out_shape,
        scratch_shapes=scratch, compiler_params=_params(), name=name,
    )(*args)
    return res if side is None else (res[:no], res[no:])


def _one(res, side):
    return res[0] if side is None else (res[0][0], res[1])


def _store_epilogue(parts, ex, out):
    out[0][...] = parts[0].astype(out[0].dtype)


def _residual_epilogue(parts, ex, out):
    out[0][...] = (parts[0] + ex[0][...]).astype(out[0].dtype)


def _rmsnorm_fwd(x, g, name):
    rows, d = x.shape
    tm = _row_tile(rows, 512)

    def body(x_ref, g_ref, o_ref):
        xf = x_ref[...]
        r = lax.rsqrt(jnp.mean(xf * xf, axis=-1, keepdims=True) + NORM_EPS)
        o_ref[...] = (xf * r * g_ref[...]).astype(o_ref.dtype)

    return pl.pallas_call(
        body, grid=(rows // tm,),
        in_specs=[pl.BlockSpec((tm, d), lambda i: (i, 0)), pl.BlockSpec((1, d), lambda i: (0, 0))],
        out_specs=pl.BlockSpec((tm, d), lambda i: (i, 0)),
        out_shape=jax.ShapeDtypeStruct((rows, d), BF16),
        compiler_params=_params(), name=name,
    )(x, g)


def _rmsnorm_bwd(x, g, dh, dres, name):
    rows, d = x.shape
    tm = _row_tile(rows, 256)
    has_res = dres is not None

    def body(*refs):
        if has_res:
            x_ref, g_ref, dh_ref, dres_ref, dx_ref, dxb_ref, dg_ref = refs
        else:
            x_ref, g_ref, dh_ref, dx_ref, dxb_ref, dg_ref = refs
        i = pl.program_id(0)
        xf = x_ref[...]
        r = lax.rsqrt(jnp.mean(xf * xf, axis=-1, keepdims=True) + NORM_EPS)
        xhat = xf * r
        dy = dh_ref[...]
        gdy = dy * g_ref[...]
        c = jnp.mean(gdy * xhat, axis=-1, keepdims=True)
        dx = r * (gdy - xhat * c)
        if has_res:
            dx = dx + dres_ref[...]
        dx_ref[...] = dx
        dxb_ref[...] = dx.astype(BF16)

        @pl.when(i == 0)
        def _():
            dg_ref[...] = jnp.zeros_like(dg_ref)

        dg_ref[...] += jnp.sum(dy * xhat, axis=0, keepdims=True)

    row_spec = pl.BlockSpec((tm, d), lambda i: (i, 0))
    vec_spec = pl.BlockSpec((1, d), lambda i: (0, 0))
    in_specs = [row_spec, vec_spec, row_spec] + ([row_spec] if has_res else [])
    args = [x, g, dh] + ([dres] if has_res else [])
    return pl.pallas_call(
        body, grid=(rows // tm,), in_specs=in_specs,
        out_specs=[row_spec, row_spec, vec_spec],
        out_shape=[jax.ShapeDtypeStruct((rows, d), F32), jax.ShapeDtypeStruct((rows, d), BF16),
                   jax.ShapeDtypeStruct((1, d), F32)],
        compiler_params=_params(), name=name,
    )(*args)


def _loss_head(x, g, target):
    rows, d = x.shape
    tm = _row_tile(rows, 256)

    def body(x_ref, g_ref, t_ref, loss_ref, dx_ref, dxb_ref, dg_ref):
        i = pl.program_id(0)
        xf = x_ref[...]
        gv = g_ref[...]
        r = lax.rsqrt(jnp.mean(xf * xf, axis=-1, keepdims=True) + NORM_EPS)
        xhat = xf * r
        err = xhat * gv - t_ref[...]
        row_loss = jnp.mean(err * err, axis=-1, keepdims=True)
        dy = err * (1.0 / d)
        gdy = dy * gv
        c = jnp.mean(gdy * xhat, axis=-1, keepdims=True)
        dx = r * (gdy - xhat * c)
        dx_ref[...] = dx
        dxb_ref[...] = dx.astype(BF16)

        @pl.when(i == 0)
        def _():
            dg_ref[...] = jnp.zeros_like(dg_ref)
            loss_ref[...] = jnp.zeros_like(loss_ref)

        dg_ref[...] += jnp.sum(dy * xhat, axis=0, keepdims=True)
        loss_ref[...] += 0.5 * jnp.sum(row_loss, axis=0, keepdims=True)

    row_spec = pl.BlockSpec((tm, d), lambda i: (i, 0))
    vec_spec = pl.BlockSpec((1, d), lambda i: (0, 0))
    return pl.pallas_call(
        body, grid=(rows // tm,), in_specs=[row_spec, vec_spec, row_spec],
        out_specs=[pl.BlockSpec((1, 1), lambda i: (0, 0)), row_spec, row_spec, vec_spec],
        out_shape=[jax.ShapeDtypeStruct((1, 1), F32), jax.ShapeDtypeStruct((rows, d), F32),
                   jax.ShapeDtypeStruct((rows, d), BF16), jax.ShapeDtypeStruct((1, d), F32)],
        compiler_params=_params(), name="loss_head",
    )(x, g, target)


def _rotary_tables(positions):
    inv_freq = ROPE_THETA ** (-jnp.arange(ROT_HALF, dtype=F32) / ROT_HALF)
    ang = positions.astype(F32)[:, None] * inv_freq
    cos, sin = jnp.cos(ang), jnp.sin(ang)
    s = positions.shape[0]
    z = jnp.zeros((s, HEAD - 2 * ROT_HALF), F32)
    z16 = jnp.zeros((s, ROT_HALF), F32)
    c = jnp.concatenate([cos, cos, jnp.ones_like(z)], axis=1)
    s1 = jnp.concatenate([z16, sin, z], axis=1)
    s2 = jnp.concatenate([-sin, z16, z], axis=1)
    return c, s1, s2


ATTN_PROJ_HEADS_PER_STEP = 10


def _attn_in_proj(h, w_t, tabs, side=None):
    s, d = h.shape
    n = w_t.shape[0]
    tn = ATTN_PROJ_HEADS_PER_STEP * HEAD
    n_rot_heads = 2 * N_GROUPS * HPG
    n_rot_steps = -(-n_rot_heads // ATTN_PROJ_HEADS_PER_STEP)
    tm = _row_tile(s, 1024)

    def epilogue(parts, ex, out):
        j = pl.program_id(0)
        acc = parts[0]

        @pl.when(j < n_rot_steps)
        def _():
            c, s1, s2 = ex[0][...], ex[1][...], ex[2][...]
            for t in range(ATTN_PROJ_HEADS_PER_STEP):
                seg = acc[:, t * HEAD:(t + 1) * HEAD]
                rot = seg * c + pltpu.roll(seg, ROT_HALF, 1) * s1 + pltpu.roll(seg, HEAD - ROT_HALF, 1) * s2
                is_rot = (j * ATTN_PROJ_HEADS_PER_STEP + t) < n_rot_heads
                out[0][:, t * HEAD:(t + 1) * HEAD] = jnp.where(is_rot, rot, seg).astype(BF16)

        @pl.when(j >= n_rot_steps)
        def _():
            out[0][...] = acc.astype(BF16)

    tab_spec = pl.BlockSpec((tm, HEAD), lambda j, m, k: (m, 0))
    return _one(_matmul(
        "attn_in_proj", (n // tn, s // tm, 1),
        [(h, pl.BlockSpec((tm, d), lambda j, m, k: (m, 0)),
          w_t, pl.BlockSpec((tn, d), lambda j, m, k: (j, 0)), "nt", 0)],
        1, None, [(tabs[0], tab_spec), (tabs[1], tab_spec), (tabs[2], tab_spec)],
        [(jax.ShapeDtypeStruct((s, n), BF16), pl.BlockSpec((tm, tn), lambda j, m, k: (m, j)))],
        epilogue, side), side)


ATT_TILE_BLOCKS = 4


def _att_blocks(seq_blocks):
    return min(ATT_TILE_BLOCKS, seq_blocks)


def _band_masks():
    qi = lax.broadcasted_iota(jnp.int32, (BLK, BLK), 0)
    ki = lax.broadcasted_iota(jnp.int32, (BLK, BLK), 1)
    return ki <= qi, ki >= qi


def _attn_fwd(q_arr, k_arr, v_arr, offs, seq_blocks, name):
    s = q_arr.shape[0]
    qo, ko, vo = offs
    nb = _att_blocks(seq_blocks)

    def body(q_ref, kc_ref, kp_ref, vc_ref, vp_ref, o_ref, lse_ref):
        n = pl.program_id(0)
        tile_starts_seq = (n * nb) % seq_blocks == 0
        mask_c, mask_p = _band_masks()
        pairs = [(b * BLK, h * HEAD) for b in range(nb) for h in range(HPG)]

        def keys_prev(ref_c, ref_p, r0, c0):
            return ref_p[:, c0:c0 + HEAD] if r0 == 0 else ref_c[r0 - BLK:r0, c0:c0 + HEAD]

        s_c, s_p = [], []
        for r0, c0 in pairs:
            q = q_ref[r0:r0 + BLK, c0:c0 + HEAD]
            s_c.append(jnp.where(mask_c, _dot(q, kc_ref[r0:r0 + BLK, c0:c0 + HEAD], "nt") * SCALE, NEG_INF))
            sp = jnp.where(mask_p, _dot(q, keys_prev(kc_ref, kp_ref, r0, c0), "nt") * SCALE, NEG_INF)
            s_p.append(jnp.where(tile_starts_seq, NEG_INF, sp) if r0 == 0 else sp)
        m = [jnp.maximum(jnp.max(a, axis=-1, keepdims=True), jnp.max(b, axis=-1, keepdims=True))
             for a, b in zip(s_c, s_p)]
        p_c = [jnp.exp(a - mm) for a, mm in zip(s_c, m)]
        p_p = [jnp.exp(a - mm) for a, mm in zip(s_p, m)]
        l = [jnp.sum(a, axis=-1, keepdims=True) + jnp.sum(b, axis=-1, keepdims=True) for a, b in zip(p_c, p_p)]
        inv = [1.0 / ll for ll in l]
        for i, (r0, c0) in enumerate(pairs):
            o = (_dot((p_c[i] * inv[i]).astype(BF16), vc_ref[r0:r0 + BLK, c0:c0 + HEAD])
                 + _dot((p_p[i] * inv[i]).astype(BF16), keys_prev(vc_ref, vp_ref, r0, c0)))
            o_ref[r0:r0 + BLK, c0:c0 + HEAD] = o.astype(BF16)
            lse_ref[r0:r0 + BLK, c0:c0 + HEAD] = jnp.broadcast_to(m[i] + jnp.log(l[i]), (BLK, HEAD))

    def cur(off):
        return pl.BlockSpec((nb * BLK, GROUP_W), lambda n: (n, off))

    def prev(off):
        return pl.BlockSpec((BLK, GROUP_W), lambda n: (jnp.maximum(n * nb - 1, 0), off))

    return pl.pallas_call(
        body, grid=(s // (nb * BLK),),
        in_specs=[cur(qo), cur(ko), prev(ko), cur(vo), prev(vo)],
        out_specs=[cur(0), cur(0)],
        out_shape=[jax.ShapeDtypeStruct((s, GROUP_W), BF16), jax.ShapeDtypeStruct((s, GROUP_W), F32)],
        compiler_params=_params(), name=name,
    )(q_arr, k_arr, k_arr, v_arr, v_arr)


def _attn_bwd_dq(q_arr, k_arr, v_arr, offs, do, lse, delta, seq_blocks, name):
    s = q_arr.shape[0]
    qo, ko, vo = offs
    nb = _att_blocks(seq_blocks)

    def body(q_ref, kc_ref, kp_ref, vc_ref, vp_ref, do_ref, lse_ref, dl_ref, dq_ref):
        n = pl.program_id(0)
        tile_starts_seq = (n * nb) % seq_blocks == 0
        mask_c, mask_p = _band_masks()
        pairs = [(b * BLK, h * HEAD) for b in range(nb) for h in range(HPG)]

        def prev_blk(ref_c, ref_p, r0, c0):
            return ref_p[:, c0:c0 + HEAD] if r0 == 0 else ref_c[r0 - BLK:r0, c0:c0 + HEAD]

        def own_blk(ref, r0, c0):
            return ref[r0:r0 + BLK, c0:c0 + HEAD]

        s_c, s_p, dp_c, dp_p = [], [], [], []
        for r0, c0 in pairs:
            q, dob = own_blk(q_ref, r0, c0), own_blk(do_ref, r0, c0)
            s_c.append(jnp.where(mask_c, _dot(q, own_blk(kc_ref, r0, c0), "nt") * SCALE, NEG_INF))
            sp = jnp.where(mask_p, _dot(q, prev_blk(kc_ref, kp_ref, r0, c0), "nt") * SCALE, NEG_INF)
            s_p.append(jnp.where(tile_starts_seq, NEG_INF, sp) if r0 == 0 else sp)
            dp_c.append(_dot(dob, own_blk(vc_ref, r0, c0), "nt"))
            dp_p.append(_dot(dob, prev_blk(vc_ref, vp_ref, r0, c0), "nt"))
        ds_c, ds_p = [], []
        for i, (r0, c0) in enumerate(pairs):
            lse_b, dl_b = own_blk(lse_ref, r0, c0), own_blk(dl_ref, r0, c0)
            ds_c.append((jnp.exp(s_c[i] - lse_b) * (dp_c[i] - dl_b) * SCALE).astype(BF16))
            ds_p.append((jnp.exp(s_p[i] - lse_b) * (dp_p[i] - dl_b) * SCALE).astype(BF16))
        for i, (r0, c0) in enumerate(pairs):
            dq = _dot(ds_c[i], own_blk(kc_ref, r0, c0)) + _dot(ds_p[i], prev_blk(kc_ref, kp_ref, r0, c0))
            dq_ref[r0:r0 + BLK, c0:c0 + HEAD] = dq.astype(BF16)

    def cur(off):
        return pl.BlockSpec((nb * BLK, GROUP_W), lambda n: (n, off))

    def prev(off):
        return pl.BlockSpec((BLK, GROUP_W), lambda n: (jnp.maximum(n * nb - 1, 0), off))

    return pl.pallas_call(
        body, grid=(s // (nb * BLK),),
        in_specs=[cur(qo), cur(ko), prev(ko), cur(vo), prev(vo), cur(0), cur(0), cur(0)],
        out_specs=cur(0),
        out_shape=jax.ShapeDtypeStruct((s, GROUP_W), BF16),
        compiler_params=_params(), name=name,
    )(q_arr, k_arr, k_arr, v_arr, v_arr, do, lse, delta)


def _attn_bwd_dkv(q_arr, k_arr, v_arr, offs, do, lse, delta, seq_blocks, name):
    s = q_arr.shape[0]
    qo, ko, vo = offs
    nb = _att_blocks(seq_blocks)
    n_blocks = s // BLK

    def body(k_ref, v_ref, qc_ref, qn_ref, doc_ref, don_ref, lsec_ref, lsen_ref, dlc_ref, dln_ref,
             dk_ref, dv_ref):
        n = pl.program_id(0)
        next_in_seq = ((n + 1) * nb) % seq_blocks != 0
        mask_c, mask_p = _band_masks()
        pairs = [(b * BLK, h * HEAD) for b in range(nb) for h in range(HPG)]
        last = (nb - 1) * BLK

        def own_blk(ref, r0, c0):
            return ref[r0:r0 + BLK, c0:c0 + HEAD]

        def next_blk(ref_c, ref_n, r0, c0):
            return ref_n[:, c0:c0 + HEAD] if r0 == last else ref_c[r0 + BLK:r0 + 2 * BLK, c0:c0 + HEAD]

        s_c, s_p, dp_c, dp_p = [], [], [], []
        for r0, c0 in pairs:
            k, v = own_blk(k_ref, r0, c0), own_blk(v_ref, r0, c0)
            s_c.append(jnp.where(mask_c, _dot(own_blk(qc_ref, r0, c0), k, "nt") * SCALE, NEG_INF))
            sp = jnp.where(mask_p, _dot(next_blk(qc_ref, qn_ref, r0, c0), k, "nt") * SCALE, NEG_INF)
            s_p.append(jnp.where(next_in_seq, sp, NEG_INF) if r0 == last else sp)
            dp_c.append(_dot(own_blk(doc_ref, r0, c0), v, "nt"))
            dp_p.append(_dot(next_blk(doc_ref, don_ref, r0, c0), v, "nt"))
        p_c, p_p, ds_c, ds_p = [], [], [], []
        for i, (r0, c0) in enumerate(pairs):
            pc = jnp.exp(s_c[i] - own_blk(lsec_ref, r0, c0))
            pp = jnp.exp(s_p[i] - next_blk(lsec_ref, lsen_ref, r0, c0))
            ds_c.append((pc * (dp_c[i] - own_blk(dlc_ref, r0, c0)) * SCALE).astype(BF16))
            ds_p.append((pp * (dp_p[i] - next_blk(dlc_ref, dln_ref, r0, c0)) * SCALE).astype(BF16))
            p_c.append(pc.astype(BF16))
            p_p.append(pp.astype(BF16))
        for i, (r0, c0) in enumerate(pairs):
            dv = (_dot(p_c[i], own_blk(doc_ref, r0, c0), "tn")
                  + _dot(p_p[i], next_blk(doc_ref, don_ref, r0, c0), "tn"))
            dk = (_dot(ds_c[i], own_blk(qc_ref, r0, c0), "tn")
                  + _dot(ds_p[i], next_blk(qc_ref, qn_ref, r0, c0), "tn"))
            dk_ref[r0:r0 + BLK, c0:c0 + HEAD] = dk.astype(BF16)
            dv_ref[r0:r0 + BLK, c0:c0 + HEAD] = dv.astype(BF16)

    def cur(off):
        return pl.BlockSpec((nb * BLK, GROUP_W), lambda n: (n, off))

    def nxt(off):
        return pl.BlockSpec((BLK, GROUP_W), lambda n: (jnp.minimum((n + 1) * nb, n_blocks - 1), off))

    return pl.pallas_call(
        body, grid=(s // (nb * BLK),),
        in_specs=[cur(ko), cur(vo), cur(qo), nxt(qo), cur(0), nxt(0), cur(0), nxt(0), cur(0), nxt(0)],
        out_specs=[cur(0), cur(0)],
        out_shape=[jax.ShapeDtypeStruct((s, GROUP_W), BF16), jax.ShapeDtypeStruct((s, GROUP_W), BF16)],
        compiler_params=_params(), name=name,
    )(k_arr, v_arr, q_arr, q_arr, do, do, lse, lse, delta, delta)


def _merge_weights(lse_refs, c0):
    ls = [r[:, c0:c0 + HEAD] for r in lse_refs]
    m = jnp.maximum(jnp.maximum(ls[0], ls[1]), ls[2])
    es = [jnp.exp(l - m) for l in ls]
    inv = 1.0 / (es[0] + es[1] + es[2])
    return [e * inv for e in es]


def _merge_fwd(os_, lses):
    s = os_[0].shape[0]
    tm = _row_tile(s, 512)

    def body(o0, o1, o2, l0, l1, l2, out_ref):
        for h in range(HPG):
            c0 = h * HEAD
            w = _merge_weights((l0, l1, l2), c0)
            acc = None
            for wg, o in zip(w, (o0, o1, o2)):
                t = wg * o[:, c0:c0 + HEAD].astype(F32)
                acc = t if acc is None else acc + t
            out_ref[:, c0:c0 + HEAD] = acc.astype(BF16)

    spec = pl.BlockSpec((tm, GROUP_W), lambda i: (i, 0))
    return pl.pallas_call(
        body, grid=(s // tm,), in_specs=[spec] * 6, out_specs=spec,
        out_shape=jax.ShapeDtypeStruct((s, GROUP_W), BF16),
        compiler_params=_params(), name="merge_fwd",
    )(*os_, *lses)


def _merge_bwd(dcat, os_, lses):
    s = os_[0].shape[0]
    tm = _row_tile(s, 512)

    def body(d_ref, o0, o1, o2, l0, l1, l2, do0, do1, do2, dl0, dl1, dl2):
        for h in range(HPG):
            c0 = h * HEAD
            w = _merge_weights((l0, l1, l2), c0)
            dm = d_ref[:, c0:c0 + HEAD].astype(F32)
            merged = None
            for wg, o in zip(w, (o0, o1, o2)):
                t = wg * o[:, c0:c0 + HEAD].astype(F32)
                merged = t if merged is None else merged + t
            abar = jnp.sum(dm * merged, axis=-1, keepdims=True)
            for wg, do_ref, dl_ref in zip(w, (do0, do1, do2), (dl0, dl1, dl2)):
                do_ref[:, c0:c0 + HEAD] = (wg * dm).astype(BF16)
                dl_ref[:, c0:c0 + HEAD] = wg * abar

    spec = pl.BlockSpec((tm, GROUP_W), lambda i: (i, 0))
    return pl.pallas_call(
        body, grid=(s // tm,), in_specs=[spec] * 7, out_specs=[spec] * 6,
        out_shape=[jax.ShapeDtypeStruct((s, GROUP_W), BF16)] * 3 + [jax.ShapeDtypeStruct((s, GROUP_W), F32)] * 3,
        compiler_params=_params(), name="merge_bwd",
    )(dcat, *os_, *lses)


def _assemble_dproj(dqkv, dqm, tabs):
    s = dqm.shape[0]
    tm = _row_tile(s, 256)
    width = 3 * N_GROUPS * GROUP_W + GROUP_W

    def body(d0, d1, d2, dm_ref, c_ref, s1_ref, s2_ref, out_ref):
        c, s1, s2 = c_ref[...], s1_ref[...], s2_ref[...]
        for g, d_ref in enumerate((d0, d1, d2)):
            for part in range(3):
                for h in range(HPG):
                    src = part * GROUP_W + h * HEAD
                    dst = part * N_GROUPS * GROUP_W + g * GROUP_W + h * HEAD
                    seg = d_ref[:, src:src + HEAD]
                    if part < 2:
                        t = seg.astype(F32)
                        t = t * c - pltpu.roll(t, HEAD - ROT_HALF, 1) * s2 - pltpu.roll(t, ROT_HALF, 1) * s1
                        seg = t.astype(BF16)
                    out_ref[:, dst:dst + HEAD] = seg
        out_ref[:, 3 * N_GROUPS * GROUP_W:] = dm_ref[...]

    g_spec = pl.BlockSpec((tm, 3 * GROUP_W), lambda i: (i, 0))
    m_spec = pl.BlockSpec((tm, GROUP_W), lambda i: (i, 0))
    t_spec = pl.BlockSpec((tm, HEAD), lambda i: (i, 0))
    return pl.pallas_call(
        body, grid=(s // tm,), in_specs=[g_spec] * 3 + [m_spec] + [t_spec] * 3,
        out_specs=pl.BlockSpec((tm, width), lambda i: (i, 0)),
        out_shape=jax.ShapeDtypeStruct((s, width), BF16),
        compiler_params=_params(), name="assemble_dproj",
    )(*dqkv, dqm, *tabs)


def _mem_softmax(q, k):
    s = _dot(q, k, "nt") * SCALE
    m = jnp.max(s, axis=-1, keepdims=True)
    p = jnp.exp(s - m)
    return p * (1.0 / jnp.sum(p, axis=-1, keepdims=True))


def _memattn_fwd(q_arr, q_off, kv, name):
    s = q_arr.shape[0]
    mlen = kv.shape[0]
    tq = _row_tile(s, 512)

    def body(q_ref, kv_ref, o_ref):
        for h in range(HPG):
            c0 = h * HEAD
            p = _mem_softmax(q_ref[:, c0:c0 + HEAD], kv_ref[:, c0:c0 + HEAD])
            o = _dot(p.astype(BF16), kv_ref[:, GROUP_W + c0:GROUP_W + c0 + HEAD])
            o_ref[:, c0:c0 + HEAD] = o.astype(BF16)

    return pl.pallas_call(
        body, grid=(s // tq,),
        in_specs=[pl.BlockSpec((tq, GROUP_W), lambda i: (i, q_off)),
                  pl.BlockSpec((mlen, 2 * GROUP_W), lambda i: (0, 0))],
        out_specs=pl.BlockSpec((tq, GROUP_W), lambda i: (i, 0)),
        out_shape=jax.ShapeDtypeStruct((s, GROUP_W), BF16),
        compiler_params=_params(), name=name,
    )(q_arr, kv)


def _memattn_bwd(q_arr, q_off, kv, dcat, d_off, name):
    s = q_arr.shape[0]
    mlen = kv.shape[0]
    tq = _row_tile(s, 512)

    def body(q_ref, kv_ref, d_ref, dq_ref, dkv_ref):
        i = pl.program_id(0)

        @pl.when(i == 0)
        def _():
            dkv_ref[...] = jnp.zeros_like(dkv_ref)

        for h in range(HPG):
            c0 = h * HEAD
            q = q_ref[:, c0:c0 + HEAD]
            k = kv_ref[:, c0:c0 + HEAD]
            v = kv_ref[:, GROUP_W + c0:GROUP_W + c0 + HEAD]
            do = d_ref[:, c0:c0 + HEAD]
            p = _mem_softmax(q, k)
            dp = _dot(do, v, "nt")
            ds = p * (dp - jnp.sum(p * dp, axis=-1, keepdims=True)) * SCALE
            dsb = ds.astype(BF16)
            dq_ref[:, c0:c0 + HEAD] = _dot(dsb, k).astype(BF16)
            dkv_ref[:, c0:c0 + HEAD] += _dot(dsb, q, "tn")
            dkv_ref[:, GROUP_W + c0:GROUP_W + c0 + HEAD] += _dot(p.astype(BF16), do, "tn")

    return pl.pallas_call(
        body, grid=(s // tq,),
        in_specs=[pl.BlockSpec((tq, GROUP_W), lambda i: (i, q_off)),
                  pl.BlockSpec((mlen, 2 * GROUP_W), lambda i: (0, 0)),
                  pl.BlockSpec((tq, GROUP_W), lambda i: (i, d_off))],
        out_specs=[pl.BlockSpec((tq, GROUP_W), lambda i: (i, 0)),
                   pl.BlockSpec((mlen, 2 * GROUP_W), lambda i: (0, 0))],
        out_shape=[jax.ShapeDtypeStruct((s, GROUP_W), BF16), jax.ShapeDtypeStruct((mlen, 2 * GROUP_W), F32)],
        compiler_params=_params(), name=name,
    )(q_arr, kv, dcat)


SGU_TILE = 256


def _sgu_norm(vg):
    mu = jnp.mean(vg, axis=-1, keepdims=True)
    xc = vg - mu
    var = jnp.mean(xc * xc, axis=-1, keepdims=True)
    rstd = lax.rsqrt(var + LN_EPS)
    return xc * rstd, rstd


def _tril_mask():
    r = lax.broadcasted_iota(jnp.int32, (BLK, BLK), 0)
    c = lax.broadcasted_iota(jnp.int32, (BLK, BLK), 1)
    return r >= c


def _sgu_fwd(proj, ln_g, ln_b, w_s, b_st):
    s = proj.shape[0]
    ts = _row_tile(s, SGU_TILE)

    def body(u_ref, v_ref, g_ref, b_ref, ws_ref, bst_ref, o_ref):
        ug = _gelu(u_ref[...].astype(F32))
        xhat, _ = _sgu_norm(_gelu(v_ref[...].astype(F32)))
        vn = (xhat * g_ref[...] + b_ref[...]).astype(BF16)
        tri = _tril_mask()
        for g in range(SGU_GROUPS):
            c0 = g * HEAD
            w = jnp.where(tri, ws_ref[g], 0.0).astype(BF16)
            bias = bst_ref[:, g:g + 1]
            for ch in range(ts // BLK):
                r0 = ch * BLK
                mixed = _dot(w, vn[r0:r0 + BLK, c0:c0 + HEAD]) + bias
                o_ref[r0:r0 + BLK, c0:c0 + HEAD] = (ug[r0:r0 + BLK, c0:c0 + HEAD] * mixed).astype(BF16)

    vec = pl.BlockSpec((1, SGU_W), lambda i: (0, 0))
    return pl.pallas_call(
        body, grid=(s // ts,),
        in_specs=[pl.BlockSpec((ts, SGU_W), lambda i: (i, 0)), pl.BlockSpec((ts, SGU_W), lambda i: (i, 1)),
                  vec, vec, pl.BlockSpec((SGU_GROUPS, BLK, BLK), lambda i: (0, 0, 0)),
                  pl.BlockSpec((BLK, SGU_GROUPS), lambda i: (0, 0))],
        out_specs=pl.BlockSpec((ts, SGU_W), lambda i: (i, 0)),
        out_shape=jax.ShapeDtypeStruct((s, SGU_W), BF16),
        compiler_params=_params(), name="sgu_fwd",
    )(proj, proj, ln_g, ln_b, w_s, b_st)


def _sgu_bwd(proj, dcat, ln_g, ln_b, w_s, b_st):
    s = proj.shape[0]
    ts = _row_tile(s, SGU_TILE)

    def body(u_ref, v_ref, d_ref, g_ref, b_ref, ws_ref, bst_ref,
             duv_ref, dws_ref, dbs_ref, dg_ref, db_ref, dvn_ref):
        i = pl.program_id(0)

        @pl.when(i == 0)
        def _():
            dws_ref[...] = jnp.zeros_like(dws_ref)
            dbs_ref[...] = jnp.zeros_like(dbs_ref)
            dg_ref[...] = jnp.zeros_like(dg_ref)
            db_ref[...] = jnp.zeros_like(db_ref)

        u = u_ref[...].astype(F32)
        v = v_ref[...].astype(F32)
        ug, dug = _gelu_and_grad(u)
        vg, dvg_dv = _gelu_and_grad(v)
        xhat, rstd = _sgu_norm(vg)
        lng = g_ref[...]
        vn = (xhat * lng + b_ref[...]).astype(BF16)
        dout = d_ref[...].astype(F32)
        tri = _tril_mask()
        lane = lax.broadcasted_iota(jnp.int32, (BLK, BLK), 1)
        dbs = jnp.zeros((BLK, BLK), F32)
        for g in range(SGU_GROUPS):
            c0 = g * HEAD
            w = jnp.where(tri, ws_ref[g], 0.0).astype(BF16)
            bias = bst_ref[:, g:g + 1]
            dws = jnp.zeros((BLK, BLK), F32)
            for ch in range(ts // BLK):
                r0 = ch * BLK
                vn_gc = vn[r0:r0 + BLK, c0:c0 + HEAD]
                mixed = _dot(w, vn_gc) + bias
                do_gc = dout[r0:r0 + BLK, c0:c0 + HEAD]
                dmixed = do_gc * ug[r0:r0 + BLK, c0:c0 + HEAD]
                du = do_gc * mixed * dug[r0:r0 + BLK, c0:c0 + HEAD]
                duv_ref[r0:r0 + BLK, c0:c0 + HEAD] = du.astype(BF16)
                dmb = dmixed.astype(BF16)
                dws = dws + _dot(dmb, vn_gc, "nt")
                dbs = dbs + jnp.where(lane == g, jnp.sum(dmixed, axis=-1, keepdims=True), 0.0)
                dvn_ref[r0:r0 + BLK, c0:c0 + HEAD] = _dot(w, dmb, "tn")
            dws_ref[g] += jnp.where(tri, dws, 0.0)
        dbs_ref[...] += dbs
        dvn = dvn_ref[...]
        gd = dvn * lng
        c1 = jnp.mean(gd, axis=-1, keepdims=True)
        c2 = jnp.mean(gd * xhat, axis=-1, keepdims=True)
        dvg = rstd * (gd - c1 - xhat * c2)
        duv_ref[:, SGU_W:] = (dvg * dvg_dv).astype(BF16)
        dg_ref[...] += jnp.sum(dvn * xhat, axis=0, keepdims=True)
        db_ref[...] += jnp.sum(dvn, axis=0, keepdims=True)

    vec = pl.BlockSpec((1, SGU_W), lambda i: (0, 0))
    ws_spec = pl.BlockSpec((SGU_GROUPS, BLK, BLK), lambda i: (0, 0, 0))
    return pl.pallas_call(
        body, grid=(s // ts,),
        in_specs=[pl.BlockSpec((ts, SGU_W), lambda i: (i, 0)), pl.BlockSpec((ts, SGU_W), lambda i: (i, 1)),
                  pl.BlockSpec((ts, SGU_W), lambda i: (i, 0)),
                  vec, vec, ws_spec, pl.BlockSpec((BLK, SGU_GROUPS), lambda i: (0, 0))],
        out_specs=[pl.BlockSpec((ts, 2 * SGU_W), lambda i: (i, 0)), ws_spec,
                   pl.BlockSpec((BLK, BLK), lambda i: (0, 0)), vec, vec],
        out_shape=[jax.ShapeDtypeStruct((s, 2 * SGU_W), BF16),
                   jax.ShapeDtypeStruct((SGU_GROUPS, BLK, BLK), F32),
                   jax.ShapeDtypeStruct((BLK, BLK), F32),
                   jax.ShapeDtypeStruct((1, SGU_W), F32), jax.ShapeDtypeStruct((1, SGU_W), F32)],
        scratch_shapes=[pltpu.VMEM((ts, SGU_W), F32)],
        compiler_params=_params(), name="sgu_bwd",
    )(proj, proj, dcat, ln_g, ln_b, w_s, b_st)


def _swiglu_fwd(h, wg_t, wu_t, name, side=None):
    s, d = h.shape
    f = wg_t.shape[0]
    tm, tn = _row_tile(s, 1024), _row_tile(f, 512)

    def epilogue(parts, ex, out):
        g, u = parts
        sg = _sigmoid(g)
        silu = g * sg
        out[0][...] = silu.astype(BF16)
        out[1][...] = (u * (sg + silu * (1.0 - sg))).astype(BF16)
        out[2][...] = (silu * u).astype(BF16)

    a_spec = pl.BlockSpec((tm, d), lambda n, m, k: (m, 0))
    w_spec = pl.BlockSpec((tn, d), lambda n, m, k: (n, 0))
    o_spec = pl.BlockSpec((tm, tn), lambda n, m, k: (m, n))
    o_shape = jax.ShapeDtypeStruct((s, f), BF16)
    return _matmul(name, (f // tn, s // tm, 1),
                   [(h, a_spec, wg_t, w_spec, "nt", 0), (h, a_spec, wu_t, w_spec, "nt", 1)],
                   2, None, [], [(o_shape, o_spec)] * 3, epilogue, side)


def _swiglu_down(hid, wd, res, name, side=None):
    s, f = hid.shape
    d = wd.shape[1]
    tm, tn = _row_tile(s, 1024), _row_tile(d, 512)
    o_spec = pl.BlockSpec((tm, tn), lambda n, m, k: (m, n))
    return _one(_matmul(
        name, (d // tn, s // tm, 1),
        [(hid, pl.BlockSpec((tm, f), lambda n, m, k: (m, 0)),
          wd, pl.BlockSpec((f, tn), lambda n, m, k: (0, n)), "nn", 0)],
        1, None, [(res, o_spec)], [(jax.ShapeDtypeStruct((s, d), F32), o_spec)],
        _residual_epilogue, side), side)


def _swiglu_bwd_hidden(dxb, wd, silu, up_dsilu, name):
    s, f = silu.shape
    d = dxb.shape[1]
    tm, tn = _row_tile(s, 1024), _row_tile(f, 512)

    def epilogue(parts, ex, out):
        dh = parts[0]
        out[0][...] = (dh * ex[1][...].astype(F32)).astype(BF16)
        out[1][...] = (dh * ex[0][...].astype(F32)).astype(BF16)

    blk = pl.BlockSpec((tm, tn), lambda m, n, k: (m, n))
    o_shape = jax.ShapeDtypeStruct((s, f), BF16)
    return _matmul(
        name, (s // tm, f // tn, 1),
        [(dxb, pl.BlockSpec((tm, d), lambda m, n, k: (m, 0)),
          wd, pl.BlockSpec((tn, d), lambda m, n, k: (n, 0)), "nt", 0)],
        1, None, [(silu, blk), (up_dsilu, blk)], [(o_shape, blk)] * 2, epilogue)


def _swiglu_bwd_input(dgate, dup, wg_t, wu_t, name, side=None):
    s, f = dgate.shape
    d = wg_t.shape[1]
    tm, tn, tk = _row_tile(s, 1024), _row_tile(d, 512), f // 2
    a_spec = pl.BlockSpec((tm, tk), lambda n, m, k: (m, k))
    w_spec = pl.BlockSpec((tk, tn), lambda n, m, k: (k, n))
    return _one(_matmul(
        name, (d // tn, s // tm, f // tk),
        [(dgate, a_spec, wg_t, w_spec, "nn", 0), (dup, a_spec, wu_t, w_spec, "nn", 0)],
        1, (tm, tn), [],
        [(jax.ShapeDtypeStruct((s, d), F32), pl.BlockSpec((tm, tn), lambda n, m, k: (m, n)))],
        _store_epilogue, side), side)


def _mm_tn_full(a, b, name, side=None):
    s, m = a.shape
    n = b.shape[1]
    tm, tn = _row_tile(m, 512), _row_tile(n, 512)
    return _one(_matmul(
        name, (m // tm, n // tn, 1),
        [(a, pl.BlockSpec((s, tm), lambda i, j, k: (0, i)),
          b, pl.BlockSpec((s, tn), lambda i, j, k: (0, j)), "tn", 0)],
        1, None, [],
        [(jax.ShapeDtypeStruct((m, n), BF16), pl.BlockSpec((tm, tn), lambda i, j, k: (i, j)))],
        _store_epilogue, side), side)


def _mm_nn(a, b, name, tn, out_dtype=BF16, res=None, side=None):
    m, k = a.shape
    n = b.shape[1]
    tm = _row_tile(m, 1024)
    extras = [] if res is None else [(res, pl.BlockSpec((tm, tn), lambda j, i, kk: (i, j)))]
    return _one(_matmul(
        name, (n // tn, m // tm, 1),
        [(a, pl.BlockSpec((tm, k), lambda j, i, kk: (i, 0)),
          b, pl.BlockSpec((k, tn), lambda j, i, kk: (0, j)), "nn", 0)],
        1, None, extras,
        [(jax.ShapeDtypeStruct((m, n), out_dtype), pl.BlockSpec((tm, tn), lambda j, i, kk: (i, j)))],
        _store_epilogue if res is None else _residual_epilogue, side), side)


def _mm_nn_colblocked(a, b_blk, name, res, side=None):
    m, k = a.shape
    nb, _, bw = b_blk.shape
    tm = _row_tile(m, 2048)
    o_spec = pl.BlockSpec((tm, bw), lambda j, i, kk: (i, j))
    return _one(_matmul(
        name, (nb, m // tm, 1),
        [(a, pl.BlockSpec((tm, k), lambda j, i, kk: (i, 0)),
          b_blk, pl.BlockSpec((None, k, bw), lambda j, i, kk: (j, 0, 0)), "nn", 0)],
        1, None, [(res, o_spec)],
        [(jax.ShapeDtypeStruct((m, nb * bw), F32), o_spec)], _residual_epilogue, side), side)


def _mm_nt_colblocked(a, b_blk, name, out_dtype, jb, side=None):
    m = a.shape[0]
    nb, n, bw = b_blk.shape
    tm = _row_tile(m, 512)
    return _one(_matmul(
        name, (m // tm, nb // jb),
        [(a, pl.BlockSpec((tm, jb * bw), lambda i, j: (i, j)),
          b_blk, pl.BlockSpec((jb, n, bw), lambda i, j: (j, 0, 0)), "nt", 0)],
        1, (tm, n), [],
        [(jax.ShapeDtypeStruct((m, n), out_dtype), pl.BlockSpec((tm, n), lambda i, j: (i, 0)))],
        _store_epilogue, side), side)


def _mm_nt_rowblocked(a, b, name, tn, out_dtype, side=None):
    m, k = a.shape
    n = b.shape[0]
    tm = _row_tile(m, 1024)
    return _one(_matmul(
        name, (n // tn, m // tm, 1),
        [(a, pl.BlockSpec((tm, k), lambda j, i, kk: (i, 0)),
          b, pl.BlockSpec((tn, k), lambda j, i, kk: (j, 0)), "nt", 0)],
        1, None, [],
        [(jax.ShapeDtypeStruct((m, n), out_dtype), pl.BlockSpec((tm, tn), lambda j, i, kk: (i, j)))],
        _store_epilogue, side), side)


def _mm_tn_colblocked(a, b, name, bw, side=None):
    s, m = a.shape
    nb = b.shape[1] // bw
    tm = _row_tile(m, 512)
    return _one(_matmul(
        name, (nb, m // tm, 1),
        [(a, pl.BlockSpec((s, tm), lambda j, i, k: (0, i)),
          b, pl.BlockSpec((s, bw), lambda j, i, k: (0, j)), "tn", 0)],
        1, None, [],
        [(jax.ShapeDtypeStruct((nb, m, bw), BF16), pl.BlockSpec((None, tm, bw), lambda j, i, k: (j, i, 0)))],
        _store_epilogue, side), side)


def _mm_tn_rowblocked(a, b, name, bh):
    s, n = b.shape
    nb = a.shape[1] // bh
    tn = _row_tile(n, 512)
    return _matmul(
        name, (nb, n // tn, 1),
        [(a, pl.BlockSpec((s, bh), lambda j, i, k: (0, j)),
          b, pl.BlockSpec((s, tn), lambda j, i, k: (0, i)), "tn", 0)],
        1, None, [],
        [(jax.ShapeDtypeStruct((nb, bh, n), BF16), pl.BlockSpec((None, bh, tn), lambda j, i, k: (j, 0, i)))],
        _store_epilogue)[0]


def _as2d(a):
    return a.reshape(-1, a.shape[-1])


def _cast_bf16(w, name):
    w2 = _as2d(w)
    rows, cols = w2.shape
    tr = _row_tile(rows, 256)

    def body(w_ref, o_ref):
        o_ref[...] = w_ref[...].astype(BF16)

    spec = pl.BlockSpec((tr, cols), lambda i: (i, 0))
    out = pl.pallas_call(
        body, grid=(rows // tr,), in_specs=[spec], out_specs=spec,
        out_shape=jax.ShapeDtypeStruct((rows, cols), BF16),
        compiler_params=_params(), name=name,
    )(w2)
    return out.reshape(w.shape)


def _cast_bf16_layer(w, layer, name):
    _, rows, cols = w.shape
    tr = _row_tile(rows, 256)

    def body(w_ref, o_ref):
        o_ref[...] = w_ref[...].astype(BF16)

    return pl.pallas_call(
        body, grid=(rows // tr,),
        in_specs=[pl.BlockSpec((None, tr, cols), lambda i: (layer, i, 0))],
        out_specs=pl.BlockSpec((tr, cols), lambda i: (i, 0)),
        out_shape=jax.ShapeDtypeStruct((rows, cols), BF16),
        compiler_params=_params(), name=name,
    )(w)


def _reduce_adam(recvs, w, m, v, name):
    n_layers, rows, cols = w.shape
    n_slots = recvs[0].shape[0]
    tr = _row_tile(rows, max(16, (128 * 1024 // cols) // 16 * 16))
    nt = rows // tr
    c1 = 1.0 - ADAM_B1 ** ADAM_STEP
    c2 = 1.0 - ADAM_B2 ** ADAM_STEP

    def body(*refs):
        r_refs = refs[:n_layers]
        w_ref, m_ref, v_ref, g_out, d_out, m_out, v_out = refs[n_layers:]
        layer = pl.program_id(0)

        def update(r_ref):
            g = r_ref[0].astype(F32)
            for k in range(1, n_slots):
                g = g + r_ref[k].astype(F32)
            mm = ADAM_B1 * m_ref[...] + (1.0 - ADAM_B1) * g
            vv = ADAM_B2 * v_ref[...] + (1.0 - ADAM_B2) * (g * g)
            m_hat = mm / c1
            v_hat = vv / c2
            g_out[...] = g
            d_out[...] = -ADAM_LR * (m_hat / (jnp.sqrt(v_hat) + ADAM_EPS) + ADAM_WD * w_ref[...])
            m_out[...] = mm
            v_out[...] = vv

        for li in range(n_layers):
            if n_layers == 1:
                update(r_refs[li])
            else:
                pl.when(layer == li)(functools.partial(update, r_refs[li]))

    def recv_spec(li):
        def imap(layer, i):
            return (0, jnp.where(layer == li, i, jnp.where(layer < li, 0, nt - 1)), 0)
        return pl.BlockSpec((n_slots, tr, cols), imap)

    spec = pl.BlockSpec((None, tr, cols), lambda layer, i: (layer, i, 0))
    o_shape = jax.ShapeDtypeStruct(w.shape, F32)
    return pl.pallas_call(
        body, grid=(n_layers, nt),
        in_specs=[recv_spec(li) for li in range(n_layers)] + [spec] * 3,
        out_specs=[spec] * 4, out_shape=[o_shape] * 4,
        compiler_params=_params(), name=name,
    )(*recvs, w, m, v)


def _my_place():
    return lax.axis_index("x"), lax.axis_index("y"), lax.axis_index("c")


class _GatherSide:
    def __init__(self, blocks):
        self.ins = list(blocks)
        self.outs = [jax.ShapeDtypeStruct((N_DEV,) + b.shape, b.dtype) for b in blocks]

    def scratch(self):
        n = len(self.ins)
        return [pltpu.SemaphoreType.DMA((7 * n,)), pltpu.SemaphoreType.DMA((7 * n,)),
                pltpu.SemaphoreType.DMA((n,))]

    def phases(self, x_refs, out_refs, send_sems, recv_sems, local_sems):
        n = len(self.ins)
        x, y, c = _my_place()
        me, sibling = (x, y, c), (x, y, 1 - c)
        chips = [(1 - x, y), (x, 1 - y), (1 - x, 1 - y)]

        def slot(t, px, py, pc):
            return out_refs[t].at[4 * px + 2 * py + pc]

        def copy(t, k, blk, to, src=None):
            return pltpu.make_async_remote_copy(
                src_ref=slot(t, *blk) if src is None else src, dst_ref=slot(t, *blk),
                send_sem=send_sems.at[7 * t + k], recv_sem=recv_sems.at[7 * t + k],
                device_id=to, device_id_type=pl.DeviceIdType.MESH)

        def own(t):
            return pltpu.make_async_copy(x_refs[t], slot(t, *me), local_sems.at[t])

        def first(t):
            return [copy(t, 0, me, sibling, src=x_refs[t])] + [
                copy(t, 1 + j, me, (*chip, c), src=x_refs[t]) for j, chip in enumerate(chips)]

        def passed(t):
            return [copy(t, 4 + j, (*chip, c), sibling) for j, chip in enumerate(chips)]

        def start():
            for t in range(n):
                own(t).start()
                for cp in first(t):
                    cp.start()

        def mid():
            for t in range(n):
                fwd = passed(t)
                for j, chip in enumerate(chips):
                    copy(t, 1 + j, (*chip, c), me).wait_recv()
                    fwd[j].start()

        def finish():
            for t in range(n):
                copy(t, 0, sibling, me).wait_recv()
                for j, chip in enumerate(chips):
                    copy(t, 4 + j, (*chip, 1 - c), me).wait_recv()
                for cp in first(t) + passed(t):
                    cp.wait_send()
                own(t).wait()

        return start, mid, finish


class _ExchangeSide:
    def __init__(self, blocked):
        self.ins = list(blocked)
        self.outs = [jax.ShapeDtypeStruct(b.shape, b.dtype) for b in blocked]

    def scratch(self):
        n = len(self.ins)
        return [pltpu.SemaphoreType.DMA((7 * n,)), pltpu.SemaphoreType.DMA((7 * n,)),
                pltpu.SemaphoreType.DMA((n,))]

    def phases(self, srcs, dsts, send_sems, recv_sems, local_sems):
        n = len(self.ins)
        x, y, c = _my_place()
        me = 4 * x + 2 * y + c

        def own(t):
            return pltpu.make_async_copy(srcs[t].at[me], dsts[t].at[me], local_sems.at[t])

        def copies(t, arriving):
            res = []
            for k in range(1, N_DEV):
                px, py, pc = x ^ (k >> 2), y ^ ((k >> 1) & 1), c ^ (k & 1)
                peer = 4 * px + 2 * py + pc
                sem = 7 * t + k - 1
                res.append(pltpu.make_async_remote_copy(
                    src_ref=srcs[t].at[peer], dst_ref=dsts[t].at[peer if arriving else me],
                    send_sem=send_sems.at[sem], recv_sem=recv_sems.at[sem],
                    device_id=(px, py, pc), device_id_type=pl.DeviceIdType.MESH))
            return res

        def start():
            for t in range(n):
                own(t).start()
                for send in copies(t, False):
                    send.start()

        def mid():
            pass

        def finish():
            for t in range(n):
                for arrival in copies(t, True):
                    arrival.wait_recv()
                for send in copies(t, False):
                    send.wait_send()
                own(t).wait()

        return start, mid, finish


def _run_side(side, name):
    n_in, n_out = len(side.ins), len(side.outs)

    def body(*refs):
        start, mid, finish = side.phases(refs[:n_in], refs[n_in:n_in + n_out], *refs[n_in + n_out:])
        start()
        mid()
        finish()

    hbm = pl.BlockSpec(memory_space=pltpu.HBM)
    return pl.pallas_call(
        body, out_shape=list(side.outs), in_specs=[hbm] * n_in, out_specs=[hbm] * n_out,
        scratch_shapes=side.scratch(), name=name,
    )(*side.ins)


def _to_residue_major(a, dilation):
    s, w = a.shape
    return a.reshape(s // dilation, dilation, w).transpose(1, 0, 2).reshape(s, w)


def _from_residue_major(a, dilation):
    s, w = a.shape
    return a.reshape(dilation, s // dilation, w).transpose(1, 0, 2).reshape(s, w)


def _mem_kv(mem, gain, wkv, layer, tag):
    mem_n = _rmsnorm_fwd(mem, gain, "mem_norm_" + tag)
    mlen, d = mem.shape
    nb, _, bh, n = wkv.shape
    kv = _matmul(
        "mem_kv_" + tag, (1, nb),
        [(mem_n, pl.BlockSpec((mlen, bh), lambda i, j: (0, j)),
          wkv, pl.BlockSpec((None, None, bh, n), lambda i, j: (j, layer, 0, 0)), "nn", 0)],
        1, (mlen, n), [],
        [(jax.ShapeDtypeStruct((mlen, n), BF16), pl.BlockSpec((mlen, n), lambda i, j: (0, 0)))],
        _store_epilogue)[0]
    return mem_n, kv


def _mem_kv_bwd(mem, gain, mem_n, wkv, layer, dkv, tag):
    mlen, d = mem.shape
    nb, _, bh, n = wkv.shape
    dkvb = dkv.astype(BF16)
    dw = _mm_tn_rowblocked(mem_n, dkvb, "mem_kv_dw_" + tag, bh)
    dmem_n = _matmul(
        "mem_kv_dx_" + tag, (nb, 1),
        [(dkvb, pl.BlockSpec((mlen, n), lambda j, k: (0, 0)),
          wkv, pl.BlockSpec((None, None, bh, n), lambda j, k: (j, layer, 0, 0)), "nt", 0)],
        1, None, [],
        [(jax.ShapeDtypeStruct((mlen, d), F32), pl.BlockSpec((mlen, bh), lambda j, k: (0, j)))],
        _store_epilogue)[0]
    _, _, dgain = _rmsnorm_bwd(mem, gain, dmem_n, None, "mem_norm_bwd_" + tag)
    return dw, dgain


def _row_blocks(a):
    return a.reshape(N_DEV, -1, a.shape[-1])


def _rows(a):
    return a.reshape(-1, a.shape[-1])


def _ffn_bwd(x, gain, w_gate, w_up, w_down, saved, dx, dxb, tag):
    hf, silu, up_dsilu, hid = saved
    dgate, dup = _swiglu_bwd_hidden(dxb, w_down, silu, up_dsilu, "swiglu_bwd_hidden_" + tag)
    dwd = _mm_tn_full(hid, dxb, "swiglu_bwd_wdown_" + tag)
    dwg, (r_wd,) = _mm_tn_full(dgate, hf, "swiglu_bwd_wgate_" + tag, _ExchangeSide([_row_blocks(dwd)]))
    dwu, (r_wg,) = _mm_tn_full(dup, hf, "swiglu_bwd_wup_" + tag, _ExchangeSide([_row_blocks(dwg)]))
    dhf, (r_wu,) = _swiglu_bwd_input(dgate, dup, w_gate, w_up, "swiglu_bwd_input_" + tag,
                                     _ExchangeSide([_row_blocks(dwu)]))
    dx_new, dxb_new, dgain = _rmsnorm_bwd(x, gain, dhf, dx, "ffn_norm_bwd_" + tag)
    return dx_new, dxb_new, dgain, r_wg, r_wu, r_wd


def _local_step(x, mem, positions, target, w_attn_in, shards, small):
    s, d = x.shape
    tabs = _rotary_tables(positions)
    mix_norm, mem_norm, ffn_norm = small["mix_norm"], small["mem_norm"], small["ffn_norm"]

    h0 = _rmsnorm_fwd(x, mix_norm[0:1], "mix_norm_0")
    proj0, (w_mem_kv, w_attn_out, w_gate0) = _attn_in_proj(
        h0, w_attn_in, tabs, _GatherSide([shards["w_mem_kv"], shards["attn_w_out"], shards["w_gate"][0]]))
    qkv, offs, outs, lses = [], [], [], []
    for g, dil in enumerate(DILATIONS):
        if dil == 1:
            arr, off = proj0, (g, N_GROUPS + g, 2 * N_GROUPS + g)
        else:
            cols = [proj0[:, (p * N_GROUPS + g) * GROUP_W:(p * N_GROUPS + g + 1) * GROUP_W] for p in range(3)]
            arr, off = _to_residue_major(jnp.concatenate(cols, axis=1), dil), (0, 1, 2)
        o, lse = _attn_fwd(arr, arr, arr, off, s // dil // BLK, "attn_fwd_%d" % g)
        qkv.append(arr)
        offs.append(off)
        if dil > 1:
            o, lse = _from_residue_major(o, dil), _from_residue_major(lse, dil)
        outs.append(o)
        lses.append(lse)
    mix0 = _merge_fwd(outs, lses)
    qm_off0 = 3 * N_GROUPS
    mem_n0, kv0 = _mem_kv(mem, mem_norm[0:1], w_mem_kv, 0, "0")
    memo0 = _memattn_fwd(proj0, qm_off0, kv0, "memattn_fwd_0")
    cat0 = jnp.concatenate([mix0, memo0], axis=1)
    x1, (w_up0,) = _mm_nn_colblocked(cat0, w_attn_out, "attn_out_proj", x, _GatherSide([shards["w_up"][0]]))
    hf0 = _rmsnorm_fwd(x1, ffn_norm[0:1], "ffn_norm_0")
    w_gate0, w_up0 = _rows(w_gate0), _rows(w_up0)
    (silu0, ud0, hid0), (w_down0, w_sgu_in, w_sgu_out) = _swiglu_fwd(
        hf0, w_gate0, w_up0, "swiglu_fwd_0",
        _GatherSide([shards["w_down"][0], shards["sgu_w_in"], shards["sgu_w_out"]]))
    w_down0 = _rows(w_down0)
    x2, (w_gate1,) = _swiglu_down(hid0, w_down0, x1, "swiglu_down_0", _GatherSide([shards["w_gate"][1]]))
    ffn_saved0 = (hf0, silu0, ud0, hid0)

    h1 = _rmsnorm_fwd(x2, mix_norm[1:2], "mix_norm_1")
    w_sgu_in = _rows(w_sgu_in)
    w_sgu_out = _rows(w_sgu_out)
    proj1, (w_up1,) = _mm_nt_rowblocked(h1, w_sgu_in, "sgu_in_proj", w_sgu_in.shape[0] // 7, BF16,
                                        _GatherSide([shards["w_up"][1]]))
    w_gate1, w_up1 = _rows(w_gate1), _rows(w_up1)
    b_st = small["sgu_b_spatial"].T
    mix1 = _sgu_fwd(proj1, small["sgu_ln_g"], small["sgu_ln_b"], small["sgu_w_spatial"], b_st)
    qm_off1 = 2 * SGU_W // GROUP_W
    mem_n1, kv1 = _mem_kv(mem, mem_norm[1:2], w_mem_kv, 1, "1")
    memo1 = _memattn_fwd(proj1, qm_off1, kv1, "memattn_fwd_1")
    cat1 = jnp.concatenate([mix1, memo1], axis=1)
    x3 = _mm_nn(cat1, w_sgu_out, "sgu_out_proj", d // 2, out_dtype=F32, res=x2)
    hf1 = _rmsnorm_fwd(x3, ffn_norm[1:2], "ffn_norm_1")
    (silu1, ud1, hid1), (w_down1,) = _swiglu_fwd(hf1, w_gate1, w_up1, "swiglu_fwd_1",
                                                 _GatherSide([shards["w_down"][1]]))
    w_down1 = _rows(w_down1)
    x4 = _swiglu_down(hid1, w_down1, x3, "swiglu_down_1")
    ffn_saved1 = (hf1, silu1, ud1, hid1)

    loss, dx, dxb, d_final = _loss_head(x4, small["final_norm"], target)

    recvs, sgrads = {}, {}
    dx, dxb, d_ffn1, r_wg1, r_wu1, r_wd1 = _ffn_bwd(x3, ffn_norm[1:2], w_gate1, w_up1, w_down1, ffn_saved1,
                                                    dx, dxb, "1")
    dcat1 = _mm_nt_rowblocked(dxb, w_sgu_out, "sgu_out_proj_dx", w_sgu_out.shape[0] // 2, BF16)
    dwsout = _mm_tn_rowblocked(cat1, dxb, "sgu_out_proj_dw", w_sgu_out.shape[0] // N_DEV)
    duv, dws, dbs, dlng, dlnb = _sgu_bwd(proj1, dcat1, small["sgu_ln_g"], small["sgu_ln_b"],
                                         small["sgu_w_spatial"], b_st)
    dqm1, dkv1 = _memattn_bwd(proj1, qm_off1, kv1, dcat1, SGU_W // GROUP_W, "memattn_bwd_1")
    dwkv1, d_memnorm1 = _mem_kv_bwd(mem, mem_norm[1:2], mem_n1, w_mem_kv, 1, dkv1, "1")
    dproj1 = jnp.concatenate([duv, dqm1], axis=1)
    dwsin, (r_wsout, r_wkv1) = _mm_tn_full(dproj1, h1, "sgu_in_proj_dw", _ExchangeSide([dwsout, dwkv1]))
    dh1, (r_wsin,) = _mm_nn(dproj1, w_sgu_in, "sgu_in_proj_dx", d // 4, out_dtype=F32,
                            side=_ExchangeSide([_row_blocks(dwsin)]))
    dx, dxb, d_mix1 = _rmsnorm_bwd(x2, mix_norm[1:2], dh1, dx, "mix_norm_bwd_1")

    dx, dxb, d_ffn0, r_wg0, r_wu0, r_wd0 = _ffn_bwd(x1, ffn_norm[0:1], w_gate0, w_up0, w_down0, ffn_saved0,
                                                    dx, dxb, "0")
    dcat0 = _mm_nt_colblocked(dxb, w_attn_out, "attn_out_proj_dx", BF16, 4)
    dwout0 = _mm_tn_colblocked(cat0, dxb, "attn_out_proj_dw", w_attn_out.shape[2])
    dos_and_deltas = _merge_bwd(dcat0, outs, lses)
    dqkv = []
    for g, dil in enumerate(DILATIONS):
        do_g, dl_g = dos_and_deltas[g], dos_and_deltas[N_GROUPS + g]
        lse_g = lses[g]
        if dil > 1:
            do_g, dl_g, lse_g = (_to_residue_major(t, dil) for t in (do_g, dl_g, lse_g))
        args = (qkv[g], qkv[g], qkv[g], offs[g], do_g, lse_g, dl_g, s // dil // BLK)
        dq = _attn_bwd_dq(*args, "attn_bwd_dq_%d" % g)
        dk, dv = _attn_bwd_dkv(*args, "attn_bwd_dkv_%d" % g)
        t = jnp.concatenate([dq, dk, dv], axis=1)
        dqkv.append(_from_residue_major(t, dil) if dil > 1 else t)
    dqm0, dkv0 = _memattn_bwd(proj0, qm_off0, kv0, dcat0, 1, "memattn_bwd_0")
    dwkv0, d_memnorm0 = _mem_kv_bwd(mem, mem_norm[0:1], mem_n0, w_mem_kv, 0, dkv0, "0")
    dproj0 = _assemble_dproj(dqkv, dqm0, tabs)
    dwin0, (r_wout0, r_wkv0) = _mm_tn_full(dproj0, h0, "attn_in_proj_dw", _ExchangeSide([dwout0, dwkv0]))
    dh0, (r_win0,) = _mm_nn(dproj0, w_attn_in, "attn_in_proj_dx", d // 4, out_dtype=F32,
                            side=_ExchangeSide([_row_blocks(dwin0)]))
    grad_x, _, d_mix0 = _rmsnorm_bwd(x, mix_norm[0:1], dh0, dx, "mix_norm_bwd_0")

    recvs["w_gate"] = [r_wg0, r_wg1]
    recvs["w_up"] = [r_wu0, r_wu1]
    recvs["w_down"] = [r_wd0, r_wd1]
    recvs["w_mem_kv"] = [r_wkv0, r_wkv1]
    recvs["attn_w_in"] = [r_win0]
    recvs["attn_w_out"] = [r_wout0]
    recvs["sgu_w_in"] = [r_wsin]
    recvs["sgu_w_out"] = [r_wsout]
    sgrads["mix_norm"] = jnp.concatenate([d_mix0, d_mix1], axis=0)
    sgrads["mem_norm"] = jnp.concatenate([d_memnorm0, d_memnorm1], axis=0)
    sgrads["ffn_norm"] = jnp.concatenate([d_ffn0, d_ffn1], axis=0)
    sgrads["final_norm"] = d_final
    sgrads["sgu_w_spatial"] = dws
    sgrads["sgu_b_spatial"] = dbs[:, :SGU_GROUPS].T
    sgrads["sgu_ln_g"] = dlng
    sgrads["sgu_ln_b"] = dlnb
    return loss, grad_x, recvs, sgrads


BIG = ("w_mem_kv", "w_gate", "w_up", "w_down", "attn_w_in", "attn_w_out", "sgu_w_in", "sgu_w_out")
TRANSPOSED = ("w_gate", "w_up", "sgu_w_in", "attn_w_in")
SMALL_REPLICATED = ("mix_norm", "mem_norm", "ffn_norm", "final_norm", "sgu_w_spatial", "sgu_b_spatial")
SMALL_SHARDED = ("sgu_ln_g", "sgu_ln_b")
WEIGHT_ORDER = ("mix_norm", "mem_norm", "w_mem_kv", "ffn_norm", "w_gate", "w_up", "w_down", "attn_w_in",
                "attn_w_out", "sgu_w_in", "sgu_ln_g", "sgu_ln_b", "sgu_w_spatial", "sgu_b_spatial",
                "sgu_w_out", "final_norm")
PACK_LANES = 128


def _pack(parts):
    flat = [p.reshape(-1) for p in parts]
    sizes = [f.shape[0] for f in flat]
    total = sum(sizes)
    rows = -(-total // PACK_LANES)
    rows = -(-rows // 8) * 8
    pad = rows * PACK_LANES - total
    packed = jnp.concatenate(flat + [jnp.zeros((pad,), F32)]).reshape(rows, PACK_LANES)
    offs, o = [], 0
    for sz in sizes:
        offs.append(o)
        o += sz
    return packed, offs, sizes


def _unpack(packed, offs, sizes, shapes):
    flat = packed.reshape(-1)
    return [flat[o:o + sz].reshape(shp) for o, sz, shp in zip(offs, sizes, shapes)]


def kernel(x, mem, positions, mix_norm, mem_norm, w_mem_kv, ffn_norm, w_gate, w_up, w_down, attn_w_in, attn_w_out, sgu_w_in, sgu_ln_g, sgu_ln_b, sgu_w_spatial, sgu_b_spatial, sgu_w_out, final_norm, loss_target, m_mix_norm, m_mem_norm, m_w_mem_kv, m_ffn_norm, m_w_gate, m_w_up, m_w_down, m_attn_w_in, m_attn_w_out, m_sgu_w_in, m_sgu_ln_g, m_sgu_ln_b, m_sgu_w_spatial, m_sgu_b_spatial, m_sgu_w_out, m_final_norm, v_mix_norm, v_mem_norm, v_w_mem_kv, v_ffn_norm, v_w_gate, v_w_up, v_w_down, v_attn_w_in, v_attn_w_out, v_sgu_w_in, v_sgu_ln_g, v_sgu_ln_b, v_sgu_w_spatial, v_sgu_b_spatial, v_sgu_w_out, v_final_norm):
    w = dict(mix_norm=mix_norm, mem_norm=mem_norm, w_mem_kv=w_mem_kv, ffn_norm=ffn_norm, w_gate=w_gate,
             w_up=w_up, w_down=w_down, attn_w_in=attn_w_in, attn_w_out=attn_w_out, sgu_w_in=sgu_w_in,
             sgu_ln_g=sgu_ln_g, sgu_ln_b=sgu_ln_b, sgu_w_spatial=sgu_w_spatial, sgu_b_spatial=sgu_b_spatial,
             sgu_w_out=sgu_w_out, final_norm=final_norm)
    mo = dict(mix_norm=m_mix_norm, mem_norm=m_mem_norm, w_mem_kv=m_w_mem_kv, ffn_norm=m_ffn_norm,
              w_gate=m_w_gate, w_up=m_w_up, w_down=m_w_down, attn_w_in=m_attn_w_in, attn_w_out=m_attn_w_out,
              sgu_w_in=m_sgu_w_in, sgu_ln_g=m_sgu_ln_g, sgu_ln_b=m_sgu_ln_b, sgu_w_spatial=m_sgu_w_spatial,
              sgu_b_spatial=m_sgu_b_spatial, sgu_w_out=m_sgu_w_out, final_norm=m_final_norm)
    vo = dict(mix_norm=v_mix_norm, mem_norm=v_mem_norm, w_mem_kv=v_w_mem_kv, ffn_norm=v_ffn_norm,
              w_gate=v_w_gate, w_up=v_w_up, w_down=v_w_down, attn_w_in=v_attn_w_in, attn_w_out=v_attn_w_out,
              sgu_w_in=v_sgu_w_in, sgu_ln_g=v_sgu_ln_g, sgu_ln_b=v_sgu_ln_b, sgu_w_spatial=v_sgu_w_spatial,
              sgu_b_spatial=v_sgu_b_spatial, sgu_w_out=v_sgu_w_out, final_norm=v_final_norm)
    me = 4 * lax.axis_index("x") + 2 * lax.axis_index("y") + lax.axis_index("c")
    d_model = x.shape[-1]

    for n in TRANSPOSED:
        w[n], mo[n], vo[n] = (jnp.swapaxes(t, 1, 2) for t in (w[n], mo[n], vo[n]))

    shards = {
        "w_mem_kv": _cast_bf16(w_mem_kv, "cast_w_mem_kv"),
        "attn_w_out": _cast_bf16(attn_w_out[0], "cast_attn_w_out"),
        "sgu_w_in": _cast_bf16(w["sgu_w_in"][0], "cast_sgu_w_in"),
        "sgu_w_out": _cast_bf16(sgu_w_out[0], "cast_sgu_w_out"),
    }
    for n in ("w_gate", "w_up", "w_down"):
        shards[n] = [_cast_bf16_layer(w[n], layer, "cast_%s_%d" % (n, layer)) for layer in range(w[n].shape[0])]
    ln_pack = jnp.concatenate([sgu_ln_g, sgu_ln_b], axis=0)
    w_attn_in, ln_all = _run_side(
        _GatherSide([_cast_bf16(w["attn_w_in"][0], "cast_attn_w_in"), ln_pack]), "gather_attn_w_in")
    w_attn_in = _rows(w_attn_in)
    ln_full = ln_all.transpose(1, 0, 2).reshape(2, 1, -1)
    small = dict(mix_norm=mix_norm, mem_norm=mem_norm, ffn_norm=ffn_norm, final_norm=final_norm.reshape(1, -1),
                 sgu_w_spatial=sgu_w_spatial[0], sgu_b_spatial=sgu_b_spatial[0],
                 sgu_ln_g=ln_full[0], sgu_ln_b=ln_full[1])

    loss, grad_x, recvs, sgrads = _local_step(x[0], mem[0], positions[0], loss_target[0], w_attn_in, shards,
                                              small)
    loss = lax.psum(loss[0, 0], MESH_AXES)

    out_g, out_d, out_m, out_v = {}, {}, {}, {}
    for n in BIG:
        shard = w[n]
        w3 = shard.reshape(shard.shape[0], -1, shard.shape[-1])
        rs = [r.reshape(N_DEV, -1, shard.shape[-1]) for r in recvs[n]]
        res = _reduce_adam(rs, w3, mo[n].reshape(w3.shape), vo[n].reshape(w3.shape), "adam_" + n)
        res = [r.reshape(shard.shape) for r in res]
        if n in TRANSPOSED:
            res = [jnp.swapaxes(r, 1, 2) for r in res]
        out_g[n], out_d[n], out_m[n], out_v[n] = res

    small_names = SMALL_REPLICATED + SMALL_SHARDED
    packed, offs, sizes = _pack([sgrads[n] for n in small_names])
    all_packs = _run_side(_GatherSide([packed]), "gather_small_grads")[0]
    rep_shapes = [w[n].shape for n in SMALL_REPLICATED]
    w_pack, w_offs, w_sizes = _pack([w[n] for n in SMALL_REPLICATED])
    m_pack, _, _ = _pack([mo[n] for n in SMALL_REPLICATED])
    v_pack, _, _ = _pack([vo[n] for n in SMALL_REPLICATED])
    n_rep_rows = w_pack.shape[0]
    res = _reduce_adam([all_packs[:, :n_rep_rows]], w_pack[None], m_pack[None], v_pack[None], "adam_small")
    for dst, r in zip((out_g, out_d, out_m, out_v), res):
        for n, val in zip(SMALL_REPLICATED, _unpack(r[0], w_offs, w_sizes, rep_shapes)):
            dst[n] = val
    ln_rows0 = offs[len(SMALL_REPLICATED)] // PACK_LANES
    ln_rows = 2 * SGU_W // PACK_LANES
    ln_sum = _reduce_adam([all_packs[:, ln_rows0:ln_rows0 + ln_rows]], jnp.zeros((1, ln_rows, PACK_LANES), F32),
                          jnp.zeros((1, ln_rows, PACK_LANES), F32), jnp.zeros((1, ln_rows, PACK_LANES), F32),
                          "sum_ln_grads")[0]
    ln_grads = ln_sum.reshape(2, N_DEV, -1)
    ln_mine = lax.dynamic_index_in_dim(ln_grads, me, axis=1, keepdims=False)
    w_ln = jnp.concatenate([sgu_ln_g, sgu_ln_b], axis=0)[None]
    m_ln = jnp.concatenate([m_sgu_ln_g, m_sgu_ln_b], axis=0)[None]
    v_ln = jnp.concatenate([v_sgu_ln_g, v_sgu_ln_b], axis=0)[None]
    res = _reduce_adam([ln_mine[None]], w_ln, m_ln, v_ln, "adam_ln")
    for dst, r in zip((out_g, out_d, out_m, out_v), res):
        dst["sgu_ln_g"], dst["sgu_ln_b"] = r[0, 0:1], r[0, 1:2]

    return (loss, grad_x[None], *[out_g[n] for n in WEIGHT_ORDER], *[out_d[n] for n in WEIGHT_ORDER],
            *[out_m[n] for n in WEIGHT_ORDER], *[out_v[n] for n in WEIGHT_ORDER])
```

```python
import functools

import jax
import jax.numpy as jnp
from jax import lax
from jax.experimental import pallas as pl
from jax.experimental.pallas import tpu as pltpu

F32 = jnp.float32
BF16 = jnp.bfloat16

N_DEV = 8
HEAD = 128
HPG = 4
GROUP_W = HPG * HEAD
N_GROUPS = 3
DILATIONS = (1, 4, 16)
BLK = 128
SGU_GROUPS = 12
SGU_W = SGU_GROUPS * HEAD
ROT_HALF = 16
ROPE_THETA = 500000.0
NORM_EPS = 1e-6
LN_EPS = 1e-5
NEG_INF = -1e30
SCALE = HEAD ** -0.5

ADAM_LR = 0.001
ADAM_B1 = 0.9
ADAM_B2 = 0.999
ADAM_EPS = 1e-08
ADAM_WD = 0.01
ADAM_STEP = 10

VMEM_LIMIT_V7X = 56 * 1024 * 1024
MESH_AXES = ("x", "y", "c")

_DN = {
    "nn": (((1,), (0,)), ((), ())),
    "nt": (((1,), (1,)), ((), ())),
    "tn": (((0,), (0,)), ((), ())),
}


def _dot(a, b, kind="nn"):
    return lax.dot_general(a, b, _DN[kind], preferred_element_type=F32)


def _params():
    return pltpu.CompilerParams(vmem_limit_bytes=VMEM_LIMIT_V7X)


def _row_tile(rows, cap):
    if rows <= cap:
        return rows
    t = cap - cap % 16
    while t >= 16:
        if rows % t == 0:
            return t
        t -= 16
    return rows


def _gelu(x):
    c = 0.7978845608028654
    return 0.5 * x * (1.0 + jnp.tanh(c * (x + 0.044715 * x * x * x)))


def _gelu_and_grad(x):
    c = 0.7978845608028654
    x2 = x * x
    t = jnp.tanh(c * x * (1.0 + 0.044715 * x2))
    half = 0.5 * (1.0 + t)
    return x * half, half + 0.5 * x * (1.0 - t * t) * c * (1.0 + 3.0 * 0.044715 * x2)


def _sigmoid(x):
    return 1.0 / (1.0 + jnp.exp(-x))


def _matmul(name, grid, terms, n_acc, acc_shape, extras, outs, epilogue, side=None):
    nk = grid[-1]
    nt, ne, no = len(terms), len(extras), len(outs)
    kinds = [(t[4], t[5]) for t in terms]
    n_scratch_acc = 0 if nk == 1 else n_acc
    ns_in = len(side.ins) if side else 0
    ns_out = len(side.outs) if side else 0
    n_steps = 1
    for g in grid:
        n_steps *= g

    def body(*refs):
        pos = 0
        ab = refs[pos:pos + 2 * nt]
        pos += 2 * nt
        ex = refs[pos:pos + ne]
        pos += ne
        s_in = refs[pos:pos + ns_in]
        pos += ns_in
        out = refs[pos:pos + no]
        pos += no
        s_out = refs[pos:pos + ns_out]
        pos += ns_out
        accs = refs[pos:pos + n_scratch_acc]
        s_sems = refs[pos + n_scratch_acc:]
        if side:
            step = pl.program_id(0)
            for ax in range(1, len(grid)):
                step = step * grid[ax] + pl.program_id(ax)
            start, mid, finish = side.phases(s_in, s_out, *s_sems)
            pl.when(step == 0)(start)
        parts = [None] * n_acc
        for t, (kind, ai) in enumerate(kinds):
            a_ref, b_ref = ab[2 * t], ab[2 * t + 1]
            if len(b_ref.shape) == 2:
                pairs = [(a_ref[...], b_ref[...])]
            elif len(a_ref.shape) == 3:
                pairs = [(a_ref[q], b_ref[q]) for q in range(b_ref.shape[0])]
            else:
                bw = b_ref.shape[2]
                pairs = [(a_ref[:, q * bw:(q + 1) * bw], b_ref[q]) for q in range(b_ref.shape[0])]
            for a, b in pairs:
                p = _dot(a.astype(BF16), b.astype(BF16), kind)
                parts[ai] = p if parts[ai] is None else parts[ai] + p
        if nk == 1:
            epilogue(parts, ex, out)
        else:
            k = pl.program_id(len(grid) - 1)

            @pl.when(k == 0)
            def _():
                for ai in range(n_acc):
                    accs[ai][...] = parts[ai]

            @pl.when(k > 0)
            def _():
                for ai in range(n_acc):
                    accs[ai][...] += parts[ai]

            @pl.when(k == nk - 1)
            def _():
                epilogue([a[...] for a in accs], ex, out)

        if side:
            pl.when(step == (3 * n_steps) // 4)(mid)
            pl.when(step == n_steps - 1)(finish)

    hbm = pl.BlockSpec(memory_space=pltpu.HBM)
    in_specs, args = [], []
    for (a, a_spec, b, b_spec, _, _) in terms:
        in_specs += [a_spec, b_spec]
        args += [a, b]
    for (e, e_spec) in extras:
        in_specs.append(e_spec)
        args.append(e)
    scratch = [pltpu.VMEM(acc_shape, F32) for _ in range(n_scratch_acc)]
    out_specs = [o[1] for o in outs]
    out_shape = [o[0] for o in outs]
    if side:
        in_specs += [hbm] * ns_in
        args += list(side.ins)
        out_specs += [hbm] * ns_out
        out_shape += list(side.outs)
        scratch += side.scratch()
    res = pl.pallas_call(
        body, grid=grid, in_specs=in_specs, out_specs=out_specs, out_shape=out_shape,
        scratch_shapes=scratch, compiler_params=_params(), name=name,
    )(*args)
    return res if side is None else (res[:no], res[no:])


def _one(res, side):
    return res[0] if side is None else (res[0][0], res[1])


def _store_epilogue(parts, ex, out):
    out[0][...] = parts[0].astype(out[0].dtype)


def _residual_epilogue(parts, ex, out):
    out[0][...] = (parts[0] + ex[0][...]).astype(out[0].dtype)


def _rmsnorm_fwd(x, g, name):
    rows, d = x.shape
    tm = _row_tile(rows, 512)

    def body(x_ref, g_ref, o_ref):
        xf = x_ref[...]
        r = lax.rsqrt(jnp.mean(xf * xf, axis=-1, keepdims=True) + NORM_EPS)
        o_ref[...] = (xf * r * g_ref[...]).astype(o_ref.dtype)

    return pl.pallas_call(
        body, grid=(rows // tm,),
        in_specs=[pl.BlockSpec((tm, d), lambda i: (i, 0)), pl.BlockSpec((1, d), lambda i: (0, 0))],
        out_specs=pl.BlockSpec((tm, d), lambda i: (i, 0)),
        out_shape=jax.ShapeDtypeStruct((rows, d), BF16),
        compiler_params=_params(), name=name,
    )(x, g)


def _rmsnorm_bwd(x, g, dh, dres, name):
    rows, d = x.shape
    tm = _row_tile(rows, 256)
    has_res = dres is not None

    def body(*refs):
        if has_res:
            x_ref, g_ref, dh_ref, dres_ref, dx_ref, dxb_ref, dg_ref = refs
        else:
            x_ref, g_ref, dh_ref, dx_ref, dxb_ref, dg_ref = refs
        i = pl.program_id(0)
        xf = x_ref[...]
        r = lax.rsqrt(jnp.mean(xf * xf, axis=-1, keepdims=True) + NORM_EPS)
        xhat = xf * r
        dy = dh_ref[...].astype(F32)
        gdy = dy * g_ref[...]
        c = jnp.mean(gdy * xhat, axis=-1, keepdims=True)
        dx = r * (gdy - xhat * c)
        if has_res:
            dx = dx + dres_ref[...]
        dx_ref[...] = dx
        dxb_ref[...] = dx.astype(BF16)

        @pl.when(i == 0)
        def _():
            dg_ref[...] = jnp.zeros_like(dg_ref)

        dg_ref[...] += jnp.sum(dy * xhat, axis=0, keepdims=True)

    row_spec = pl.BlockSpec((tm, d), lambda i: (i, 0))
    vec_spec = pl.BlockSpec((1, d), lambda i: (0, 0))
    in_specs = [row_spec, vec_spec, row_spec] + ([row_spec] if has_res else [])
    args = [x, g, dh] + ([dres] if has_res else [])
    return pl.pallas_call(
        body, grid=(rows // tm,), in_specs=in_specs,
        out_specs=[row_spec, row_spec, vec_spec],
        out_shape=[jax.ShapeDtypeStruct((rows, d), F32), jax.ShapeDtypeStruct((rows, d), BF16),
                   jax.ShapeDtypeStruct((1, d), F32)],
        compiler_params=_params(), name=name,
    )(*args)


def _loss_head(x, g, target):
    rows, d = x.shape
    tm = _row_tile(rows, 256)

    def body(x_ref, g_ref, t_ref, loss_ref, dx_ref, dxb_ref, dg_ref):
        i = pl.program_id(0)
        xf = x_ref[...]
        gv = g_ref[...]
        r = lax.rsqrt(jnp.mean(xf * xf, axis=-1, keepdims=True) + NORM_EPS)
        xhat = xf * r
        err = xhat * gv - t_ref[...]
        row_loss = jnp.mean(err * err, axis=-1, keepdims=True)
        dy = err * (1.0 / d)
        gdy = dy * gv
        c = jnp.mean(gdy * xhat, axis=-1, keepdims=True)
        dx = r * (gdy - xhat * c)
        dx_ref[...] = dx
        dxb_ref[...] = dx.astype(BF16)

        @pl.when(i == 0)
        def _():
            dg_ref[...] = jnp.zeros_like(dg_ref)
            loss_ref[...] = jnp.zeros_like(loss_ref)

        dg_ref[...] += jnp.sum(dy * xhat, axis=0, keepdims=True)
        loss_ref[...] += 0.5 * jnp.sum(row_loss, axis=0, keepdims=True)

    row_spec = pl.BlockSpec((tm, d), lambda i: (i, 0))
    vec_spec = pl.BlockSpec((1, d), lambda i: (0, 0))
    return pl.pallas_call(
        body, grid=(rows // tm,), in_specs=[row_spec, vec_spec, row_spec],
        out_specs=[pl.BlockSpec((1, 1), lambda i: (0, 0)), row_spec, row_spec, vec_spec],
        out_shape=[jax.ShapeDtypeStruct((1, 1), F32), jax.ShapeDtypeStruct((rows, d), F32),
                   jax.ShapeDtypeStruct((rows, d), BF16), jax.ShapeDtypeStruct((1, d), F32)],
        compiler_params=_params(), name="loss_head",
    )(x, g, target)


def _rotary_tables(positions):
    inv_freq = ROPE_THETA ** (-jnp.arange(ROT_HALF, dtype=F32) / ROT_HALF)
    ang = positions.astype(F32)[:, None] * inv_freq
    cos, sin = jnp.cos(ang), jnp.sin(ang)
    s = positions.shape[0]
    z = jnp.zeros((s, HEAD - 2 * ROT_HALF), F32)
    z16 = jnp.zeros((s, ROT_HALF), F32)
    c = jnp.concatenate([cos, cos, jnp.ones_like(z)], axis=1)
    s1 = jnp.concatenate([z16, sin, z], axis=1)
    s2 = jnp.concatenate([-sin, z16, z], axis=1)
    return c, s1, s2


ATTN_PROJ_HEADS_PER_STEP = 10


def _attn_in_proj(h, w_t, tabs, side=None):
    s, d = h.shape
    n = w_t.shape[0]
    tn = ATTN_PROJ_HEADS_PER_STEP * HEAD
    n_rot_heads = 2 * N_GROUPS * HPG
    n_rot_steps = -(-n_rot_heads // ATTN_PROJ_HEADS_PER_STEP)
    tm = _row_tile(s, 1024)

    def epilogue(parts, ex, out):
        j = pl.program_id(0)
        acc = parts[0]

        @pl.when(j < n_rot_steps)
        def _():
            c, s1, s2 = ex[0][...], ex[1][...], ex[2][...]
            for t in range(ATTN_PROJ_HEADS_PER_STEP):
                seg = acc[:, t * HEAD:(t + 1) * HEAD]
                rot = seg * c + pltpu.roll(seg, ROT_HALF, 1) * s1 + pltpu.roll(seg, HEAD - ROT_HALF, 1) * s2
                is_rot = (j * ATTN_PROJ_HEADS_PER_STEP + t) < n_rot_heads
                out[0][:, t * HEAD:(t + 1) * HEAD] = jnp.where(is_rot, rot, seg).astype(BF16)

        @pl.when(j >= n_rot_steps)
        def _():
            out[0][...] = acc.astype(BF16)

    tab_spec = pl.BlockSpec((tm, HEAD), lambda j, m, k: (m, 0))
    return _one(_matmul(
        "attn_in_proj", (n // tn, s // tm, 1),
        [(h, pl.BlockSpec((tm, d), lambda j, m, k: (m, 0)),
          w_t, pl.BlockSpec((tn, d), lambda j, m, k: (j, 0)), "nt", 0)],
        1, None, [(tabs[0], tab_spec), (tabs[1], tab_spec), (tabs[2], tab_spec)],
        [(jax.ShapeDtypeStruct((s, n), BF16), pl.BlockSpec((tm, tn), lambda j, m, k: (m, j)))],
        epilogue, side), side)


ATT_TILE_BLOCKS = 4


def _att_blocks(seq_blocks):
    return min(ATT_TILE_BLOCKS, seq_blocks)


def _band_masks():
    qi = lax.broadcasted_iota(jnp.int32, (BLK, BLK), 0)
    ki = lax.broadcasted_iota(jnp.int32, (BLK, BLK), 1)
    return ki <= qi, ki >= qi


def _attn_fwd(q_arr, k_arr, v_arr, offs, seq_blocks, name):
    s = q_arr.shape[0]
    qo, ko, vo = offs
    nb = _att_blocks(seq_blocks)

    def body(q_ref, kc_ref, kp_ref, vc_ref, vp_ref, o_ref, lse_ref):
        n = pl.program_id(0)
        tile_starts_seq = (n * nb) % seq_blocks == 0
        mask_c, mask_p = _band_masks()
        pairs = [(b * BLK, h * HEAD) for b in range(nb) for h in range(HPG)]

        def keys_prev(ref_c, ref_p, r0, c0):
            return ref_p[:, c0:c0 + HEAD] if r0 == 0 else ref_c[r0 - BLK:r0, c0:c0 + HEAD]

        s_c, s_p = [], []
        for r0, c0 in pairs:
            q = q_ref[r0:r0 + BLK, c0:c0 + HEAD]
            s_c.append(jnp.where(mask_c, _dot(q, kc_ref[r0:r0 + BLK, c0:c0 + HEAD], "nt") * SCALE, NEG_INF))
            sp = jnp.where(mask_p, _dot(q, keys_prev(kc_ref, kp_ref, r0, c0), "nt") * SCALE, NEG_INF)
            s_p.append(jnp.where(tile_starts_seq, NEG_INF, sp) if r0 == 0 else sp)
        m = [jnp.maximum(jnp.max(a, axis=-1, keepdims=True), jnp.max(b, axis=-1, keepdims=True))
             for a, b in zip(s_c, s_p)]
        p_c = [jnp.exp(a - mm) for a, mm in zip(s_c, m)]
        p_p = [jnp.exp(a - mm) for a, mm in zip(s_p, m)]
        l = [jnp.sum(a, axis=-1, keepdims=True) + jnp.sum(b, axis=-1, keepdims=True) for a, b in zip(p_c, p_p)]
        inv = [1.0 / ll for ll in l]
        for i, (r0, c0) in enumerate(pairs):
            o = (_dot((p_c[i] * inv[i]).astype(BF16), vc_ref[r0:r0 + BLK, c0:c0 + HEAD])
                 + _dot((p_p[i] * inv[i]).astype(BF16), keys_prev(vc_ref, vp_ref, r0, c0)))
            o_ref[r0:r0 + BLK, c0:c0 + HEAD] = o.astype(BF16)
            lse_ref[r0:r0 + BLK, c0:c0 + HEAD] = jnp.broadcast_to(m[i] + jnp.log(l[i]), (BLK, HEAD))

    def cur(off):
        return pl.BlockSpec((nb * BLK, GROUP_W), lambda n: (n, off))

    def prev(off):
        return pl.BlockSpec((BLK, GROUP_W), lambda n: (jnp.maximum(n * nb - 1, 0), off))

    return pl.pallas_call(
        body, grid=(s // (nb * BLK),),
        in_specs=[cur(qo), cur(ko), prev(ko), cur(vo), prev(vo)],
        out_specs=[cur(0), cur(0)],
        out_shape=[jax.ShapeDtypeStruct((s, GROUP_W), BF16), jax.ShapeDtypeStruct((s, GROUP_W), F32)],
        compiler_params=_params(), name=name,
    )(q_arr, k_arr, k_arr, v_arr, v_arr)


def _attn_bwd(q_arr, k_arr, v_arr, offs, do, lse, delta, seq_blocks, name):
    s = q_arr.shape[0]
    qo, ko, vo = offs
    nb = _att_blocks(seq_blocks)
    n_blocks = s // BLK

    def body(qc_ref, qn_ref, kc_ref, kp_ref, vc_ref, vp_ref, doc_ref, don_ref, lsec_ref, lsen_ref,
             dlc_ref, dln_ref, out_ref):
        n = pl.program_id(0)
        tile_starts_seq = (n * nb) % seq_blocks == 0
        next_in_seq = ((n + 1) * nb) % seq_blocks != 0
        mask_c, mask_p = _band_masks()

        def blk(ref_c, ref_edge, b, c0):
            if b < 0 or b >= nb:
                return ref_edge[:, c0:c0 + HEAD]
            return ref_c[b * BLK:(b + 1) * BLK, c0:c0 + HEAD]

        heads = [h * HEAD for h in range(HPG)]
        own = [(b, c0) for b in range(nb) for c0 in heads]
        cross = [(c, c0) for c in range(nb + 1) for c0 in heads]
        s_o, dp_o, s_x, dp_x = [], [], [], []
        for b, c0 in own:
            s_o.append(jnp.where(mask_c, _dot(blk(qc_ref, None, b, c0), blk(kc_ref, None, b, c0), "nt") * SCALE,
                                 NEG_INF))
            dp_o.append(_dot(blk(doc_ref, None, b, c0), blk(vc_ref, None, b, c0), "nt"))
        for c, c0 in cross:
            sx = jnp.where(mask_p, _dot(blk(qc_ref, qn_ref, c, c0), blk(kc_ref, kp_ref, c - 1, c0), "nt") * SCALE,
                           NEG_INF)
            if c == 0:
                sx = jnp.where(tile_starts_seq, NEG_INF, sx)
            if c == nb:
                sx = jnp.where(next_in_seq, sx, NEG_INF)
            s_x.append(sx)
            dp_x.append(_dot(blk(doc_ref, don_ref, c, c0), blk(vc_ref, vp_ref, c - 1, c0), "nt"))
        p_o, ds_o, p_x, ds_x = [], [], [], []
        for i, (b, c0) in enumerate(own):
            p = jnp.exp(s_o[i] - blk(lsec_ref, None, b, c0))
            ds_o.append((p * (dp_o[i] - blk(dlc_ref, None, b, c0)) * SCALE).astype(BF16))
            p_o.append(p.astype(BF16))
        for i, (c, c0) in enumerate(cross):
            p = jnp.exp(s_x[i] - blk(lsec_ref, lsen_ref, c, c0))
            ds_x.append((p * (dp_x[i] - blk(dlc_ref, dln_ref, c, c0)) * SCALE).astype(BF16))
            p_x.append(p.astype(BF16))
        for i, (b, c0) in enumerate(own):
            xq = b * HPG + i % HPG
            xk = (b + 1) * HPG + i % HPG
            dq = _dot(ds_o[i], blk(kc_ref, None, b, c0)) + _dot(ds_x[xq], blk(kc_ref, kp_ref, b - 1, c0))
            dk = (_dot(ds_o[i], blk(qc_ref, None, b, c0), "tn")
                  + _dot(ds_x[xk], blk(qc_ref, qn_ref, b + 1, c0), "tn"))
            dv = (_dot(p_o[i], blk(doc_ref, None, b, c0), "tn")
                  + _dot(p_x[xk], blk(doc_ref, don_ref, b + 1, c0), "tn"))
            r0 = b * BLK
            out_ref[r0:r0 + BLK, c0:c0 + HEAD] = dq.astype(BF16)
            out_ref[r0:r0 + BLK, GROUP_W + c0:GROUP_W + c0 + HEAD] = dk.astype(BF16)
            out_ref[r0:r0 + BLK, 2 * GROUP_W + c0:2 * GROUP_W + c0 + HEAD] = dv.astype(BF16)

    def cur(off):
        return pl.BlockSpec((nb * BLK, GROUP_W), lambda n: (n, off))

    def prev(off):
        return pl.BlockSpec((BLK, GROUP_W), lambda n: (jnp.maximum(n * nb - 1, 0), off))

    def nxt(off):
        return pl.BlockSpec((BLK, GROUP_W), lambda n: (jnp.minimum((n + 1) * nb, n_blocks - 1), off))

    return pl.pallas_call(
        body, grid=(s // (nb * BLK),),
        in_specs=[cur(qo), nxt(qo), cur(ko), prev(ko), cur(vo), prev(vo), cur(0), nxt(0), cur(0), nxt(0),
                  cur(0), nxt(0)],
        out_specs=pl.BlockSpec((nb * BLK, 3 * GROUP_W), lambda n: (n, 0)),
        out_shape=jax.ShapeDtypeStruct((s, 3 * GROUP_W), BF16),
        compiler_params=_params(), name=name,
    )(q_arr, q_arr, k_arr, k_arr, v_arr, v_arr, do, do, lse, lse, delta, delta)


def _merge_weights(lse_refs, c0):
    ls = [r[:, c0:c0 + HEAD] for r in lse_refs]
    m = jnp.maximum(jnp.maximum(ls[0], ls[1]), ls[2])
    es = [jnp.exp(l - m) for l in ls]
    inv = 1.0 / (es[0] + es[1] + es[2])
    return [e * inv for e in es]


def _merge_fwd(os_, lses):
    s = os_[0].shape[0]
    tm = _row_tile(s, 512)

    def body(o0, o1, o2, l0, l1, l2, out_ref):
        for h in range(HPG):
            c0 = h * HEAD
            w = _merge_weights((l0, l1, l2), c0)
            acc = None
            for wg, o in zip(w, (o0, o1, o2)):
                t = wg * o[:, c0:c0 + HEAD].astype(F32)
                acc = t if acc is None else acc + t
            out_ref[:, c0:c0 + HEAD] = acc.astype(BF16)

    spec = pl.BlockSpec((tm, GROUP_W), lambda i: (i, 0))
    return pl.pallas_call(
        body, grid=(s // tm,), in_specs=[spec] * 6, out_specs=spec,
        out_shape=jax.ShapeDtypeStruct((s, 2 * GROUP_W), BF16),
        compiler_params=_params(), name="merge_fwd",
    )(*os_, *lses)


def _merge_bwd(dcat, os_, lses):
    s = os_[0].shape[0]
    tm = _row_tile(s, 512)

    def body(d_ref, o0, o1, o2, l0, l1, l2, do0, do1, do2, dl0, dl1, dl2):
        for h in range(HPG):
            c0 = h * HEAD
            w = _merge_weights((l0, l1, l2), c0)
            dm = d_ref[:, c0:c0 + HEAD].astype(F32)
            merged = None
            for wg, o in zip(w, (o0, o1, o2)):
                t = wg * o[:, c0:c0 + HEAD].astype(F32)
                merged = t if merged is None else merged + t
            abar = jnp.sum(dm * merged, axis=-1, keepdims=True)
            for wg, do_ref, dl_ref in zip(w, (do0, do1, do2), (dl0, dl1, dl2)):
                do_ref[:, c0:c0 + HEAD] = (wg * dm).astype(BF16)
                dl_ref[:, c0:c0 + HEAD] = wg * abar

    spec = pl.BlockSpec((tm, GROUP_W), lambda i: (i, 0))
    return pl.pallas_call(
        body, grid=(s // tm,), in_specs=[spec] * 7, out_specs=[spec] * 6,
        out_shape=[jax.ShapeDtypeStruct((s, GROUP_W), BF16)] * 3 + [jax.ShapeDtypeStruct((s, GROUP_W), F32)] * 3,
        compiler_params=_params(), name="merge_bwd",
    )(dcat, *os_, *lses)


def _assemble_dproj(dqkv, dqm, tabs):
    s = dqm.shape[0]
    tm = _row_tile(s, 256)
    width = 3 * N_GROUPS * GROUP_W + GROUP_W

    def body(d0, d1, d2, dm_ref, c_ref, s1_ref, s2_ref, out_ref):
        c, s1, s2 = c_ref[...], s1_ref[...], s2_ref[...]
        for g, d_ref in enumerate((d0, d1, d2)):
            for part in range(3):
                for h in range(HPG):
                    src = part * GROUP_W + h * HEAD
                    dst = part * N_GROUPS * GROUP_W + g * GROUP_W + h * HEAD
                    seg = d_ref[:, src:src + HEAD]
                    if part < 2:
                        t = seg.astype(F32)
                        t = t * c - pltpu.roll(t, HEAD - ROT_HALF, 1) * s2 - pltpu.roll(t, ROT_HALF, 1) * s1
                        seg = t.astype(BF16)
                    out_ref[:, dst:dst + HEAD] = seg
        out_ref[:, 3 * N_GROUPS * GROUP_W:] = dm_ref[...]

    g_spec = pl.BlockSpec((tm, 3 * GROUP_W), lambda i: (i, 0))
    m_spec = pl.BlockSpec((tm, GROUP_W), lambda i: (i, 0))
    t_spec = pl.BlockSpec((tm, HEAD), lambda i: (i, 0))
    return pl.pallas_call(
        body, grid=(s // tm,), in_specs=[g_spec] * 3 + [m_spec] + [t_spec] * 3,
        out_specs=pl.BlockSpec((tm, width), lambda i: (i, 0)),
        out_shape=jax.ShapeDtypeStruct((s, width), BF16),
        compiler_params=_params(), name="assemble_dproj",
    )(*dqkv, dqm, *tabs)


def _mem_softmax(q, k):
    s = _dot(q, k, "nt") * SCALE
    m = jnp.max(s, axis=-1, keepdims=True)
    p = jnp.exp(s - m)
    return p * (1.0 / jnp.sum(p, axis=-1, keepdims=True))


def _memattn_fwd(q_arr, q_off, kv, into, into_off, name):
    s = q_arr.shape[0]
    mlen = kv.shape[0]
    tq = _row_tile(s, 512)

    def body(q_ref, kv_ref, into_ref, o_ref):
        for h in range(HPG):
            c0 = h * HEAD
            p = _mem_softmax(q_ref[:, c0:c0 + HEAD], kv_ref[:, c0:c0 + HEAD])
            o = _dot(p.astype(BF16), kv_ref[:, GROUP_W + c0:GROUP_W + c0 + HEAD])
            o_ref[:, c0:c0 + HEAD] = o.astype(BF16)

    return pl.pallas_call(
        body, grid=(s // tq,),
        in_specs=[pl.BlockSpec((tq, GROUP_W), lambda i: (i, q_off)),
                  pl.BlockSpec((mlen, 2 * GROUP_W), lambda i: (0, 0)),
                  pl.BlockSpec(memory_space=pl.ANY)],
        out_specs=pl.BlockSpec((tq, GROUP_W), lambda i: (i, into_off)),
        out_shape=jax.ShapeDtypeStruct(into.shape, into.dtype),
        input_output_aliases={2: 0},
        compiler_params=_params(), name=name,
    )(q_arr, kv, into)


def _memattn_bwd(q_arr, q_off, kv, dcat, d_off, name, into=None, into_off=0):
    s = q_arr.shape[0]
    mlen = kv.shape[0]
    tq = _row_tile(s, 512)

    def body(*refs):
        q_ref, kv_ref, d_ref = refs[:3]
        dq_ref, dkv_ref = refs[-2:]
        i = pl.program_id(0)

        @pl.when(i == 0)
        def _():
            dkv_ref[...] = jnp.zeros_like(dkv_ref)

        for h in range(HPG):
            c0 = h * HEAD
            q = q_ref[:, c0:c0 + HEAD]
            k = kv_ref[:, c0:c0 + HEAD]
            v = kv_ref[:, GROUP_W + c0:GROUP_W + c0 + HEAD]
            do = d_ref[:, c0:c0 + HEAD]
            p = _mem_softmax(q, k)
            dp = _dot(do, v, "nt")
            ds = p * (dp - jnp.sum(p * dp, axis=-1, keepdims=True)) * SCALE
            dsb = ds.astype(BF16)
            dq_ref[:, c0:c0 + HEAD] = _dot(dsb, k).astype(BF16)
            dkv_ref[:, c0:c0 + HEAD] += _dot(dsb, q, "tn")
            dkv_ref[:, GROUP_W + c0:GROUP_W + c0 + HEAD] += _dot(p.astype(BF16), do, "tn")

    in_specs = [pl.BlockSpec((tq, GROUP_W), lambda i: (i, q_off)),
                pl.BlockSpec((mlen, 2 * GROUP_W), lambda i: (0, 0)),
                pl.BlockSpec((tq, GROUP_W), lambda i: (i, d_off))]
    args = [q_arr, kv, dcat]
    dq_shape = jax.ShapeDtypeStruct((s, GROUP_W), BF16)
    aliases = {}
    if into is not None:
        in_specs.append(pl.BlockSpec(memory_space=pl.ANY))
        args.append(into)
        dq_shape = jax.ShapeDtypeStruct(into.shape, into.dtype)
        aliases = {3: 0}
    return pl.pallas_call(
        body, grid=(s // tq,), in_specs=in_specs,
        out_specs=[pl.BlockSpec((tq, GROUP_W), lambda i: (i, into_off)),
                   pl.BlockSpec((mlen, 2 * GROUP_W), lambda i: (0, 0))],
        out_shape=[dq_shape, jax.ShapeDtypeStruct((mlen, 2 * GROUP_W), F32)],
        input_output_aliases=aliases, compiler_params=_params(), name=name,
    )(*args)


SGU_TILE = 256


def _sgu_norm(vg):
    mu = jnp.mean(vg, axis=-1, keepdims=True)
    xc = vg - mu
    var = jnp.mean(xc * xc, axis=-1, keepdims=True)
    rstd = lax.rsqrt(var + LN_EPS)
    return xc * rstd, rstd


def _tril_mask():
    r = lax.broadcasted_iota(jnp.int32, (BLK, BLK), 0)
    c = lax.broadcasted_iota(jnp.int32, (BLK, BLK), 1)
    return r >= c


def _sgu_fwd(proj, ln_g, ln_b, w_s, b_st):
    s = proj.shape[0]
    ts = _row_tile(s, SGU_TILE)

    def body(u_ref, v_ref, g_ref, b_ref, ws_ref, bst_ref, o_ref):
        ug = _gelu(u_ref[...].astype(F32))
        xhat, _ = _sgu_norm(_gelu(v_ref[...].astype(F32)))
        vn = (xhat * g_ref[...] + b_ref[...]).astype(BF16)
        tri = _tril_mask()
        for g in range(SGU_GROUPS):
            c0 = g * HEAD
            w = jnp.where(tri, ws_ref[g], 0.0).astype(BF16)
            bias = bst_ref[:, g:g + 1]
            for ch in range(ts // BLK):
                r0 = ch * BLK
                mixed = _dot(w, vn[r0:r0 + BLK, c0:c0 + HEAD]) + bias
                o_ref[r0:r0 + BLK, c0:c0 + HEAD] = (ug[r0:r0 + BLK, c0:c0 + HEAD] * mixed).astype(BF16)

    vec = pl.BlockSpec((1, SGU_W), lambda i: (0, 0))
    return pl.pallas_call(
        body, grid=(s // ts,),
        in_specs=[pl.BlockSpec((ts, SGU_W), lambda i: (i, 0)), pl.BlockSpec((ts, SGU_W), lambda i: (i, 1)),
                  vec, vec, pl.BlockSpec((SGU_GROUPS, BLK, BLK), lambda i: (0, 0, 0)),
                  pl.BlockSpec((BLK, SGU_GROUPS), lambda i: (0, 0))],
        out_specs=pl.BlockSpec((ts, SGU_W), lambda i: (i, 0)),
        out_shape=jax.ShapeDtypeStruct((s, SGU_W + GROUP_W), BF16),
        compiler_params=_params(), name="sgu_fwd",
    )(proj, proj, ln_g, ln_b, w_s, b_st)


def _sgu_bwd(proj, dcat, ln_g, ln_b, w_s, b_st):
    s = proj.shape[0]
    ts = _row_tile(s, SGU_TILE)

    def body(u_ref, v_ref, d_ref, g_ref, b_ref, ws_ref, bst_ref,
             duv_ref, dws_ref, dbs_ref, dg_ref, db_ref, dvn_ref):
        i = pl.program_id(0)

        @pl.when(i == 0)
        def _():
            dws_ref[...] = jnp.zeros_like(dws_ref)
            dbs_ref[...] = jnp.zeros_like(dbs_ref)
            dg_ref[...] = jnp.zeros_like(dg_ref)
            db_ref[...] = jnp.zeros_like(db_ref)

        u = u_ref[...].astype(F32)
        v = v_ref[...].astype(F32)
        ug, dug = _gelu_and_grad(u)
        vg, dvg_dv = _gelu_and_grad(v)
        xhat, rstd = _sgu_norm(vg)
        lng = g_ref[...]
        vn = (xhat * lng + b_ref[...]).astype(BF16)
        dout = d_ref[...].astype(F32)
        tri = _tril_mask()
        lane = lax.broadcasted_iota(jnp.int32, (BLK, BLK), 1)
        dbs = jnp.zeros((BLK, BLK), F32)
        for g in range(SGU_GROUPS):
            c0 = g * HEAD
            w = jnp.where(tri, ws_ref[g], 0.0).astype(BF16)
            bias = bst_ref[:, g:g + 1]
            dws = jnp.zeros((BLK, BLK), F32)
            for ch in range(ts // BLK):
                r0 = ch * BLK
                vn_gc = vn[r0:r0 + BLK, c0:c0 + HEAD]
                mixed = _dot(w, vn_gc) + bias
                do_gc = dout[r0:r0 + BLK, c0:c0 + HEAD]
                dmixed = do_gc * ug[r0:r0 + BLK, c0:c0 + HEAD]
                du = do_gc * mixed * dug[r0:r0 + BLK, c0:c0 + HEAD]
                duv_ref[r0:r0 + BLK, c0:c0 + HEAD] = du.astype(BF16)
                dmb = dmixed.astype(BF16)
                dws = dws + _dot(dmb, vn_gc, "nt")
                dbs = dbs + jnp.where(lane == g, jnp.sum(dmixed, axis=-1, keepdims=True), 0.0)
                dvn_ref[r0:r0 + BLK, c0:c0 + HEAD] = _dot(w, dmb, "tn")
            dws_ref[g] += jnp.where(tri, dws, 0.0)
        dbs_ref[...] += dbs
        dvn = dvn_ref[...]
        gd = dvn * lng
        c1 = jnp.mean(gd, axis=-1, keepdims=True)
        c2 = jnp.mean(gd * xhat, axis=-1, keepdims=True)
        dvg = rstd * (gd - c1 - xhat * c2)
        duv_ref[:, SGU_W:] = (dvg * dvg_dv).astype(BF16)
        dg_ref[...] += jnp.sum(dvn * xhat, axis=0, keepdims=True)
        db_ref[...] += jnp.sum(dvn, axis=0, keepdims=True)

    vec = pl.BlockSpec((1, SGU_W), lambda i: (0, 0))
    ws_spec = pl.BlockSpec((SGU_GROUPS, BLK, BLK), lambda i: (0, 0, 0))
    return pl.pallas_call(
        body, grid=(s // ts,),
        in_specs=[pl.BlockSpec((ts, SGU_W), lambda i: (i, 0)), pl.BlockSpec((ts, SGU_W), lambda i: (i, 1)),
                  pl.BlockSpec((ts, SGU_W), lambda i: (i, 0)),
                  vec, vec, ws_spec, pl.BlockSpec((BLK, SGU_GROUPS), lambda i: (0, 0))],
        out_specs=[pl.BlockSpec((ts, 2 * SGU_W), lambda i: (i, 0)), ws_spec,
                   pl.BlockSpec((BLK, BLK), lambda i: (0, 0)), vec, vec],
        out_shape=[jax.ShapeDtypeStruct((s, 2 * SGU_W + GROUP_W), BF16),
                   jax.ShapeDtypeStruct((SGU_GROUPS, BLK, BLK), F32),
                   jax.ShapeDtypeStruct((BLK, BLK), F32),
                   jax.ShapeDtypeStruct((1, SGU_W), F32), jax.ShapeDtypeStruct((1, SGU_W), F32)],
        scratch_shapes=[pltpu.VMEM((ts, SGU_W), F32)],
        compiler_params=_params(), name="sgu_bwd",
    )(proj, proj, dcat, ln_g, ln_b, w_s, b_st)


def _swiglu_fwd(h, wg_t, wu_t, name, side=None):
    s, d = h.shape
    f = wg_t.shape[0]
    tm, tn = _row_tile(s, 1024), _row_tile(f, 512)

    def epilogue(parts, ex, out):
        g, u = parts
        sg = _sigmoid(g)
        silu = g * sg
        out[0][...] = silu.astype(BF16)
        out[1][...] = (u * (sg + silu * (1.0 - sg))).astype(BF16)
        out[2][...] = (silu * u).astype(BF16)

    a_spec = pl.BlockSpec((tm, d), lambda n, m, k: (m, 0))
    w_spec = pl.BlockSpec((tn, d), lambda n, m, k: (n, 0))
    o_spec = pl.BlockSpec((tm, tn), lambda n, m, k: (m, n))
    o_shape = jax.ShapeDtypeStruct((s, f), BF16)
    return _matmul(name, (f // tn, s // tm, 1),
                   [(h, a_spec, wg_t, w_spec, "nt", 0), (h, a_spec, wu_t, w_spec, "nt", 1)],
                   2, None, [], [(o_shape, o_spec)] * 3, epilogue, side)


def _swiglu_down(hid, wd, res, name, side=None):
    s, f = hid.shape
    d = wd.shape[1]
    tm, tn = _row_tile(s, 1024), _row_tile(d, 512)
    o_spec = pl.BlockSpec((tm, tn), lambda n, m, k: (m, n))
    return _one(_matmul(
        name, (d // tn, s // tm, 1),
        [(hid, pl.BlockSpec((tm, f), lambda n, m, k: (m, 0)),
          wd, pl.BlockSpec((f, tn), lambda n, m, k: (0, n)), "nn", 0)],
        1, None, [(res, o_spec)], [(jax.ShapeDtypeStruct((s, d), F32), o_spec)],
        _residual_epilogue, side), side)


def _swiglu_bwd_hidden(dxb, wd, silu, up_dsilu, name):
    s, f = silu.shape
    d = dxb.shape[1]
    tm, tn = _row_tile(s, 1024), _row_tile(f, 512)

    def epilogue(parts, ex, out):
        dh = parts[0]
        out[0][...] = (dh * ex[1][...].astype(F32)).astype(BF16)
        out[1][...] = (dh * ex[0][...].astype(F32)).astype(BF16)

    blk = pl.BlockSpec((tm, tn), lambda m, n, k: (m, n))
    o_shape = jax.ShapeDtypeStruct((s, f), BF16)
    return _matmul(
        name, (s // tm, f // tn, 1),
        [(dxb, pl.BlockSpec((tm, d), lambda m, n, k: (m, 0)),
          wd, pl.BlockSpec((tn, d), lambda m, n, k: (n, 0)), "nt", 0)],
        1, None, [(silu, blk), (up_dsilu, blk)], [(o_shape, blk)] * 2, epilogue)


def _swiglu_bwd_input(dgate, dup, wg_t, wu_t, name, side=None):
    s, f = dgate.shape
    d = wg_t.shape[1]
    tm, tn, tk = _row_tile(s, 1024), _row_tile(d, 512), f // 2
    a_spec = pl.BlockSpec((tm, tk), lambda n, m, k: (m, k))
    w_spec = pl.BlockSpec((tk, tn), lambda n, m, k: (k, n))
    return _one(_matmul(
        name, (d // tn, s // tm, f // tk),
        [(dgate, a_spec, wg_t, w_spec, "nn", 0), (dup, a_spec, wu_t, w_spec, "nn", 0)],
        1, (tm, tn), [],
        [(jax.ShapeDtypeStruct((s, d), BF16), pl.BlockSpec((tm, tn), lambda n, m, k: (m, n)))],
        _store_epilogue, side), side)


def _mm_tn_full(a, b, name, side=None):
    s, m = a.shape
    n = b.shape[1]
    tm, tn = _row_tile(m, 512), _row_tile(n, 512)
    return _one(_matmul(
        name, (m // tm, n // tn, 1),
        [(a, pl.BlockSpec((s, tm), lambda i, j, k: (0, i)),
          b, pl.BlockSpec((s, tn), lambda i, j, k: (0, j)), "tn", 0)],
        1, None, [],
        [(jax.ShapeDtypeStruct((m, n), BF16), pl.BlockSpec((tm, tn), lambda i, j, k: (i, j)))],
        _store_epilogue, side), side)


def _mm_nn(a, b, name, tn, out_dtype=BF16, res=None, side=None):
    m, k = a.shape
    n = b.shape[1]
    tm = _row_tile(m, 1024)
    extras = [] if res is None else [(res, pl.BlockSpec((tm, tn), lambda j, i, kk: (i, j)))]
    return _one(_matmul(
        name, (n // tn, m // tm, 1),
        [(a, pl.BlockSpec((tm, k), lambda j, i, kk: (i, 0)),
          b, pl.BlockSpec((k, tn), lambda j, i, kk: (0, j)), "nn", 0)],
        1, None, extras,
        [(jax.ShapeDtypeStruct((m, n), out_dtype), pl.BlockSpec((tm, tn), lambda j, i, kk: (i, j)))],
        _store_epilogue if res is None else _residual_epilogue, side), side)


def _mm_nn_colblocked(a, b_blk, name, res, side=None):
    m, k = a.shape
    nb, _, bw = b_blk.shape
    tm = _row_tile(m, 2048)
    o_spec = pl.BlockSpec((tm, bw), lambda j, i, kk: (i, j))
    return _one(_matmul(
        name, (nb, m // tm, 1),
        [(a, pl.BlockSpec((tm, k), lambda j, i, kk: (i, 0)),
          b_blk, pl.BlockSpec((None, k, bw), lambda j, i, kk: (j, 0, 0)), "nn", 0)],
        1, None, [(res, o_spec)],
        [(jax.ShapeDtypeStruct((m, nb * bw), F32), o_spec)], _residual_epilogue, side), side)


def _mm_nt_colblocked(a, b_blk, name, out_dtype, jb, side=None):
    m = a.shape[0]
    nb, n, bw = b_blk.shape
    tm = _row_tile(m, 512)
    return _one(_matmul(
        name, (m // tm, nb // jb),
        [(a, pl.BlockSpec((tm, jb * bw), lambda i, j: (i, j)),
          b_blk, pl.BlockSpec((jb, n, bw), lambda i, j: (j, 0, 0)), "nt", 0)],
        1, (tm, n), [],
        [(jax.ShapeDtypeStruct((m, n), out_dtype), pl.BlockSpec((tm, n), lambda i, j: (i, 0)))],
        _store_epilogue, side), side)


def _mm_nt_rowblocked(a, b, name, tn, out_dtype, side=None):
    m, k = a.shape
    n = b.shape[0]
    tm = _row_tile(m, 1024)
    return _one(_matmul(
        name, (n // tn, m // tm, 1),
        [(a, pl.BlockSpec((tm, k), lambda j, i, kk: (i, 0)),
          b, pl.BlockSpec((tn, k), lambda j, i, kk: (j, 0)), "nt", 0)],
        1, None, [],
        [(jax.ShapeDtypeStruct((m, n), out_dtype), pl.BlockSpec((tm, tn), lambda j, i, kk: (i, j)))],
        _store_epilogue, side), side)


def _mm_tn_colblocked(a, b, name, bw, side=None):
    s, m = a.shape
    nb = b.shape[1] // bw
    tm = _row_tile(m, 512)
    return _one(_matmul(
        name, (nb, m // tm, 1),
        [(a, pl.BlockSpec((s, tm), lambda j, i, k: (0, i)),
          b, pl.BlockSpec((s, bw), lambda j, i, k: (0, j)), "tn", 0)],
        1, None, [],
        [(jax.ShapeDtypeStruct((nb, m, bw), BF16), pl.BlockSpec((None, tm, bw), lambda j, i, k: (j, i, 0)))],
        _store_epilogue, side), side)


def _mm_tn_rowblocked(a, b, name, bh):
    s, n = b.shape
    nb = a.shape[1] // bh
    tn = _row_tile(n, 512)
    return _matmul(
        name, (nb, n // tn, 1),
        [(a, pl.BlockSpec((s, bh), lambda j, i, k: (0, j)),
          b, pl.BlockSpec((s, tn), lambda j, i, k: (0, i)), "tn", 0)],
        1, None, [],
        [(jax.ShapeDtypeStruct((nb, bh, n), BF16), pl.BlockSpec((None, bh, tn), lambda j, i, k: (j, 0, i)))],
        _store_epilogue)[0]


def _as2d(a):
    return a.reshape(-1, a.shape[-1])


def _cast_bf16(w, name):
    w2 = _as2d(w)
    rows, cols = w2.shape
    tr = _row_tile(rows, 256)

    def body(w_ref, o_ref):
        o_ref[...] = w_ref[...].astype(BF16)

    spec = pl.BlockSpec((tr, cols), lambda i: (i, 0))
    out = pl.pallas_call(
        body, grid=(rows // tr,), in_specs=[spec], out_specs=spec,
        out_shape=jax.ShapeDtypeStruct((rows, cols), BF16),
        compiler_params=_params(), name=name,
    )(w2)
    return out.reshape(w.shape)


def _cast_bf16_layer(w, layer, name):
    _, rows, cols = w.shape
    tr = _row_tile(rows, 256)

    def body(w_ref, o_ref):
        o_ref[...] = w_ref[...].astype(BF16)

    return pl.pallas_call(
        body, grid=(rows // tr,),
        in_specs=[pl.BlockSpec((None, tr, cols), lambda i: (layer, i, 0))],
        out_specs=pl.BlockSpec((tr, cols), lambda i: (i, 0)),
        out_shape=jax.ShapeDtypeStruct((rows, cols), BF16),
        compiler_params=_params(), name=name,
    )(w)


def _reduce_adam(recvs, w, m, v, name):
    n_layers, rows, cols = w.shape
    n_slots = recvs[0].shape[0]
    tr = _row_tile(rows, max(16, (128 * 1024 // cols) // 16 * 16))
    nt = rows // tr
    c1 = 1.0 - ADAM_B1 ** ADAM_STEP
    c2 = 1.0 - ADAM_B2 ** ADAM_STEP

    def body(*refs):
        r_refs = refs[:n_layers]
        w_ref, m_ref, v_ref, g_out, d_out, m_out, v_out = refs[n_layers:]
        layer = pl.program_id(0)

        def update(r_ref):
            g = r_ref[0].astype(F32)
            for k in range(1, n_slots):
                g = g + r_ref[k].astype(F32)
            mm = ADAM_B1 * m_ref[...] + (1.0 - ADAM_B1) * g
            vv = ADAM_B2 * v_ref[...] + (1.0 - ADAM_B2) * (g * g)
            m_hat = mm / c1
            v_hat = vv / c2
            g_out[...] = g
            d_out[...] = -ADAM_LR * (m_hat / (jnp.sqrt(v_hat) + ADAM_EPS) + ADAM_WD * w_ref[...])
            m_out[...] = mm
            v_out[...] = vv

        for li in range(n_layers):
            if n_layers == 1:
                update(r_refs[li])
            else:
                pl.when(layer == li)(functools.partial(update, r_refs[li]))

    def recv_spec(li):
        def imap(layer, i):
            return (0, jnp.where(layer == li, i, jnp.where(layer < li, 0, nt - 1)), 0)
        return pl.BlockSpec((n_slots, tr, cols), imap)

    spec = pl.BlockSpec((None, tr, cols), lambda layer, i: (layer, i, 0))
    o_shape = jax.ShapeDtypeStruct(w.shape, F32)
    return pl.pallas_call(
        body, grid=(n_layers, nt),
        in_specs=[recv_spec(li) for li in range(n_layers)] + [spec] * 3,
        out_specs=[spec] * 4, out_shape=[o_shape] * 4,
        compiler_params=_params(), name=name,
    )(*recvs, w, m, v)


def _my_place():
    return lax.axis_index("x"), lax.axis_index("y"), lax.axis_index("c")


class _GatherSide:
    def __init__(self, blocks):
        self.ins = list(blocks)
        self.outs = [jax.ShapeDtypeStruct((N_DEV,) + b.shape, b.dtype) for b in blocks]

    def scratch(self):
        n = len(self.ins)
        return [pltpu.SemaphoreType.DMA((7 * n,)), pltpu.SemaphoreType.DMA((7 * n,)),
                pltpu.SemaphoreType.DMA((n,))]

    def phases(self, x_refs, out_refs, send_sems, recv_sems, local_sems):
        n = len(self.ins)
        x, y, c = _my_place()
        me, sibling = (x, y, c), (x, y, 1 - c)
        chips = [(1 - x, y), (x, 1 - y), (1 - x, 1 - y)]

        def slot(t, px, py, pc):
            return out_refs[t].at[4 * px + 2 * py + pc]

        def copy(t, k, blk, to, src=None):
            return pltpu.make_async_remote_copy(
                src_ref=slot(t, *blk) if src is None else src, dst_ref=slot(t, *blk),
                send_sem=send_sems.at[7 * t + k], recv_sem=recv_sems.at[7 * t + k],
                device_id=to, device_id_type=pl.DeviceIdType.MESH)

        def own(t):
            return pltpu.make_async_copy(x_refs[t], slot(t, *me), local_sems.at[t])

        def first(t):
            return [copy(t, 0, me, sibling, src=x_refs[t])] + [
                copy(t, 1 + j, me, (*chip, c), src=x_refs[t]) for j, chip in enumerate(chips)]

        def passed(t):
            return [copy(t, 4 + j, (*chip, c), sibling) for j, chip in enumerate(chips)]

        def start():
            for t in range(n):
                own(t).start()
                for cp in first(t):
                    cp.start()

        def mid():
            for t in range(n):
                fwd = passed(t)
                for j, chip in enumerate(chips):
                    copy(t, 1 + j, (*chip, c), me).wait_recv()
                    fwd[j].start()

        def finish():
            for t in range(n):
                copy(t, 0, sibling, me).wait_recv()
                for j, chip in enumerate(chips):
                    copy(t, 4 + j, (*chip, 1 - c), me).wait_recv()
                for cp in first(t) + passed(t):
                    cp.wait_send()
                own(t).wait()

        return start, mid, finish


class _ExchangeSide:
    def __init__(self, blocked):
        self.ins = list(blocked)
        self.outs = [jax.ShapeDtypeStruct(b.shape, b.dtype) for b in blocked]

    def scratch(self):
        n = len(self.ins)
        return [pltpu.SemaphoreType.DMA((7 * n,)), pltpu.SemaphoreType.DMA((7 * n,)),
                pltpu.SemaphoreType.DMA((n,))]

    def phases(self, srcs, dsts, send_sems, recv_sems, local_sems):
        n = len(self.ins)
        x, y, c = _my_place()
        me = 4 * x + 2 * y + c

        def own(t):
            return pltpu.make_async_copy(srcs[t].at[me], dsts[t].at[me], local_sems.at[t])

        def copies(t, arriving):
            res = []
            for k in range(1, N_DEV):
                px, py, pc = x ^ (k >> 2), y ^ ((k >> 1) & 1), c ^ (k & 1)
                peer = 4 * px + 2 * py + pc
                sem = 7 * t + k - 1
                res.append(pltpu.make_async_remote_copy(
                    src_ref=srcs[t].at[peer], dst_ref=dsts[t].at[peer if arriving else me],
                    send_sem=send_sems.at[sem], recv_sem=recv_sems.at[sem],
                    device_id=(px, py, pc), device_id_type=pl.DeviceIdType.MESH))
            return res

        def start():
            for t in range(n):
                own(t).start()
                for send in copies(t, False):
                    send.start()

        def mid():
            pass

        def finish():
            for t in range(n):
                for arrival in copies(t, True):
                    arrival.wait_recv()
                for send in copies(t, False):
                    send.wait_send()
                own(t).wait()

        return start, mid, finish


def _run_side(side, name):
    n_in, n_out = len(side.ins), len(side.outs)

    def body(*refs):
        start, mid, finish = side.phases(refs[:n_in], refs[n_in:n_in + n_out], *refs[n_in + n_out:])
        start()
        mid()
        finish()

    hbm = pl.BlockSpec(memory_space=pltpu.HBM)
    return pl.pallas_call(
        body, out_shape=list(side.outs), in_specs=[hbm] * n_in, out_specs=[hbm] * n_out,
        scratch_shapes=side.scratch(), name=name,
    )(*side.ins)


def _to_residue_major(a, dilation):
    s, w = a.shape
    return a.reshape(s // dilation, dilation, w).transpose(1, 0, 2).reshape(s, w)


def _from_residue_major(a, dilation):
    s, w = a.shape
    return a.reshape(dilation, s // dilation, w).transpose(1, 0, 2).reshape(s, w)


def _mem_kv(mem, gain, wkv, layer, tag):
    mem_n = _rmsnorm_fwd(mem, gain, "mem_norm_" + tag)
    mlen, d = mem.shape
    nb, _, bh, n = wkv.shape
    kv = _matmul(
        "mem_kv_" + tag, (1, nb),
        [(mem_n, pl.BlockSpec((mlen, bh), lambda i, j: (0, j)),
          wkv, pl.BlockSpec((None, None, bh, n), lambda i, j: (j, layer, 0, 0)), "nn", 0)],
        1, (mlen, n), [],
        [(jax.ShapeDtypeStruct((mlen, n), BF16), pl.BlockSpec((mlen, n), lambda i, j: (0, 0)))],
        _store_epilogue)[0]
    return mem_n, kv


def _mem_kv_bwd(mem, gain, mem_n, wkv, layer, dkv, tag):
    mlen, d = mem.shape
    nb, _, bh, n = wkv.shape
    dkvb = dkv.astype(BF16)
    dw = _mm_tn_rowblocked(mem_n, dkvb, "mem_kv_dw_" + tag, bh)
    dmem_n = _matmul(
        "mem_kv_dx_" + tag, (nb, 1),
        [(dkvb, pl.BlockSpec((mlen, n), lambda j, k: (0, 0)),
          wkv, pl.BlockSpec((None, None, bh, n), lambda j, k: (j, layer, 0, 0)), "nt", 0)],
        1, None, [],
        [(jax.ShapeDtypeStruct((mlen, d), F32), pl.BlockSpec((mlen, bh), lambda j, k: (0, j)))],
        _store_epilogue)[0]
    _, _, dgain = _rmsnorm_bwd(mem, gain, dmem_n, None, "mem_norm_bwd_" + tag)
    return dw, dgain


def _row_blocks(a):
    return a.reshape(N_DEV, -1, a.shape[-1])


def _rows(a):
    return a.reshape(-1, a.shape[-1])


def _ffn_bwd(x, gain, w_gate, w_up, w_down, saved, dx, dxb, tag):
    hf, silu, up_dsilu, hid = saved
    dgate, dup = _swiglu_bwd_hidden(dxb, w_down, silu, up_dsilu, "swiglu_bwd_hidden_" + tag)
    dwd = _mm_tn_full(hid, dxb, "swiglu_bwd_wdown_" + tag)
    dwg, (r_wd,) = _mm_tn_full(dgate, hf, "swiglu_bwd_wgate_" + tag, _ExchangeSide([_row_blocks(dwd)]))
    dwu, (r_wg,) = _mm_tn_full(dup, hf, "swiglu_bwd_wup_" + tag, _ExchangeSide([_row_blocks(dwg)]))
    dhf, (r_wu,) = _swiglu_bwd_input(dgate, dup, w_gate, w_up, "swiglu_bwd_input_" + tag,
                                     _ExchangeSide([_row_blocks(dwu)]))
    dx_new, dxb_new, dgain = _rmsnorm_bwd(x, gain, dhf, dx, "ffn_norm_bwd_" + tag)
    return dx_new, dxb_new, dgain, r_wg, r_wu, r_wd


def _local_step(x, mem, positions, target, w_attn_in, shards, small):
    s, d = x.shape
    tabs = _rotary_tables(positions)
    mix_norm, mem_norm, ffn_norm = small["mix_norm"], small["mem_norm"], small["ffn_norm"]

    h0 = _rmsnorm_fwd(x, mix_norm[0:1], "mix_norm_0")
    proj0, (w_mem_kv, w_attn_out, w_gate0) = _attn_in_proj(
        h0, w_attn_in, tabs, _GatherSide([shards["w_mem_kv"], shards["attn_w_out"], shards["w_gate"][0]]))
    qkv, offs, outs, lses = [], [], [], []
    for g, dil in enumerate(DILATIONS):
        if dil == 1:
            arr, off = proj0, (g, N_GROUPS + g, 2 * N_GROUPS + g)
        else:
            cols = [proj0[:, (p * N_GROUPS + g) * GROUP_W:(p * N_GROUPS + g + 1) * GROUP_W] for p in range(3)]
            arr, off = _to_residue_major(jnp.concatenate(cols, axis=1), dil), (0, 1, 2)
        o, lse = _attn_fwd(arr, arr, arr, off, s // dil // BLK, "attn_fwd_%d" % g)
        qkv.append(arr)
        offs.append(off)
        if dil > 1:
            o, lse = _from_residue_major(o, dil), _from_residue_major(lse, dil)
        outs.append(o)
        lses.append(lse)
    mix0 = _merge_fwd(outs, lses)
    qm_off0 = 3 * N_GROUPS
    mem_n0, kv0 = _mem_kv(mem, mem_norm[0:1], w_mem_kv, 0, "0")
    cat0 = _memattn_fwd(proj0, qm_off0, kv0, mix0, 1, "memattn_fwd_0")
    x1, (w_up0,) = _mm_nn_colblocked(cat0, w_attn_out, "attn_out_proj", x, _GatherSide([shards["w_up"][0]]))
    hf0 = _rmsnorm_fwd(x1, ffn_norm[0:1], "ffn_norm_0")
    w_gate0, w_up0 = _rows(w_gate0), _rows(w_up0)
    (silu0, ud0, hid0), (w_down0, w_sgu_in, w_sgu_out) = _swiglu_fwd(
        hf0, w_gate0, w_up0, "swiglu_fwd_0",
        _GatherSide([shards["w_down"][0], shards["sgu_w_in"], shards["sgu_w_out"]]))
    w_down0 = _rows(w_down0)
    x2, (w_gate1,) = _swiglu_down(hid0, w_down0, x1, "swiglu_down_0", _GatherSide([shards["w_gate"][1]]))
    ffn_saved0 = (hf0, silu0, ud0, hid0)

    h1 = _rmsnorm_fwd(x2, mix_norm[1:2], "mix_norm_1")
    w_sgu_in = _rows(w_sgu_in)
    w_sgu_out = _rows(w_sgu_out)
    proj1, (w_up1,) = _mm_nt_rowblocked(h1, w_sgu_in, "sgu_in_proj", w_sgu_in.shape[0] // 7, BF16,
                                        _GatherSide([shards["w_up"][1]]))
    w_gate1, w_up1 = _rows(w_gate1), _rows(w_up1)
    b_st = small["sgu_b_spatial"].T
    mix1 = _sgu_fwd(proj1, small["sgu_ln_g"], small["sgu_ln_b"], small["sgu_w_spatial"], b_st)
    qm_off1 = 2 * SGU_W // GROUP_W
    mem_n1, kv1 = _mem_kv(mem, mem_norm[1:2], w_mem_kv, 1, "1")
    cat1 = _memattn_fwd(proj1, qm_off1, kv1, mix1, SGU_W // GROUP_W, "memattn_fwd_1")
    x3 = _mm_nn(cat1, w_sgu_out, "sgu_out_proj", d // 2, out_dtype=F32, res=x2)
    hf1 = _rmsnorm_fwd(x3, ffn_norm[1:2], "ffn_norm_1")
    (silu1, ud1, hid1), (w_down1,) = _swiglu_fwd(hf1, w_gate1, w_up1, "swiglu_fwd_1",
                                                 _GatherSide([shards["w_down"][1]]))
    w_down1 = _rows(w_down1)
    x4 = _swiglu_down(hid1, w_down1, x3, "swiglu_down_1")
    ffn_saved1 = (hf1, silu1, ud1, hid1)

    loss, dx, dxb, d_final = _loss_head(x4, small["final_norm"], target)

    recvs, sgrads = {}, {}
    dx, dxb, d_ffn1, r_wg1, r_wu1, r_wd1 = _ffn_bwd(x3, ffn_norm[1:2], w_gate1, w_up1, w_down1, ffn_saved1,
                                                    dx, dxb, "1")
    dcat1 = _mm_nt_rowblocked(dxb, w_sgu_out, "sgu_out_proj_dx", w_sgu_out.shape[0] // 2, BF16)
    dwsout = _mm_tn_rowblocked(cat1, dxb, "sgu_out_proj_dw", w_sgu_out.shape[0] // N_DEV)
    duv, dws, dbs, dlng, dlnb = _sgu_bwd(proj1, dcat1, small["sgu_ln_g"], small["sgu_ln_b"],
                                         small["sgu_w_spatial"], b_st)
    dproj1, dkv1 = _memattn_bwd(proj1, qm_off1, kv1, dcat1, SGU_W // GROUP_W, "memattn_bwd_1",
                                into=duv, into_off=2 * SGU_W // GROUP_W)
    dwkv1, d_memnorm1 = _mem_kv_bwd(mem, mem_norm[1:2], mem_n1, w_mem_kv, 1, dkv1, "1")
    dwsin, (r_wsout, r_wkv1) = _mm_tn_full(dproj1, h1, "sgu_in_proj_dw", _ExchangeSide([dwsout, dwkv1]))
    dh1, (r_wsin,) = _mm_nn(dproj1, w_sgu_in, "sgu_in_proj_dx", d // 4, out_dtype=BF16,
                            side=_ExchangeSide([_row_blocks(dwsin)]))
    dx, dxb, d_mix1 = _rmsnorm_bwd(x2, mix_norm[1:2], dh1, dx, "mix_norm_bwd_1")

    dx, dxb, d_ffn0, r_wg0, r_wu0, r_wd0 = _ffn_bwd(x1, ffn_norm[0:1], w_gate0, w_up0, w_down0, ffn_saved0,
                                                    dx, dxb, "0")
    dcat0 = _mm_nt_colblocked(dxb, w_attn_out, "attn_out_proj_dx", BF16, 4)
    dwout0 = _mm_tn_colblocked(cat0, dxb, "attn_out_proj_dw", w_attn_out.shape[2])
    dos_and_deltas = _merge_bwd(dcat0, outs, lses)
    dqkv = []
    for g, dil in enumerate(DILATIONS):
        do_g, dl_g = dos_and_deltas[g], dos_and_deltas[N_GROUPS + g]
        lse_g = lses[g]
        if dil > 1:
            do_g, dl_g, lse_g = (_to_residue_major(t, dil) for t in (do_g, dl_g, lse_g))
        t = _attn_bwd(qkv[g], qkv[g], qkv[g], offs[g], do_g, lse_g, dl_g, s // dil // BLK, "attn_bwd_%d" % g)
        dqkv.append(_from_residue_major(t, dil) if dil > 1 else t)
    dqm0, dkv0 = _memattn_bwd(proj0, qm_off0, kv0, dcat0, 1, "memattn_bwd_0")
    dwkv0, d_memnorm0 = _mem_kv_bwd(mem, mem_norm[0:1], mem_n0, w_mem_kv, 0, dkv0, "0")
    dproj0 = _assemble_dproj(dqkv, dqm0, tabs)
    dwin0, (r_wout0, r_wkv0) = _mm_tn_full(dproj0, h0, "attn_in_proj_dw", _ExchangeSide([dwout0, dwkv0]))
    dh0, (r_win0,) = _mm_nn(dproj0, w_attn_in, "attn_in_proj_dx", d // 4, out_dtype=BF16,
                            side=_ExchangeSide([_row_blocks(dwin0)]))
    grad_x, _, d_mix0 = _rmsnorm_bwd(x, mix_norm[0:1], dh0, dx, "mix_norm_bwd_0")

    recvs["w_gate"] = [r_wg0, r_wg1]
    recvs["w_up"] = [r_wu0, r_wu1]
    recvs["w_down"] = [r_wd0, r_wd1]
    recvs["w_mem_kv"] = [r_wkv0, r_wkv1]
    recvs["attn_w_in"] = [r_win0]
    recvs["attn_w_out"] = [r_wout0]
    recvs["sgu_w_in"] = [r_wsin]
    recvs["sgu_w_out"] = [r_wsout]
    sgrads["mix_norm"] = jnp.concatenate([d_mix0, d_mix1], axis=0)
    sgrads["mem_norm"] = jnp.concatenate([d_memnorm0, d_memnorm1], axis=0)
    sgrads["ffn_norm"] = jnp.concatenate([d_ffn0, d_ffn1], axis=0)
    sgrads["final_norm"] = d_final
    sgrads["sgu_w_spatial"] = dws
    sgrads["sgu_b_spatial"] = dbs[:, :SGU_GROUPS].T
    sgrads["sgu_ln_g"] = dlng
    sgrads["sgu_ln_b"] = dlnb
    return loss, grad_x, recvs, sgrads


BIG = ("w_mem_kv", "w_gate", "w_up", "w_down", "attn_w_in", "attn_w_out", "sgu_w_in", "sgu_w_out")
TRANSPOSED = ("w_gate", "w_up", "sgu_w_in", "attn_w_in")
SMALL_REPLICATED = ("mix_norm", "mem_norm", "ffn_norm", "final_norm", "sgu_w_spatial", "sgu_b_spatial")
SMALL_SHARDED = ("sgu_ln_g", "sgu_ln_b")
WEIGHT_ORDER = ("mix_norm", "mem_norm", "w_mem_kv", "ffn_norm", "w_gate", "w_up", "w_down", "attn_w_in",
                "attn_w_out", "sgu_w_in", "sgu_ln_g", "sgu_ln_b", "sgu_w_spatial", "sgu_b_spatial",
                "sgu_w_out", "final_norm")
PACK_LANES = 128


def _pack(parts):
    flat = [p.reshape(-1) for p in parts]
    sizes = [f.shape[0] for f in flat]
    total = sum(sizes)
    rows = -(-total // PACK_LANES)
    rows = -(-rows // 8) * 8
    pad = rows * PACK_LANES - total
    packed = jnp.concatenate(flat + [jnp.zeros((pad,), F32)]).reshape(rows, PACK_LANES)
    offs, o = [], 0
    for sz in sizes:
        offs.append(o)
        o += sz
    return packed, offs, sizes


def _unpack(packed, offs, sizes, shapes):
    flat = packed.reshape(-1)
    return [flat[o:o + sz].reshape(shp) for o, sz, shp in zip(offs, sizes, shapes)]


def kernel(x, mem, positions, mix_norm, mem_norm, w_mem_kv, ffn_norm, w_gate, w_up, w_down, attn_w_in, attn_w_out, sgu_w_in, sgu_ln_g, sgu_ln_b, sgu_w_spatial, sgu_b_spatial, sgu_w_out, final_norm, loss_target, m_mix_norm, m_mem_norm, m_w_mem_kv, m_ffn_norm, m_w_gate, m_w_up, m_w_down, m_attn_w_in, m_attn_w_out, m_sgu_w_in, m_sgu_ln_g, m_sgu_ln_b, m_sgu_w_spatial, m_sgu_b_spatial, m_sgu_w_out, m_final_norm, v_mix_norm, v_mem_norm, v_w_mem_kv, v_ffn_norm, v_w_gate, v_w_up, v_w_down, v_attn_w_in, v_attn_w_out, v_sgu_w_in, v_sgu_ln_g, v_sgu_ln_b, v_sgu_w_spatial, v_sgu_b_spatial, v_sgu_w_out, v_final_norm):
    w = dict(mix_norm=mix_norm, mem_norm=mem_norm, w_mem_kv=w_mem_kv, ffn_norm=ffn_norm, w_gate=w_gate,
             w_up=w_up, w_down=w_down, attn_w_in=attn_w_in, attn_w_out=attn_w_out, sgu_w_in=sgu_w_in,
             sgu_ln_g=sgu_ln_g, sgu_ln_b=sgu_ln_b, sgu_w_spatial=sgu_w_spatial, sgu_b_spatial=sgu_b_spatial,
             sgu_w_out=sgu_w_out, final_norm=final_norm)
    mo = dict(mix_norm=m_mix_norm, mem_norm=m_mem_norm, w_mem_kv=m_w_mem_kv, ffn_norm=m_ffn_norm,
              w_gate=m_w_gate, w_up=m_w_up, w_down=m_w_down, attn_w_in=m_attn_w_in, attn_w_out=m_attn_w_out,
              sgu_w_in=m_sgu_w_in, sgu_ln_g=m_sgu_ln_g, sgu_ln_b=m_sgu_ln_b, sgu_w_spatial=m_sgu_w_spatial,
              sgu_b_spatial=m_sgu_b_spatial, sgu_w_out=m_sgu_w_out, final_norm=m_final_norm)
    vo = dict(mix_norm=v_mix_norm, mem_norm=v_mem_norm, w_mem_kv=v_w_mem_kv, ffn_norm=v_ffn_norm,
              w_gate=v_w_gate, w_up=v_w_up, w_down=v_w_down, attn_w_in=v_attn_w_in, attn_w_out=v_attn_w_out,
              sgu_w_in=v_sgu_w_in, sgu_ln_g=v_sgu_ln_g, sgu_ln_b=v_sgu_ln_b, sgu_w_spatial=v_sgu_w_spatial,
              sgu_b_spatial=v_sgu_b_spatial, sgu_w_out=v_sgu_w_out, final_norm=v_final_norm)
    me = 4 * lax.axis_index("x") + 2 * lax.axis_index("y") + lax.axis_index("c")
    d_model = x.shape[-1]

    for n in TRANSPOSED:
        w[n], mo[n], vo[n] = (jnp.swapaxes(t, 1, 2) for t in (w[n], mo[n], vo[n]))

    shards = {
        "w_mem_kv": _cast_bf16(w_mem_kv, "cast_w_mem_kv"),
        "attn_w_out": _cast_bf16(attn_w_out[0], "cast_attn_w_out"),
        "sgu_w_in": _cast_bf16(w["sgu_w_in"][0], "cast_sgu_w_in"),
        "sgu_w_out": _cast_bf16(sgu_w_out[0], "cast_sgu_w_out"),
    }
    for n in ("w_gate", "w_up", "w_down"):
        shards[n] = [_cast_bf16_layer(w[n], layer, "cast_%s_%d" % (n, layer)) for layer in range(w[n].shape[0])]
    ln_pack = jnp.concatenate([sgu_ln_g, sgu_ln_b], axis=0)
    w_attn_in, ln_all = _run_side(
        _GatherSide([_cast_bf16(w["attn_w_in"][0], "cast_attn_w_in"), ln_pack]), "gather_attn_w_in")
    w_attn_in = _rows(w_attn_in)
    ln_full = ln_all.transpose(1, 0, 2).reshape(2, 1, -1)
    small = dict(mix_norm=mix_norm, mem_norm=mem_norm, ffn_norm=ffn_norm, final_norm=final_norm.reshape(1, -1),
                 sgu_w_spatial=sgu_w_spatial[0], sgu_b_spatial=sgu_b_spatial[0],
                 sgu_ln_g=ln_full[0], sgu_ln_b=ln_full[1])

    loss, grad_x, recvs, sgrads = _local_step(x[0], mem[0], positions[0], loss_target[0], w_attn_in, shards,
                                              small)
    loss = lax.psum(loss[0, 0], MESH_AXES)

    out_g, out_d, out_m, out_v = {}, {}, {}, {}
    for n in BIG:
        shard = w[n]
        w3 = shard.reshape(shard.shape[0], -1, shard.shape[-1])
        rs = [r.reshape(N_DEV, -1, shard.shape[-1]) for r in recvs[n]]
        res = _reduce_adam(rs, w3, mo[n].reshape(w3.shape), vo[n].reshape(w3.shape), "adam_" + n)
        res = [r.reshape(shard.shape) for r in res]
        if n in TRANSPOSED:
            res = [jnp.swapaxes(r, 1, 2) for r in res]
        out_g[n], out_d[n], out_m[n], out_v[n] = res

    small_names = SMALL_REPLICATED + SMALL_SHARDED
    packed, offs, sizes = _pack([sgrads[n] for n in small_names])
    all_packs = _run_side(_GatherSide([packed]), "gather_small_grads")[0]
    rep_shapes = [w[n].shape for n in SMALL_REPLICATED]
    w_pack, w_offs, w_sizes = _pack([w[n] for n in SMALL_REPLICATED])
    m_pack, _, _ = _pack([mo[n] for n in SMALL_REPLICATED])
    v_pack, _, _ = _pack([vo[n] for n in SMALL_REPLICATED])
    n_rep_rows = w_pack.shape[0]
    res = _reduce_adam([all_packs[:, :n_rep_rows]], w_pack[None], m_pack[None], v_pack[None], "adam_small")
    for dst, r in zip((out_g, out_d, out_m, out_v), res):
        for n, val in zip(SMALL_REPLICATED, _unpack(r[0], w_offs, w_sizes, rep_shapes)):
            dst[n] = val
    ln_rows0 = offs[len(SMALL_REPLICATED)] // PACK_LANES
    ln_rows = 2 * SGU_W // PACK_LANES
    ln_sum = _reduce_adam([all_packs[:, ln_rows0:ln_rows0 + ln_rows]], jnp.zeros((1, ln_rows, PACK_LANES), F32),
                          jnp.zeros((1, ln_rows, PACK_LANES), F32), jnp.zeros((1, ln_rows, PACK_LANES), F32),
                          "sum_ln_grads")[0]
    ln_grads = ln_sum.reshape(2, N_DEV, -1)
    ln_mine = lax.dynamic_index_in_dim(ln_grads, me, axis=1, keepdims=False)
    w_ln = jnp.concatenate([sgu_ln_g, sgu_ln_b], axis=0)[None]
    m_ln = jnp.concatenate([m_sgu_ln_g, m_sgu_ln_b], axis=0)[None]
    v_ln = jnp.concatenate([v_sgu_ln_g, v_sgu_ln_b], axis=0)[None]
    res = _reduce_adam([ln_mine[None]], w_ln, m_ln, v_ln, "adam_ln")
    for dst, r in zip((out_g, out_d, out_m, out_v), res):
        dst["sgu_ln_g"], dst["sgu_ln_b"] = r[0, 0:1], r[0, 1:2]

    return (loss, grad_x[None], *[out_g[n] for n in WEIGHT_ORDER], *[out_d[n] for n in WEIGHT_ORDER],
            *[out_m[n] for n in WEIGHT_ORDER], *[out_v[n] for n in WEIGHT_ORDER])
```

```python
import functools

import jax
import jax.numpy as jnp
from jax import lax
from jax.experimental import pallas as pl
from jax.experimental.pallas import tpu as pltpu

F32 = jnp.float32
BF16 = jnp.bfloat16

N_DEV = 8
HEAD = 128
HPG = 4
GROUP_W = HPG * HEAD
N_GROUPS = 3
DILATIONS = (1, 4, 16)
BLK = 128
SGU_GROUPS = 12
SGU_W = SGU_GROUPS * HEAD
ROT_HALF = 16
ROPE_THETA = 500000.0
NORM_EPS = 1e-6
LN_EPS = 1e-5
NEG_INF = -1e30
SCALE = HEAD ** -0.5

ADAM_LR = 0.001
ADAM_B1 = 0.9
ADAM_B2 = 0.999
ADAM_EPS = 1e-08
ADAM_WD = 0.01
ADAM_STEP = 10

VMEM_LIMIT_V7X = 56 * 1024 * 1024
MESH_AXES = ("x", "y", "c")

_DN = {
    "nn": (((1,), (0,)), ((), ())),
    "nt": (((1,), (1,)), ((), ())),
    "tn": (((0,), (0,)), ((), ())),
}


def _dot(a, b, kind="nn"):
    return lax.dot_general(a, b, _DN[kind], preferred_element_type=F32)


def _params():
    return pltpu.CompilerParams(vmem_limit_bytes=VMEM_LIMIT_V7X)


def _row_tile(rows, cap):
    if rows <= cap:
        return rows
    t = cap - cap % 16
    while t >= 16:
        if rows % t == 0:
            return t
        t -= 16
    return rows


def _gelu(x):
    c = 0.7978845608028654
    return 0.5 * x * (1.0 + jnp.tanh(c * (x + 0.044715 * x * x * x)))


def _gelu_and_grad(x):
    c = 0.7978845608028654
    x2 = x * x
    t = jnp.tanh(c * x * (1.0 + 0.044715 * x2))
    half = 0.5 * (1.0 + t)
    return x * half, half + 0.5 * x * (1.0 - t * t) * c * (1.0 + 3.0 * 0.044715 * x2)


def _sigmoid(x):
    return 1.0 / (1.0 + jnp.exp(-x))


def _matmul(name, grid, terms, n_acc, acc_shape, extras, outs, epilogue, side=None):
    nk = grid[-1]
    nt, ne, no = len(terms), len(extras), len(outs)
    kinds = [(t[4], t[5]) for t in terms]
    n_scratch_acc = 0 if nk == 1 else n_acc
    ns_in = len(side.ins) if side else 0
    ns_out = len(side.outs) if side else 0
    n_steps = 1
    for g in grid:
        n_steps *= g

    def body(*refs):
        pos = 0
        ab = refs[pos:pos + 2 * nt]
        pos += 2 * nt
        ex = refs[pos:pos + ne]
        pos += ne
        s_in = refs[pos:pos + ns_in]
        pos += ns_in
        out = refs[pos:pos + no]
        pos += no
        s_out = refs[pos:pos + ns_out]
        pos += ns_out
        accs = refs[pos:pos + n_scratch_acc]
        s_sems = refs[pos + n_scratch_acc:]
        if side:
            step = pl.program_id(0)
            for ax in range(1, len(grid)):
                step = step * grid[ax] + pl.program_id(ax)
            start, mid, finish = side.phases(s_in, s_out, *s_sems)
            pl.when(step == 0)(start)
        parts = [None] * n_acc
        for t, (kind, ai) in enumerate(kinds):
            a_ref, b_ref = ab[2 * t], ab[2 * t + 1]
            if len(b_ref.shape) == 2:
                pairs = [(a_ref[...], b_ref[...])]
            elif len(a_ref.shape) == 3:
                pairs = [(a_ref[q], b_ref[q]) for q in range(b_ref.shape[0])]
            else:
                bw = b_ref.shape[2]
                pairs = [(a_ref[:, q * bw:(q + 1) * bw], b_ref[q]) for q in range(b_ref.shape[0])]
            for a, b in pairs:
                p = _dot(a.astype(BF16), b.astype(BF16), kind)
                parts[ai] = p if parts[ai] is None else parts[ai] + p
        if nk == 1:
            epilogue(parts, ex, out)
        else:
            k = pl.program_id(len(grid) - 1)

            @pl.when(k == 0)
            def _():
                for ai in range(n_acc):
                    accs[ai][...] = parts[ai]

            @pl.when(k > 0)
            def _():
                for ai in range(n_acc):
                    accs[ai][...] += parts[ai]

            @pl.when(k == nk - 1)
            def _():
                epilogue([a[...] for a in accs], ex, out)

        if side:
            pl.when(step == (3 * n_steps) // 4)(mid)
            pl.when(step == n_steps - 1)(finish)

    hbm = pl.BlockSpec(memory_space=pltpu.HBM)
    in_specs, args = [], []
    for (a, a_spec, b, b_spec, _, _) in terms:
        in_specs += [a_spec, b_spec]
        args += [a, b]
    for (e, e_spec) in extras:
        in_specs.append(e_spec)
        args.append(e)
    scratch = [pltpu.VMEM(acc_shape, F32) for _ in range(n_scratch_acc)]
    out_specs = [o[1] for o in outs]
    out_shape = [o[0] for o in outs]
    if side:
        in_specs += [hbm] * ns_in
        args += list(side.ins)
        out_specs += [hbm] * ns_out
        out_shape += list(side.outs)
        scratch += side.scratch()
    res = pl.pallas_call(
        body, grid=grid, in_specs=in_specs, out_specs=out_specs, out_shape=out_shape,
        scratch_shapes=scratch, compiler_params=_params(), name=name,
    )(*args)
    return res if side is None else (res[:no], res[no:])


def _carrying(body, n_in, n_out, n_scratch, grid, side):
    if side is None:
        return body
    ns_in, ns_out = len(side.ins), len(side.outs)
    n_steps = 1
    for g in grid:
        n_steps *= g

    def wrapped(*refs):
        pos = n_in + ns_in
        ins, s_in = refs[:n_in], refs[n_in:pos]
        outs, s_out = refs[pos:pos + n_out], refs[pos + n_out:pos + n_out + ns_out]
        pos += n_out + ns_out
        scratch, sems = refs[pos:pos + n_scratch], refs[pos + n_scratch:]
        step = pl.program_id(0)
        for ax in range(1, len(grid)):
            step = step * grid[ax] + pl.program_id(ax)
        start, mid, finish = side.phases(s_in, s_out, *sems)
        pl.when(step == 0)(start)
        body(*ins, *outs, *scratch)
        pl.when(step == (3 * n_steps) // 4)(mid)
        pl.when(step == n_steps - 1)(finish)

    return wrapped


def _side_specs(side):
    if side is None:
        return [], [], [], [], []
    hbm = pl.BlockSpec(memory_space=pltpu.HBM)
    return ([hbm] * len(side.ins), list(side.ins), [hbm] * len(side.outs), list(side.outs), side.scratch())


def _one(res, side):
    return res[0] if side is None else (res[0][0], res[1])


def _store_epilogue(parts, ex, out):
    out[0][...] = parts[0].astype(out[0].dtype)


def _residual_epilogue(parts, ex, out):
    out[0][...] = (parts[0] + ex[0][...]).astype(out[0].dtype)


def _rmsnorm_fwd(x, g, name, side=None):
    rows, d = x.shape
    tm = _row_tile(rows, 512)
    grid = (rows // tm,)

    def body(x_ref, g_ref, o_ref):
        xf = x_ref[...]
        r = lax.rsqrt(jnp.mean(xf * xf, axis=-1, keepdims=True) + NORM_EPS)
        o_ref[...] = (xf * r * g_ref[...]).astype(o_ref.dtype)

    s_in, s_args, s_out, s_shapes, s_scratch = _side_specs(side)
    res = pl.pallas_call(
        _carrying(body, 2, 1, 0, grid, side), grid=grid,
        in_specs=[pl.BlockSpec((tm, d), lambda i: (i, 0)), pl.BlockSpec((1, d), lambda i: (0, 0))] + s_in,
        out_specs=[pl.BlockSpec((tm, d), lambda i: (i, 0))] + s_out,
        out_shape=[jax.ShapeDtypeStruct((rows, d), BF16)] + s_shapes,
        scratch_shapes=s_scratch, compiler_params=_params(), name=name,
    )(x, g, *s_args)
    return res[0] if side is None else (res[0], res[1:])


def _rmsnorm_bwd(x, g, dh, dres, name):
    rows, d = x.shape
    tm = _row_tile(rows, 256)
    has_res = dres is not None

    def body(*refs):
        if has_res:
            x_ref, g_ref, dh_ref, dres_ref, dx_ref, dxb_ref, dg_ref = refs
        else:
            x_ref, g_ref, dh_ref, dx_ref, dxb_ref, dg_ref = refs
        i = pl.program_id(0)
        xf = x_ref[...]
        r = lax.rsqrt(jnp.mean(xf * xf, axis=-1, keepdims=True) + NORM_EPS)
        xhat = xf * r
        dy = dh_ref[...].astype(F32)
        gdy = dy * g_ref[...]
        c = jnp.mean(gdy * xhat, axis=-1, keepdims=True)
        dx = r * (gdy - xhat * c)
        if has_res:
            dx = dx + dres_ref[...]
        dx_ref[...] = dx
        dxb_ref[...] = dx.astype(BF16)

        @pl.when(i == 0)
        def _():
            dg_ref[...] = jnp.zeros_like(dg_ref)

        dg_ref[...] += jnp.sum(dy * xhat, axis=0, keepdims=True)

    row_spec = pl.BlockSpec((tm, d), lambda i: (i, 0))
    vec_spec = pl.BlockSpec((1, d), lambda i: (0, 0))
    in_specs = [row_spec, vec_spec, row_spec] + ([row_spec] if has_res else [])
    args = [x, g, dh] + ([dres] if has_res else [])
    return pl.pallas_call(
        body, grid=(rows // tm,), in_specs=in_specs,
        out_specs=[row_spec, row_spec, vec_spec],
        out_shape=[jax.ShapeDtypeStruct((rows, d), F32), jax.ShapeDtypeStruct((rows, d), BF16),
                   jax.ShapeDtypeStruct((1, d), F32)],
        compiler_params=_params(), name=name,
    )(*args)


def _loss_head(x, g, target):
    rows, d = x.shape
    tm = _row_tile(rows, 256)

    def body(x_ref, g_ref, t_ref, loss_ref, dx_ref, dxb_ref, dg_ref):
        i = pl.program_id(0)
        xf = x_ref[...]
        gv = g_ref[...]
        r = lax.rsqrt(jnp.mean(xf * xf, axis=-1, keepdims=True) + NORM_EPS)
        xhat = xf * r
        err = xhat * gv - t_ref[...]
        row_loss = jnp.mean(err * err, axis=-1, keepdims=True)
        dy = err * (1.0 / d)
        gdy = dy * gv
        c = jnp.mean(gdy * xhat, axis=-1, keepdims=True)
        dx = r * (gdy - xhat * c)
        dx_ref[...] = dx
        dxb_ref[...] = dx.astype(BF16)

        @pl.when(i == 0)
        def _():
            dg_ref[...] = jnp.zeros_like(dg_ref)
            loss_ref[...] = jnp.zeros_like(loss_ref)

        dg_ref[...] += jnp.sum(dy * xhat, axis=0, keepdims=True)
        loss_ref[...] += 0.5 * jnp.sum(row_loss, axis=0, keepdims=True)

    row_spec = pl.BlockSpec((tm, d), lambda i: (i, 0))
    vec_spec = pl.BlockSpec((1, d), lambda i: (0, 0))
    return pl.pallas_call(
        body, grid=(rows // tm,), in_specs=[row_spec, vec_spec, row_spec],
        out_specs=[pl.BlockSpec((1, 1), lambda i: (0, 0)), row_spec, row_spec, vec_spec],
        out_shape=[jax.ShapeDtypeStruct((1, 1), F32), jax.ShapeDtypeStruct((rows, d), F32),
                   jax.ShapeDtypeStruct((rows, d), BF16), jax.ShapeDtypeStruct((1, d), F32)],
        compiler_params=_params(), name="loss_head",
    )(x, g, target)


def _rotary_tables(positions):
    inv_freq = ROPE_THETA ** (-jnp.arange(ROT_HALF, dtype=F32) / ROT_HALF)
    ang = positions.astype(F32)[:, None] * inv_freq
    cos, sin = jnp.cos(ang), jnp.sin(ang)
    s = positions.shape[0]
    z = jnp.zeros((s, HEAD - 2 * ROT_HALF), F32)
    z16 = jnp.zeros((s, ROT_HALF), F32)
    c = jnp.concatenate([cos, cos, jnp.ones_like(z)], axis=1)
    s1 = jnp.concatenate([z16, sin, z], axis=1)
    s2 = jnp.concatenate([-sin, z16, z], axis=1)
    return c, s1, s2


ATTN_PROJ_HEADS_PER_STEP = 10


def _attn_in_proj(h, w_t, tabs, side=None):
    s, d = h.shape
    n = w_t.shape[0]
    tn = ATTN_PROJ_HEADS_PER_STEP * HEAD
    n_rot_heads = 2 * N_GROUPS * HPG
    n_rot_steps = -(-n_rot_heads // ATTN_PROJ_HEADS_PER_STEP)
    tm = _row_tile(s, 1024)

    def epilogue(parts, ex, out):
        j = pl.program_id(0)
        acc = parts[0]

        @pl.when(j < n_rot_steps)
        def _():
            c, s1, s2 = ex[0][...], ex[1][...], ex[2][...]
            for t in range(ATTN_PROJ_HEADS_PER_STEP):
                seg = acc[:, t * HEAD:(t + 1) * HEAD]
                rot = seg * c + pltpu.roll(seg, ROT_HALF, 1) * s1 + pltpu.roll(seg, HEAD - ROT_HALF, 1) * s2
                is_rot = (j * ATTN_PROJ_HEADS_PER_STEP + t) < n_rot_heads
                out[0][:, t * HEAD:(t + 1) * HEAD] = jnp.where(is_rot, rot, seg).astype(BF16)

        @pl.when(j >= n_rot_steps)
        def _():
            out[0][...] = acc.astype(BF16)

    tab_spec = pl.BlockSpec((tm, HEAD), lambda j, m, k: (m, 0))
    return _one(_matmul(
        "attn_in_proj", (n // tn, s // tm, 1),
        [(h, pl.BlockSpec((tm, d), lambda j, m, k: (m, 0)),
          w_t, pl.BlockSpec((tn, d), lambda j, m, k: (j, 0)), "nt", 0)],
        1, None, [(tabs[0], tab_spec), (tabs[1], tab_spec), (tabs[2], tab_spec)],
        [(jax.ShapeDtypeStruct((s, n), BF16), pl.BlockSpec((tm, tn), lambda j, m, k: (m, j)))],
        epilogue, side), side)


ATT_TILE_BLOCKS = 4


def _att_blocks(seq_blocks):
    return min(ATT_TILE_BLOCKS, seq_blocks)


def _band_masks():
    qi = lax.broadcasted_iota(jnp.int32, (BLK, BLK), 0)
    ki = lax.broadcasted_iota(jnp.int32, (BLK, BLK), 1)
    return ki <= qi, ki >= qi


def _attn_fwd(q_arr, k_arr, v_arr, offs, seq_blocks, name, side=None):
    s = q_arr.shape[0]
    qo, ko, vo = offs
    nb = _att_blocks(seq_blocks)

    def body(q_ref, kc_ref, kp_ref, vc_ref, vp_ref, o_ref, lse_ref):
        n = pl.program_id(0)
        tile_starts_seq = (n * nb) % seq_blocks == 0
        mask_c, mask_p = _band_masks()
        pairs = [(b * BLK, h * HEAD) for b in range(nb) for h in range(HPG)]

        def keys_prev(ref_c, ref_p, r0, c0):
            return ref_p[:, c0:c0 + HEAD] if r0 == 0 else ref_c[r0 - BLK:r0, c0:c0 + HEAD]

        s_c, s_p = [], []
        for r0, c0 in pairs:
            q = q_ref[r0:r0 + BLK, c0:c0 + HEAD]
            s_c.append(jnp.where(mask_c, _dot(q, kc_ref[r0:r0 + BLK, c0:c0 + HEAD], "nt") * SCALE, NEG_INF))
            sp = jnp.where(mask_p, _dot(q, keys_prev(kc_ref, kp_ref, r0, c0), "nt") * SCALE, NEG_INF)
            s_p.append(jnp.where(tile_starts_seq, NEG_INF, sp) if r0 == 0 else sp)
        m = [jnp.maximum(jnp.max(a, axis=-1, keepdims=True), jnp.max(b, axis=-1, keepdims=True))
             for a, b in zip(s_c, s_p)]
        p_c = [jnp.exp(a - mm) for a, mm in zip(s_c, m)]
        p_p = [jnp.exp(a - mm) for a, mm in zip(s_p, m)]
        l = [jnp.sum(a, axis=-1, keepdims=True) + jnp.sum(b, axis=-1, keepdims=True) for a, b in zip(p_c, p_p)]
        inv = [1.0 / ll for ll in l]
        for i, (r0, c0) in enumerate(pairs):
            o = (_dot((p_c[i] * inv[i]).astype(BF16), vc_ref[r0:r0 + BLK, c0:c0 + HEAD])
                 + _dot((p_p[i] * inv[i]).astype(BF16), keys_prev(vc_ref, vp_ref, r0, c0)))
            o_ref[r0:r0 + BLK, c0:c0 + HEAD] = o.astype(BF16)
            lse_ref[r0:r0 + BLK, c0:c0 + HEAD] = jnp.broadcast_to(m[i] + jnp.log(l[i]), (BLK, HEAD))

    def cur(off):
        return pl.BlockSpec((nb * BLK, GROUP_W), lambda n: (n, off))

    def prev(off):
        return pl.BlockSpec((BLK, GROUP_W), lambda n: (jnp.maximum(n * nb - 1, 0), off))

    grid = (s // (nb * BLK),)
    s_in, s_args, s_out, s_shapes, s_scratch = _side_specs(side)
    return pl.pallas_call(
        _carrying(body, 5, 2, 0, grid, side), grid=grid,
        in_specs=[cur(qo), cur(ko), prev(ko), cur(vo), prev(vo)] + s_in,
        out_specs=[cur(0), cur(0)] + s_out,
        out_shape=[jax.ShapeDtypeStruct((s, GROUP_W), BF16), jax.ShapeDtypeStruct((s, GROUP_W), F32)] + s_shapes,
        scratch_shapes=s_scratch, compiler_params=_params(), name=name,
    )(q_arr, k_arr, k_arr, v_arr, v_arr, *s_args)


def _attn_bwd(q_arr, k_arr, v_arr, offs, do, lse, delta, seq_blocks, name):
    s = q_arr.shape[0]
    qo, ko, vo = offs
    nb = _att_blocks(seq_blocks)
    n_blocks = s // BLK

    def body(qc_ref, qn_ref, kc_ref, kp_ref, vc_ref, vp_ref, doc_ref, don_ref, lsec_ref, lsen_ref,
             dlc_ref, dln_ref, out_ref):
        n = pl.program_id(0)
        tile_starts_seq = (n * nb) % seq_blocks == 0
        next_in_seq = ((n + 1) * nb) % seq_blocks != 0
        mask_c, mask_p = _band_masks()

        def blk(ref_c, ref_edge, b, c0):
            if b < 0 or b >= nb:
                return ref_edge[:, c0:c0 + HEAD]
            return ref_c[b * BLK:(b + 1) * BLK, c0:c0 + HEAD]

        heads = [h * HEAD for h in range(HPG)]
        own = [(b, c0) for b in range(nb) for c0 in heads]
        cross = [(c, c0) for c in range(nb + 1) for c0 in heads]
        s_o, dp_o, s_x, dp_x = [], [], [], []
        for b, c0 in own:
            s_o.append(jnp.where(mask_c, _dot(blk(qc_ref, None, b, c0), blk(kc_ref, None, b, c0), "nt") * SCALE,
                                 NEG_INF))
            dp_o.append(_dot(blk(doc_ref, None, b, c0), blk(vc_ref, None, b, c0), "nt"))
        for c, c0 in cross:
            sx = jnp.where(mask_p, _dot(blk(qc_ref, qn_ref, c, c0), blk(kc_ref, kp_ref, c - 1, c0), "nt") * SCALE,
                           NEG_INF)
            if c == 0:
                sx = jnp.where(tile_starts_seq, NEG_INF, sx)
            if c == nb:
                sx = jnp.where(next_in_seq, sx, NEG_INF)
            s_x.append(sx)
            dp_x.append(_dot(blk(doc_ref, don_ref, c, c0), blk(vc_ref, vp_ref, c - 1, c0), "nt"))
        p_o, ds_o, p_x, ds_x = [], [], [], []
        for i, (b, c0) in enumerate(own):
            p = jnp.exp(s_o[i] - blk(lsec_ref, None, b, c0))
            ds_o.append((p * (dp_o[i] - blk(dlc_ref, None, b, c0)) * SCALE).astype(BF16))
            p_o.append(p.astype(BF16))
        for i, (c, c0) in enumerate(cross):
            p = jnp.exp(s_x[i] - blk(lsec_ref, lsen_ref, c, c0))
            ds_x.append((p * (dp_x[i] - blk(dlc_ref, dln_ref, c, c0)) * SCALE).astype(BF16))
            p_x.append(p.astype(BF16))
        for i, (b, c0) in enumerate(own):
            xq = b * HPG + i % HPG
            xk = (b + 1) * HPG + i % HPG
            dq = _dot(ds_o[i], blk(kc_ref, None, b, c0)) + _dot(ds_x[xq], blk(kc_ref, kp_ref, b - 1, c0))
            dk = (_dot(ds_o[i], blk(qc_ref, None, b, c0), "tn")
                  + _dot(ds_x[xk], blk(qc_ref, qn_ref, b + 1, c0), "tn"))
            dv = (_dot(p_o[i], blk(doc_ref, None, b, c0), "tn")
                  + _dot(p_x[xk], blk(doc_ref, don_ref, b + 1, c0), "tn"))
            r0 = b * BLK
            out_ref[r0:r0 + BLK, c0:c0 + HEAD] = dq.astype(BF16)
            out_ref[r0:r0 + BLK, GROUP_W + c0:GROUP_W + c0 + HEAD] = dk.astype(BF16)
            out_ref[r0:r0 + BLK, 2 * GROUP_W + c0:2 * GROUP_W + c0 + HEAD] = dv.astype(BF16)

    def cur(off):
        return pl.BlockSpec((nb * BLK, GROUP_W), lambda n: (n, off))

    def prev(off):
        return pl.BlockSpec((BLK, GROUP_W), lambda n: (jnp.maximum(n * nb - 1, 0), off))

    def nxt(off):
        return pl.BlockSpec((BLK, GROUP_W), lambda n: (jnp.minimum((n + 1) * nb, n_blocks - 1), off))

    return pl.pallas_call(
        body, grid=(s // (nb * BLK),),
        in_specs=[cur(qo), nxt(qo), cur(ko), prev(ko), cur(vo), prev(vo), cur(0), nxt(0), cur(0), nxt(0),
                  cur(0), nxt(0)],
        out_specs=pl.BlockSpec((nb * BLK, 3 * GROUP_W), lambda n: (n, 0)),
        out_shape=jax.ShapeDtypeStruct((s, 3 * GROUP_W), BF16),
        compiler_params=_params(), name=name,
    )(q_arr, q_arr, k_arr, k_arr, v_arr, v_arr, do, do, lse, lse, delta, delta)


def _merge_weights(lse_refs, c0):
    ls = [r[:, c0:c0 + HEAD] for r in lse_refs]
    m = jnp.maximum(jnp.maximum(ls[0], ls[1]), ls[2])
    es = [jnp.exp(l - m) for l in ls]
    inv = 1.0 / (es[0] + es[1] + es[2])
    return [e * inv for e in es]


def _merge_fwd(os_, lses):
    s = os_[0].shape[0]
    tm = _row_tile(s, 512)

    def body(o0, o1, o2, l0, l1, l2, out_ref):
        for h in range(HPG):
            c0 = h * HEAD
            w = _merge_weights((l0, l1, l2), c0)
            acc = None
            for wg, o in zip(w, (o0, o1, o2)):
                t = wg * o[:, c0:c0 + HEAD].astype(F32)
                acc = t if acc is None else acc + t
            out_ref[:, c0:c0 + HEAD] = acc.astype(BF16)

    spec = pl.BlockSpec((tm, GROUP_W), lambda i: (i, 0))
    return pl.pallas_call(
        body, grid=(s // tm,), in_specs=[spec] * 6, out_specs=spec,
        out_shape=jax.ShapeDtypeStruct((s, 2 * GROUP_W), BF16),
        compiler_params=_params(), name="merge_fwd",
    )(*os_, *lses)


def _merge_bwd(dcat, os_, lses):
    s = os_[0].shape[0]
    tm = _row_tile(s, 512)

    def body(d_ref, o0, o1, o2, l0, l1, l2, do0, do1, do2, dl0, dl1, dl2):
        for h in range(HPG):
            c0 = h * HEAD
            w = _merge_weights((l0, l1, l2), c0)
            dm = d_ref[:, c0:c0 + HEAD].astype(F32)
            merged = None
            for wg, o in zip(w, (o0, o1, o2)):
                t = wg * o[:, c0:c0 + HEAD].astype(F32)
                merged = t if merged is None else merged + t
            abar = jnp.sum(dm * merged, axis=-1, keepdims=True)
            for wg, do_ref, dl_ref in zip(w, (do0, do1, do2), (dl0, dl1, dl2)):
                do_ref[:, c0:c0 + HEAD] = (wg * dm).astype(BF16)
                dl_ref[:, c0:c0 + HEAD] = wg * abar

    spec = pl.BlockSpec((tm, GROUP_W), lambda i: (i, 0))
    return pl.pallas_call(
        body, grid=(s // tm,), in_specs=[spec] * 7, out_specs=[spec] * 6,
        out_shape=[jax.ShapeDtypeStruct((s, GROUP_W), BF16)] * 3 + [jax.ShapeDtypeStruct((s, GROUP_W), F32)] * 3,
        compiler_params=_params(), name="merge_bwd",
    )(dcat, *os_, *lses)


def _assemble_dproj(dqkv, dqm, tabs):
    s = dqm.shape[0]
    tm = _row_tile(s, 256)
    width = 3 * N_GROUPS * GROUP_W + GROUP_W

    def body(d0, d1, d2, dm_ref, c_ref, s1_ref, s2_ref, out_ref):
        c, s1, s2 = c_ref[...], s1_ref[...], s2_ref[...]
        for g, d_ref in enumerate((d0, d1, d2)):
            for part in range(3):
                for h in range(HPG):
                    src = part * GROUP_W + h * HEAD
                    dst = part * N_GROUPS * GROUP_W + g * GROUP_W + h * HEAD
                    seg = d_ref[:, src:src + HEAD]
                    if part < 2:
                        t = seg.astype(F32)
                        t = t * c - pltpu.roll(t, HEAD - ROT_HALF, 1) * s2 - pltpu.roll(t, ROT_HALF, 1) * s1
                        seg = t.astype(BF16)
                    out_ref[:, dst:dst + HEAD] = seg
        out_ref[:, 3 * N_GROUPS * GROUP_W:] = dm_ref[...]

    g_spec = pl.BlockSpec((tm, 3 * GROUP_W), lambda i: (i, 0))
    m_spec = pl.BlockSpec((tm, GROUP_W), lambda i: (i, 0))
    t_spec = pl.BlockSpec((tm, HEAD), lambda i: (i, 0))
    return pl.pallas_call(
        body, grid=(s // tm,), in_specs=[g_spec] * 3 + [m_spec] + [t_spec] * 3,
        out_specs=pl.BlockSpec((tm, width), lambda i: (i, 0)),
        out_shape=jax.ShapeDtypeStruct((s, width), BF16),
        compiler_params=_params(), name="assemble_dproj",
    )(*dqkv, dqm, *tabs)


def _mem_softmax(q, k):
    s = _dot(q, k, "nt") * SCALE
    m = jnp.max(s, axis=-1, keepdims=True)
    p = jnp.exp(s - m)
    return p * (1.0 / jnp.sum(p, axis=-1, keepdims=True))


def _memattn_fwd(q_arr, q_off, kv, into, into_off, name):
    s = q_arr.shape[0]
    mlen = kv.shape[0]
    tq = _row_tile(s, 512)

    def body(q_ref, kv_ref, into_ref, o_ref):
        for h in range(HPG):
            c0 = h * HEAD
            p = _mem_softmax(q_ref[:, c0:c0 + HEAD], kv_ref[:, c0:c0 + HEAD])
            o = _dot(p.astype(BF16), kv_ref[:, GROUP_W + c0:GROUP_W + c0 + HEAD])
            o_ref[:, c0:c0 + HEAD] = o.astype(BF16)

    return pl.pallas_call(
        body, grid=(s // tq,),
        in_specs=[pl.BlockSpec((tq, GROUP_W), lambda i: (i, q_off)),
                  pl.BlockSpec((mlen, 2 * GROUP_W), lambda i: (0, 0)),
                  pl.BlockSpec(memory_space=pl.ANY)],
        out_specs=pl.BlockSpec((tq, GROUP_W), lambda i: (i, into_off)),
        out_shape=jax.ShapeDtypeStruct(into.shape, into.dtype),
        input_output_aliases={2: 0},
        compiler_params=_params(), name=name,
    )(q_arr, kv, into)


def _memattn_bwd(q_arr, q_off, kv, dcat, d_off, name, into=None, into_off=0):
    s = q_arr.shape[0]
    mlen = kv.shape[0]
    tq = _row_tile(s, 512)

    def body(*refs):
        q_ref, kv_ref, d_ref = refs[:3]
        dq_ref, dkv_ref = refs[-2:]
        i = pl.program_id(0)

        @pl.when(i == 0)
        def _():
            dkv_ref[...] = jnp.zeros_like(dkv_ref)

        for h in range(HPG):
            c0 = h * HEAD
            q = q_ref[:, c0:c0 + HEAD]
            k = kv_ref[:, c0:c0 + HEAD]
            v = kv_ref[:, GROUP_W + c0:GROUP_W + c0 + HEAD]
            do = d_ref[:, c0:c0 + HEAD]
            p = _mem_softmax(q, k)
            dp = _dot(do, v, "nt")
            ds = p * (dp - jnp.sum(p * dp, axis=-1, keepdims=True)) * SCALE
            dsb = ds.astype(BF16)
            dq_ref[:, c0:c0 + HEAD] = _dot(dsb, k).astype(BF16)
            dkv_ref[:, c0:c0 + HEAD] += _dot(dsb, q, "tn")
            dkv_ref[:, GROUP_W + c0:GROUP_W + c0 + HEAD] += _dot(p.astype(BF16), do, "tn")

    in_specs = [pl.BlockSpec((tq, GROUP_W), lambda i: (i, q_off)),
                pl.BlockSpec((mlen, 2 * GROUP_W), lambda i: (0, 0)),
                pl.BlockSpec((tq, GROUP_W), lambda i: (i, d_off))]
    args = [q_arr, kv, dcat]
    dq_shape = jax.ShapeDtypeStruct((s, GROUP_W), BF16)
    aliases = {}
    if into is not None:
        in_specs.append(pl.BlockSpec(memory_space=pl.ANY))
        args.append(into)
        dq_shape = jax.ShapeDtypeStruct(into.shape, into.dtype)
        aliases = {3: 0}
    return pl.pallas_call(
        body, grid=(s // tq,), in_specs=in_specs,
        out_specs=[pl.BlockSpec((tq, GROUP_W), lambda i: (i, into_off)),
                   pl.BlockSpec((mlen, 2 * GROUP_W), lambda i: (0, 0))],
        out_shape=[dq_shape, jax.ShapeDtypeStruct((mlen, 2 * GROUP_W), F32)],
        input_output_aliases=aliases, compiler_params=_params(), name=name,
    )(*args)


SGU_TILE = 256


def _sgu_norm(vg):
    mu = jnp.mean(vg, axis=-1, keepdims=True)
    xc = vg - mu
    var = jnp.mean(xc * xc, axis=-1, keepdims=True)
    rstd = lax.rsqrt(var + LN_EPS)
    return xc * rstd, rstd


def _tril_mask():
    r = lax.broadcasted_iota(jnp.int32, (BLK, BLK), 0)
    c = lax.broadcasted_iota(jnp.int32, (BLK, BLK), 1)
    return r >= c


def _sgu_fwd(proj, ln_g, ln_b, w_s, b_st):
    s = proj.shape[0]
    ts = _row_tile(s, SGU_TILE)

    def body(u_ref, v_ref, g_ref, b_ref, ws_ref, bst_ref, o_ref):
        ug = _gelu(u_ref[...].astype(F32))
        xhat, _ = _sgu_norm(_gelu(v_ref[...].astype(F32)))
        vn = (xhat * g_ref[...] + b_ref[...]).astype(BF16)
        tri = _tril_mask()
        for g in range(SGU_GROUPS):
            c0 = g * HEAD
            w = jnp.where(tri, ws_ref[g], 0.0).astype(BF16)
            bias = bst_ref[:, g:g + 1]
            for ch in range(ts // BLK):
                r0 = ch * BLK
                mixed = _dot(w, vn[r0:r0 + BLK, c0:c0 + HEAD]) + bias
                o_ref[r0:r0 + BLK, c0:c0 + HEAD] = (ug[r0:r0 + BLK, c0:c0 + HEAD] * mixed).astype(BF16)

    vec = pl.BlockSpec((1, SGU_W), lambda i: (0, 0))
    return pl.pallas_call(
        body, grid=(s // ts,),
        in_specs=[pl.BlockSpec((ts, SGU_W), lambda i: (i, 0)), pl.BlockSpec((ts, SGU_W), lambda i: (i, 1)),
                  vec, vec, pl.BlockSpec((SGU_GROUPS, BLK, BLK), lambda i: (0, 0, 0)),
                  pl.BlockSpec((BLK, SGU_GROUPS), lambda i: (0, 0))],
        out_specs=pl.BlockSpec((ts, SGU_W), lambda i: (i, 0)),
        out_shape=jax.ShapeDtypeStruct((s, SGU_W + GROUP_W), BF16),
        compiler_params=_params(), name="sgu_fwd",
    )(proj, proj, ln_g, ln_b, w_s, b_st)


def _sgu_bwd(proj, dcat, ln_g, ln_b, w_s, b_st):
    s = proj.shape[0]
    ts = _row_tile(s, SGU_TILE)

    def body(u_ref, v_ref, d_ref, g_ref, b_ref, ws_ref, bst_ref,
             duv_ref, dws_ref, dbs_ref, dg_ref, db_ref, dvn_ref):
        i = pl.program_id(0)

        @pl.when(i == 0)
        def _():
            dws_ref[...] = jnp.zeros_like(dws_ref)
            dbs_ref[...] = jnp.zeros_like(dbs_ref)
            dg_ref[...] = jnp.zeros_like(dg_ref)
            db_ref[...] = jnp.zeros_like(db_ref)

        u = u_ref[...].astype(F32)
        v = v_ref[...].astype(F32)
        ug, dug = _gelu_and_grad(u)
        vg, dvg_dv = _gelu_and_grad(v)
        xhat, rstd = _sgu_norm(vg)
        lng = g_ref[...]
        vn = (xhat * lng + b_ref[...]).astype(BF16)
        dout = d_ref[...].astype(F32)
        tri = _tril_mask()
        lane = lax.broadcasted_iota(jnp.int32, (BLK, BLK), 1)
        dbs = jnp.zeros((BLK, BLK), F32)
        for g in range(SGU_GROUPS):
            c0 = g * HEAD
            w = jnp.where(tri, ws_ref[g], 0.0).astype(BF16)
            bias = bst_ref[:, g:g + 1]
            dws = jnp.zeros((BLK, BLK), F32)
            for ch in range(ts // BLK):
                r0 = ch * BLK
                vn_gc = vn[r0:r0 + BLK, c0:c0 + HEAD]
                mixed = _dot(w, vn_gc) + bias
                do_gc = dout[r0:r0 + BLK, c0:c0 + HEAD]
                dmixed = do_gc * ug[r0:r0 + BLK, c0:c0 + HEAD]
                du = do_gc * mixed * dug[r0:r0 + BLK, c0:c0 + HEAD]
                duv_ref[r0:r0 + BLK, c0:c0 + HEAD] = du.astype(BF16)
                dmb = dmixed.astype(BF16)
                dws = dws + _dot(dmb, vn_gc, "nt")
                dbs = dbs + jnp.where(lane == g, jnp.sum(dmixed, axis=-1, keepdims=True), 0.0)
                dvn_ref[r0:r0 + BLK, c0:c0 + HEAD] = _dot(w, dmb, "tn")
            dws_ref[g] += jnp.where(tri, dws, 0.0)
        dbs_ref[...] += dbs
        dvn = dvn_ref[...]
        gd = dvn * lng
        c1 = jnp.mean(gd, axis=-1, keepdims=True)
        c2 = jnp.mean(gd * xhat, axis=-1, keepdims=True)
        dvg = rstd * (gd - c1 - xhat * c2)
        duv_ref[:, SGU_W:] = (dvg * dvg_dv).astype(BF16)
        dg_ref[...] += jnp.sum(dvn * xhat, axis=0, keepdims=True)
        db_ref[...] += jnp.sum(dvn, axis=0, keepdims=True)

    vec = pl.BlockSpec((1, SGU_W), lambda i: (0, 0))
    ws_spec = pl.BlockSpec((SGU_GROUPS, BLK, BLK), lambda i: (0, 0, 0))
    return pl.pallas_call(
        body, grid=(s // ts,),
        in_specs=[pl.BlockSpec((ts, SGU_W), lambda i: (i, 0)), pl.BlockSpec((ts, SGU_W), lambda i: (i, 1)),
                  pl.BlockSpec((ts, SGU_W), lambda i: (i, 0)),
                  vec, vec, ws_spec, pl.BlockSpec((BLK, SGU_GROUPS), lambda i: (0, 0))],
        out_specs=[pl.BlockSpec((ts, 2 * SGU_W), lambda i: (i, 0)), ws_spec,
                   pl.BlockSpec((BLK, BLK), lambda i: (0, 0)), vec, vec],
        out_shape=[jax.ShapeDtypeStruct((s, 2 * SGU_W + GROUP_W), BF16),
                   jax.ShapeDtypeStruct((SGU_GROUPS, BLK, BLK), F32),
                   jax.ShapeDtypeStruct((BLK, BLK), F32),
                   jax.ShapeDtypeStruct((1, SGU_W), F32), jax.ShapeDtypeStruct((1, SGU_W), F32)],
        scratch_shapes=[pltpu.VMEM((ts, SGU_W), F32)],
        compiler_params=_params(), name="sgu_bwd",
    )(proj, proj, dcat, ln_g, ln_b, w_s, b_st)


def _swiglu_fwd(h, wg_t, wu_t, name, side=None):
    s, d = h.shape
    f = wg_t.shape[0]
    tm, tn = _row_tile(s, 1024), _row_tile(f, 512)

    def epilogue(parts, ex, out):
        g, u = parts
        sg = _sigmoid(g)
        silu = g * sg
        out[0][...] = silu.astype(BF16)
        out[1][...] = (u * (sg + silu * (1.0 - sg))).astype(BF16)
        out[2][...] = (silu * u).astype(BF16)

    a_spec = pl.BlockSpec((tm, d), lambda n, m, k: (m, 0))
    w_spec = pl.BlockSpec((tn, d), lambda n, m, k: (n, 0))
    o_spec = pl.BlockSpec((tm, tn), lambda n, m, k: (m, n))
    o_shape = jax.ShapeDtypeStruct((s, f), BF16)
    return _matmul(name, (f // tn, s // tm, 1),
                   [(h, a_spec, wg_t, w_spec, "nt", 0), (h, a_spec, wu_t, w_spec, "nt", 1)],
                   2, None, [], [(o_shape, o_spec)] * 3, epilogue, side)


def _swiglu_down(hid, wd, res, name, side=None):
    s, f = hid.shape
    d = wd.shape[1]
    tm, tn = _row_tile(s, 1024), _row_tile(d, 512)
    o_spec = pl.BlockSpec((tm, tn), lambda n, m, k: (m, n))
    return _one(_matmul(
        name, (d // tn, s // tm, 1),
        [(hid, pl.BlockSpec((tm, f), lambda n, m, k: (m, 0)),
          wd, pl.BlockSpec((f, tn), lambda n, m, k: (0, n)), "nn", 0)],
        1, None, [(res, o_spec)], [(jax.ShapeDtypeStruct((s, d), F32), o_spec)],
        _residual_epilogue, side), side)


def _swiglu_bwd_hidden(dxb, wd, silu, up_dsilu, name):
    s, f = silu.shape
    d = dxb.shape[1]
    tm, tn = _row_tile(s, 1024), _row_tile(f, 512)

    def epilogue(parts, ex, out):
        dh = parts[0]
        out[0][...] = (dh * ex[1][...].astype(F32)).astype(BF16)
        out[1][...] = (dh * ex[0][...].astype(F32)).astype(BF16)

    blk = pl.BlockSpec((tm, tn), lambda m, n, k: (m, n))
    o_shape = jax.ShapeDtypeStruct((s, f), BF16)
    return _matmul(
        name, (s // tm, f // tn, 1),
        [(dxb, pl.BlockSpec((tm, d), lambda m, n, k: (m, 0)),
          wd, pl.BlockSpec((tn, d), lambda m, n, k: (n, 0)), "nt", 0)],
        1, None, [(silu, blk), (up_dsilu, blk)], [(o_shape, blk)] * 2, epilogue)


def _swiglu_bwd_input(dgate, dup, wg_t, wu_t, name, side=None):
    s, f = dgate.shape
    d = wg_t.shape[1]
    tm, tn, tk = _row_tile(s, 1024), _row_tile(d, 512), f // 2
    a_spec = pl.BlockSpec((tm, tk), lambda n, m, k: (m, k))
    w_spec = pl.BlockSpec((tk, tn), lambda n, m, k: (k, n))
    return _one(_matmul(
        name, (d // tn, s // tm, f // tk),
        [(dgate, a_spec, wg_t, w_spec, "nn", 0), (dup, a_spec, wu_t, w_spec, "nn", 0)],
        1, (tm, tn), [],
        [(jax.ShapeDtypeStruct((s, d), BF16), pl.BlockSpec((tm, tn), lambda n, m, k: (m, n)))],
        _store_epilogue, side), side)


def _mm_tn_full(a, b, name, side=None):
    s, m = a.shape
    n = b.shape[1]
    tm, tn = _row_tile(m, 512), _row_tile(n, 512)
    return _one(_matmul(
        name, (m // tm, n // tn, 1),
        [(a, pl.BlockSpec((s, tm), lambda i, j, k: (0, i)),
          b, pl.BlockSpec((s, tn), lambda i, j, k: (0, j)), "tn", 0)],
        1, None, [],
        [(jax.ShapeDtypeStruct((m, n), BF16), pl.BlockSpec((tm, tn), lambda i, j, k: (i, j)))],
        _store_epilogue, side), side)


def _mm_nn(a, b, name, tn, out_dtype=BF16, res=None, side=None):
    m, k = a.shape
    n = b.shape[1]
    tm = _row_tile(m, 1024)
    extras = [] if res is None else [(res, pl.BlockSpec((tm, tn), lambda j, i, kk: (i, j)))]
    return _one(_matmul(
        name, (n // tn, m // tm, 1),
        [(a, pl.BlockSpec((tm, k), lambda j, i, kk: (i, 0)),
          b, pl.BlockSpec((k, tn), lambda j, i, kk: (0, j)), "nn", 0)],
        1, None, extras,
        [(jax.ShapeDtypeStruct((m, n), out_dtype), pl.BlockSpec((tm, tn), lambda j, i, kk: (i, j)))],
        _store_epilogue if res is None else _residual_epilogue, side), side)


def _mm_nn_colblocked(a, b_blk, name, res, side=None):
    m, k = a.shape
    nb, _, bw = b_blk.shape
    tm = _row_tile(m, 2048)
    o_spec = pl.BlockSpec((tm, bw), lambda j, i, kk: (i, j))
    return _one(_matmul(
        name, (nb, m // tm, 1),
        [(a, pl.BlockSpec((tm, k), lambda j, i, kk: (i, 0)),
          b_blk, pl.BlockSpec((None, k, bw), lambda j, i, kk: (j, 0, 0)), "nn", 0)],
        1, None, [(res, o_spec)],
        [(jax.ShapeDtypeStruct((m, nb * bw), F32), o_spec)], _residual_epilogue, side), side)


def _mm_nt_colblocked(a, b_blk, name, out_dtype, jb, side=None):
    m = a.shape[0]
    nb, n, bw = b_blk.shape
    tm = _row_tile(m, 512)
    return _one(_matmul(
        name, (m // tm, nb // jb),
        [(a, pl.BlockSpec((tm, jb * bw), lambda i, j: (i, j)),
          b_blk, pl.BlockSpec((jb, n, bw), lambda i, j: (j, 0, 0)), "nt", 0)],
        1, (tm, n), [],
        [(jax.ShapeDtypeStruct((m, n), out_dtype), pl.BlockSpec((tm, n), lambda i, j: (i, 0)))],
        _store_epilogue, side), side)


def _mm_nt_rowblocked(a, b, name, tn, out_dtype, side=None):
    m, k = a.shape
    n = b.shape[0]
    tm = _row_tile(m, 1024)
    return _one(_matmul(
        name, (n // tn, m // tm, 1),
        [(a, pl.BlockSpec((tm, k), lambda j, i, kk: (i, 0)),
          b, pl.BlockSpec((tn, k), lambda j, i, kk: (j, 0)), "nt", 0)],
        1, None, [],
        [(jax.ShapeDtypeStruct((m, n), out_dtype), pl.BlockSpec((tm, tn), lambda j, i, kk: (i, j)))],
        _store_epilogue, side), side)


def _mm_tn_colblocked(a, b, name, bw, side=None):
    s, m = a.shape
    nb = b.shape[1] // bw
    tm = _row_tile(m, 512)
    return _one(_matmul(
        name, (nb, m // tm, 1),
        [(a, pl.BlockSpec((s, tm), lambda j, i, k: (0, i)),
          b, pl.BlockSpec((s, bw), lambda j, i, k: (0, j)), "tn", 0)],
        1, None, [],
        [(jax.ShapeDtypeStruct((nb, m, bw), BF16), pl.BlockSpec((None, tm, bw), lambda j, i, k: (j, i, 0)))],
        _store_epilogue, side), side)


def _mm_tn_rowblocked(a, b, name, bh):
    s, n = b.shape
    nb = a.shape[1] // bh
    tn = _row_tile(n, 512)
    return _matmul(
        name, (nb, n // tn, 1),
        [(a, pl.BlockSpec((s, bh), lambda j, i, k: (0, j)),
          b, pl.BlockSpec((s, tn), lambda j, i, k: (0, i)), "tn", 0)],
        1, None, [],
        [(jax.ShapeDtypeStruct((nb, bh, n), BF16), pl.BlockSpec((None, bh, tn), lambda j, i, k: (j, 0, i)))],
        _store_epilogue)[0]


def _as2d(a):
    return a.reshape(-1, a.shape[-1])


def _cast_bf16(w, name):
    w2 = _as2d(w)
    rows, cols = w2.shape
    tr = _row_tile(rows, 256)

    def body(w_ref, o_ref):
        o_ref[...] = w_ref[...].astype(BF16)

    spec = pl.BlockSpec((tr, cols), lambda i: (i, 0))
    out = pl.pallas_call(
        body, grid=(rows // tr,), in_specs=[spec], out_specs=spec,
        out_shape=jax.ShapeDtypeStruct((rows, cols), BF16),
        compiler_params=_params(), name=name,
    )(w2)
    return out.reshape(w.shape)


def _cast_bf16_layer(w, layer, name):
    _, rows, cols = w.shape
    tr = _row_tile(rows, 256)

    def body(w_ref, o_ref):
        o_ref[...] = w_ref[...].astype(BF16)

    return pl.pallas_call(
        body, grid=(rows // tr,),
        in_specs=[pl.BlockSpec((None, tr, cols), lambda i: (layer, i, 0))],
        out_specs=pl.BlockSpec((tr, cols), lambda i: (i, 0)),
        out_shape=jax.ShapeDtypeStruct((rows, cols), BF16),
        compiler_params=_params(), name=name,
    )(w)


def _reduce_adam(recvs, w, m, v, name, side=None):
    n_layers, rows, cols = w.shape
    n_slots = recvs[0].shape[0]
    tr = _row_tile(rows, max(16, (128 * 1024 // cols) // 16 * 16))
    nt = rows // tr
    c1 = 1.0 - ADAM_B1 ** ADAM_STEP
    c2 = 1.0 - ADAM_B2 ** ADAM_STEP

    def body(*refs):
        r_refs = refs[:n_layers]
        w_ref, m_ref, v_ref, g_out, d_out, m_out, v_out = refs[n_layers:]
        layer = pl.program_id(0)

        def update(r_ref):
            g = r_ref[0].astype(F32)
            for k in range(1, n_slots):
                g = g + r_ref[k].astype(F32)
            mm = ADAM_B1 * m_ref[...] + (1.0 - ADAM_B1) * g
            vv = ADAM_B2 * v_ref[...] + (1.0 - ADAM_B2) * (g * g)
            m_hat = mm / c1
            v_hat = vv / c2
            g_out[...] = g
            d_out[...] = -ADAM_LR * (m_hat / (jnp.sqrt(v_hat) + ADAM_EPS) + ADAM_WD * w_ref[...])
            m_out[...] = mm
            v_out[...] = vv

        for li in range(n_layers):
            if n_layers == 1:
                update(r_refs[li])
            else:
                pl.when(layer == li)(functools.partial(update, r_refs[li]))

    def recv_spec(li):
        def imap(layer, i):
            return (0, jnp.where(layer == li, i, jnp.where(layer < li, 0, nt - 1)), 0)
        return pl.BlockSpec((n_slots, tr, cols), imap)

    spec = pl.BlockSpec((None, tr, cols), lambda layer, i: (layer, i, 0))
    o_shape = jax.ShapeDtypeStruct(w.shape, F32)
    grid = (n_layers, nt)
    s_in, s_args, s_out, s_shapes, s_scratch = _side_specs(side)
    res = pl.pallas_call(
        _carrying(body, n_layers + 3, 4, 0, grid, side), grid=grid,
        in_specs=[recv_spec(li) for li in range(n_layers)] + [spec] * 3 + s_in,
        out_specs=[spec] * 4 + s_out, out_shape=[o_shape] * 4 + s_shapes,
        scratch_shapes=s_scratch, compiler_params=_params(), name=name,
    )(*recvs, w, m, v, *s_args)
    return res if side is None else (res[:4], res[4:])


def _my_place():
    return lax.axis_index("x"), lax.axis_index("y"), lax.axis_index("c")


class _GatherSide:
    def __init__(self, blocks):
        self.ins = list(blocks)
        self.outs = [jax.ShapeDtypeStruct((N_DEV,) + b.shape, b.dtype) for b in blocks]

    def scratch(self):
        n = len(self.ins)
        return [pltpu.SemaphoreType.DMA((7 * n,)), pltpu.SemaphoreType.DMA((7 * n,)),
                pltpu.SemaphoreType.DMA((n,))]

    def phases(self, x_refs, out_refs, send_sems, recv_sems, local_sems):
        n = len(self.ins)
        x, y, c = _my_place()
        me, sibling = (x, y, c), (x, y, 1 - c)
        chips = [(1 - x, y), (x, 1 - y), (1 - x, 1 - y)]

        def slot(t, px, py, pc):
            return out_refs[t].at[4 * px + 2 * py + pc]

        def copy(t, k, blk, to, src=None):
            return pltpu.make_async_remote_copy(
                src_ref=slot(t, *blk) if src is None else src, dst_ref=slot(t, *blk),
                send_sem=send_sems.at[7 * t + k], recv_sem=recv_sems.at[7 * t + k],
                device_id=to, device_id_type=pl.DeviceIdType.MESH)

        def own(t):
            return pltpu.make_async_copy(x_refs[t], slot(t, *me), local_sems.at[t])

        def first(t):
            return [copy(t, 0, me, sibling, src=x_refs[t])] + [
                copy(t, 1 + j, me, (*chip, c), src=x_refs[t]) for j, chip in enumerate(chips)]

        def passed(t):
            return [copy(t, 4 + j, (*chip, c), sibling) for j, chip in enumerate(chips)]

        def start():
            for t in range(n):
                own(t).start()
                for cp in first(t):
                    cp.start()

        def mid():
            for t in range(n):
                fwd = passed(t)
                for j, chip in enumerate(chips):
                    copy(t, 1 + j, (*chip, c), me).wait_recv()
                    fwd[j].start()

        def finish():
            for t in range(n):
                copy(t, 0, sibling, me).wait_recv()
                for j, chip in enumerate(chips):
                    copy(t, 4 + j, (*chip, 1 - c), me).wait_recv()
                for cp in first(t) + passed(t):
                    cp.wait_send()
                own(t).wait()

        return start, mid, finish


class _ExchangeSide:
    def __init__(self, blocked):
        self.ins = list(blocked)
        self.outs = [jax.ShapeDtypeStruct(b.shape, b.dtype) for b in blocked]

    def scratch(self):
        n = len(self.ins)
        return [pltpu.SemaphoreType.DMA((7 * n,)), pltpu.SemaphoreType.DMA((7 * n,)),
                pltpu.SemaphoreType.DMA((n,))]

    def phases(self, srcs, dsts, send_sems, recv_sems, local_sems):
        n = len(self.ins)
        x, y, c = _my_place()
        me = 4 * x + 2 * y + c

        def own(t):
            return pltpu.make_async_copy(srcs[t].at[me], dsts[t].at[me], local_sems.at[t])

        def copies(t, arriving):
            res = []
            for k in range(1, N_DEV):
                px, py, pc = x ^ (k >> 2), y ^ ((k >> 1) & 1), c ^ (k & 1)
                peer = 4 * px + 2 * py + pc
                sem = 7 * t + k - 1
                res.append(pltpu.make_async_remote_copy(
                    src_ref=srcs[t].at[peer], dst_ref=dsts[t].at[peer if arriving else me],
                    send_sem=send_sems.at[sem], recv_sem=recv_sems.at[sem],
                    device_id=(px, py, pc), device_id_type=pl.DeviceIdType.MESH))
            return res

        def start():
            for t in range(n):
                own(t).start()
                for send in copies(t, False):
                    send.start()

        def mid():
            pass

        def finish():
            for t in range(n):
                for arrival in copies(t, True):
                    arrival.wait_recv()
                for send in copies(t, False):
                    send.wait_send()
                own(t).wait()

        return start, mid, finish


def _to_residue_major(a, dilation):
    s, w = a.shape
    return a.reshape(s // dilation, dilation, w).transpose(1, 0, 2).reshape(s, w)


def _from_residue_major(a, dilation):
    s, w = a.shape
    return a.reshape(dilation, s // dilation, w).transpose(1, 0, 2).reshape(s, w)


def _mem_kv(mem, gain, wkv, layer, tag):
    mem_n = _rmsnorm_fwd(mem, gain, "mem_norm_" + tag)
    mlen, d = mem.shape
    nb, _, bh, n = wkv.shape
    kv = _matmul(
        "mem_kv_" + tag, (1, nb),
        [(mem_n, pl.BlockSpec((mlen, bh), lambda i, j: (0, j)),
          wkv, pl.BlockSpec((None, None, bh, n), lambda i, j: (j, layer, 0, 0)), "nn", 0)],
        1, (mlen, n), [],
        [(jax.ShapeDtypeStruct((mlen, n), BF16), pl.BlockSpec((mlen, n), lambda i, j: (0, 0)))],
        _store_epilogue)[0]
    return mem_n, kv


def _mem_kv_bwd(mem, gain, mem_n, wkv, layer, dkv, tag):
    mlen, d = mem.shape
    nb, _, bh, n = wkv.shape
    dkvb = dkv.astype(BF16)
    dw = _mm_tn_rowblocked(mem_n, dkvb, "mem_kv_dw_" + tag, bh)
    dmem_n = _matmul(
        "mem_kv_dx_" + tag, (nb, 1),
        [(dkvb, pl.BlockSpec((mlen, n), lambda j, k: (0, 0)),
          wkv, pl.BlockSpec((None, None, bh, n), lambda j, k: (j, layer, 0, 0)), "nt", 0)],
        1, None, [],
        [(jax.ShapeDtypeStruct((mlen, d), F32), pl.BlockSpec((mlen, bh), lambda j, k: (0, j)))],
        _store_epilogue)[0]
    _, _, dgain = _rmsnorm_bwd(mem, gain, dmem_n, None, "mem_norm_bwd_" + tag)
    return dw, dgain


def _row_blocks(a):
    return a.reshape(N_DEV, -1, a.shape[-1])


def _rows(a):
    return a.reshape(-1, a.shape[-1])


def _ffn_bwd(x, gain, w_gate, w_up, w_down, saved, dx, dxb, tag):
    hf, silu, up_dsilu, hid = saved
    dgate, dup = _swiglu_bwd_hidden(dxb, w_down, silu, up_dsilu, "swiglu_bwd_hidden_" + tag)
    dwd = _mm_tn_full(hid, dxb, "swiglu_bwd_wdown_" + tag)
    dwg, (r_wd,) = _mm_tn_full(dgate, hf, "swiglu_bwd_wgate_" + tag, _ExchangeSide([_row_blocks(dwd)]))
    dwu, (r_wg,) = _mm_tn_full(dup, hf, "swiglu_bwd_wup_" + tag, _ExchangeSide([_row_blocks(dwg)]))
    dhf, (r_wu,) = _swiglu_bwd_input(dgate, dup, w_gate, w_up, "swiglu_bwd_input_" + tag,
                                     _ExchangeSide([_row_blocks(dwu)]))
    dx_new, dxb_new, dgain = _rmsnorm_bwd(x, gain, dhf, dx, "ffn_norm_bwd_" + tag)
    return dx_new, dxb_new, dgain, r_wg, r_wu, r_wd


def _local_step(x, mem, positions, target, first_blocks, shards, small):
    s, d = x.shape
    tabs = _rotary_tables(positions)
    mix_norm, mem_norm, ffn_norm = small["mix_norm"], small["mem_norm"], small["ffn_norm"]

    h0, (w_attn_in, ln_all) = _rmsnorm_fwd(x, mix_norm[0:1], "mix_norm_0", _GatherSide(first_blocks))
    w_attn_in = _rows(w_attn_in)
    ln_full = ln_all.transpose(1, 0, 2).reshape(2, 1, -1)
    ln_g, ln_b = ln_full[0], ln_full[1]
    proj0, (w_mem_kv, w_attn_out, w_gate0) = _attn_in_proj(
        h0, w_attn_in, tabs, _GatherSide([shards["w_mem_kv"], shards["attn_w_out"], shards["w_gate"][0]]))
    up_shard = shards["w_up"][0]
    cut = -(-up_shard.shape[0] // N_GROUPS // 16) * 16
    up_pieces = [up_shard[g * cut:min((g + 1) * cut, up_shard.shape[0])] for g in range(N_GROUPS)]
    qkv, offs, outs, lses, up_gathered = [], [], [], [], []
    for g, dil in enumerate(DILATIONS):
        if dil == 1:
            arr, off = proj0, (g, N_GROUPS + g, 2 * N_GROUPS + g)
        else:
            cols = [proj0[:, (p * N_GROUPS + g) * GROUP_W:(p * N_GROUPS + g + 1) * GROUP_W] for p in range(3)]
            arr, off = _to_residue_major(jnp.concatenate(cols, axis=1), dil), (0, 1, 2)
        o, lse, piece = _attn_fwd(arr, arr, arr, off, s // dil // BLK, "attn_fwd_%d" % g,
                                  _GatherSide([up_pieces[g]]))
        up_gathered.append(piece)
        qkv.append(arr)
        offs.append(off)
        if dil > 1:
            o, lse = _from_residue_major(o, dil), _from_residue_major(lse, dil)
        outs.append(o)
        lses.append(lse)
    mix0 = _merge_fwd(outs, lses)
    qm_off0 = 3 * N_GROUPS
    mem_n0, kv0 = _mem_kv(mem, mem_norm[0:1], w_mem_kv, 0, "0")
    cat0 = _memattn_fwd(proj0, qm_off0, kv0, mix0, 1, "memattn_fwd_0")
    x1 = _mm_nn_colblocked(cat0, w_attn_out, "attn_out_proj", x)
    hf0 = _rmsnorm_fwd(x1, ffn_norm[0:1], "ffn_norm_0")
    w_gate0, w_up0 = _rows(w_gate0), _rows(jnp.concatenate(up_gathered, axis=1))
    (silu0, ud0, hid0), (w_down0, w_sgu_in, w_sgu_out) = _swiglu_fwd(
        hf0, w_gate0, w_up0, "swiglu_fwd_0",
        _GatherSide([shards["w_down"][0], shards["sgu_w_in"], shards["sgu_w_out"]]))
    w_down0 = _rows(w_down0)
    x2, (w_gate1,) = _swiglu_down(hid0, w_down0, x1, "swiglu_down_0", _GatherSide([shards["w_gate"][1]]))
    ffn_saved0 = (hf0, silu0, ud0, hid0)

    h1 = _rmsnorm_fwd(x2, mix_norm[1:2], "mix_norm_1")
    w_sgu_in = _rows(w_sgu_in)
    w_sgu_out = _rows(w_sgu_out)
    proj1, (w_up1,) = _mm_nt_rowblocked(h1, w_sgu_in, "sgu_in_proj", w_sgu_in.shape[0] // 7, BF16,
                                        _GatherSide([shards["w_up"][1]]))
    w_gate1, w_up1 = _rows(w_gate1), _rows(w_up1)
    b_st = small["sgu_b_spatial"].T
    mix1 = _sgu_fwd(proj1, ln_g, ln_b, small["sgu_w_spatial"], b_st)
    qm_off1 = 2 * SGU_W // GROUP_W
    mem_n1, kv1 = _mem_kv(mem, mem_norm[1:2], w_mem_kv, 1, "1")
    cat1 = _memattn_fwd(proj1, qm_off1, kv1, mix1, SGU_W // GROUP_W, "memattn_fwd_1")
    x3 = _mm_nn(cat1, w_sgu_out, "sgu_out_proj", d // 2, out_dtype=F32, res=x2)
    hf1 = _rmsnorm_fwd(x3, ffn_norm[1:2], "ffn_norm_1")
    (silu1, ud1, hid1), (w_down1,) = _swiglu_fwd(hf1, w_gate1, w_up1, "swiglu_fwd_1",
                                                 _GatherSide([shards["w_down"][1]]))
    w_down1 = _rows(w_down1)
    x4 = _swiglu_down(hid1, w_down1, x3, "swiglu_down_1")
    ffn_saved1 = (hf1, silu1, ud1, hid1)

    loss, dx, dxb, d_final = _loss_head(x4, small["final_norm"], target)

    recvs, sgrads = {}, {}
    dx, dxb, d_ffn1, r_wg1, r_wu1, r_wd1 = _ffn_bwd(x3, ffn_norm[1:2], w_gate1, w_up1, w_down1, ffn_saved1,
                                                    dx, dxb, "1")
    dcat1 = _mm_nt_rowblocked(dxb, w_sgu_out, "sgu_out_proj_dx", w_sgu_out.shape[0] // 2, BF16)
    dwsout = _mm_tn_rowblocked(cat1, dxb, "sgu_out_proj_dw", w_sgu_out.shape[0] // N_DEV)
    duv, dws, dbs, dlng, dlnb = _sgu_bwd(proj1, dcat1, ln_g, ln_b, small["sgu_w_spatial"], b_st)
    dproj1, dkv1 = _memattn_bwd(proj1, qm_off1, kv1, dcat1, SGU_W // GROUP_W, "memattn_bwd_1",
                                into=duv, into_off=2 * SGU_W // GROUP_W)
    dwkv1, d_memnorm1 = _mem_kv_bwd(mem, mem_norm[1:2], mem_n1, w_mem_kv, 1, dkv1, "1")
    dwsin, (r_wsout, r_wkv1) = _mm_tn_full(dproj1, h1, "sgu_in_proj_dw", _ExchangeSide([dwsout, dwkv1]))
    dh1, (r_wsin,) = _mm_nn(dproj1, w_sgu_in, "sgu_in_proj_dx", d // 4, out_dtype=BF16,
                            side=_ExchangeSide([_row_blocks(dwsin)]))
    dx, dxb, d_mix1 = _rmsnorm_bwd(x2, mix_norm[1:2], dh1, dx, "mix_norm_bwd_1")

    dx, dxb, d_ffn0, r_wg0, r_wu0, r_wd0 = _ffn_bwd(x1, ffn_norm[0:1], w_gate0, w_up0, w_down0, ffn_saved0,
                                                    dx, dxb, "0")
    dcat0 = _mm_nt_colblocked(dxb, w_attn_out, "attn_out_proj_dx", BF16, 4)
    dwout0 = _mm_tn_colblocked(cat0, dxb, "attn_out_proj_dw", w_attn_out.shape[2])
    dos_and_deltas = _merge_bwd(dcat0, outs, lses)
    dqkv = []
    for g, dil in enumerate(DILATIONS):
        do_g, dl_g = dos_and_deltas[g], dos_and_deltas[N_GROUPS + g]
        lse_g = lses[g]
        if dil > 1:
            do_g, dl_g, lse_g = (_to_residue_major(t, dil) for t in (do_g, dl_g, lse_g))
        t = _attn_bwd(qkv[g], qkv[g], qkv[g], offs[g], do_g, lse_g, dl_g, s // dil // BLK, "attn_bwd_%d" % g)
        dqkv.append(_from_residue_major(t, dil) if dil > 1 else t)
    dqm0, dkv0 = _memattn_bwd(proj0, qm_off0, kv0, dcat0, 1, "memattn_bwd_0")
    dwkv0, d_memnorm0 = _mem_kv_bwd(mem, mem_norm[0:1], mem_n0, w_mem_kv, 0, dkv0, "0")
    dproj0 = _assemble_dproj(dqkv, dqm0, tabs)
    dwin0, (r_wout0, r_wkv0) = _mm_tn_full(dproj0, h0, "attn_in_proj_dw", _ExchangeSide([dwout0, dwkv0]))
    dh0, (r_win0,) = _mm_nn(dproj0, w_attn_in, "attn_in_proj_dx", d // 4, out_dtype=BF16,
                            side=_ExchangeSide([_row_blocks(dwin0)]))
    grad_x, _, d_mix0 = _rmsnorm_bwd(x, mix_norm[0:1], dh0, dx, "mix_norm_bwd_0")

    recvs["w_gate"] = [r_wg0, r_wg1]
    recvs["w_up"] = [r_wu0, r_wu1]
    recvs["w_down"] = [r_wd0, r_wd1]
    recvs["w_mem_kv"] = [r_wkv0, r_wkv1]
    recvs["attn_w_in"] = [r_win0]
    recvs["attn_w_out"] = [r_wout0]
    recvs["sgu_w_in"] = [r_wsin]
    recvs["sgu_w_out"] = [r_wsout]
    sgrads["mix_norm"] = jnp.concatenate([d_mix0, d_mix1], axis=0)
    sgrads["mem_norm"] = jnp.concatenate([d_memnorm0, d_memnorm1], axis=0)
    sgrads["ffn_norm"] = jnp.concatenate([d_ffn0, d_ffn1], axis=0)
    sgrads["final_norm"] = d_final
    sgrads["sgu_w_spatial"] = dws
    sgrads["sgu_b_spatial"] = dbs[:, :SGU_GROUPS].T
    sgrads["sgu_ln_g"] = dlng
    sgrads["sgu_ln_b"] = dlnb
    return loss, grad_x, recvs, sgrads


ADAM_ORDER = ("w_gate", "w_up", "w_down", "w_mem_kv", "sgu_w_in", "sgu_w_out", "attn_w_out", "attn_w_in")
TRANSPOSED = ("w_gate", "w_up", "sgu_w_in", "attn_w_in")
SMALL_REPLICATED = ("mix_norm", "mem_norm", "ffn_norm", "final_norm", "sgu_w_spatial", "sgu_b_spatial")
SMALL_SHARDED = ("sgu_ln_g", "sgu_ln_b")
WEIGHT_ORDER = ("mix_norm", "mem_norm", "w_mem_kv", "ffn_norm", "w_gate", "w_up", "w_down", "attn_w_in",
                "attn_w_out", "sgu_w_in", "sgu_ln_g", "sgu_ln_b", "sgu_w_spatial", "sgu_b_spatial",
                "sgu_w_out", "final_norm")
PACK_LANES = 128


def _pack(parts):
    flat = [p.reshape(-1) for p in parts]
    sizes = [f.shape[0] for f in flat]
    total = sum(sizes)
    rows = -(-total // PACK_LANES)
    rows = -(-rows // 8) * 8
    pad = rows * PACK_LANES - total
    packed = jnp.concatenate(flat + [jnp.zeros((pad,), F32)]).reshape(rows, PACK_LANES)
    offs, o = [], 0
    for sz in sizes:
        offs.append(o)
        o += sz
    return packed, offs, sizes


def _unpack(packed, offs, sizes, shapes):
    flat = packed.reshape(-1)
    return [flat[o:o + sz].reshape(shp) for o, sz, shp in zip(offs, sizes, shapes)]


def kernel(x, mem, positions, mix_norm, mem_norm, w_mem_kv, ffn_norm, w_gate, w_up, w_down, attn_w_in, attn_w_out, sgu_w_in, sgu_ln_g, sgu_ln_b, sgu_w_spatial, sgu_b_spatial, sgu_w_out, final_norm, loss_target, m_mix_norm, m_mem_norm, m_w_mem_kv, m_ffn_norm, m_w_gate, m_w_up, m_w_down, m_attn_w_in, m_attn_w_out, m_sgu_w_in, m_sgu_ln_g, m_sgu_ln_b, m_sgu_w_spatial, m_sgu_b_spatial, m_sgu_w_out, m_final_norm, v_mix_norm, v_mem_norm, v_w_mem_kv, v_ffn_norm, v_w_gate, v_w_up, v_w_down, v_attn_w_in, v_attn_w_out, v_sgu_w_in, v_sgu_ln_g, v_sgu_ln_b, v_sgu_w_spatial, v_sgu_b_spatial, v_sgu_w_out, v_final_norm):
    w = dict(mix_norm=mix_norm, mem_norm=mem_norm, w_mem_kv=w_mem_kv, ffn_norm=ffn_norm, w_gate=w_gate,
             w_up=w_up, w_down=w_down, attn_w_in=attn_w_in, attn_w_out=attn_w_out, sgu_w_in=sgu_w_in,
             sgu_ln_g=sgu_ln_g, sgu_ln_b=sgu_ln_b, sgu_w_spatial=sgu_w_spatial, sgu_b_spatial=sgu_b_spatial,
             sgu_w_out=sgu_w_out, final_norm=final_norm)
    mo = dict(mix_norm=m_mix_norm, mem_norm=m_mem_norm, w_mem_kv=m_w_mem_kv, ffn_norm=m_ffn_norm,
              w_gate=m_w_gate, w_up=m_w_up, w_down=m_w_down, attn_w_in=m_attn_w_in, attn_w_out=m_attn_w_out,
              sgu_w_in=m_sgu_w_in, sgu_ln_g=m_sgu_ln_g, sgu_ln_b=m_sgu_ln_b, sgu_w_spatial=m_sgu_w_spatial,
              sgu_b_spatial=m_sgu_b_spatial, sgu_w_out=m_sgu_w_out, final_norm=m_final_norm)
    vo = dict(mix_norm=v_mix_norm, mem_norm=v_mem_norm, w_mem_kv=v_w_mem_kv, ffn_norm=v_ffn_norm,
              w_gate=v_w_gate, w_up=v_w_up, w_down=v_w_down, attn_w_in=v_attn_w_in, attn_w_out=v_attn_w_out,
              sgu_w_in=v_sgu_w_in, sgu_ln_g=v_sgu_ln_g, sgu_ln_b=v_sgu_ln_b, sgu_w_spatial=v_sgu_w_spatial,
              sgu_b_spatial=v_sgu_b_spatial, sgu_w_out=v_sgu_w_out, final_norm=v_final_norm)
    me = 4 * lax.axis_index("x") + 2 * lax.axis_index("y") + lax.axis_index("c")
    d_model = x.shape[-1]

    for n in TRANSPOSED:
        w[n], mo[n], vo[n] = (jnp.swapaxes(t, 1, 2) for t in (w[n], mo[n], vo[n]))

    shards = {
        "w_mem_kv": _cast_bf16(w_mem_kv, "cast_w_mem_kv"),
        "attn_w_out": _cast_bf16(attn_w_out[0], "cast_attn_w_out"),
        "sgu_w_in": _cast_bf16(w["sgu_w_in"][0], "cast_sgu_w_in"),
        "sgu_w_out": _cast_bf16(sgu_w_out[0], "cast_sgu_w_out"),
    }
    for n in ("w_gate", "w_up", "w_down"):
        shards[n] = [_cast_bf16_layer(w[n], layer, "cast_%s_%d" % (n, layer)) for layer in range(w[n].shape[0])]
    ln_pack = jnp.concatenate([sgu_ln_g, sgu_ln_b], axis=0)
    first_blocks = [_cast_bf16(w["attn_w_in"][0], "cast_attn_w_in"), ln_pack]
    small = dict(mix_norm=mix_norm, mem_norm=mem_norm, ffn_norm=ffn_norm, final_norm=final_norm.reshape(1, -1),
                 sgu_w_spatial=sgu_w_spatial[0], sgu_b_spatial=sgu_b_spatial[0])

    loss, grad_x, recvs, sgrads = _local_step(x[0], mem[0], positions[0], loss_target[0], first_blocks, shards,
                                              small)
    loss = lax.psum(loss[0, 0], MESH_AXES)

    small_names = SMALL_REPLICATED + SMALL_SHARDED
    packed, offs, sizes = _pack([sgrads[n] for n in small_names])
    out_g, out_d, out_m, out_v = {}, {}, {}, {}
    for n in ADAM_ORDER:
        shard = w[n]
        w3 = shard.reshape(shard.shape[0], -1, shard.shape[-1])
        rs = [r.reshape(N_DEV, -1, shard.shape[-1]) for r in recvs[n]]
        operands = (rs, w3, mo[n].reshape(w3.shape), vo[n].reshape(w3.shape), "adam_" + n)
        if n == ADAM_ORDER[0]:
            res, (all_packs,) = _reduce_adam(*operands, _GatherSide([packed]))
        else:
            res = _reduce_adam(*operands)
        res = [r.reshape(shard.shape) for r in res]
        if n in TRANSPOSED:
            res = [jnp.swapaxes(r, 1, 2) for r in res]
        out_g[n], out_d[n], out_m[n], out_v[n] = res

    rep_shapes = [w[n].shape for n in SMALL_REPLICATED]
    w_pack, w_offs, w_sizes = _pack([w[n] for n in SMALL_REPLICATED])
    m_pack, _, _ = _pack([mo[n] for n in SMALL_REPLICATED])
    v_pack, _, _ = _pack([vo[n] for n in SMALL_REPLICATED])
    n_rep_rows = w_pack.shape[0]
    res = _reduce_adam([all_packs[:, :n_rep_rows]], w_pack[None], m_pack[None], v_pack[None], "adam_small")
    for dst, r in zip((out_g, out_d, out_m, out_v), res):
        for n, val in zip(SMALL_REPLICATED, _unpack(r[0], w_offs, w_sizes, rep_shapes)):
            dst[n] = val
    ln_rows0 = offs[len(SMALL_REPLICATED)] // PACK_LANES
    ln_rows = 2 * SGU_W // PACK_LANES
    ln_sum = _reduce_adam([all_packs[:, ln_rows0:ln_rows0 + ln_rows]], jnp.zeros((1, ln_rows, PACK_LANES), F32),
                          jnp.zeros((1, ln_rows, PACK_LANES), F32), jnp.zeros((1, ln_rows, PACK_LANES), F32),
                          "sum_ln_grads")[0]
    ln_grads = ln_sum.reshape(2, N_DEV, -1)
    ln_mine = lax.dynamic_index_in_dim(ln_grads, me, axis=1, keepdims=False)
    w_ln = jnp.concatenate([sgu_ln_g, sgu_ln_b], axis=0)[None]
    m_ln = jnp.concatenate([m_sgu_ln_g, m_sgu_ln_b], axis=0)[None]
    v_ln = jnp.concatenate([v_sgu_ln_g, v_sgu_ln_b], axis=0)[None]
    res = _reduce_adam([ln_mine[None]], w_ln, m_ln, v_ln, "adam_ln")
    for dst, r in zip((out_g, out_d, out_m, out_v), res):
        dst["sgu_ln_g"], dst["sgu_ln_b"] = r[0, 0:1], r[0, 1:2]

    return (loss, grad_x[None], *[out_g[n] for n in WEIGHT_ORDER], *[out_d[n] for n in WEIGHT_ORDER],
            *[out_m[n] for n in WEIGHT_ORDER], *[out_v[n] for n in WEIGHT_ORDER])
```

```python
import functools

import jax
import jax.numpy as jnp
from jax import lax
from jax.experimental import pallas as pl
from jax.experimental.pallas import tpu as pltpu

F32 = jnp.float32
BF16 = jnp.bfloat16

N_DEV = 8
HEAD = 128
HPG = 4
GROUP_W = HPG * HEAD
N_GROUPS = 3
DILATIONS = (1, 4, 16)
BLK = 128
SGU_GROUPS = 12
SGU_W = SGU_GROUPS * HEAD
ROT_HALF = 16
ROPE_THETA = 500000.0
NORM_EPS = 1e-6
LN_EPS = 1e-5
NEG_INF = -1e30
SCALE = HEAD ** -0.5

ADAM_LR = 0.001
ADAM_B1 = 0.9
ADAM_B2 = 0.999
ADAM_EPS = 1e-08
ADAM_WD = 0.01
ADAM_STEP = 10

VMEM_LIMIT_V7X = 56 * 1024 * 1024
MESH_AXES = ("x", "y", "c")

_DN = {
    "nn": (((1,), (0,)), ((), ())),
    "nt": (((1,), (1,)), ((), ())),
    "tn": (((0,), (0,)), ((), ())),
}


def _dot(a, b, kind="nn"):
    return lax.dot_general(a, b, _DN[kind], preferred_element_type=F32)


def _params():
    return pltpu.CompilerParams(vmem_limit_bytes=VMEM_LIMIT_V7X)


def _row_tile(rows, cap):
    if rows <= cap:
        return rows
    t = cap - cap % 16
    while t >= 16:
        if rows % t == 0:
            return t
        t -= 16
    return rows


def _gelu(x):
    c = 0.7978845608028654
    return 0.5 * x * (1.0 + jnp.tanh(c * (x + 0.044715 * x * x * x)))


def _gelu_and_grad(x):
    c = 0.7978845608028654
    x2 = x * x
    t = jnp.tanh(c * x * (1.0 + 0.044715 * x2))
    half = 0.5 * (1.0 + t)
    return x * half, half + 0.5 * x * (1.0 - t * t) * c * (1.0 + 3.0 * 0.044715 * x2)


def _sigmoid(x):
    return 1.0 / (1.0 + jnp.exp(-x))


def _matmul(name, grid, terms, n_acc, acc_shape, extras, outs, epilogue, side=None):
    nk = grid[-1]
    nt, ne, no = len(terms), len(extras), len(outs)
    kinds = [(t[4], t[5]) for t in terms]
    n_scratch_acc = 0 if nk == 1 else n_acc
    ns_in = len(side.ins) if side else 0
    ns_out = len(side.outs) if side else 0
    n_steps = 1
    for g in grid:
        n_steps *= g

    def body(*refs):
        pos = 0
        ab = refs[pos:pos + 2 * nt]
        pos += 2 * nt
        ex = refs[pos:pos + ne]
        pos += ne
        s_in = refs[pos:pos + ns_in]
        pos += ns_in
        out = refs[pos:pos + no]
        pos += no
        s_out = refs[pos:pos + ns_out]
        pos += ns_out
        accs = refs[pos:pos + n_scratch_acc]
        s_sems = refs[pos + n_scratch_acc:]
        if side:
            step = pl.program_id(0)
            for ax in range(1, len(grid)):
                step = step * grid[ax] + pl.program_id(ax)
            start, mid, finish = side.phases(s_in, s_out, *s_sems)
            pl.when(step == 0)(start)
        parts = [None] * n_acc
        for t, (kind, ai) in enumerate(kinds):
            a_ref, b_ref = ab[2 * t], ab[2 * t + 1]
            if len(b_ref.shape) == 2:
                pairs = [(a_ref[...], b_ref[...])]
            elif len(a_ref.shape) == 3:
                pairs = [(a_ref[q], b_ref[q]) for q in range(b_ref.shape[0])]
            else:
                bw = b_ref.shape[2]
                pairs = [(a_ref[:, q * bw:(q + 1) * bw], b_ref[q]) for q in range(b_ref.shape[0])]
            for a, b in pairs:
                p = _dot(a.astype(BF16), b.astype(BF16), kind)
                parts[ai] = p if parts[ai] is None else parts[ai] + p
        if nk == 1:
            epilogue(parts, ex, out)
        else:
            k = pl.program_id(len(grid) - 1)

            @pl.when(k == 0)
            def _():
                for ai in range(n_acc):
                    accs[ai][...] = parts[ai]

            @pl.when(k > 0)
            def _():
                for ai in range(n_acc):
                    accs[ai][...] += parts[ai]

            @pl.when(k == nk - 1)
            def _():
                epilogue([a[...] for a in accs], ex, out)

        if side:
            pl.when(step == (3 * n_steps) // 4)(mid)
            pl.when(step == n_steps - 1)(finish)

    hbm = pl.BlockSpec(memory_space=pltpu.HBM)
    in_specs, args = [], []
    for (a, a_spec, b, b_spec, _, _) in terms:
        in_specs += [a_spec, b_spec]
        args += [a, b]
    for (e, e_spec) in extras:
        in_specs.append(e_spec)
        args.append(e)
    scratch = [pltpu.VMEM(acc_shape, F32) for _ in range(n_scratch_acc)]
    out_specs = [o[1] for o in outs]
    out_shape = [o[0] for o in outs]
    if side:
        in_specs += [hbm] * ns_in
        args += list(side.ins)
        out_specs += [hbm] * ns_out
        out_shape += list(side.outs)
        scratch += side.scratch()
    res = pl.pallas_call(
        body, grid=grid, in_specs=in_specs, out_specs=out_specs, out_shape=out_shape,
        scratch_shapes=scratch, compiler_params=_params(), name=name,
    )(*args)
    return res if side is None else (res[:no], res[no:])


def _carrying(body, n_in, n_out, n_scratch, grid, side):
    if side is None:
        return body
    ns_in, ns_out = len(side.ins), len(side.outs)
    n_steps = 1
    for g in grid:
        n_steps *= g

    def wrapped(*refs):
        pos = n_in + ns_in
        ins, s_in = refs[:n_in], refs[n_in:pos]
        outs, s_out = refs[pos:pos + n_out], refs[pos + n_out:pos + n_out + ns_out]
        pos += n_out + ns_out
        scratch, sems = refs[pos:pos + n_scratch], refs[pos + n_scratch:]
        step = pl.program_id(0)
        for ax in range(1, len(grid)):
            step = step * grid[ax] + pl.program_id(ax)
        start, mid, finish = side.phases(s_in, s_out, *sems)
        pl.when(step == 0)(start)
        body(*ins, *outs, *scratch)
        pl.when(step == (3 * n_steps) // 4)(mid)
        pl.when(step == n_steps - 1)(finish)

    return wrapped


def _side_specs(side):
    if side is None:
        return [], [], [], [], []
    hbm = pl.BlockSpec(memory_space=pltpu.HBM)
    return ([hbm] * len(side.ins), list(side.ins), [hbm] * len(side.outs), list(side.outs), side.scratch())


def _one(res, side):
    return res[0] if side is None else (res[0][0], res[1])


def _store_epilogue(parts, ex, out):
    out[0][...] = parts[0].astype(out[0].dtype)


def _residual_epilogue(parts, ex, out):
    out[0][...] = (parts[0] + ex[0][...]).astype(out[0].dtype)


def _rmsnorm_fwd(x, g, name, side=None):
    rows, d = x.shape
    tm = _row_tile(rows, 512)
    grid = (rows // tm,)

    def body(x_ref, g_ref, o_ref):
        xf = x_ref[...]
        r = lax.rsqrt(jnp.mean(xf * xf, axis=-1, keepdims=True) + NORM_EPS)
        o_ref[...] = (xf * r * g_ref[...]).astype(o_ref.dtype)

    s_in, s_args, s_out, s_shapes, s_scratch = _side_specs(side)
    res = pl.pallas_call(
        _carrying(body, 2, 1, 0, grid, side), grid=grid,
        in_specs=[pl.BlockSpec((tm, d), lambda i: (i, 0)), pl.BlockSpec((1, d), lambda i: (0, 0))] + s_in,
        out_specs=[pl.BlockSpec((tm, d), lambda i: (i, 0))] + s_out,
        out_shape=[jax.ShapeDtypeStruct((rows, d), BF16)] + s_shapes,
        scratch_shapes=s_scratch, compiler_params=_params(), name=name,
    )(x, g, *s_args)
    return res[0] if side is None else (res[0], res[1:])


def _rmsnorm_bwd(x, g, dh, dres, name, out_dtype=BF16):
    rows, d = x.shape
    tm = _row_tile(rows, 256)
    has_res = dres is not None

    def body(*refs):
        if has_res:
            x_ref, g_ref, dh_ref, dres_ref, dx_ref, dg_ref = refs
        else:
            x_ref, g_ref, dh_ref, dx_ref, dg_ref = refs
        i = pl.program_id(0)
        xf = x_ref[...]
        r = lax.rsqrt(jnp.mean(xf * xf, axis=-1, keepdims=True) + NORM_EPS)
        xhat = xf * r
        dy = dh_ref[...].astype(F32)
        gdy = dy * g_ref[...]
        c = jnp.mean(gdy * xhat, axis=-1, keepdims=True)
        dx = r * (gdy - xhat * c)
        if has_res:
            dx = dx + dres_ref[...].astype(F32)
        dx_ref[...] = dx.astype(out_dtype)

        @pl.when(i == 0)
        def _():
            dg_ref[...] = jnp.zeros_like(dg_ref)

        dg_ref[...] += jnp.sum(dy * xhat, axis=0, keepdims=True)

    row_spec = pl.BlockSpec((tm, d), lambda i: (i, 0))
    vec_spec = pl.BlockSpec((1, d), lambda i: (0, 0))
    in_specs = [row_spec, vec_spec, row_spec] + ([row_spec] if has_res else [])
    args = [x, g, dh] + ([dres] if has_res else [])
    return pl.pallas_call(
        body, grid=(rows // tm,), in_specs=in_specs,
        out_specs=[row_spec, vec_spec],
        out_shape=[jax.ShapeDtypeStruct((rows, d), out_dtype), jax.ShapeDtypeStruct((1, d), F32)],
        compiler_params=_params(), name=name,
    )(*args)


def _loss_head(x, g, target):
    rows, d = x.shape
    tm = _row_tile(rows, 256)

    def body(x_ref, g_ref, t_ref, loss_ref, dxb_ref, dg_ref):
        i = pl.program_id(0)
        xf = x_ref[...]
        gv = g_ref[...]
        r = lax.rsqrt(jnp.mean(xf * xf, axis=-1, keepdims=True) + NORM_EPS)
        xhat = xf * r
        err = xhat * gv - t_ref[...]
        row_loss = jnp.mean(err * err, axis=-1, keepdims=True)
        dy = err * (1.0 / d)
        gdy = dy * gv
        c = jnp.mean(gdy * xhat, axis=-1, keepdims=True)
        dxb_ref[...] = (r * (gdy - xhat * c)).astype(BF16)

        @pl.when(i == 0)
        def _():
            dg_ref[...] = jnp.zeros_like(dg_ref)
            loss_ref[...] = jnp.zeros_like(loss_ref)

        dg_ref[...] += jnp.sum(dy * xhat, axis=0, keepdims=True)
        loss_ref[...] += 0.5 * jnp.sum(row_loss, axis=0, keepdims=True)

    row_spec = pl.BlockSpec((tm, d), lambda i: (i, 0))
    vec_spec = pl.BlockSpec((1, d), lambda i: (0, 0))
    return pl.pallas_call(
        body, grid=(rows // tm,), in_specs=[row_spec, vec_spec, row_spec],
        out_specs=[pl.BlockSpec((1, 1), lambda i: (0, 0)), row_spec, vec_spec],
        out_shape=[jax.ShapeDtypeStruct((1, 1), F32), jax.ShapeDtypeStruct((rows, d), BF16),
                   jax.ShapeDtypeStruct((1, d), F32)],
        compiler_params=_params(), name="loss_head",
    )(x, g, target)


def _rotary_tables(positions):
    inv_freq = ROPE_THETA ** (-jnp.arange(ROT_HALF, dtype=F32) / ROT_HALF)
    ang = positions.astype(F32)[:, None] * inv_freq
    cos, sin = jnp.cos(ang), jnp.sin(ang)
    s = positions.shape[0]
    z = jnp.zeros((s, HEAD - 2 * ROT_HALF), F32)
    z16 = jnp.zeros((s, ROT_HALF), F32)
    c = jnp.concatenate([cos, cos, jnp.ones_like(z)], axis=1)
    s1 = jnp.concatenate([z16, sin, z], axis=1)
    s2 = jnp.concatenate([-sin, z16, z], axis=1)
    return c, s1, s2


ATTN_PROJ_HEADS_PER_STEP = 10


def _attn_in_proj(h, w_t, tabs, side=None):
    s, d = h.shape
    n = w_t.shape[0]
    tn = ATTN_PROJ_HEADS_PER_STEP * HEAD
    n_rot_heads = 2 * N_GROUPS * HPG
    n_rot_steps = -(-n_rot_heads // ATTN_PROJ_HEADS_PER_STEP)
    tm = _row_tile(s, 1024)

    def epilogue(parts, ex, out):
        j = pl.program_id(0)
        acc = parts[0]

        @pl.when(j < n_rot_steps)
        def _():
            c, s1, s2 = ex[0][...], ex[1][...], ex[2][...]
            for t in range(ATTN_PROJ_HEADS_PER_STEP):
                seg = acc[:, t * HEAD:(t + 1) * HEAD]
                rot = seg * c + pltpu.roll(seg, ROT_HALF, 1) * s1 + pltpu.roll(seg, HEAD - ROT_HALF, 1) * s2
                is_rot = (j * ATTN_PROJ_HEADS_PER_STEP + t) < n_rot_heads
                out[0][:, t * HEAD:(t + 1) * HEAD] = jnp.where(is_rot, rot, seg).astype(BF16)

        @pl.when(j >= n_rot_steps)
        def _():
            out[0][...] = acc.astype(BF16)

    tab_spec = pl.BlockSpec((tm, HEAD), lambda j, m, k: (m, 0))
    return _one(_matmul(
        "attn_in_proj", (n // tn, s // tm, 1),
        [(h, pl.BlockSpec((tm, d), lambda j, m, k: (m, 0)),
          w_t, pl.BlockSpec((tn, d), lambda j, m, k: (j, 0)), "nt", 0)],
        1, None, [(tabs[0], tab_spec), (tabs[1], tab_spec), (tabs[2], tab_spec)],
        [(jax.ShapeDtypeStruct((s, n), BF16), pl.BlockSpec((tm, tn), lambda j, m, k: (m, j)))],
        epilogue, side), side)


ATT_TILE_BLOCKS = 4


def _att_blocks(seq_blocks):
    return min(ATT_TILE_BLOCKS, seq_blocks)


def _band_masks():
    qi = lax.broadcasted_iota(jnp.int32, (BLK, BLK), 0)
    ki = lax.broadcasted_iota(jnp.int32, (BLK, BLK), 1)
    return ki <= qi, ki >= qi


def _attn_fwd(q_arr, k_arr, v_arr, offs, seq_blocks, name, side=None):
    s = q_arr.shape[0]
    qo, ko, vo = offs
    nb = _att_blocks(seq_blocks)

    def body(q_ref, kc_ref, kp_ref, vc_ref, vp_ref, o_ref, lse_ref):
        n = pl.program_id(0)
        tile_starts_seq = (n * nb) % seq_blocks == 0
        mask_c, mask_p = _band_masks()
        pairs = [(b * BLK, h * HEAD) for b in range(nb) for h in range(HPG)]

        def keys_prev(ref_c, ref_p, r0, c0):
            return ref_p[:, c0:c0 + HEAD] if r0 == 0 else ref_c[r0 - BLK:r0, c0:c0 + HEAD]

        s_c, s_p = [], []
        for r0, c0 in pairs:
            q = q_ref[r0:r0 + BLK, c0:c0 + HEAD]
            s_c.append(jnp.where(mask_c, _dot(q, kc_ref[r0:r0 + BLK, c0:c0 + HEAD], "nt") * SCALE, NEG_INF))
            sp = jnp.where(mask_p, _dot(q, keys_prev(kc_ref, kp_ref, r0, c0), "nt") * SCALE, NEG_INF)
            s_p.append(jnp.where(tile_starts_seq, NEG_INF, sp) if r0 == 0 else sp)
        m = [jnp.maximum(jnp.max(a, axis=-1, keepdims=True), jnp.max(b, axis=-1, keepdims=True))
             for a, b in zip(s_c, s_p)]
        p_c = [jnp.exp(a - mm) for a, mm in zip(s_c, m)]
        p_p = [jnp.exp(a - mm) for a, mm in zip(s_p, m)]
        l = [jnp.sum(a, axis=-1, keepdims=True) + jnp.sum(b, axis=-1, keepdims=True) for a, b in zip(p_c, p_p)]
        inv = [1.0 / ll for ll in l]
        for i, (r0, c0) in enumerate(pairs):
            o = (_dot((p_c[i] * inv[i]).astype(BF16), vc_ref[r0:r0 + BLK, c0:c0 + HEAD])
                 + _dot((p_p[i] * inv[i]).astype(BF16), keys_prev(vc_ref, vp_ref, r0, c0)))
            o_ref[r0:r0 + BLK, c0:c0 + HEAD] = o.astype(BF16)
            lse_ref[r0:r0 + BLK, c0:c0 + HEAD] = jnp.broadcast_to(m[i] + jnp.log(l[i]), (BLK, HEAD))

    def cur(off):
        return pl.BlockSpec((nb * BLK, GROUP_W), lambda n: (n, off))

    def prev(off):
        return pl.BlockSpec((BLK, GROUP_W), lambda n: (jnp.maximum(n * nb - 1, 0), off))

    grid = (s // (nb * BLK),)
    s_in, s_args, s_out, s_shapes, s_scratch = _side_specs(side)
    return pl.pallas_call(
        _carrying(body, 5, 2, 0, grid, side), grid=grid,
        in_specs=[cur(qo), cur(ko), prev(ko), cur(vo), prev(vo)] + s_in,
        out_specs=[cur(0), cur(0)] + s_out,
        out_shape=[jax.ShapeDtypeStruct((s, GROUP_W), BF16), jax.ShapeDtypeStruct((s, GROUP_W), F32)] + s_shapes,
        scratch_shapes=s_scratch, compiler_params=_params(), name=name,
    )(q_arr, k_arr, k_arr, v_arr, v_arr, *s_args)


def _attn_bwd(q_arr, k_arr, v_arr, offs, do, lse, delta, seq_blocks, name):
    s = q_arr.shape[0]
    qo, ko, vo = offs
    nb = _att_blocks(seq_blocks)
    n_blocks = s // BLK

    def body(qc_ref, qn_ref, kc_ref, kp_ref, vc_ref, vp_ref, doc_ref, don_ref, lsec_ref, lsen_ref,
             dlc_ref, dln_ref, out_ref):
        n = pl.program_id(0)
        tile_starts_seq = (n * nb) % seq_blocks == 0
        next_in_seq = ((n + 1) * nb) % seq_blocks != 0
        mask_c, mask_p = _band_masks()

        def blk(ref_c, ref_edge, b, c0):
            if b < 0 or b >= nb:
                return ref_edge[:, c0:c0 + HEAD]
            return ref_c[b * BLK:(b + 1) * BLK, c0:c0 + HEAD]

        heads = [h * HEAD for h in range(HPG)]
        own = [(b, c0) for b in range(nb) for c0 in heads]
        cross = [(c, c0) for c in range(nb + 1) for c0 in heads]
        s_o, dp_o, s_x, dp_x = [], [], [], []
        for b, c0 in own:
            s_o.append(jnp.where(mask_c, _dot(blk(qc_ref, None, b, c0), blk(kc_ref, None, b, c0), "nt") * SCALE,
                                 NEG_INF))
            dp_o.append(_dot(blk(doc_ref, None, b, c0), blk(vc_ref, None, b, c0), "nt"))
        for c, c0 in cross:
            sx = jnp.where(mask_p, _dot(blk(qc_ref, qn_ref, c, c0), blk(kc_ref, kp_ref, c - 1, c0), "nt") * SCALE,
                           NEG_INF)
            if c == 0:
                sx = jnp.where(tile_starts_seq, NEG_INF, sx)
            if c == nb:
                sx = jnp.where(next_in_seq, sx, NEG_INF)
            s_x.append(sx)
            dp_x.append(_dot(blk(doc_ref, don_ref, c, c0), blk(vc_ref, vp_ref, c - 1, c0), "nt"))
        p_o, ds_o, p_x, ds_x = [], [], [], []
        for i, (b, c0) in enumerate(own):
            p = jnp.exp(s_o[i] - blk(lsec_ref, None, b, c0))
            ds_o.append((p * (dp_o[i] - blk(dlc_ref, None, b, c0)) * SCALE).astype(BF16))
            p_o.append(p.astype(BF16))
        for i, (c, c0) in enumerate(cross):
            p = jnp.exp(s_x[i] - blk(lsec_ref, lsen_ref, c, c0))
            ds_x.append((p * (dp_x[i] - blk(dlc_ref, dln_ref, c, c0)) * SCALE).astype(BF16))
            p_x.append(p.astype(BF16))
        for i, (b, c0) in enumerate(own):
            xq = b * HPG + i % HPG
            xk = (b + 1) * HPG + i % HPG
            dq = _dot(ds_o[i], blk(kc_ref, None, b, c0)) + _dot(ds_x[xq], blk(kc_ref, kp_ref, b - 1, c0))
            dk = (_dot(ds_o[i], blk(qc_ref, None, b, c0), "tn")
                  + _dot(ds_x[xk], blk(qc_ref, qn_ref, b + 1, c0), "tn"))
            dv = (_dot(p_o[i], blk(doc_ref, None, b, c0), "tn")
                  + _dot(p_x[xk], blk(doc_ref, don_ref, b + 1, c0), "tn"))
            r0 = b * BLK
            out_ref[r0:r0 + BLK, c0:c0 + HEAD] = dq.astype(BF16)
            out_ref[r0:r0 + BLK, GROUP_W + c0:GROUP_W + c0 + HEAD] = dk.astype(BF16)
            out_ref[r0:r0 + BLK, 2 * GROUP_W + c0:2 * GROUP_W + c0 + HEAD] = dv.astype(BF16)

    def cur(off):
        return pl.BlockSpec((nb * BLK, GROUP_W), lambda n: (n, off))

    def prev(off):
        return pl.BlockSpec((BLK, GROUP_W), lambda n: (jnp.maximum(n * nb - 1, 0), off))

    def nxt(off):
        return pl.BlockSpec((BLK, GROUP_W), lambda n: (jnp.minimum((n + 1) * nb, n_blocks - 1), off))

    return pl.pallas_call(
        body, grid=(s // (nb * BLK),),
        in_specs=[cur(qo), nxt(qo), cur(ko), prev(ko), cur(vo), prev(vo), cur(0), nxt(0), cur(0), nxt(0),
                  cur(0), nxt(0)],
        out_specs=pl.BlockSpec((nb * BLK, 3 * GROUP_W), lambda n: (n, 0)),
        out_shape=jax.ShapeDtypeStruct((s, 3 * GROUP_W), BF16),
        compiler_params=_params(), name=name,
    )(q_arr, q_arr, k_arr, k_arr, v_arr, v_arr, do, do, lse, lse, delta, delta)


def _merge_weights(lse_refs, c0):
    ls = [r[:, c0:c0 + HEAD] for r in lse_refs]
    m = jnp.maximum(jnp.maximum(ls[0], ls[1]), ls[2])
    es = [jnp.exp(l - m) for l in ls]
    inv = 1.0 / (es[0] + es[1] + es[2])
    return [e * inv for e in es]


def _merge_fwd(os_, lses):
    s = os_[0].shape[0]
    tm = _row_tile(s, 512)

    def body(o0, o1, o2, l0, l1, l2, out_ref):
        for h in range(HPG):
            c0 = h * HEAD
            w = _merge_weights((l0, l1, l2), c0)
            acc = None
            for wg, o in zip(w, (o0, o1, o2)):
                t = wg * o[:, c0:c0 + HEAD].astype(F32)
                acc = t if acc is None else acc + t
            out_ref[:, c0:c0 + HEAD] = acc.astype(BF16)

    spec = pl.BlockSpec((tm, GROUP_W), lambda i: (i, 0))
    return pl.pallas_call(
        body, grid=(s // tm,), in_specs=[spec] * 6, out_specs=spec,
        out_shape=jax.ShapeDtypeStruct((s, 2 * GROUP_W), BF16),
        compiler_params=_params(), name="merge_fwd",
    )(*os_, *lses)


def _merge_bwd(dcat, os_, lses):
    s = os_[0].shape[0]
    tm = _row_tile(s, 512)

    def body(d_ref, o0, o1, o2, l0, l1, l2, do0, do1, do2, dl0, dl1, dl2):
        for h in range(HPG):
            c0 = h * HEAD
            w = _merge_weights((l0, l1, l2), c0)
            dm = d_ref[:, c0:c0 + HEAD].astype(F32)
            merged = None
            for wg, o in zip(w, (o0, o1, o2)):
                t = wg * o[:, c0:c0 + HEAD].astype(F32)
                merged = t if merged is None else merged + t
            abar = jnp.sum(dm * merged, axis=-1, keepdims=True)
            for wg, do_ref, dl_ref in zip(w, (do0, do1, do2), (dl0, dl1, dl2)):
                do_ref[:, c0:c0 + HEAD] = (wg * dm).astype(BF16)
                dl_ref[:, c0:c0 + HEAD] = wg * abar

    spec = pl.BlockSpec((tm, GROUP_W), lambda i: (i, 0))
    return pl.pallas_call(
        body, grid=(s // tm,), in_specs=[spec] * 7, out_specs=[spec] * 6,
        out_shape=[jax.ShapeDtypeStruct((s, GROUP_W), BF16)] * 3 + [jax.ShapeDtypeStruct((s, GROUP_W), F32)] * 3,
        compiler_params=_params(), name="merge_bwd",
    )(dcat, *os_, *lses)


def _assemble_dproj(dqkv, dqm, tabs):
    s = dqm.shape[0]
    tm = _row_tile(s, 256)
    width = 3 * N_GROUPS * GROUP_W + GROUP_W

    def body(d0, d1, d2, dm_ref, c_ref, s1_ref, s2_ref, out_ref):
        c, s1, s2 = c_ref[...], s1_ref[...], s2_ref[...]
        for g, d_ref in enumerate((d0, d1, d2)):
            for part in range(3):
                for h in range(HPG):
                    src = part * GROUP_W + h * HEAD
                    dst = part * N_GROUPS * GROUP_W + g * GROUP_W + h * HEAD
                    seg = d_ref[:, src:src + HEAD]
                    if part < 2:
                        t = seg.astype(F32)
                        t = t * c - pltpu.roll(t, HEAD - ROT_HALF, 1) * s2 - pltpu.roll(t, ROT_HALF, 1) * s1
                        seg = t.astype(BF16)
                    out_ref[:, dst:dst + HEAD] = seg
        out_ref[:, 3 * N_GROUPS * GROUP_W:] = dm_ref[...]

    g_spec = pl.BlockSpec((tm, 3 * GROUP_W), lambda i: (i, 0))
    m_spec = pl.BlockSpec((tm, GROUP_W), lambda i: (i, 0))
    t_spec = pl.BlockSpec((tm, HEAD), lambda i: (i, 0))
    return pl.pallas_call(
        body, grid=(s // tm,), in_specs=[g_spec] * 3 + [m_spec] + [t_spec] * 3,
        out_specs=pl.BlockSpec((tm, width), lambda i: (i, 0)),
        out_shape=jax.ShapeDtypeStruct((s, width), BF16),
        compiler_params=_params(), name="assemble_dproj",
    )(*dqkv, dqm, *tabs)


def _mem_softmax(q, k):
    s = _dot(q, k, "nt") * SCALE
    m = jnp.max(s, axis=-1, keepdims=True)
    p = jnp.exp(s - m)
    return p * (1.0 / jnp.sum(p, axis=-1, keepdims=True))


def _memattn_fwd(q_arr, q_off, kv, into, into_off, name):
    s = q_arr.shape[0]
    mlen = kv.shape[0]
    tq = _row_tile(s, 512)

    def body(q_ref, kv_ref, into_ref, o_ref):
        for h in range(HPG):
            c0 = h * HEAD
            p = _mem_softmax(q_ref[:, c0:c0 + HEAD], kv_ref[:, c0:c0 + HEAD])
            o = _dot(p.astype(BF16), kv_ref[:, GROUP_W + c0:GROUP_W + c0 + HEAD])
            o_ref[:, c0:c0 + HEAD] = o.astype(BF16)

    return pl.pallas_call(
        body, grid=(s // tq,),
        in_specs=[pl.BlockSpec((tq, GROUP_W), lambda i: (i, q_off)),
                  pl.BlockSpec((mlen, 2 * GROUP_W), lambda i: (0, 0)),
                  pl.BlockSpec(memory_space=pl.ANY)],
        out_specs=pl.BlockSpec((tq, GROUP_W), lambda i: (i, into_off)),
        out_shape=jax.ShapeDtypeStruct(into.shape, into.dtype),
        input_output_aliases={2: 0},
        compiler_params=_params(), name=name,
    )(q_arr, kv, into)


def _memattn_bwd(q_arr, q_off, kv, dcat, d_off, name, into=None, into_off=0):
    s = q_arr.shape[0]
    mlen = kv.shape[0]
    tq = _row_tile(s, 512)

    def body(*refs):
        q_ref, kv_ref, d_ref = refs[:3]
        dq_ref, dkv_ref = refs[-2:]
        i = pl.program_id(0)

        @pl.when(i == 0)
        def _():
            dkv_ref[...] = jnp.zeros_like(dkv_ref)

        for h in range(HPG):
            c0 = h * HEAD
            q = q_ref[:, c0:c0 + HEAD]
            k = kv_ref[:, c0:c0 + HEAD]
            v = kv_ref[:, GROUP_W + c0:GROUP_W + c0 + HEAD]
            do = d_ref[:, c0:c0 + HEAD]
            p = _mem_softmax(q, k)
            dp = _dot(do, v, "nt")
            ds = p * (dp - jnp.sum(p * dp, axis=-1, keepdims=True)) * SCALE
            dsb = ds.astype(BF16)
            dq_ref[:, c0:c0 + HEAD] = _dot(dsb, k).astype(BF16)
            dkv_ref[:, c0:c0 + HEAD] += _dot(dsb, q, "tn")
            dkv_ref[:, GROUP_W + c0:GROUP_W + c0 + HEAD] += _dot(p.astype(BF16), do, "tn")

    in_specs = [pl.BlockSpec((tq, GROUP_W), lambda i: (i, q_off)),
                pl.BlockSpec((mlen, 2 * GROUP_W), lambda i: (0, 0)),
                pl.BlockSpec((tq, GROUP_W), lambda i: (i, d_off))]
    args = [q_arr, kv, dcat]
    dq_shape = jax.ShapeDtypeStruct((s, GROUP_W), BF16)
    aliases = {}
    if into is not None:
        in_specs.append(pl.BlockSpec(memory_space=pl.ANY))
        args.append(into)
        dq_shape = jax.ShapeDtypeStruct(into.shape, into.dtype)
        aliases = {3: 0}
    return pl.pallas_call(
        body, grid=(s // tq,), in_specs=in_specs,
        out_specs=[pl.BlockSpec((tq, GROUP_W), lambda i: (i, into_off)),
                   pl.BlockSpec((mlen, 2 * GROUP_W), lambda i: (0, 0))],
        out_shape=[dq_shape, jax.ShapeDtypeStruct((mlen, 2 * GROUP_W), F32)],
        input_output_aliases=aliases, compiler_params=_params(), name=name,
    )(*args)


SGU_TILE = 256


def _sgu_norm(vg):
    mu = jnp.mean(vg, axis=-1, keepdims=True)
    xc = vg - mu
    var = jnp.mean(xc * xc, axis=-1, keepdims=True)
    rstd = lax.rsqrt(var + LN_EPS)
    return xc * rstd, rstd


def _tril_mask():
    r = lax.broadcasted_iota(jnp.int32, (BLK, BLK), 0)
    c = lax.broadcasted_iota(jnp.int32, (BLK, BLK), 1)
    return r >= c


def _sgu_fwd(proj, ln_g, ln_b, w_s, b_st):
    s = proj.shape[0]
    ts = _row_tile(s, SGU_TILE)

    def body(u_ref, v_ref, g_ref, b_ref, ws_ref, bst_ref, o_ref):
        ug = _gelu(u_ref[...].astype(F32))
        xhat, _ = _sgu_norm(_gelu(v_ref[...].astype(F32)))
        vn = (xhat * g_ref[...] + b_ref[...]).astype(BF16)
        tri = _tril_mask()
        for g in range(SGU_GROUPS):
            c0 = g * HEAD
            w = jnp.where(tri, ws_ref[g], 0.0).astype(BF16)
            bias = bst_ref[:, g:g + 1]
            for ch in range(ts // BLK):
                r0 = ch * BLK
                mixed = _dot(w, vn[r0:r0 + BLK, c0:c0 + HEAD]) + bias
                o_ref[r0:r0 + BLK, c0:c0 + HEAD] = (ug[r0:r0 + BLK, c0:c0 + HEAD] * mixed).astype(BF16)

    vec = pl.BlockSpec((1, SGU_W), lambda i: (0, 0))
    return pl.pallas_call(
        body, grid=(s // ts,),
        in_specs=[pl.BlockSpec((ts, SGU_W), lambda i: (i, 0)), pl.BlockSpec((ts, SGU_W), lambda i: (i, 1)),
                  vec, vec, pl.BlockSpec((SGU_GROUPS, BLK, BLK), lambda i: (0, 0, 0)),
                  pl.BlockSpec((BLK, SGU_GROUPS), lambda i: (0, 0))],
        out_specs=pl.BlockSpec((ts, SGU_W), lambda i: (i, 0)),
        out_shape=jax.ShapeDtypeStruct((s, SGU_W + GROUP_W), BF16),
        compiler_params=_params(), name="sgu_fwd",
    )(proj, proj, ln_g, ln_b, w_s, b_st)


def _sgu_bwd(proj, dcat, ln_g, ln_b, w_s, b_st):
    s = proj.shape[0]
    ts = _row_tile(s, SGU_TILE)

    def body(u_ref, v_ref, d_ref, g_ref, b_ref, ws_ref, bst_ref,
             duv_ref, dws_ref, dbs_ref, dg_ref, db_ref, dvn_ref):
        i = pl.program_id(0)

        @pl.when(i == 0)
        def _():
            dws_ref[...] = jnp.zeros_like(dws_ref)
            dbs_ref[...] = jnp.zeros_like(dbs_ref)
            dg_ref[...] = jnp.zeros_like(dg_ref)
            db_ref[...] = jnp.zeros_like(db_ref)

        u = u_ref[...].astype(F32)
        v = v_ref[...].astype(F32)
        ug, dug = _gelu_and_grad(u)
        vg, dvg_dv = _gelu_and_grad(v)
        xhat, rstd = _sgu_norm(vg)
        lng = g_ref[...]
        vn = (xhat * lng + b_ref[...]).astype(BF16)
        dout = d_ref[...].astype(F32)
        tri = _tril_mask()
        lane = lax.broadcasted_iota(jnp.int32, (BLK, BLK), 1)
        dbs = jnp.zeros((BLK, BLK), F32)
        for g in range(SGU_GROUPS):
            c0 = g * HEAD
            w = jnp.where(tri, ws_ref[g], 0.0).astype(BF16)
            bias = bst_ref[:, g:g + 1]
            dws = jnp.zeros((BLK, BLK), F32)
            for ch in range(ts // BLK):
                r0 = ch * BLK
                vn_gc = vn[r0:r0 + BLK, c0:c0 + HEAD]
                mixed = _dot(w, vn_gc) + bias
                do_gc = dout[r0:r0 + BLK, c0:c0 + HEAD]
                dmixed = do_gc * ug[r0:r0 + BLK, c0:c0 + HEAD]
                du = do_gc * mixed * dug[r0:r0 + BLK, c0:c0 + HEAD]
                duv_ref[r0:r0 + BLK, c0:c0 + HEAD] = du.astype(BF16)
                dmb = dmixed.astype(BF16)
                dws = dws + _dot(dmb, vn_gc, "nt")
                dbs = dbs + jnp.where(lane == g, jnp.sum(dmixed, axis=-1, keepdims=True), 0.0)
                dvn_ref[r0:r0 + BLK, c0:c0 + HEAD] = _dot(w, dmb, "tn")
            dws_ref[g] += jnp.where(tri, dws, 0.0)
        dbs_ref[...] += dbs
        dvn = dvn_ref[...]
        gd = dvn * lng
        c1 = jnp.mean(gd, axis=-1, keepdims=True)
        c2 = jnp.mean(gd * xhat, axis=-1, keepdims=True)
        dvg = rstd * (gd - c1 - xhat * c2)
        duv_ref[:, SGU_W:] = (dvg * dvg_dv).astype(BF16)
        dg_ref[...] += jnp.sum(dvn * xhat, axis=0, keepdims=True)
        db_ref[...] += jnp.sum(dvn, axis=0, keepdims=True)

    vec = pl.BlockSpec((1, SGU_W), lambda i: (0, 0))
    ws_spec = pl.BlockSpec((SGU_GROUPS, BLK, BLK), lambda i: (0, 0, 0))
    return pl.pallas_call(
        body, grid=(s // ts,),
        in_specs=[pl.BlockSpec((ts, SGU_W), lambda i: (i, 0)), pl.BlockSpec((ts, SGU_W), lambda i: (i, 1)),
                  pl.BlockSpec((ts, SGU_W), lambda i: (i, 0)),
                  vec, vec, ws_spec, pl.BlockSpec((BLK, SGU_GROUPS), lambda i: (0, 0))],
        out_specs=[pl.BlockSpec((ts, 2 * SGU_W), lambda i: (i, 0)), ws_spec,
                   pl.BlockSpec((BLK, BLK), lambda i: (0, 0)), vec, vec],
        out_shape=[jax.ShapeDtypeStruct((s, 2 * SGU_W + GROUP_W), BF16),
                   jax.ShapeDtypeStruct((SGU_GROUPS, BLK, BLK), F32),
                   jax.ShapeDtypeStruct((BLK, BLK), F32),
                   jax.ShapeDtypeStruct((1, SGU_W), F32), jax.ShapeDtypeStruct((1, SGU_W), F32)],
        scratch_shapes=[pltpu.VMEM((ts, SGU_W), F32)],
        compiler_params=_params(), name="sgu_bwd",
    )(proj, proj, dcat, ln_g, ln_b, w_s, b_st)


def _swiglu_fwd(h, wg_t, wu_t, name, side=None):
    s, d = h.shape
    f = wg_t.shape[0]
    tm, tn = _row_tile(s, 1024), _row_tile(f, 512)

    def epilogue(parts, ex, out):
        g, u = parts
        sg = _sigmoid(g)
        silu = g * sg
        out[0][...] = silu.astype(BF16)
        out[1][...] = (u * (sg + silu * (1.0 - sg))).astype(BF16)
        out[2][...] = (silu * u).astype(BF16)

    a_spec = pl.BlockSpec((tm, d), lambda n, m, k: (m, 0))
    w_spec = pl.BlockSpec((tn, d), lambda n, m, k: (n, 0))
    o_spec = pl.BlockSpec((tm, tn), lambda n, m, k: (m, n))
    o_shape = jax.ShapeDtypeStruct((s, f), BF16)
    return _matmul(name, (f // tn, s // tm, 1),
                   [(h, a_spec, wg_t, w_spec, "nt", 0), (h, a_spec, wu_t, w_spec, "nt", 1)],
                   2, None, [], [(o_shape, o_spec)] * 3, epilogue, side)


def _swiglu_down(hid, wd, res, name, side=None):
    s, f = hid.shape
    d = wd.shape[1]
    tm, tn = _row_tile(s, 1024), _row_tile(d, 512)
    o_spec = pl.BlockSpec((tm, tn), lambda n, m, k: (m, n))
    return _one(_matmul(
        name, (d // tn, s // tm, 1),
        [(hid, pl.BlockSpec((tm, f), lambda n, m, k: (m, 0)),
          wd, pl.BlockSpec((f, tn), lambda n, m, k: (0, n)), "nn", 0)],
        1, None, [(res, o_spec)], [(jax.ShapeDtypeStruct((s, d), F32), o_spec)],
        _residual_epilogue, side), side)


def _swiglu_bwd_hidden(dxb, wd, silu, up_dsilu, name):
    s, f = silu.shape
    d = dxb.shape[1]
    tm, tn = _row_tile(s, 2048), _row_tile(f, 512)

    def epilogue(parts, ex, out):
        dh = parts[0]
        out[0][...] = (dh * ex[1][...].astype(F32)).astype(BF16)
        out[1][...] = (dh * ex[0][...].astype(F32)).astype(BF16)

    blk = pl.BlockSpec((tm, tn), lambda m, n, k: (m, n))
    o_shape = jax.ShapeDtypeStruct((s, f), BF16)
    return _matmul(
        name, (s // tm, f // tn, 1),
        [(dxb, pl.BlockSpec((tm, d), lambda m, n, k: (m, 0)),
          wd, pl.BlockSpec((tn, d), lambda m, n, k: (n, 0)), "nt", 0)],
        1, None, [(silu, blk), (up_dsilu, blk)], [(o_shape, blk)] * 2, epilogue)


def _swiglu_bwd_input(dgate, dup, wg_t, wu_t, name, side=None):
    s, f = dgate.shape
    d = wg_t.shape[1]
    tm, tn, tk = _row_tile(s, 1024), _row_tile(d, 512), f // 2
    a_spec = pl.BlockSpec((tm, tk), lambda n, m, k: (m, k))
    w_spec = pl.BlockSpec((tk, tn), lambda n, m, k: (k, n))
    return _one(_matmul(
        name, (d // tn, s // tm, f // tk),
        [(dgate, a_spec, wg_t, w_spec, "nn", 0), (dup, a_spec, wu_t, w_spec, "nn", 0)],
        1, (tm, tn), [],
        [(jax.ShapeDtypeStruct((s, d), BF16), pl.BlockSpec((tm, tn), lambda n, m, k: (m, n)))],
        _store_epilogue, side), side)


def _mm_tn_full(a, b, name, side=None):
    s, m = a.shape
    n = b.shape[1]
    tm, tn = _row_tile(m, 512), _row_tile(n, 512)
    return _one(_matmul(
        name, (m // tm, n // tn, 1),
        [(a, pl.BlockSpec((s, tm), lambda i, j, k: (0, i)),
          b, pl.BlockSpec((s, tn), lambda i, j, k: (0, j)), "tn", 0)],
        1, None, [],
        [(jax.ShapeDtypeStruct((m, n), BF16), pl.BlockSpec((tm, tn), lambda i, j, k: (i, j)))],
        _store_epilogue, side), side)


def _mm_nn(a, b, name, tn, out_dtype=BF16, res=None, side=None):
    m, k = a.shape
    n = b.shape[1]
    tm = _row_tile(m, 1024)
    extras = [] if res is None else [(res, pl.BlockSpec((tm, tn), lambda j, i, kk: (i, j)))]
    return _one(_matmul(
        name, (n // tn, m // tm, 1),
        [(a, pl.BlockSpec((tm, k), lambda j, i, kk: (i, 0)),
          b, pl.BlockSpec((k, tn), lambda j, i, kk: (0, j)), "nn", 0)],
        1, None, extras,
        [(jax.ShapeDtypeStruct((m, n), out_dtype), pl.BlockSpec((tm, tn), lambda j, i, kk: (i, j)))],
        _store_epilogue if res is None else _residual_epilogue, side), side)


def _mm_nn_colblocked(a, b_blk, name, res, side=None):
    m, k = a.shape
    nb, _, bw = b_blk.shape
    tm = _row_tile(m, 2048)
    o_spec = pl.BlockSpec((tm, bw), lambda j, i, kk: (i, j))
    return _one(_matmul(
        name, (nb, m // tm, 1),
        [(a, pl.BlockSpec((tm, k), lambda j, i, kk: (i, 0)),
          b_blk, pl.BlockSpec((None, k, bw), lambda j, i, kk: (j, 0, 0)), "nn", 0)],
        1, None, [(res, o_spec)],
        [(jax.ShapeDtypeStruct((m, nb * bw), F32), o_spec)], _residual_epilogue, side), side)


def _mm_nt_colblocked(a, b_blk, name, out_dtype, jb, side=None):
    m = a.shape[0]
    nb, n, bw = b_blk.shape
    tm = _row_tile(m, 512)
    return _one(_matmul(
        name, (m // tm, nb // jb),
        [(a, pl.BlockSpec((tm, jb * bw), lambda i, j: (i, j)),
          b_blk, pl.BlockSpec((jb, n, bw), lambda i, j: (j, 0, 0)), "nt", 0)],
        1, (tm, n), [],
        [(jax.ShapeDtypeStruct((m, n), out_dtype), pl.BlockSpec((tm, n), lambda i, j: (i, 0)))],
        _store_epilogue, side), side)


def _mm_nt_rowblocked(a, b, name, tn, out_dtype, side=None):
    m, k = a.shape
    n = b.shape[0]
    tm = _row_tile(m, 1024)
    return _one(_matmul(
        name, (n // tn, m // tm, 1),
        [(a, pl.BlockSpec((tm, k), lambda j, i, kk: (i, 0)),
          b, pl.BlockSpec((tn, k), lambda j, i, kk: (j, 0)), "nt", 0)],
        1, None, [],
        [(jax.ShapeDtypeStruct((m, n), out_dtype), pl.BlockSpec((tm, tn), lambda j, i, kk: (i, j)))],
        _store_epilogue, side), side)


def _mm_tn_colblocked(a, b, name, bw, side=None):
    s, m = a.shape
    nb = b.shape[1] // bw
    tm = _row_tile(m, 512)
    return _one(_matmul(
        name, (nb, m // tm, 1),
        [(a, pl.BlockSpec((s, tm), lambda j, i, k: (0, i)),
          b, pl.BlockSpec((s, bw), lambda j, i, k: (0, j)), "tn", 0)],
        1, None, [],
        [(jax.ShapeDtypeStruct((nb, m, bw), BF16), pl.BlockSpec((None, tm, bw), lambda j, i, k: (j, i, 0)))],
        _store_epilogue, side), side)


def _mm_tn_rowblocked(a, b, name, bh):
    s, n = b.shape
    nb = a.shape[1] // bh
    tn = _row_tile(n, 512)
    return _matmul(
        name, (nb, n // tn, 1),
        [(a, pl.BlockSpec((s, bh), lambda j, i, k: (0, j)),
          b, pl.BlockSpec((s, tn), lambda j, i, k: (0, i)), "tn", 0)],
        1, None, [],
        [(jax.ShapeDtypeStruct((nb, bh, n), BF16), pl.BlockSpec((None, bh, tn), lambda j, i, k: (j, 0, i)))],
        _store_epilogue)[0]


def _as2d(a):
    return a.reshape(-1, a.shape[-1])


def _cast_bf16(w, name):
    w2 = _as2d(w)
    rows, cols = w2.shape
    tr = _row_tile(rows, 256)

    def body(w_ref, o_ref):
        o_ref[...] = w_ref[...].astype(BF16)

    spec = pl.BlockSpec((tr, cols), lambda i: (i, 0))
    out = pl.pallas_call(
        body, grid=(rows // tr,), in_specs=[spec], out_specs=spec,
        out_shape=jax.ShapeDtypeStruct((rows, cols), BF16),
        compiler_params=_params(), name=name,
    )(w2)
    return out.reshape(w.shape)


def _cast_bf16_layer(w, layer, name):
    _, rows, cols = w.shape
    tr = _row_tile(rows, 256)

    def body(w_ref, o_ref):
        o_ref[...] = w_ref[...].astype(BF16)

    return pl.pallas_call(
        body, grid=(rows // tr,),
        in_specs=[pl.BlockSpec((None, tr, cols), lambda i: (layer, i, 0))],
        out_specs=pl.BlockSpec((tr, cols), lambda i: (i, 0)),
        out_shape=jax.ShapeDtypeStruct((rows, cols), BF16),
        compiler_params=_params(), name=name,
    )(w)


ADAM_TILE_ELEMS = 384 * 1024


def _reduce_adam(recvs, w, m, v, name, side=None):
    n_layers, rows, cols = w.shape
    n_slots = recvs[0].shape[0]
    tr = _row_tile(rows, max(16, (ADAM_TILE_ELEMS // cols) // 16 * 16))
    nt = rows // tr
    c1 = 1.0 - ADAM_B1 ** ADAM_STEP
    c2 = 1.0 - ADAM_B2 ** ADAM_STEP

    def body(*refs):
        r_refs = refs[:n_layers]
        w_ref, m_ref, v_ref, g_out, d_out, m_out, v_out = refs[n_layers:]
        layer = pl.program_id(0)

        def update(r_ref):
            g = r_ref[0].astype(F32)
            for k in range(1, n_slots):
                g = g + r_ref[k].astype(F32)
            mm = ADAM_B1 * m_ref[...] + (1.0 - ADAM_B1) * g
            vv = ADAM_B2 * v_ref[...] + (1.0 - ADAM_B2) * (g * g)
            m_hat = mm / c1
            v_hat = vv / c2
            g_out[...] = g
            d_out[...] = -ADAM_LR * (m_hat / (jnp.sqrt(v_hat) + ADAM_EPS) + ADAM_WD * w_ref[...])
            m_out[...] = mm
            v_out[...] = vv

        for li in range(n_layers):
            if n_layers == 1:
                update(r_refs[li])
            else:
                pl.when(layer == li)(functools.partial(update, r_refs[li]))

    def recv_spec(li):
        def imap(layer, i):
            return (0, jnp.where(layer == li, i, jnp.where(layer < li, 0, nt - 1)), 0)
        return pl.BlockSpec((n_slots, tr, cols), imap)

    spec = pl.BlockSpec((None, tr, cols), lambda layer, i: (layer, i, 0))
    o_shape = jax.ShapeDtypeStruct(w.shape, F32)
    grid = (n_layers, nt)
    s_in, s_args, s_out, s_shapes, s_scratch = _side_specs(side)
    res = pl.pallas_call(
        _carrying(body, n_layers + 3, 4, 0, grid, side), grid=grid,
        in_specs=[recv_spec(li) for li in range(n_layers)] + [spec] * 3 + s_in,
        out_specs=[spec] * 4 + s_out, out_shape=[o_shape] * 4 + s_shapes,
        scratch_shapes=s_scratch, compiler_params=_params(), name=name,
    )(*recvs, w, m, v, *s_args)
    return res if side is None else (res[:4], res[4:])


def _my_place():
    return lax.axis_index("x"), lax.axis_index("y"), lax.axis_index("c")


class _GatherSide:
    def __init__(self, blocks):
        self.ins = list(blocks)
        self.outs = [jax.ShapeDtypeStruct((N_DEV,) + b.shape, b.dtype) for b in blocks]

    def scratch(self):
        n = len(self.ins)
        return [pltpu.SemaphoreType.DMA((7 * n,)), pltpu.SemaphoreType.DMA((7 * n,)),
                pltpu.SemaphoreType.DMA((n,))]

    def phases(self, x_refs, out_refs, send_sems, recv_sems, local_sems):
        n = len(self.ins)
        x, y, c = _my_place()
        me, sibling = (x, y, c), (x, y, 1 - c)
        chips = [(1 - x, y), (x, 1 - y), (1 - x, 1 - y)]

        def slot(t, px, py, pc):
            return out_refs[t].at[4 * px + 2 * py + pc]

        def copy(t, k, blk, to, src=None):
            return pltpu.make_async_remote_copy(
                src_ref=slot(t, *blk) if src is None else src, dst_ref=slot(t, *blk),
                send_sem=send_sems.at[7 * t + k], recv_sem=recv_sems.at[7 * t + k],
                device_id=to, device_id_type=pl.DeviceIdType.MESH)

        def own(t):
            return pltpu.make_async_copy(x_refs[t], slot(t, *me), local_sems.at[t])

        def first(t):
            return [copy(t, 0, me, sibling, src=x_refs[t])] + [
                copy(t, 1 + j, me, (*chip, c), src=x_refs[t]) for j, chip in enumerate(chips)]

        def passed(t):
            return [copy(t, 4 + j, (*chip, c), sibling) for j, chip in enumerate(chips)]

        def start():
            for t in range(n):
                own(t).start()
                for cp in first(t):
                    cp.start()

        def mid():
            for t in range(n):
                fwd = passed(t)
                for j, chip in enumerate(chips):
                    copy(t, 1 + j, (*chip, c), me).wait_recv()
                    fwd[j].start()

        def finish():
            for t in range(n):
                copy(t, 0, sibling, me).wait_recv()
                for j, chip in enumerate(chips):
                    copy(t, 4 + j, (*chip, 1 - c), me).wait_recv()
                for cp in first(t) + passed(t):
                    cp.wait_send()
                own(t).wait()

        return start, mid, finish


class _ExchangeSide:
    def __init__(self, blocked):
        self.ins = list(blocked)
        self.outs = [jax.ShapeDtypeStruct(b.shape, b.dtype) for b in blocked]

    def scratch(self):
        n = len(self.ins)
        return [pltpu.SemaphoreType.DMA((7 * n,)), pltpu.SemaphoreType.DMA((7 * n,)),
                pltpu.SemaphoreType.DMA((n,))]

    def phases(self, srcs, dsts, send_sems, recv_sems, local_sems):
        n = len(self.ins)
        x, y, c = _my_place()
        me = 4 * x + 2 * y + c

        def own(t):
            return pltpu.make_async_copy(srcs[t].at[me], dsts[t].at[me], local_sems.at[t])

        def copies(t, arriving):
            res = []
            for k in range(1, N_DEV):
                px, py, pc = x ^ (k >> 2), y ^ ((k >> 1) & 1), c ^ (k & 1)
                peer = 4 * px + 2 * py + pc
                sem = 7 * t + k - 1
                res.append(pltpu.make_async_remote_copy(
                    src_ref=srcs[t].at[peer], dst_ref=dsts[t].at[peer if arriving else me],
                    send_sem=send_sems.at[sem], recv_sem=recv_sems.at[sem],
                    device_id=(px, py, pc), device_id_type=pl.DeviceIdType.MESH))
            return res

        def start():
            for t in range(n):
                own(t).start()
                for send in copies(t, False):
                    send.start()

        def mid():
            pass

        def finish():
            for t in range(n):
                for arrival in copies(t, True):
                    arrival.wait_recv()
                for send in copies(t, False):
                    send.wait_send()
                own(t).wait()

        return start, mid, finish


def _to_residue_major(a, dilation):
    s, w = a.shape
    return a.reshape(s // dilation, dilation, w).transpose(1, 0, 2).reshape(s, w)


def _from_residue_major(a, dilation):
    s, w = a.shape
    return a.reshape(dilation, s // dilation, w).transpose(1, 0, 2).reshape(s, w)


def _mem_kv(mem, gain, wkv, layer, tag):
    mem_n = _rmsnorm_fwd(mem, gain, "mem_norm_" + tag)
    mlen, d = mem.shape
    nb, _, bh, n = wkv.shape
    kv = _matmul(
        "mem_kv_" + tag, (1, nb),
        [(mem_n, pl.BlockSpec((mlen, bh), lambda i, j: (0, j)),
          wkv, pl.BlockSpec((None, None, bh, n), lambda i, j: (j, layer, 0, 0)), "nn", 0)],
        1, (mlen, n), [],
        [(jax.ShapeDtypeStruct((mlen, n), BF16), pl.BlockSpec((mlen, n), lambda i, j: (0, 0)))],
        _store_epilogue)[0]
    return mem_n, kv


def _mem_kv_bwd(mem, gain, mem_n, wkv, layer, dkv, tag):
    mlen, d = mem.shape
    nb, _, bh, n = wkv.shape
    dkvb = dkv.astype(BF16)
    dw = _mm_tn_rowblocked(mem_n, dkvb, "mem_kv_dw_" + tag, bh)
    dmem_n = _matmul(
        "mem_kv_dx_" + tag, (nb, 1),
        [(dkvb, pl.BlockSpec((mlen, n), lambda j, k: (0, 0)),
          wkv, pl.BlockSpec((None, None, bh, n), lambda j, k: (j, layer, 0, 0)), "nt", 0)],
        1, None, [],
        [(jax.ShapeDtypeStruct((mlen, d), F32), pl.BlockSpec((mlen, bh), lambda j, k: (0, j)))],
        _store_epilogue)[0]
    _, dgain = _rmsnorm_bwd(mem, gain, dmem_n, None, "mem_norm_bwd_" + tag)
    return dw, dgain


def _row_blocks(a):
    return a.reshape(N_DEV, -1, a.shape[-1])


def _rows(a):
    return a.reshape(-1, a.shape[-1])


def _ffn_bwd(x, gain, w_gate, w_up, w_down, saved, dxb, tag):
    hf, silu, up_dsilu, hid = saved
    dgate, dup = _swiglu_bwd_hidden(dxb, w_down, silu, up_dsilu, "swiglu_bwd_hidden_" + tag)
    dwd = _mm_tn_full(hid, dxb, "swiglu_bwd_wdown_" + tag)
    dwg, (r_wd,) = _mm_tn_full(dgate, hf, "swiglu_bwd_wgate_" + tag, _ExchangeSide([_row_blocks(dwd)]))
    dwu, (r_wg,) = _mm_tn_full(dup, hf, "swiglu_bwd_wup_" + tag, _ExchangeSide([_row_blocks(dwg)]))
    dhf, (r_wu,) = _swiglu_bwd_input(dgate, dup, w_gate, w_up, "swiglu_bwd_input_" + tag,
                                     _ExchangeSide([_row_blocks(dwu)]))
    dxb_new, dgain = _rmsnorm_bwd(x, gain, dhf, dxb, "ffn_norm_bwd_" + tag)
    return dxb_new, dgain, r_wg, r_wu, r_wd


def _local_step(x, mem, positions, target, first_blocks, shards, small):
    s, d = x.shape
    tabs = _rotary_tables(positions)
    mix_norm, mem_norm, ffn_norm = small["mix_norm"], small["mem_norm"], small["ffn_norm"]

    h0, (w_attn_in, ln_all) = _rmsnorm_fwd(x, mix_norm[0:1], "mix_norm_0", _GatherSide(first_blocks))
    w_attn_in = _rows(w_attn_in)
    ln_full = ln_all.transpose(1, 0, 2).reshape(2, 1, -1)
    ln_g, ln_b = ln_full[0], ln_full[1]
    proj0, (w_mem_kv, w_attn_out, w_gate0) = _attn_in_proj(
        h0, w_attn_in, tabs, _GatherSide([shards["w_mem_kv"], shards["attn_w_out"], shards["w_gate"][0]]))
    up_shard = shards["w_up"][0]
    cut = -(-up_shard.shape[0] // N_GROUPS // 16) * 16
    up_pieces = [up_shard[g * cut:min((g + 1) * cut, up_shard.shape[0])] for g in range(N_GROUPS)]
    qkv, offs, outs, lses, up_gathered = [], [], [], [], []
    for g, dil in enumerate(DILATIONS):
        if dil == 1:
            arr, off = proj0, (g, N_GROUPS + g, 2 * N_GROUPS + g)
        else:
            cols = [proj0[:, (p * N_GROUPS + g) * GROUP_W:(p * N_GROUPS + g + 1) * GROUP_W] for p in range(3)]
            arr, off = _to_residue_major(jnp.concatenate(cols, axis=1), dil), (0, 1, 2)
        o, lse, piece = _attn_fwd(arr, arr, arr, off, s // dil // BLK, "attn_fwd_%d" % g,
                                  _GatherSide([up_pieces[g]]))
        up_gathered.append(piece)
        qkv.append(arr)
        offs.append(off)
        if dil > 1:
            o, lse = _from_residue_major(o, dil), _from_residue_major(lse, dil)
        outs.append(o)
        lses.append(lse)
    mix0 = _merge_fwd(outs, lses)
    qm_off0 = 3 * N_GROUPS
    mem_n0, kv0 = _mem_kv(mem, mem_norm[0:1], w_mem_kv, 0, "0")
    cat0 = _memattn_fwd(proj0, qm_off0, kv0, mix0, 1, "memattn_fwd_0")
    x1 = _mm_nn_colblocked(cat0, w_attn_out, "attn_out_proj", x)
    hf0 = _rmsnorm_fwd(x1, ffn_norm[0:1], "ffn_norm_0")
    w_gate0, w_up0 = _rows(w_gate0), _rows(jnp.concatenate(up_gathered, axis=1))
    (silu0, ud0, hid0), (w_down0, w_sgu_in, w_sgu_out) = _swiglu_fwd(
        hf0, w_gate0, w_up0, "swiglu_fwd_0",
        _GatherSide([shards["w_down"][0], shards["sgu_w_in"], shards["sgu_w_out"]]))
    w_down0 = _rows(w_down0)
    x2, (w_gate1,) = _swiglu_down(hid0, w_down0, x1, "swiglu_down_0", _GatherSide([shards["w_gate"][1]]))
    ffn_saved0 = (hf0, silu0, ud0, hid0)

    h1 = _rmsnorm_fwd(x2, mix_norm[1:2], "mix_norm_1")
    w_sgu_in = _rows(w_sgu_in)
    w_sgu_out = _rows(w_sgu_out)
    proj1, (w_up1,) = _mm_nt_rowblocked(h1, w_sgu_in, "sgu_in_proj", w_sgu_in.shape[0] // 7, BF16,
                                        _GatherSide([shards["w_up"][1]]))
    w_gate1, w_up1 = _rows(w_gate1), _rows(w_up1)
    b_st = small["sgu_b_spatial"].T
    mix1 = _sgu_fwd(proj1, ln_g, ln_b, small["sgu_w_spatial"], b_st)
    qm_off1 = 2 * SGU_W // GROUP_W
    mem_n1, kv1 = _mem_kv(mem, mem_norm[1:2], w_mem_kv, 1, "1")
    cat1 = _memattn_fwd(proj1, qm_off1, kv1, mix1, SGU_W // GROUP_W, "memattn_fwd_1")
    x3 = _mm_nn(cat1, w_sgu_out, "sgu_out_proj", d // 2, out_dtype=F32, res=x2)
    hf1 = _rmsnorm_fwd(x3, ffn_norm[1:2], "ffn_norm_1")
    (silu1, ud1, hid1), (w_down1,) = _swiglu_fwd(hf1, w_gate1, w_up1, "swiglu_fwd_1",
                                                 _GatherSide([shards["w_down"][1]]))
    w_down1 = _rows(w_down1)
    x4 = _swiglu_down(hid1, w_down1, x3, "swiglu_down_1")
    ffn_saved1 = (hf1, silu1, ud1, hid1)

    loss, dxb, d_final = _loss_head(x4, small["final_norm"], target)

    recvs, sgrads = {}, {}
    dxb, d_ffn1, r_wg1, r_wu1, r_wd1 = _ffn_bwd(x3, ffn_norm[1:2], w_gate1, w_up1, w_down1, ffn_saved1, dxb, "1")
    dcat1 = _mm_nt_rowblocked(dxb, w_sgu_out, "sgu_out_proj_dx", w_sgu_out.shape[0] // 2, BF16)
    dwsout = _mm_tn_rowblocked(cat1, dxb, "sgu_out_proj_dw", w_sgu_out.shape[0] // N_DEV)
    duv, dws, dbs, dlng, dlnb = _sgu_bwd(proj1, dcat1, ln_g, ln_b, small["sgu_w_spatial"], b_st)
    dproj1, dkv1 = _memattn_bwd(proj1, qm_off1, kv1, dcat1, SGU_W // GROUP_W, "memattn_bwd_1",
                                into=duv, into_off=2 * SGU_W // GROUP_W)
    dwkv1, d_memnorm1 = _mem_kv_bwd(mem, mem_norm[1:2], mem_n1, w_mem_kv, 1, dkv1, "1")
    dwsin, (r_wsout, r_wkv1) = _mm_tn_full(dproj1, h1, "sgu_in_proj_dw", _ExchangeSide([dwsout, dwkv1]))
    dh1, (r_wsin,) = _mm_nn(dproj1, w_sgu_in, "sgu_in_proj_dx", d // 4, out_dtype=BF16,
                            side=_ExchangeSide([_row_blocks(dwsin)]))
    dxb, d_mix1 = _rmsnorm_bwd(x2, mix_norm[1:2], dh1, dxb, "mix_norm_bwd_1")

    dxb, d_ffn0, r_wg0, r_wu0, r_wd0 = _ffn_bwd(x1, ffn_norm[0:1], w_gate0, w_up0, w_down0, ffn_saved0, dxb, "0")
    dcat0 = _mm_nt_colblocked(dxb, w_attn_out, "attn_out_proj_dx", BF16, 4)
    dwout0 = _mm_tn_colblocked(cat0, dxb, "attn_out_proj_dw", w_attn_out.shape[2])
    dos_and_deltas = _merge_bwd(dcat0, outs, lses)
    dqkv = []
    for g, dil in enumerate(DILATIONS):
        do_g, dl_g = dos_and_deltas[g], dos_and_deltas[N_GROUPS + g]
        lse_g = lses[g]
        if dil > 1:
            do_g, dl_g, lse_g = (_to_residue_major(t, dil) for t in (do_g, dl_g, lse_g))
        t = _attn_bwd(qkv[g], qkv[g], qkv[g], offs[g], do_g, lse_g, dl_g, s // dil // BLK, "attn_bwd_%d" % g)
        dqkv.append(_from_residue_major(t, dil) if dil > 1 else t)
    dqm0, dkv0 = _memattn_bwd(proj0, qm_off0, kv0, dcat0, 1, "memattn_bwd_0")
    dwkv0, d_memnorm0 = _mem_kv_bwd(mem, mem_norm[0:1], mem_n0, w_mem_kv, 0, dkv0, "0")
    dproj0 = _assemble_dproj(dqkv, dqm0, tabs)
    dwin0, (r_wout0, r_wkv0) = _mm_tn_full(dproj0, h0, "attn_in_proj_dw", _ExchangeSide([dwout0, dwkv0]))
    dh0, (r_win0,) = _mm_nn(dproj0, w_attn_in, "attn_in_proj_dx", d // 4, out_dtype=BF16,
                            side=_ExchangeSide([_row_blocks(dwin0)]))
    grad_x, d_mix0 = _rmsnorm_bwd(x, mix_norm[0:1], dh0, dxb, "mix_norm_bwd_0", out_dtype=F32)

    recvs["w_gate"] = [r_wg0, r_wg1]
    recvs["w_up"] = [r_wu0, r_wu1]
    recvs["w_down"] = [r_wd0, r_wd1]
    recvs["w_mem_kv"] = [r_wkv0, r_wkv1]
    recvs["attn_w_in"] = [r_win0]
    recvs["attn_w_out"] = [r_wout0]
    recvs["sgu_w_in"] = [r_wsin]
    recvs["sgu_w_out"] = [r_wsout]
    sgrads["mix_norm"] = jnp.concatenate([d_mix0, d_mix1], axis=0)
    sgrads["mem_norm"] = jnp.concatenate([d_memnorm0, d_memnorm1], axis=0)
    sgrads["ffn_norm"] = jnp.concatenate([d_ffn0, d_ffn1], axis=0)
    sgrads["final_norm"] = d_final
    sgrads["sgu_w_spatial"] = dws
    sgrads["sgu_b_spatial"] = dbs[:, :SGU_GROUPS].T
    sgrads["sgu_ln_g"] = dlng
    sgrads["sgu_ln_b"] = dlnb
    return loss, grad_x, recvs, sgrads


ADAM_ORDER = ("w_gate", "w_up", "w_down", "w_mem_kv", "sgu_w_in", "sgu_w_out", "attn_w_out", "attn_w_in")
TRANSPOSED = ("w_gate", "w_up", "sgu_w_in", "attn_w_in")
SMALL_REPLICATED = ("mix_norm", "mem_norm", "ffn_norm", "final_norm", "sgu_w_spatial", "sgu_b_spatial")
SMALL_SHARDED = ("sgu_ln_g", "sgu_ln_b")
WEIGHT_ORDER = ("mix_norm", "mem_norm", "w_mem_kv", "ffn_norm", "w_gate", "w_up", "w_down", "attn_w_in",
                "attn_w_out", "sgu_w_in", "sgu_ln_g", "sgu_ln_b", "sgu_w_spatial", "sgu_b_spatial",
                "sgu_w_out", "final_norm")
PACK_LANES = 128


def _pack(parts):
    flat = [p.reshape(-1) for p in parts]
    sizes = [f.shape[0] for f in flat]
    total = sum(sizes)
    rows = -(-total // PACK_LANES)
    rows = -(-rows // 8) * 8
    pad = rows * PACK_LANES - total
    packed = jnp.concatenate(flat + [jnp.zeros((pad,), F32)]).reshape(rows, PACK_LANES)
    offs, o = [], 0
    for sz in sizes:
        offs.append(o)
        o += sz
    return packed, offs, sizes


def _unpack(packed, offs, sizes, shapes):
    flat = packed.reshape(-1)
    return [flat[o:o + sz].reshape(shp) for o, sz, shp in zip(offs, sizes, shapes)]


def kernel(x, mem, positions, mix_norm, mem_norm, w_mem_kv, ffn_norm, w_gate, w_up, w_down, attn_w_in, attn_w_out, sgu_w_in, sgu_ln_g, sgu_ln_b, sgu_w_spatial, sgu_b_spatial, sgu_w_out, final_norm, loss_target, m_mix_norm, m_mem_norm, m_w_mem_kv, m_ffn_norm, m_w_gate, m_w_up, m_w_down, m_attn_w_in, m_attn_w_out, m_sgu_w_in, m_sgu_ln_g, m_sgu_ln_b, m_sgu_w_spatial, m_sgu_b_spatial, m_sgu_w_out, m_final_norm, v_mix_norm, v_mem_norm, v_w_mem_kv, v_ffn_norm, v_w_gate, v_w_up, v_w_down, v_attn_w_in, v_attn_w_out, v_sgu_w_in, v_sgu_ln_g, v_sgu_ln_b, v_sgu_w_spatial, v_sgu_b_spatial, v_sgu_w_out, v_final_norm):
    w = dict(mix_norm=mix_norm, mem_norm=mem_norm, w_mem_kv=w_mem_kv, ffn_norm=ffn_norm, w_gate=w_gate,
             w_up=w_up, w_down=w_down, attn_w_in=attn_w_in, attn_w_out=attn_w_out, sgu_w_in=sgu_w_in,
             sgu_ln_g=sgu_ln_g, sgu_ln_b=sgu_ln_b, sgu_w_spatial=sgu_w_spatial, sgu_b_spatial=sgu_b_spatial,
             sgu_w_out=sgu_w_out, final_norm=final_norm)
    mo = dict(mix_norm=m_mix_norm, mem_norm=m_mem_norm, w_mem_kv=m_w_mem_kv, ffn_norm=m_ffn_norm,
              w_gate=m_w_gate, w_up=m_w_up, w_down=m_w_down, attn_w_in=m_attn_w_in, attn_w_out=m_attn_w_out,
              sgu_w_in=m_sgu_w_in, sgu_ln_g=m_sgu_ln_g, sgu_ln_b=m_sgu_ln_b, sgu_w_spatial=m_sgu_w_spatial,
              sgu_b_spatial=m_sgu_b_spatial, sgu_w_out=m_sgu_w_out, final_norm=m_final_norm)
    vo = dict(mix_norm=v_mix_norm, mem_norm=v_mem_norm, w_mem_kv=v_w_mem_kv, ffn_norm=v_ffn_norm,
              w_gate=v_w_gate, w_up=v_w_up, w_down=v_w_down, attn_w_in=v_attn_w_in, attn_w_out=v_attn_w_out,
              sgu_w_in=v_sgu_w_in, sgu_ln_g=v_sgu_ln_g, sgu_ln_b=v_sgu_ln_b, sgu_w_spatial=v_sgu_w_spatial,
              sgu_b_spatial=v_sgu_b_spatial, sgu_w_out=v_sgu_w_out, final_norm=v_final_norm)
    me = 4 * lax.axis_index("x") + 2 * lax.axis_index("y") + lax.axis_index("c")
    d_model = x.shape[-1]

    for n in TRANSPOSED:
        w[n], mo[n], vo[n] = (jnp.swapaxes(t, 1, 2) for t in (w[n], mo[n], vo[n]))

    shards = {
        "w_mem_kv": _cast_bf16(w_mem_kv, "cast_w_mem_kv"),
        "attn_w_out": _cast_bf16(attn_w_out[0], "cast_attn_w_out"),
        "sgu_w_in": _cast_bf16(w["sgu_w_in"][0], "cast_sgu_w_in"),
        "sgu_w_out": _cast_bf16(sgu_w_out[0], "cast_sgu_w_out"),
    }
    for n in ("w_gate", "w_up", "w_down"):
        shards[n] = [_cast_bf16_layer(w[n], layer, "cast_%s_%d" % (n, layer)) for layer in range(w[n].shape[0])]
    ln_pack = jnp.concatenate([sgu_ln_g, sgu_ln_b], axis=0)
    first_blocks = [_cast_bf16(w["attn_w_in"][0], "cast_attn_w_in"), ln_pack]
    small = dict(mix_norm=mix_norm, mem_norm=mem_norm, ffn_norm=ffn_norm, final_norm=final_norm.reshape(1, -1),
                 sgu_w_spatial=sgu_w_spatial[0], sgu_b_spatial=sgu_b_spatial[0])

    loss, grad_x, recvs, sgrads = _local_step(x[0], mem[0], positions[0], loss_target[0], first_blocks, shards,
                                              small)
    loss = lax.psum(loss[0, 0], MESH_AXES)

    small_names = SMALL_REPLICATED + SMALL_SHARDED
    packed, offs, sizes = _pack([sgrads[n] for n in small_names])
    out_g, out_d, out_m, out_v = {}, {}, {}, {}
    for n in ADAM_ORDER:
        shard = w[n]
        w3 = shard.reshape(shard.shape[0], -1, shard.shape[-1])
        rs = [r.reshape(N_DEV, -1, shard.shape[-1]) for r in recvs[n]]
        operands = (rs, w3, mo[n].reshape(w3.shape), vo[n].reshape(w3.shape), "adam_" + n)
        if n == ADAM_ORDER[0]:
            res, (all_packs,) = _reduce_adam(*operands, _GatherSide([packed]))
        else:
            res = _reduce_adam(*operands)
        res = [r.reshape(shard.shape) for r in res]
        if n in TRANSPOSED:
            res = [jnp.swapaxes(r, 1, 2) for r in res]
        out_g[n], out_d[n], out_m[n], out_v[n] = res

    rep_shapes = [w[n].shape for n in SMALL_REPLICATED]
    w_pack, w_offs, w_sizes = _pack([w[n] for n in SMALL_REPLICATED])
    m_pack, _, _ = _pack([mo[n] for n in SMALL_REPLICATED])
    v_pack, _, _ = _pack([vo[n] for n in SMALL_REPLICATED])
    n_rep_rows = w_pack.shape[0]
    res = _reduce_adam([all_packs[:, :n_rep_rows]], w_pack[None], m_pack[None], v_pack[None], "adam_small")
    for dst, r in zip((out_g, out_d, out_m, out_v), res):
        for n, val in zip(SMALL_REPLICATED, _unpack(r[0], w_offs, w_sizes, rep_shapes)):
            dst[n] = val
    ln_rows0 = offs[len(SMALL_REPLICATED)] // PACK_LANES
    ln_rows = 2 * SGU_W // PACK_LANES
    ln_sum = _reduce_adam([all_packs[:, ln_rows0:ln_rows0 + ln_rows]], jnp.zeros((1, ln_rows, PACK_LANES), F32),
                          jnp.zeros((1, ln_rows, PACK_LANES), F32), jnp.zeros((1, ln_rows, PACK_LANES), F32),
                          "sum_ln_grads")[0]
    ln_grads = ln_sum.reshape(2, N_DEV, -1)
    ln_mine = lax.dynamic_index_in_dim(ln_grads, me, axis=1, keepdims=False)
    w_ln = jnp.concatenate([sgu_ln_g, sgu_ln_b], axis=0)[None]
    m_ln = jnp.concatenate([m_sgu_ln_g, m_sgu_ln_b], axis=0)[None]
    v_ln = jnp.concatenate([v_sgu_ln_g, v_sgu_ln_b], axis=0)[None]
    res = _reduce_adam([ln_mine[None]], w_ln, m_ln, v_ln, "adam_ln")
    for dst, r in zip((out_g, out_d, out_m, out_v), res):
        dst["sgu_ln_g"], dst["sgu_ln_b"] = r[0, 0:1], r[0, 1:2]

    return (loss, grad_x[None], *[out_g[n] for n in WEIGHT_ORDER], *[out_d[n] for n in WEIGHT_ORDER],
            *[out_m[n] for n in WEIGHT_ORDER], *[out_v[n] for n in WEIGHT_ORDER])
```

```python
import functools

import jax
import jax.numpy as jnp
from jax import lax
from jax.experimental import pallas as pl
from jax.experimental.pallas import tpu as pltpu

F32 = jnp.float32
BF16 = jnp.bfloat16

N_DEV = 8
HEAD = 128
HPG = 4
GROUP_W = HPG * HEAD
N_GROUPS = 3
DILATIONS = (1, 4, 16)
BLK = 128
SGU_GROUPS = 12
SGU_W = SGU_GROUPS * HEAD
ROT_HALF = 16
ROPE_THETA = 500000.0
NORM_EPS = 1e-6
LN_EPS = 1e-5
NEG_INF = -1e30
SCALE = HEAD ** -0.5

ADAM_LR = 0.001
ADAM_B1 = 0.9
ADAM_B2 = 0.999
ADAM_EPS = 1e-08
ADAM_WD = 0.01
ADAM_STEP = 10

VMEM_LIMIT_V7X = 56 * 1024 * 1024
MESH_AXES = ("x", "y", "c")

_DN = {
    "nn": (((1,), (0,)), ((), ())),
    "nt": (((1,), (1,)), ((), ())),
    "tn": (((0,), (0,)), ((), ())),
}


def _dot(a, b, kind="nn"):
    return lax.dot_general(a, b, _DN[kind], preferred_element_type=F32)


def _params():
    return pltpu.CompilerParams(vmem_limit_bytes=VMEM_LIMIT_V7X)


def _row_tile(rows, cap):
    if rows <= cap:
        return rows
    t = cap - cap % 16
    while t >= 16:
        if rows % t == 0:
            return t
        t -= 16
    return rows


def _gelu(x):
    c = 0.7978845608028654
    return 0.5 * x * (1.0 + jnp.tanh(c * (x + 0.044715 * x * x * x)))


def _gelu_and_grad(x):
    c = 0.7978845608028654
    x2 = x * x
    t = jnp.tanh(c * x * (1.0 + 0.044715 * x2))
    half = 0.5 * (1.0 + t)
    return x * half, half + 0.5 * x * (1.0 - t * t) * c * (1.0 + 3.0 * 0.044715 * x2)


def _sigmoid(x):
    return 1.0 / (1.0 + jnp.exp(-x))


def _matmul(name, grid, terms, n_acc, acc_shape, extras, outs, epilogue, side=None):
    nk = grid[-1]
    nt, ne, no = len(terms), len(extras), len(outs)
    kinds = [(t[4], t[5]) for t in terms]
    n_scratch_acc = 0 if nk == 1 else n_acc
    ns_in = len(side.ins) if side else 0
    ns_out = len(side.outs) if side else 0
    n_steps = 1
    for g in grid:
        n_steps *= g

    def body(*refs):
        pos = 0
        ab = refs[pos:pos + 2 * nt]
        pos += 2 * nt
        ex = refs[pos:pos + ne]
        pos += ne
        s_in = refs[pos:pos + ns_in]
        pos += ns_in
        out = refs[pos:pos + no]
        pos += no
        s_out = refs[pos:pos + ns_out]
        pos += ns_out
        accs = refs[pos:pos + n_scratch_acc]
        s_sems = refs[pos + n_scratch_acc:]
        if side:
            step = pl.program_id(0)
            for ax in range(1, len(grid)):
                step = step * grid[ax] + pl.program_id(ax)
            start, mid, finish = side.phases(s_in, s_out, *s_sems)
            pl.when(step == 0)(start)
        parts = [None] * n_acc
        for t, (kind, ai) in enumerate(kinds):
            a_ref, b_ref = ab[2 * t], ab[2 * t + 1]
            if len(b_ref.shape) == 2:
                pairs = [(a_ref[...], b_ref[...])]
            elif len(a_ref.shape) == 3:
                pairs = [(a_ref[q], b_ref[q]) for q in range(b_ref.shape[0])]
            else:
                bw = b_ref.shape[2]
                pairs = [(a_ref[:, q * bw:(q + 1) * bw], b_ref[q]) for q in range(b_ref.shape[0])]
            for a, b in pairs:
                p = _dot(a.astype(BF16), b.astype(BF16), kind)
                parts[ai] = p if parts[ai] is None else parts[ai] + p
        if nk == 1:
            epilogue(parts, ex, out)
        else:
            k = pl.program_id(len(grid) - 1)

            @pl.when(k == 0)
            def _():
                for ai in range(n_acc):
                    accs[ai][...] = parts[ai]

            @pl.when(k > 0)
            def _():
                for ai in range(n_acc):
                    accs[ai][...] += parts[ai]

            @pl.when(k == nk - 1)
            def _():
                epilogue([a[...] for a in accs], ex, out)

        if side:
            pl.when(step == (3 * n_steps) // 4)(mid)
            pl.when(step == n_steps - 1)(finish)

    hbm = pl.BlockSpec(memory_space=pltpu.HBM)
    in_specs, args = [], []
    for (a, a_spec, b, b_spec, _, _) in terms:
        in_specs += [a_spec, b_spec]
        args += [a, b]
    for (e, e_spec) in extras:
        in_specs.append(e_spec)
        args.append(e)
    scratch = [pltpu.VMEM(acc_shape, F32) for _ in range(n_scratch_acc)]
    out_specs = [o[1] for o in outs]
    out_shape = [o[0] for o in outs]
    if side:
        in_specs += [hbm] * ns_in
        args += list(side.ins)
        out_specs += [hbm] * ns_out
        out_shape += list(side.outs)
        scratch += side.scratch()
    res = pl.pallas_call(
        body, grid=grid, in_specs=in_specs, out_specs=out_specs, out_shape=out_shape,
        scratch_shapes=scratch, compiler_params=_params(), name=name,
    )(*args)
    return res if side is None else (res[:no], res[no:])


def _carrying(body, n_in, n_out, n_scratch, grid, side):
    if side is None:
        return body
    ns_in, ns_out = len(side.ins), len(side.outs)
    n_steps = 1
    for g in grid:
        n_steps *= g

    def wrapped(*refs):
        pos = n_in + ns_in
        ins, s_in = refs[:n_in], refs[n_in:pos]
        outs, s_out = refs[pos:pos + n_out], refs[pos + n_out:pos + n_out + ns_out]
        pos += n_out + ns_out
        scratch, sems = refs[pos:pos + n_scratch], refs[pos + n_scratch:]
        step = pl.program_id(0)
        for ax in range(1, len(grid)):
            step = step * grid[ax] + pl.program_id(ax)
        start, mid, finish = side.phases(s_in, s_out, *sems)
        pl.when(step == 0)(start)
        body(*ins, *outs, *scratch)
        pl.when(step == (3 * n_steps) // 4)(mid)
        pl.when(step == n_steps - 1)(finish)

    return wrapped


def _side_specs(side):
    if side is None:
        return [], [], [], [], []
    hbm = pl.BlockSpec(memory_space=pltpu.HBM)
    return ([hbm] * len(side.ins), list(side.ins), [hbm] * len(side.outs), list(side.outs), side.scratch())


def _one(res, side):
    return res[0] if side is None else (res[0][0], res[1])


def _store_epilogue(parts, ex, out):
    out[0][...] = parts[0].astype(out[0].dtype)


def _residual_epilogue(parts, ex, out):
    out[0][...] = (parts[0] + ex[0][...]).astype(out[0].dtype)


def _rmsnorm_fwd(x, g, name, side=None):
    rows, d = x.shape
    tm = _row_tile(rows, 512)
    grid = (rows // tm,)

    def body(x_ref, g_ref, o_ref):
        xf = x_ref[...]
        r = lax.rsqrt(jnp.mean(xf * xf, axis=-1, keepdims=True) + NORM_EPS)
        o_ref[...] = (xf * r * g_ref[...]).astype(o_ref.dtype)

    s_in, s_args, s_out, s_shapes, s_scratch = _side_specs(side)
    res = pl.pallas_call(
        _carrying(body, 2, 1, 0, grid, side), grid=grid,
        in_specs=[pl.BlockSpec((tm, d), lambda i: (i, 0)), pl.BlockSpec((1, d), lambda i: (0, 0))] + s_in,
        out_specs=[pl.BlockSpec((tm, d), lambda i: (i, 0))] + s_out,
        out_shape=[jax.ShapeDtypeStruct((rows, d), BF16)] + s_shapes,
        scratch_shapes=s_scratch, compiler_params=_params(), name=name,
    )(x, g, *s_args)
    return res[0] if side is None else (res[0], res[1:])


def _rmsnorm_bwd(x, g, dh, dres, name, out_dtype=BF16):
    rows, d = x.shape
    tm = _row_tile(rows, 256)
    has_res = dres is not None

    def body(*refs):
        if has_res:
            x_ref, g_ref, dh_ref, dres_ref, dx_ref, dg_ref = refs
        else:
            x_ref, g_ref, dh_ref, dx_ref, dg_ref = refs
        i = pl.program_id(0)
        xf = x_ref[...]
        r = lax.rsqrt(jnp.mean(xf * xf, axis=-1, keepdims=True) + NORM_EPS)
        xhat = xf * r
        dy = dh_ref[...].astype(F32)
        gdy = dy * g_ref[...]
        c = jnp.mean(gdy * xhat, axis=-1, keepdims=True)
        dx = r * (gdy - xhat * c)
        if has_res:
            dx = dx + dres_ref[...].astype(F32)
        dx_ref[...] = dx.astype(out_dtype)

        @pl.when(i == 0)
        def _():
            dg_ref[...] = jnp.zeros_like(dg_ref)

        dg_ref[...] += jnp.sum(dy * xhat, axis=0, keepdims=True)

    row_spec = pl.BlockSpec((tm, d), lambda i: (i, 0))
    vec_spec = pl.BlockSpec((1, d), lambda i: (0, 0))
    in_specs = [row_spec, vec_spec, row_spec] + ([row_spec] if has_res else [])
    args = [x, g, dh] + ([dres] if has_res else [])
    return pl.pallas_call(
        body, grid=(rows // tm,), in_specs=in_specs,
        out_specs=[row_spec, vec_spec],
        out_shape=[jax.ShapeDtypeStruct((rows, d), out_dtype), jax.ShapeDtypeStruct((1, d), F32)],
        compiler_params=_params(), name=name,
    )(*args)


def _loss_head(x, g, target):
    rows, d = x.shape
    tm = _row_tile(rows, 256)

    def body(x_ref, g_ref, t_ref, loss_ref, dxb_ref, dg_ref):
        i = pl.program_id(0)
        xf = x_ref[...]
        gv = g_ref[...]
        r = lax.rsqrt(jnp.mean(xf * xf, axis=-1, keepdims=True) + NORM_EPS)
        xhat = xf * r
        err = xhat * gv - t_ref[...]
        row_loss = jnp.mean(err * err, axis=-1, keepdims=True)
        dy = err * (1.0 / d)
        gdy = dy * gv
        c = jnp.mean(gdy * xhat, axis=-1, keepdims=True)
        dxb_ref[...] = (r * (gdy - xhat * c)).astype(BF16)

        @pl.when(i == 0)
        def _():
            dg_ref[...] = jnp.zeros_like(dg_ref)
            loss_ref[...] = jnp.zeros_like(loss_ref)

        dg_ref[...] += jnp.sum(dy * xhat, axis=0, keepdims=True)
        loss_ref[...] += 0.5 * jnp.sum(row_loss, axis=0, keepdims=True)

    row_spec = pl.BlockSpec((tm, d), lambda i: (i, 0))
    vec_spec = pl.BlockSpec((1, d), lambda i: (0, 0))
    return pl.pallas_call(
        body, grid=(rows // tm,), in_specs=[row_spec, vec_spec, row_spec],
        out_specs=[pl.BlockSpec((1, 1), lambda i: (0, 0)), row_spec, vec_spec],
        out_shape=[jax.ShapeDtypeStruct((1, 1), F32), jax.ShapeDtypeStruct((rows, d), BF16),
                   jax.ShapeDtypeStruct((1, d), F32)],
        compiler_params=_params(), name="loss_head",
    )(x, g, target)


def _rotary_tables(positions):
    inv_freq = ROPE_THETA ** (-jnp.arange(ROT_HALF, dtype=F32) / ROT_HALF)
    ang = positions.astype(F32)[:, None] * inv_freq
    cos, sin = jnp.cos(ang), jnp.sin(ang)
    s = positions.shape[0]
    z = jnp.zeros((s, HEAD - 2 * ROT_HALF), F32)
    z16 = jnp.zeros((s, ROT_HALF), F32)
    c = jnp.concatenate([cos, cos, jnp.ones_like(z)], axis=1)
    s1 = jnp.concatenate([z16, sin, z], axis=1)
    s2 = jnp.concatenate([-sin, z16, z], axis=1)
    return c, s1, s2


ATTN_PROJ_HEADS_PER_STEP = 10


def _attn_in_proj(h, w_t, tabs, side=None):
    s, d = h.shape
    n = w_t.shape[0]
    tn = ATTN_PROJ_HEADS_PER_STEP * HEAD
    n_rot_heads = 2 * N_GROUPS * HPG
    n_rot_steps = -(-n_rot_heads // ATTN_PROJ_HEADS_PER_STEP)
    tm = _row_tile(s, 1024)

    def epilogue(parts, ex, out):
        j = pl.program_id(0)
        acc = parts[0]

        @pl.when(j < n_rot_steps)
        def _():
            c, s1, s2 = ex[0][...], ex[1][...], ex[2][...]
            for t in range(ATTN_PROJ_HEADS_PER_STEP):
                seg = acc[:, t * HEAD:(t + 1) * HEAD]
                rot = seg * c + pltpu.roll(seg, ROT_HALF, 1) * s1 + pltpu.roll(seg, HEAD - ROT_HALF, 1) * s2
                is_rot = (j * ATTN_PROJ_HEADS_PER_STEP + t) < n_rot_heads
                out[0][:, t * HEAD:(t + 1) * HEAD] = jnp.where(is_rot, rot, seg).astype(BF16)

        @pl.when(j >= n_rot_steps)
        def _():
            out[0][...] = acc.astype(BF16)

    tab_spec = pl.BlockSpec((tm, HEAD), lambda j, m, k: (m, 0))
    return _one(_matmul(
        "attn_in_proj", (n // tn, s // tm, 1),
        [(h, pl.BlockSpec((tm, d), lambda j, m, k: (m, 0)),
          w_t, pl.BlockSpec((tn, d), lambda j, m, k: (j, 0)), "nt", 0)],
        1, None, [(tabs[0], tab_spec), (tabs[1], tab_spec), (tabs[2], tab_spec)],
        [(jax.ShapeDtypeStruct((s, n), BF16), pl.BlockSpec((tm, tn), lambda j, m, k: (m, j)))],
        epilogue, side), side)


ATT_TILE_BLOCKS = 4


def _att_blocks(seq_blocks):
    return min(ATT_TILE_BLOCKS, seq_blocks)


def _band_masks():
    qi = lax.broadcasted_iota(jnp.int32, (BLK, BLK), 0)
    ki = lax.broadcasted_iota(jnp.int32, (BLK, BLK), 1)
    return ki <= qi, ki >= qi


def _attn_fwd(q_arr, k_arr, v_arr, offs, seq_blocks, name, side=None):
    s = q_arr.shape[0]
    qo, ko, vo = offs
    nb = _att_blocks(seq_blocks)

    def body(q_ref, kc_ref, kp_ref, vc_ref, vp_ref, o_ref, lse_ref):
        n = pl.program_id(0)
        tile_starts_seq = (n * nb) % seq_blocks == 0
        mask_c, mask_p = _band_masks()
        pairs = [(b * BLK, h * HEAD) for b in range(nb) for h in range(HPG)]

        def keys_prev(ref_c, ref_p, r0, c0):
            return ref_p[:, c0:c0 + HEAD] if r0 == 0 else ref_c[r0 - BLK:r0, c0:c0 + HEAD]

        s_c, s_p = [], []
        for r0, c0 in pairs:
            q = q_ref[r0:r0 + BLK, c0:c0 + HEAD]
            s_c.append(jnp.where(mask_c, _dot(q, kc_ref[r0:r0 + BLK, c0:c0 + HEAD], "nt") * SCALE, NEG_INF))
            sp = jnp.where(mask_p, _dot(q, keys_prev(kc_ref, kp_ref, r0, c0), "nt") * SCALE, NEG_INF)
            s_p.append(jnp.where(tile_starts_seq, NEG_INF, sp) if r0 == 0 else sp)
        m = [jnp.maximum(jnp.max(a, axis=-1, keepdims=True), jnp.max(b, axis=-1, keepdims=True))
             for a, b in zip(s_c, s_p)]
        p_c = [jnp.exp(a - mm) for a, mm in zip(s_c, m)]
        p_p = [jnp.exp(a - mm) for a, mm in zip(s_p, m)]
        l = [jnp.sum(a, axis=-1, keepdims=True) + jnp.sum(b, axis=-1, keepdims=True) for a, b in zip(p_c, p_p)]
        inv = [1.0 / ll for ll in l]
        for i, (r0, c0) in enumerate(pairs):
            o = (_dot((p_c[i] * inv[i]).astype(BF16), vc_ref[r0:r0 + BLK, c0:c0 + HEAD])
                 + _dot((p_p[i] * inv[i]).astype(BF16), keys_prev(vc_ref, vp_ref, r0, c0)))
            o_ref[r0:r0 + BLK, c0:c0 + HEAD] = o.astype(BF16)
            lse_ref[r0:r0 + BLK, c0:c0 + HEAD] = jnp.broadcast_to(m[i] + jnp.log(l[i]), (BLK, HEAD))

    def cur(off):
        return pl.BlockSpec((nb * BLK, GROUP_W), lambda n: (n, off))

    def prev(off):
        return pl.BlockSpec((BLK, GROUP_W), lambda n: (jnp.maximum(n * nb - 1, 0), off))

    grid = (s // (nb * BLK),)
    s_in, s_args, s_out, s_shapes, s_scratch = _side_specs(side)
    return pl.pallas_call(
        _carrying(body, 5, 2, 0, grid, side), grid=grid,
        in_specs=[cur(qo), cur(ko), prev(ko), cur(vo), prev(vo)] + s_in,
        out_specs=[cur(0), cur(0)] + s_out,
        out_shape=[jax.ShapeDtypeStruct((s, GROUP_W), BF16), jax.ShapeDtypeStruct((s, GROUP_W), F32)] + s_shapes,
        scratch_shapes=s_scratch, compiler_params=_params(), name=name,
    )(q_arr, k_arr, k_arr, v_arr, v_arr, *s_args)


def _attn_bwd(q_arr, k_arr, v_arr, offs, do, lse, delta, seq_blocks, name):
    s = q_arr.shape[0]
    qo, ko, vo = offs
    nb = _att_blocks(seq_blocks)
    n_blocks = s // BLK

    def body(qc_ref, qn_ref, kc_ref, kp_ref, vc_ref, vp_ref, doc_ref, don_ref, lsec_ref, lsen_ref,
             dlc_ref, dln_ref, out_ref):
        n = pl.program_id(0)
        tile_starts_seq = (n * nb) % seq_blocks == 0
        next_in_seq = ((n + 1) * nb) % seq_blocks != 0
        mask_c, mask_p = _band_masks()

        def blk(ref_c, ref_edge, b, c0):
            if b < 0 or b >= nb:
                return ref_edge[:, c0:c0 + HEAD]
            return ref_c[b * BLK:(b + 1) * BLK, c0:c0 + HEAD]

        heads = [h * HEAD for h in range(HPG)]
        own = [(b, c0) for b in range(nb) for c0 in heads]
        cross = [(c, c0) for c in range(nb + 1) for c0 in heads]
        s_o, dp_o, s_x, dp_x = [], [], [], []
        for b, c0 in own:
            s_o.append(jnp.where(mask_c, _dot(blk(qc_ref, None, b, c0), blk(kc_ref, None, b, c0), "nt") * SCALE,
                                 NEG_INF))
            dp_o.append(_dot(blk(doc_ref, None, b, c0), blk(vc_ref, None, b, c0), "nt"))
        for c, c0 in cross:
            sx = jnp.where(mask_p, _dot(blk(qc_ref, qn_ref, c, c0), blk(kc_ref, kp_ref, c - 1, c0), "nt") * SCALE,
                           NEG_INF)
            if c == 0:
                sx = jnp.where(tile_starts_seq, NEG_INF, sx)
            if c == nb:
                sx = jnp.where(next_in_seq, sx, NEG_INF)
            s_x.append(sx)
            dp_x.append(_dot(blk(doc_ref, don_ref, c, c0), blk(vc_ref, vp_ref, c - 1, c0), "nt"))
        p_o, ds_o, p_x, ds_x = [], [], [], []
        for i, (b, c0) in enumerate(own):
            p = jnp.exp(s_o[i] - blk(lsec_ref, None, b, c0))
            ds_o.append((p * (dp_o[i] - blk(dlc_ref, None, b, c0)) * SCALE).astype(BF16))
            p_o.append(p.astype(BF16))
        for i, (c, c0) in enumerate(cross):
            p = jnp.exp(s_x[i] - blk(lsec_ref, lsen_ref, c, c0))
            ds_x.append((p * (dp_x[i] - blk(dlc_ref, dln_ref, c, c0)) * SCALE).astype(BF16))
            p_x.append(p.astype(BF16))
        for i, (b, c0) in enumerate(own):
            xq = b * HPG + i % HPG
            xk = (b + 1) * HPG + i % HPG
            dq = _dot(ds_o[i], blk(kc_ref, None, b, c0)) + _dot(ds_x[xq], blk(kc_ref, kp_ref, b - 1, c0))
            dk = (_dot(ds_o[i], blk(qc_ref, None, b, c0), "tn")
                  + _dot(ds_x[xk], blk(qc_ref, qn_ref, b + 1, c0), "tn"))
            dv = (_dot(p_o[i], blk(doc_ref, None, b, c0), "tn")
                  + _dot(p_x[xk], blk(doc_ref, don_ref, b + 1, c0), "tn"))
            r0 = b * BLK
            out_ref[r0:r0 + BLK, c0:c0 + HEAD] = dq.astype(BF16)
            out_ref[r0:r0 + BLK, GROUP_W + c0:GROUP_W + c0 + HEAD] = dk.astype(BF16)
            out_ref[r0:r0 + BLK, 2 * GROUP_W + c0:2 * GROUP_W + c0 + HEAD] = dv.astype(BF16)

    def cur(off):
        return pl.BlockSpec((nb * BLK, GROUP_W), lambda n: (n, off))

    def prev(off):
        return pl.BlockSpec((BLK, GROUP_W), lambda n: (jnp.maximum(n * nb - 1, 0), off))

    def nxt(off):
        return pl.BlockSpec((BLK, GROUP_W), lambda n: (jnp.minimum((n + 1) * nb, n_blocks - 1), off))

    return pl.pallas_call(
        body, grid=(s // (nb * BLK),),
        in_specs=[cur(qo), nxt(qo), cur(ko), prev(ko), cur(vo), prev(vo), cur(0), nxt(0), cur(0), nxt(0),
                  cur(0), nxt(0)],
        out_specs=pl.BlockSpec((nb * BLK, 3 * GROUP_W), lambda n: (n, 0)),
        out_shape=jax.ShapeDtypeStruct((s, 3 * GROUP_W), BF16),
        compiler_params=_params(), name=name,
    )(q_arr, q_arr, k_arr, k_arr, v_arr, v_arr, do, do, lse, lse, delta, delta)


def _merge_weights(lse_refs, c0):
    ls = [r[:, c0:c0 + HEAD] for r in lse_refs]
    m = jnp.maximum(jnp.maximum(ls[0], ls[1]), ls[2])
    es = [jnp.exp(l - m) for l in ls]
    inv = 1.0 / (es[0] + es[1] + es[2])
    return [e * inv for e in es]


def _merge_fwd(os_, lses):
    s = os_[0].shape[0]
    tm = _row_tile(s, 512)

    def body(o0, o1, o2, l0, l1, l2, out_ref):
        for h in range(HPG):
            c0 = h * HEAD
            w = _merge_weights((l0, l1, l2), c0)
            acc = None
            for wg, o in zip(w, (o0, o1, o2)):
                t = wg * o[:, c0:c0 + HEAD].astype(F32)
                acc = t if acc is None else acc + t
            out_ref[:, c0:c0 + HEAD] = acc.astype(BF16)

    spec = pl.BlockSpec((tm, GROUP_W), lambda i: (i, 0))
    return pl.pallas_call(
        body, grid=(s // tm,), in_specs=[spec] * 6, out_specs=spec,
        out_shape=jax.ShapeDtypeStruct((s, 2 * GROUP_W), BF16),
        compiler_params=_params(), name="merge_fwd",
    )(*os_, *lses)


def _merge_bwd(dcat, os_, lses):
    s = os_[0].shape[0]
    tm = _row_tile(s, 512)

    def body(d_ref, o0, o1, o2, l0, l1, l2, do0, do1, do2, dl0, dl1, dl2):
        for h in range(HPG):
            c0 = h * HEAD
            w = _merge_weights((l0, l1, l2), c0)
            dm = d_ref[:, c0:c0 + HEAD].astype(F32)
            merged = None
            for wg, o in zip(w, (o0, o1, o2)):
                t = wg * o[:, c0:c0 + HEAD].astype(F32)
                merged = t if merged is None else merged + t
            abar = jnp.sum(dm * merged, axis=-1, keepdims=True)
            for wg, do_ref, dl_ref in zip(w, (do0, do1, do2), (dl0, dl1, dl2)):
                do_ref[:, c0:c0 + HEAD] = (wg * dm).astype(BF16)
                dl_ref[:, c0:c0 + HEAD] = wg * abar

    spec = pl.BlockSpec((tm, GROUP_W), lambda i: (i, 0))
    return pl.pallas_call(
        body, grid=(s // tm,), in_specs=[spec] * 7, out_specs=[spec] * 6,
        out_shape=[jax.ShapeDtypeStruct((s, GROUP_W), BF16)] * 3 + [jax.ShapeDtypeStruct((s, GROUP_W), F32)] * 3,
        compiler_params=_params(), name="merge_bwd",
    )(dcat, *os_, *lses)


def _assemble_dproj(dqkv, dqm, tabs):
    s = dqm.shape[0]
    tm = _row_tile(s, 256)
    width = 3 * N_GROUPS * GROUP_W + GROUP_W

    def body(d0, d1, d2, dm_ref, c_ref, s1_ref, s2_ref, out_ref):
        c, s1, s2 = c_ref[...], s1_ref[...], s2_ref[...]
        for g, d_ref in enumerate((d0, d1, d2)):
            for part in range(3):
                for h in range(HPG):
                    src = part * GROUP_W + h * HEAD
                    dst = part * N_GROUPS * GROUP_W + g * GROUP_W + h * HEAD
                    seg = d_ref[:, src:src + HEAD]
                    if part < 2:
                        t = seg.astype(F32)
                        t = t * c - pltpu.roll(t, HEAD - ROT_HALF, 1) * s2 - pltpu.roll(t, ROT_HALF, 1) * s1
                        seg = t.astype(BF16)
                    out_ref[:, dst:dst + HEAD] = seg
        out_ref[:, 3 * N_GROUPS * GROUP_W:] = dm_ref[...]

    g_spec = pl.BlockSpec((tm, 3 * GROUP_W), lambda i: (i, 0))
    m_spec = pl.BlockSpec((tm, GROUP_W), lambda i: (i, 0))
    t_spec = pl.BlockSpec((tm, HEAD), lambda i: (i, 0))
    return pl.pallas_call(
        body, grid=(s // tm,), in_specs=[g_spec] * 3 + [m_spec] + [t_spec] * 3,
        out_specs=pl.BlockSpec((tm, width), lambda i: (i, 0)),
        out_shape=jax.ShapeDtypeStruct((s, width), BF16),
        compiler_params=_params(), name="assemble_dproj",
    )(*dqkv, dqm, *tabs)


def _mem_softmax(q, k):
    s = _dot(q, k, "nt") * SCALE
    m = jnp.max(s, axis=-1, keepdims=True)
    p = jnp.exp(s - m)
    return p * (1.0 / jnp.sum(p, axis=-1, keepdims=True))


def _memattn_fwd(q_arr, q_off, kv, into, into_off, name):
    s = q_arr.shape[0]
    mlen = kv.shape[0]
    tq = _row_tile(s, 512)

    def body(q_ref, kv_ref, into_ref, o_ref):
        for h in range(HPG):
            c0 = h * HEAD
            p = _mem_softmax(q_ref[:, c0:c0 + HEAD], kv_ref[:, c0:c0 + HEAD])
            o = _dot(p.astype(BF16), kv_ref[:, GROUP_W + c0:GROUP_W + c0 + HEAD])
            o_ref[:, c0:c0 + HEAD] = o.astype(BF16)

    return pl.pallas_call(
        body, grid=(s // tq,),
        in_specs=[pl.BlockSpec((tq, GROUP_W), lambda i: (i, q_off)),
                  pl.BlockSpec((mlen, 2 * GROUP_W), lambda i: (0, 0)),
                  pl.BlockSpec(memory_space=pl.ANY)],
        out_specs=pl.BlockSpec((tq, GROUP_W), lambda i: (i, into_off)),
        out_shape=jax.ShapeDtypeStruct(into.shape, into.dtype),
        input_output_aliases={2: 0},
        compiler_params=_params(), name=name,
    )(q_arr, kv, into)


def _memattn_bwd(q_arr, q_off, kv, dcat, d_off, name, into=None, into_off=0):
    s = q_arr.shape[0]
    mlen = kv.shape[0]
    tq = _row_tile(s, 512)

    def body(*refs):
        q_ref, kv_ref, d_ref = refs[:3]
        dq_ref, dkv_ref = refs[-2:]
        i = pl.program_id(0)

        @pl.when(i == 0)
        def _():
            dkv_ref[...] = jnp.zeros_like(dkv_ref)

        for h in range(HPG):
            c0 = h * HEAD
            q = q_ref[:, c0:c0 + HEAD]
            k = kv_ref[:, c0:c0 + HEAD]
            v = kv_ref[:, GROUP_W + c0:GROUP_W + c0 + HEAD]
            do = d_ref[:, c0:c0 + HEAD]
            p = _mem_softmax(q, k)
            dp = _dot(do, v, "nt")
            ds = p * (dp - jnp.sum(p * dp, axis=-1, keepdims=True)) * SCALE
            dsb = ds.astype(BF16)
            dq_ref[:, c0:c0 + HEAD] = _dot(dsb, k).astype(BF16)
            dkv_ref[:, c0:c0 + HEAD] += _dot(dsb, q, "tn")
            dkv_ref[:, GROUP_W + c0:GROUP_W + c0 + HEAD] += _dot(p.astype(BF16), do, "tn")

    in_specs = [pl.BlockSpec((tq, GROUP_W), lambda i: (i, q_off)),
                pl.BlockSpec((mlen, 2 * GROUP_W), lambda i: (0, 0)),
                pl.BlockSpec((tq, GROUP_W), lambda i: (i, d_off))]
    args = [q_arr, kv, dcat]
    dq_shape = jax.ShapeDtypeStruct((s, GROUP_W), BF16)
    aliases = {}
    if into is not None:
        in_specs.append(pl.BlockSpec(memory_space=pl.ANY))
        args.append(into)
        dq_shape = jax.ShapeDtypeStruct(into.shape, into.dtype)
        aliases = {3: 0}
    return pl.pallas_call(
        body, grid=(s // tq,), in_specs=in_specs,
        out_specs=[pl.BlockSpec((tq, GROUP_W), lambda i: (i, into_off)),
                   pl.BlockSpec((mlen, 2 * GROUP_W), lambda i: (0, 0))],
        out_shape=[dq_shape, jax.ShapeDtypeStruct((mlen, 2 * GROUP_W), F32)],
        input_output_aliases=aliases, compiler_params=_params(), name=name,
    )(*args)


SGU_TILE = 256


def _sgu_norm(vg):
    mu = jnp.mean(vg, axis=-1, keepdims=True)
    xc = vg - mu
    var = jnp.mean(xc * xc, axis=-1, keepdims=True)
    rstd = lax.rsqrt(var + LN_EPS)
    return xc * rstd, rstd


def _tril_mask():
    r = lax.broadcasted_iota(jnp.int32, (BLK, BLK), 0)
    c = lax.broadcasted_iota(jnp.int32, (BLK, BLK), 1)
    return r >= c


def _sgu_fwd(proj, ln_g, ln_b, w_s, b_st):
    s = proj.shape[0]
    ts = _row_tile(s, SGU_TILE)

    def body(u_ref, v_ref, g_ref, b_ref, ws_ref, bst_ref, o_ref):
        ug = _gelu(u_ref[...].astype(F32))
        xhat, _ = _sgu_norm(_gelu(v_ref[...].astype(F32)))
        vn = (xhat * g_ref[...] + b_ref[...]).astype(BF16)
        tri = _tril_mask()
        for g in range(SGU_GROUPS):
            c0 = g * HEAD
            w = jnp.where(tri, ws_ref[g], 0.0).astype(BF16)
            bias = bst_ref[:, g:g + 1]
            for ch in range(ts // BLK):
                r0 = ch * BLK
                mixed = _dot(w, vn[r0:r0 + BLK, c0:c0 + HEAD]) + bias
                o_ref[r0:r0 + BLK, c0:c0 + HEAD] = (ug[r0:r0 + BLK, c0:c0 + HEAD] * mixed).astype(BF16)

    vec = pl.BlockSpec((1, SGU_W), lambda i: (0, 0))
    return pl.pallas_call(
        body, grid=(s // ts,),
        in_specs=[pl.BlockSpec((ts, SGU_W), lambda i: (i, 0)), pl.BlockSpec((ts, SGU_W), lambda i: (i, 1)),
                  vec, vec, pl.BlockSpec((SGU_GROUPS, BLK, BLK), lambda i: (0, 0, 0)),
                  pl.BlockSpec((BLK, SGU_GROUPS), lambda i: (0, 0))],
        out_specs=pl.BlockSpec((ts, SGU_W), lambda i: (i, 0)),
        out_shape=jax.ShapeDtypeStruct((s, SGU_W + GROUP_W), BF16),
        compiler_params=_params(), name="sgu_fwd",
    )(proj, proj, ln_g, ln_b, w_s, b_st)


def _sgu_bwd(proj, dcat, ln_g, ln_b, w_s, b_st):
    s = proj.shape[0]
    ts = _row_tile(s, SGU_TILE)

    def body(u_ref, v_ref, d_ref, g_ref, b_ref, ws_ref, bst_ref,
             duv_ref, dws_ref, dbs_ref, dg_ref, db_ref, dvn_ref):
        i = pl.program_id(0)

        @pl.when(i == 0)
        def _():
            dws_ref[...] = jnp.zeros_like(dws_ref)
            dbs_ref[...] = jnp.zeros_like(dbs_ref)
            dg_ref[...] = jnp.zeros_like(dg_ref)
            db_ref[...] = jnp.zeros_like(db_ref)

        u = u_ref[...].astype(F32)
        v = v_ref[...].astype(F32)
        ug, dug = _gelu_and_grad(u)
        vg, dvg_dv = _gelu_and_grad(v)
        xhat, rstd = _sgu_norm(vg)
        lng = g_ref[...]
        vn = (xhat * lng + b_ref[...]).astype(BF16)
        dout = d_ref[...].astype(F32)
        tri = _tril_mask()
        lane = lax.broadcasted_iota(jnp.int32, (BLK, BLK), 1)
        dbs = jnp.zeros((BLK, BLK), F32)
        for g in range(SGU_GROUPS):
            c0 = g * HEAD
            w = jnp.where(tri, ws_ref[g], 0.0).astype(BF16)
            bias = bst_ref[:, g:g + 1]
            dws = jnp.zeros((BLK, BLK), F32)
            for ch in range(ts // BLK):
                r0 = ch * BLK
                vn_gc = vn[r0:r0 + BLK, c0:c0 + HEAD]
                mixed = _dot(w, vn_gc) + bias
                do_gc = dout[r0:r0 + BLK, c0:c0 + HEAD]
                dmixed = do_gc * ug[r0:r0 + BLK, c0:c0 + HEAD]
                du = do_gc * mixed * dug[r0:r0 + BLK, c0:c0 + HEAD]
                duv_ref[r0:r0 + BLK, c0:c0 + HEAD] = du.astype(BF16)
                dmb = dmixed.astype(BF16)
                dws = dws + _dot(dmb, vn_gc, "nt")
                dbs = dbs + jnp.where(lane == g, jnp.sum(dmixed, axis=-1, keepdims=True), 0.0)
                dvn_ref[r0:r0 + BLK, c0:c0 + HEAD] = _dot(w, dmb, "tn")
            dws_ref[g] += jnp.where(tri, dws, 0.0)
        dbs_ref[...] += dbs
        dvn = dvn_ref[...]
        gd = dvn * lng
        c1 = jnp.mean(gd, axis=-1, keepdims=True)
        c2 = jnp.mean(gd * xhat, axis=-1, keepdims=True)
        dvg = rstd * (gd - c1 - xhat * c2)
        duv_ref[:, SGU_W:] = (dvg * dvg_dv).astype(BF16)
        dg_ref[...] += jnp.sum(dvn * xhat, axis=0, keepdims=True)
        db_ref[...] += jnp.sum(dvn, axis=0, keepdims=True)

    vec = pl.BlockSpec((1, SGU_W), lambda i: (0, 0))
    ws_spec = pl.BlockSpec((SGU_GROUPS, BLK, BLK), lambda i: (0, 0, 0))
    return pl.pallas_call(
        body, grid=(s // ts,),
        in_specs=[pl.BlockSpec((ts, SGU_W), lambda i: (i, 0)), pl.BlockSpec((ts, SGU_W), lambda i: (i, 1)),
                  pl.BlockSpec((ts, SGU_W), lambda i: (i, 0)),
                  vec, vec, ws_spec, pl.BlockSpec((BLK, SGU_GROUPS), lambda i: (0, 0))],
        out_specs=[pl.BlockSpec((ts, 2 * SGU_W), lambda i: (i, 0)), ws_spec,
                   pl.BlockSpec((BLK, BLK), lambda i: (0, 0)), vec, vec],
        out_shape=[jax.ShapeDtypeStruct((s, 2 * SGU_W + GROUP_W), BF16),
                   jax.ShapeDtypeStruct((SGU_GROUPS, BLK, BLK), F32),
                   jax.ShapeDtypeStruct((BLK, BLK), F32),
                   jax.ShapeDtypeStruct((1, SGU_W), F32), jax.ShapeDtypeStruct((1, SGU_W), F32)],
        scratch_shapes=[pltpu.VMEM((ts, SGU_W), F32)],
        compiler_params=_params(), name="sgu_bwd",
    )(proj, proj, dcat, ln_g, ln_b, w_s, b_st)


def _swiglu_fwd(h, wg_t, wu_t, name, side=None):
    s, d = h.shape
    f = wg_t.shape[0]
    tm, tn = _row_tile(s, 1024), _row_tile(f, 512)

    def epilogue(parts, ex, out):
        g, u = parts
        sg = _sigmoid(g)
        silu = g * sg
        out[0][...] = silu.astype(BF16)
        out[1][...] = (u * (sg + silu * (1.0 - sg))).astype(BF16)
        out[2][...] = (silu * u).astype(BF16)

    a_spec = pl.BlockSpec((tm, d), lambda n, m, k: (m, 0))
    w_spec = pl.BlockSpec((tn, d), lambda n, m, k: (n, 0))
    o_spec = pl.BlockSpec((tm, tn), lambda n, m, k: (m, n))
    o_shape = jax.ShapeDtypeStruct((s, f), BF16)
    return _matmul(name, (f // tn, s // tm, 1),
                   [(h, a_spec, wg_t, w_spec, "nt", 0), (h, a_spec, wu_t, w_spec, "nt", 1)],
                   2, None, [], [(o_shape, o_spec)] * 3, epilogue, side)


def _swiglu_down(hid, wd, res, name, side=None):
    s, f = hid.shape
    d = wd.shape[1]
    tm, tn = _row_tile(s, 1024), _row_tile(d, 512)
    o_spec = pl.BlockSpec((tm, tn), lambda n, m, k: (m, n))
    return _one(_matmul(
        name, (d // tn, s // tm, 1),
        [(hid, pl.BlockSpec((tm, f), lambda n, m, k: (m, 0)),
          wd, pl.BlockSpec((f, tn), lambda n, m, k: (0, n)), "nn", 0)],
        1, None, [(res, o_spec)], [(jax.ShapeDtypeStruct((s, d), F32), o_spec)],
        _residual_epilogue, side), side)


def _swiglu_bwd_hidden(dxb, wd, silu, up_dsilu, name):
    s, f = silu.shape
    d = dxb.shape[1]
    tm, tn = _row_tile(s, 2048), _row_tile(f, 512)

    def epilogue(parts, ex, out):
        dh = parts[0]
        out[0][...] = (dh * ex[1][...].astype(F32)).astype(BF16)
        out[1][...] = (dh * ex[0][...].astype(F32)).astype(BF16)

    blk = pl.BlockSpec((tm, tn), lambda m, n, k: (m, n))
    o_shape = jax.ShapeDtypeStruct((s, f), BF16)
    return _matmul(
        name, (s // tm, f // tn, 1),
        [(dxb, pl.BlockSpec((tm, d), lambda m, n, k: (m, 0)),
          wd, pl.BlockSpec((tn, d), lambda m, n, k: (n, 0)), "nt", 0)],
        1, None, [(silu, blk), (up_dsilu, blk)], [(o_shape, blk)] * 2, epilogue)


def _swiglu_bwd_input(dgate, dup, wg_t, wu_t, name, side=None):
    s, f = dgate.shape
    d = wg_t.shape[1]
    tm, tn, tk = _row_tile(s, 1024), _row_tile(d, 512), f // 2
    a_spec = pl.BlockSpec((tm, tk), lambda n, m, k: (m, k))
    w_spec = pl.BlockSpec((tk, tn), lambda n, m, k: (k, n))
    return _one(_matmul(
        name, (d // tn, s // tm, f // tk),
        [(dgate, a_spec, wg_t, w_spec, "nn", 0), (dup, a_spec, wu_t, w_spec, "nn", 0)],
        1, (tm, tn), [],
        [(jax.ShapeDtypeStruct((s, d), BF16), pl.BlockSpec((tm, tn), lambda n, m, k: (m, n)))],
        _store_epilogue, side), side)


def _mm_tn_full(a, b, name, side=None):
    s, m = a.shape
    n = b.shape[1]
    tm, tn = _row_tile(m, 512), _row_tile(n, 512)
    return _one(_matmul(
        name, (m // tm, n // tn, 1),
        [(a, pl.BlockSpec((s, tm), lambda i, j, k: (0, i)),
          b, pl.BlockSpec((s, tn), lambda i, j, k: (0, j)), "tn", 0)],
        1, None, [],
        [(jax.ShapeDtypeStruct((m, n), BF16), pl.BlockSpec((tm, tn), lambda i, j, k: (i, j)))],
        _store_epilogue, side), side)


def _mm_nn(a, b, name, tn, out_dtype=BF16, res=None, side=None):
    m, k = a.shape
    n = b.shape[1]
    tm = _row_tile(m, 1024)
    extras = [] if res is None else [(res, pl.BlockSpec((tm, tn), lambda j, i, kk: (i, j)))]
    return _one(_matmul(
        name, (n // tn, m // tm, 1),
        [(a, pl.BlockSpec((tm, k), lambda j, i, kk: (i, 0)),
          b, pl.BlockSpec((k, tn), lambda j, i, kk: (0, j)), "nn", 0)],
        1, None, extras,
        [(jax.ShapeDtypeStruct((m, n), out_dtype), pl.BlockSpec((tm, tn), lambda j, i, kk: (i, j)))],
        _store_epilogue if res is None else _residual_epilogue, side), side)


def _mm_nn_colblocked(a, b_blk, name, res, side=None):
    m, k = a.shape
    nb, _, bw = b_blk.shape
    tm = _row_tile(m, 2048)
    o_spec = pl.BlockSpec((tm, bw), lambda j, i, kk: (i, j))
    return _one(_matmul(
        name, (nb, m // tm, 1),
        [(a, pl.BlockSpec((tm, k), lambda j, i, kk: (i, 0)),
          b_blk, pl.BlockSpec((None, k, bw), lambda j, i, kk: (j, 0, 0)), "nn", 0)],
        1, None, [(res, o_spec)],
        [(jax.ShapeDtypeStruct((m, nb * bw), F32), o_spec)], _residual_epilogue, side), side)


def _mm_nt_colblocked(a, b_blk, name, out_dtype, jb, side=None):
    m = a.shape[0]
    nb, n, bw = b_blk.shape
    tm = _row_tile(m, 512)
    return _one(_matmul(
        name, (m // tm, nb // jb),
        [(a, pl.BlockSpec((tm, jb * bw), lambda i, j: (i, j)),
          b_blk, pl.BlockSpec((jb, n, bw), lambda i, j: (j, 0, 0)), "nt", 0)],
        1, (tm, n), [],
        [(jax.ShapeDtypeStruct((m, n), out_dtype), pl.BlockSpec((tm, n), lambda i, j: (i, 0)))],
        _store_epilogue, side), side)


def _mm_nt_rowblocked(a, b, name, tn, out_dtype, side=None):
    m, k = a.shape
    n = b.shape[0]
    tm = _row_tile(m, 1024)
    return _one(_matmul(
        name, (n // tn, m // tm, 1),
        [(a, pl.BlockSpec((tm, k), lambda j, i, kk: (i, 0)),
          b, pl.BlockSpec((tn, k), lambda j, i, kk: (j, 0)), "nt", 0)],
        1, None, [],
        [(jax.ShapeDtypeStruct((m, n), out_dtype), pl.BlockSpec((tm, tn), lambda j, i, kk: (i, j)))],
        _store_epilogue, side), side)


def _mm_tn_colblocked(a, b, name, bw, side=None):
    s, m = a.shape
    nb = b.shape[1] // bw
    tm = _row_tile(m, 512)
    return _one(_matmul(
        name, (nb, m // tm, 1),
        [(a, pl.BlockSpec((s, tm), lambda j, i, k: (0, i)),
          b, pl.BlockSpec((s, bw), lambda j, i, k: (0, j)), "tn", 0)],
        1, None, [],
        [(jax.ShapeDtypeStruct((nb, m, bw), BF16), pl.BlockSpec((None, tm, bw), lambda j, i, k: (j, i, 0)))],
        _store_epilogue, side), side)


def _mm_tn_rowblocked(a, b, name, bh):
    s, n = b.shape
    nb = a.shape[1] // bh
    tn = _row_tile(n, 512)
    return _matmul(
        name, (nb, n // tn, 1),
        [(a, pl.BlockSpec((s, bh), lambda j, i, k: (0, j)),
          b, pl.BlockSpec((s, tn), lambda j, i, k: (0, i)), "tn", 0)],
        1, None, [],
        [(jax.ShapeDtypeStruct((nb, bh, n), BF16), pl.BlockSpec((None, bh, tn), lambda j, i, k: (j, 0, i)))],
        _store_epilogue)[0]


def _as2d(a):
    return a.reshape(-1, a.shape[-1])


def _cast_bf16(w, name):
    w2 = _as2d(w)
    rows, cols = w2.shape
    tr = _row_tile(rows, 256)

    def body(w_ref, o_ref):
        o_ref[...] = w_ref[...].astype(BF16)

    spec = pl.BlockSpec((tr, cols), lambda i: (i, 0))
    out = pl.pallas_call(
        body, grid=(rows // tr,), in_specs=[spec], out_specs=spec,
        out_shape=jax.ShapeDtypeStruct((rows, cols), BF16),
        compiler_params=_params(), name=name,
    )(w2)
    return out.reshape(w.shape)


def _cast_bf16_layer(w, layer, name):
    _, rows, cols = w.shape
    tr = _row_tile(rows, 256)

    def body(w_ref, o_ref):
        o_ref[...] = w_ref[...].astype(BF16)

    return pl.pallas_call(
        body, grid=(rows // tr,),
        in_specs=[pl.BlockSpec((None, tr, cols), lambda i: (layer, i, 0))],
        out_specs=pl.BlockSpec((tr, cols), lambda i: (i, 0)),
        out_shape=jax.ShapeDtypeStruct((rows, cols), BF16),
        compiler_params=_params(), name=name,
    )(w)


ADAM_TILE_ELEMS = 384 * 1024


def _reduce_adam(recvs, w, m, v, name, side=None):
    n_layers, rows, cols = w.shape
    n_slots = recvs[0].shape[0]
    tr = _row_tile(rows, max(16, (ADAM_TILE_ELEMS // cols) // 16 * 16))
    nt = rows // tr
    c1 = 1.0 - ADAM_B1 ** ADAM_STEP
    c2 = 1.0 - ADAM_B2 ** ADAM_STEP

    def body(*refs):
        r_refs = refs[:n_layers]
        w_ref, m_ref, v_ref, g_out, d_out, m_out, v_out = refs[n_layers:]
        layer = pl.program_id(0)

        def update(r_ref):
            g = r_ref[0].astype(F32)
            for k in range(1, n_slots):
                g = g + r_ref[k].astype(F32)
            mm = ADAM_B1 * m_ref[...] + (1.0 - ADAM_B1) * g
            vv = ADAM_B2 * v_ref[...] + (1.0 - ADAM_B2) * (g * g)
            m_hat = mm / c1
            v_hat = vv / c2
            g_out[...] = g
            d_out[...] = -ADAM_LR * (m_hat / (jnp.sqrt(v_hat) + ADAM_EPS) + ADAM_WD * w_ref[...])
            m_out[...] = mm
            v_out[...] = vv

        for li in range(n_layers):
            if n_layers == 1:
                update(r_refs[li])
            else:
                pl.when(layer == li)(functools.partial(update, r_refs[li]))

    def recv_spec(li):
        def imap(layer, i):
            return (0, jnp.where(layer == li, i, jnp.where(layer < li, 0, nt - 1)), 0)
        return pl.BlockSpec((n_slots, tr, cols), imap)

    spec = pl.BlockSpec((None, tr, cols), lambda layer, i: (layer, i, 0))
    o_shape = jax.ShapeDtypeStruct(w.shape, F32)
    grid = (n_layers, nt)
    s_in, s_args, s_out, s_shapes, s_scratch = _side_specs(side)
    res = pl.pallas_call(
        _carrying(body, n_layers + 3, 4, 0, grid, side), grid=grid,
        in_specs=[recv_spec(li) for li in range(n_layers)] + [spec] * 3 + s_in,
        out_specs=[spec] * 4 + s_out, out_shape=[o_shape] * 4 + s_shapes,
        scratch_shapes=s_scratch, compiler_params=_params(), name=name,
    )(*recvs, w, m, v, *s_args)
    return res if side is None else (res[:4], res[4:])


def _my_place():
    return lax.axis_index("x"), lax.axis_index("y"), lax.axis_index("c")


class _GatherSide:
    def __init__(self, blocks):
        self.ins = list(blocks)
        self.outs = [jax.ShapeDtypeStruct((N_DEV,) + b.shape, b.dtype) for b in blocks]

    def scratch(self):
        n = len(self.ins)
        return [pltpu.SemaphoreType.DMA((7 * n,)), pltpu.SemaphoreType.DMA((7 * n,)),
                pltpu.SemaphoreType.DMA((n,))]

    def phases(self, x_refs, out_refs, send_sems, recv_sems, local_sems):
        n = len(self.ins)
        x, y, c = _my_place()
        me, sibling = (x, y, c), (x, y, 1 - c)
        chips = [(1 - x, y), (x, 1 - y), (1 - x, 1 - y)]

        def slot(t, px, py, pc):
            return out_refs[t].at[4 * px + 2 * py + pc]

        def copy(t, k, blk, to, src=None):
            return pltpu.make_async_remote_copy(
                src_ref=slot(t, *blk) if src is None else src, dst_ref=slot(t, *blk),
                send_sem=send_sems.at[7 * t + k], recv_sem=recv_sems.at[7 * t + k],
                device_id=to, device_id_type=pl.DeviceIdType.MESH)

        def own(t):
            return pltpu.make_async_copy(x_refs[t], slot(t, *me), local_sems.at[t])

        def first(t):
            return [copy(t, 0, me, sibling, src=x_refs[t])] + [
                copy(t, 1 + j, me, (*chip, c), src=x_refs[t]) for j, chip in enumerate(chips)]

        def passed(t):
            return [copy(t, 4 + j, (*chip, c), sibling) for j, chip in enumerate(chips)]

        def start():
            for t in range(n):
                own(t).start()
                for cp in first(t):
                    cp.start()

        def mid():
            for t in range(n):
                fwd = passed(t)
                for j, chip in enumerate(chips):
                    copy(t, 1 + j, (*chip, c), me).wait_recv()
                    fwd[j].start()

        def finish():
            for t in range(n):
                copy(t, 0, sibling, me).wait_recv()
                for j, chip in enumerate(chips):
                    copy(t, 4 + j, (*chip, 1 - c), me).wait_recv()
                for cp in first(t) + passed(t):
                    cp.wait_send()
                own(t).wait()

        return start, mid, finish


class _ExchangeSide:
    def __init__(self, blocked):
        self.ins = list(blocked)
        self.outs = [jax.ShapeDtypeStruct(b.shape, b.dtype) for b in blocked]

    def scratch(self):
        n = len(self.ins)
        return [pltpu.SemaphoreType.DMA((7 * n,)), pltpu.SemaphoreType.DMA((7 * n,)),
                pltpu.SemaphoreType.DMA((n,))]

    def phases(self, srcs, dsts, send_sems, recv_sems, local_sems):
        n = len(self.ins)
        x, y, c = _my_place()
        me = 4 * x + 2 * y + c

        def own(t):
            return pltpu.make_async_copy(srcs[t].at[me], dsts[t].at[me], local_sems.at[t])

        def copies(t, arriving):
            res = []
            for k in range(1, N_DEV):
                px, py, pc = x ^ (k >> 2), y ^ ((k >> 1) & 1), c ^ (k & 1)
                peer = 4 * px + 2 * py + pc
                sem = 7 * t + k - 1
                res.append(pltpu.make_async_remote_copy(
                    src_ref=srcs[t].at[peer], dst_ref=dsts[t].at[peer if arriving else me],
                    send_sem=send_sems.at[sem], recv_sem=recv_sems.at[sem],
                    device_id=(px, py, pc), device_id_type=pl.DeviceIdType.MESH))
            return res

        def start():
            for t in range(n):
                own(t).start()
                for send in copies(t, False):
                    send.start()

        def mid():
            pass

        def finish():
            for t in range(n):
                for arrival in copies(t, True):
                    arrival.wait_recv()
                for send in copies(t, False):
                    send.wait_send()
                own(t).wait()

        return start, mid, finish


def _to_residue_major(a, dilation):
    s, w = a.shape
    return a.reshape(s // dilation, dilation, w).transpose(1, 0, 2).reshape(s, w)


def _from_residue_major(a, dilation):
    s, w = a.shape
    return a.reshape(dilation, s // dilation, w).transpose(1, 0, 2).reshape(s, w)


def _mem_kv(mem, gain, wkv, layer, tag):
    mem_n = _rmsnorm_fwd(mem, gain, "mem_norm_" + tag)
    mlen, d = mem.shape
    nb, _, bh, n = wkv.shape
    kv = _matmul(
        "mem_kv_" + tag, (1, nb),
        [(mem_n, pl.BlockSpec((mlen, bh), lambda i, j: (0, j)),
          wkv, pl.BlockSpec((None, None, bh, n), lambda i, j: (j, layer, 0, 0)), "nn", 0)],
        1, (mlen, n), [],
        [(jax.ShapeDtypeStruct((mlen, n), BF16), pl.BlockSpec((mlen, n), lambda i, j: (0, 0)))],
        _store_epilogue)[0]
    return mem_n, kv


def _mem_kv_bwd(mem, gain, mem_n, wkv, layer, dkv, tag):
    mlen, d = mem.shape
    nb, _, bh, n = wkv.shape
    dkvb = dkv.astype(BF16)
    dw = _mm_tn_rowblocked(mem_n, dkvb, "mem_kv_dw_" + tag, bh)
    dmem_n = _matmul(
        "mem_kv_dx_" + tag, (nb, 1),
        [(dkvb, pl.BlockSpec((mlen, n), lambda j, k: (0, 0)),
          wkv, pl.BlockSpec((None, None, bh, n), lambda j, k: (j, layer, 0, 0)), "nt", 0)],
        1, None, [],
        [(jax.ShapeDtypeStruct((mlen, d), F32), pl.BlockSpec((mlen, bh), lambda j, k: (0, j)))],
        _store_epilogue)[0]
    _, dgain = _rmsnorm_bwd(mem, gain, dmem_n, None, "mem_norm_bwd_" + tag)
    return dw, dgain


def _row_blocks(a):
    return a.reshape(N_DEV, -1, a.shape[-1])


def _rows(a):
    return a.reshape(-1, a.shape[-1])


def _ffn_bwd(x, gain, w_gate, w_up, w_down, saved, dxb, tag):
    hf, silu, up_dsilu, hid = saved
    dgate, dup = _swiglu_bwd_hidden(dxb, w_down, silu, up_dsilu, "swiglu_bwd_hidden_" + tag)
    dwd = _mm_tn_full(hid, dxb, "swiglu_bwd_wdown_" + tag)
    dwg, (r_wd,) = _mm_tn_full(dgate, hf, "swiglu_bwd_wgate_" + tag, _ExchangeSide([_row_blocks(dwd)]))
    dwu, (r_wg,) = _mm_tn_full(dup, hf, "swiglu_bwd_wup_" + tag, _ExchangeSide([_row_blocks(dwg)]))
    dhf, (r_wu,) = _swiglu_bwd_input(dgate, dup, w_gate, w_up, "swiglu_bwd_input_" + tag,
                                     _ExchangeSide([_row_blocks(dwu)]))
    dxb_new, dgain = _rmsnorm_bwd(x, gain, dhf, dxb, "ffn_norm_bwd_" + tag)
    return dxb_new, dgain, r_wg, r_wu, r_wd


def _local_step(x, mem, positions, target, first_blocks, shards, small):
    s, d = x.shape
    tabs = _rotary_tables(positions)
    mix_norm, mem_norm, ffn_norm = small["mix_norm"], small["mem_norm"], small["ffn_norm"]

    h0, (w_attn_in, ln_all) = _rmsnorm_fwd(x, mix_norm[0:1], "mix_norm_0", _GatherSide(first_blocks))
    w_attn_in = _rows(w_attn_in)
    ln_full = ln_all.transpose(1, 0, 2).reshape(2, 1, -1)
    ln_g, ln_b = ln_full[0], ln_full[1]
    proj0, (w_mem_kv, w_attn_out, w_gate0) = _attn_in_proj(
        h0, w_attn_in, tabs, _GatherSide([shards["w_mem_kv"], shards["attn_w_out"], shards["w_gate"][0]]))
    up_shard = shards["w_up"][0]
    cut = -(-up_shard.shape[0] // N_GROUPS // 16) * 16
    up_pieces = [up_shard[g * cut:min((g + 1) * cut, up_shard.shape[0])] for g in range(N_GROUPS)]
    qkv, offs, outs, lses, up_gathered = [], [], [], [], []
    for g, dil in enumerate(DILATIONS):
        if dil == 1:
            arr, off = proj0, (g, N_GROUPS + g, 2 * N_GROUPS + g)
        else:
            cols = [proj0[:, (p * N_GROUPS + g) * GROUP_W:(p * N_GROUPS + g + 1) * GROUP_W] for p in range(3)]
            arr, off = _to_residue_major(jnp.concatenate(cols, axis=1), dil), (0, 1, 2)
        o, lse, piece = _attn_fwd(arr, arr, arr, off, s // dil // BLK, "attn_fwd_%d" % g,
                                  _GatherSide([up_pieces[g]]))
        up_gathered.append(piece)
        qkv.append(arr)
        offs.append(off)
        if dil > 1:
            o, lse = _from_residue_major(o, dil), _from_residue_major(lse, dil)
        outs.append(o)
        lses.append(lse)
    mix0 = _merge_fwd(outs, lses)
    qm_off0 = 3 * N_GROUPS
    mem_n0, kv0 = _mem_kv(mem, mem_norm[0:1], w_mem_kv, 0, "0")
    cat0 = _memattn_fwd(proj0, qm_off0, kv0, mix0, 1, "memattn_fwd_0")
    x1 = _mm_nn_colblocked(cat0, w_attn_out, "attn_out_proj", x)
    hf0 = _rmsnorm_fwd(x1, ffn_norm[0:1], "ffn_norm_0")
    w_gate0, w_up0 = _rows(w_gate0), _rows(jnp.concatenate(up_gathered, axis=1))
    (silu0, ud0, hid0), (w_down0, w_sgu_in, w_sgu_out) = _swiglu_fwd(
        hf0, w_gate0, w_up0, "swiglu_fwd_0",
        _GatherSide([shards["w_down"][0], shards["sgu_w_in"], shards["sgu_w_out"]]))
    w_down0 = _rows(w_down0)
    x2, (w_gate1,) = _swiglu_down(hid0, w_down0, x1, "swiglu_down_0", _GatherSide([shards["w_gate"][1]]))
    ffn_saved0 = (hf0, silu0, ud0, hid0)

    h1 = _rmsnorm_fwd(x2, mix_norm[1:2], "mix_norm_1")
    w_sgu_in = _rows(w_sgu_in)
    w_sgu_out = _rows(w_sgu_out)
    proj1, (w_up1,) = _mm_nt_rowblocked(h1, w_sgu_in, "sgu_in_proj", w_sgu_in.shape[0] // 7, BF16,
                                        _GatherSide([shards["w_up"][1]]))
    w_gate1, w_up1 = _rows(w_gate1), _rows(w_up1)
    b_st = small["sgu_b_spatial"].T
    mix1 = _sgu_fwd(proj1, ln_g, ln_b, small["sgu_w_spatial"], b_st)
    qm_off1 = 2 * SGU_W // GROUP_W
    mem_n1, kv1 = _mem_kv(mem, mem_norm[1:2], w_mem_kv, 1, "1")
    cat1 = _memattn_fwd(proj1, qm_off1, kv1, mix1, SGU_W // GROUP_W, "memattn_fwd_1")
    x3 = _mm_nn(cat1, w_sgu_out, "sgu_out_proj", d // 2, out_dtype=F32, res=x2)
    hf1 = _rmsnorm_fwd(x3, ffn_norm[1:2], "ffn_norm_1")
    (silu1, ud1, hid1), (w_down1,) = _swiglu_fwd(hf1, w_gate1, w_up1, "swiglu_fwd_1",
                                                 _GatherSide([shards["w_down"][1]]))
    w_down1 = _rows(w_down1)
    x4 = _swiglu_down(hid1, w_down1, x3, "swiglu_down_1")
    ffn_saved1 = (hf1, silu1, ud1, hid1)

    loss, dxb, d_final = _loss_head(x4, small["final_norm"], target)

    recvs, sgrads = {}, {}
    dxb, d_ffn1, r_wg1, r_wu1, r_wd1 = _ffn_bwd(x3, ffn_norm[1:2], w_gate1, w_up1, w_down1, ffn_saved1, dxb, "1")
    dcat1 = _mm_nt_rowblocked(dxb, w_sgu_out, "sgu_out_proj_dx", w_sgu_out.shape[0] // 2, BF16)
    dwsout = _row_blocks(_mm_tn_full(cat1, dxb, "sgu_out_proj_dw"))
    duv, dws, dbs, dlng, dlnb = _sgu_bwd(proj1, dcat1, ln_g, ln_b, small["sgu_w_spatial"], b_st)
    dproj1, dkv1 = _memattn_bwd(proj1, qm_off1, kv1, dcat1, SGU_W // GROUP_W, "memattn_bwd_1",
                                into=duv, into_off=2 * SGU_W // GROUP_W)
    dwkv1, d_memnorm1 = _mem_kv_bwd(mem, mem_norm[1:2], mem_n1, w_mem_kv, 1, dkv1, "1")
    dwsin, (r_wsout, r_wkv1) = _mm_tn_full(dproj1, h1, "sgu_in_proj_dw", _ExchangeSide([dwsout, dwkv1]))
    dh1, (r_wsin,) = _mm_nn(dproj1, w_sgu_in, "sgu_in_proj_dx", d // 4, out_dtype=BF16,
                            side=_ExchangeSide([_row_blocks(dwsin)]))
    dxb, d_mix1 = _rmsnorm_bwd(x2, mix_norm[1:2], dh1, dxb, "mix_norm_bwd_1")

    dxb, d_ffn0, r_wg0, r_wu0, r_wd0 = _ffn_bwd(x1, ffn_norm[0:1], w_gate0, w_up0, w_down0, ffn_saved0, dxb, "0")
    dcat0 = _mm_nt_colblocked(dxb, w_attn_out, "attn_out_proj_dx", BF16, 4)
    dwout0 = _mm_tn_colblocked(cat0, dxb, "attn_out_proj_dw", w_attn_out.shape[2])
    dos_and_deltas = _merge_bwd(dcat0, outs, lses)
    dqkv = []
    for g, dil in enumerate(DILATIONS):
        do_g, dl_g = dos_and_deltas[g], dos_and_deltas[N_GROUPS + g]
        lse_g = lses[g]
        if dil > 1:
            do_g, dl_g, lse_g = (_to_residue_major(t, dil) for t in (do_g, dl_g, lse_g))
        t = _attn_bwd(qkv[g], qkv[g], qkv[g], offs[g], do_g, lse_g, dl_g, s // dil // BLK, "attn_bwd_%d" % g)
        dqkv.append(_from_residue_major(t, dil) if dil > 1 else t)
    dqm0, dkv0 = _memattn_bwd(proj0, qm_off0, kv0, dcat0, 1, "memattn_bwd_0")
    dwkv0, d_memnorm0 = _mem_kv_bwd(mem, mem_norm[0:1], mem_n0, w_mem_kv, 0, dkv0, "0")
    dproj0 = _assemble_dproj(dqkv, dqm0, tabs)
    dwin0, (r_wout0, r_wkv0) = _mm_tn_full(dproj0, h0, "attn_in_proj_dw", _ExchangeSide([dwout0, dwkv0]))
    dh0, (r_win0,) = _mm_nn(dproj0, w_attn_in, "attn_in_proj_dx", d // 4, out_dtype=BF16,
                            side=_ExchangeSide([_row_blocks(dwin0)]))
    grad_x, d_mix0 = _rmsnorm_bwd(x, mix_norm[0:1], dh0, dxb, "mix_norm_bwd_0", out_dtype=F32)

    recvs["w_gate"] = [r_wg0, r_wg1]
    recvs["w_up"] = [r_wu0, r_wu1]
    recvs["w_down"] = [r_wd0, r_wd1]
    recvs["w_mem_kv"] = [r_wkv0, r_wkv1]
    recvs["attn_w_in"] = [r_win0]
    recvs["attn_w_out"] = [r_wout0]
    recvs["sgu_w_in"] = [r_wsin]
    recvs["sgu_w_out"] = [r_wsout]
    sgrads["mix_norm"] = jnp.concatenate([d_mix0, d_mix1], axis=0)
    sgrads["mem_norm"] = jnp.concatenate([d_memnorm0, d_memnorm1], axis=0)
    sgrads["ffn_norm"] = jnp.concatenate([d_ffn0, d_ffn1], axis=0)
    sgrads["final_norm"] = d_final
    sgrads["sgu_w_spatial"] = dws
    sgrads["sgu_b_spatial"] = dbs[:, :SGU_GROUPS].T
    sgrads["sgu_ln_g"] = dlng
    sgrads["sgu_ln_b"] = dlnb
    return loss, grad_x, recvs, sgrads


ADAM_ORDER = ("w_gate", "w_up", "w_down", "w_mem_kv", "sgu_w_in", "sgu_w_out", "attn_w_out", "attn_w_in")
TRANSPOSED = ("w_gate", "w_up", "sgu_w_in", "attn_w_in")
SMALL_REPLICATED = ("mix_norm", "mem_norm", "ffn_norm", "final_norm", "sgu_w_spatial", "sgu_b_spatial")
SMALL_SHARDED = ("sgu_ln_g", "sgu_ln_b")
WEIGHT_ORDER = ("mix_norm", "mem_norm", "w_mem_kv", "ffn_norm", "w_gate", "w_up", "w_down", "attn_w_in",
                "attn_w_out", "sgu_w_in", "sgu_ln_g", "sgu_ln_b", "sgu_w_spatial", "sgu_b_spatial",
                "sgu_w_out", "final_norm")
PACK_LANES = 128


def _pack(parts):
    flat = [p.reshape(-1) for p in parts]
    sizes = [f.shape[0] for f in flat]
    total = sum(sizes)
    rows = -(-total // PACK_LANES)
    rows = -(-rows // 8) * 8
    pad = rows * PACK_LANES - total
    packed = jnp.concatenate(flat + [jnp.zeros((pad,), F32)]).reshape(rows, PACK_LANES)
    offs, o = [], 0
    for sz in sizes:
        offs.append(o)
        o += sz
    return packed, offs, sizes


def _unpack(packed, offs, sizes, shapes):
    flat = packed.reshape(-1)
    return [flat[o:o + sz].reshape(shp) for o, sz, shp in zip(offs, sizes, shapes)]


def kernel(x, mem, positions, mix_norm, mem_norm, w_mem_kv, ffn_norm, w_gate, w_up, w_down, attn_w_in, attn_w_out, sgu_w_in, sgu_ln_g, sgu_ln_b, sgu_w_spatial, sgu_b_spatial, sgu_w_out, final_norm, loss_target, m_mix_norm, m_mem_norm, m_w_mem_kv, m_ffn_norm, m_w_gate, m_w_up, m_w_down, m_attn_w_in, m_attn_w_out, m_sgu_w_in, m_sgu_ln_g, m_sgu_ln_b, m_sgu_w_spatial, m_sgu_b_spatial, m_sgu_w_out, m_final_norm, v_mix_norm, v_mem_norm, v_w_mem_kv, v_ffn_norm, v_w_gate, v_w_up, v_w_down, v_attn_w_in, v_attn_w_out, v_sgu_w_in, v_sgu_ln_g, v_sgu_ln_b, v_sgu_w_spatial, v_sgu_b_spatial, v_sgu_w_out, v_final_norm):
    w = dict(mix_norm=mix_norm, mem_norm=mem_norm, w_mem_kv=w_mem_kv, ffn_norm=ffn_norm, w_gate=w_gate,
             w_up=w_up, w_down=w_down, attn_w_in=attn_w_in, attn_w_out=attn_w_out, sgu_w_in=sgu_w_in,
             sgu_ln_g=sgu_ln_g, sgu_ln_b=sgu_ln_b, sgu_w_spatial=sgu_w_spatial, sgu_b_spatial=sgu_b_spatial,
             sgu_w_out=sgu_w_out, final_norm=final_norm)
    mo = dict(mix_norm=m_mix_norm, mem_norm=m_mem_norm, w_mem_kv=m_w_mem_kv, ffn_norm=m_ffn_norm,
              w_gate=m_w_gate, w_up=m_w_up, w_down=m_w_down, attn_w_in=m_attn_w_in, attn_w_out=m_attn_w_out,
              sgu_w_in=m_sgu_w_in, sgu_ln_g=m_sgu_ln_g, sgu_ln_b=m_sgu_ln_b, sgu_w_spatial=m_sgu_w_spatial,
              sgu_b_spatial=m_sgu_b_spatial, sgu_w_out=m_sgu_w_out, final_norm=m_final_norm)
    vo = dict(mix_norm=v_mix_norm, mem_norm=v_mem_norm, w_mem_kv=v_w_mem_kv, ffn_norm=v_ffn_norm,
              w_gate=v_w_gate, w_up=v_w_up, w_down=v_w_down, attn_w_in=v_attn_w_in, attn_w_out=v_attn_w_out,
              sgu_w_in=v_sgu_w_in, sgu_ln_g=v_sgu_ln_g, sgu_ln_b=v_sgu_ln_b, sgu_w_spatial=v_sgu_w_spatial,
              sgu_b_spatial=v_sgu_b_spatial, sgu_w_out=v_sgu_w_out, final_norm=v_final_norm)
    me = 4 * lax.axis_index("x") + 2 * lax.axis_index("y") + lax.axis_index("c")
    d_model = x.shape[-1]

    for n in TRANSPOSED:
        w[n], mo[n], vo[n] = (jnp.swapaxes(t, 1, 2) for t in (w[n], mo[n], vo[n]))

    shards = {
        "w_mem_kv": _cast_bf16(w_mem_kv, "cast_w_mem_kv"),
        "attn_w_out": _cast_bf16(attn_w_out[0], "cast_attn_w_out"),
        "sgu_w_in": _cast_bf16(w["sgu_w_in"][0], "cast_sgu_w_in"),
        "sgu_w_out": _cast_bf16(sgu_w_out[0], "cast_sgu_w_out"),
    }
    for n in ("w_gate", "w_up", "w_down"):
        shards[n] = [_cast_bf16_layer(w[n], layer, "cast_%s_%d" % (n, layer)) for layer in range(w[n].shape[0])]
    ln_pack = jnp.concatenate([sgu_ln_g, sgu_ln_b], axis=0)
    first_blocks = [_cast_bf16(w["attn_w_in"][0], "cast_attn_w_in"), ln_pack]
    small = dict(mix_norm=mix_norm, mem_norm=mem_norm, ffn_norm=ffn_norm, final_norm=final_norm.reshape(1, -1),
                 sgu_w_spatial=sgu_w_spatial[0], sgu_b_spatial=sgu_b_spatial[0])

    loss, grad_x, recvs, sgrads = _local_step(x[0], mem[0], positions[0], loss_target[0], first_blocks, shards,
                                              small)
    loss = lax.psum(loss[0, 0], MESH_AXES)

    small_names = SMALL_REPLICATED + SMALL_SHARDED
    packed, offs, sizes = _pack([sgrads[n] for n in small_names])
    out_g, out_d, out_m, out_v = {}, {}, {}, {}
    for n in ADAM_ORDER:
        shard = w[n]
        w3 = shard.reshape(shard.shape[0], -1, shard.shape[-1])
        rs = [r.reshape(N_DEV, -1, shard.shape[-1]) for r in recvs[n]]
        operands = (rs, w3, mo[n].reshape(w3.shape), vo[n].reshape(w3.shape), "adam_" + n)
        if n == ADAM_ORDER[0]:
            res, (all_packs,) = _reduce_adam(*operands, _GatherSide([packed]))
        else:
            res = _reduce_adam(*operands)
        res = [r.reshape(shard.shape) for r in res]
        if n in TRANSPOSED:
            res = [jnp.swapaxes(r, 1, 2) for r in res]
        out_g[n], out_d[n], out_m[n], out_v[n] = res

    rep_shapes = [w[n].shape for n in SMALL_REPLICATED]
    w_pack, w_offs, w_sizes = _pack([w[n] for n in SMALL_REPLICATED])
    m_pack, _, _ = _pack([mo[n] for n in SMALL_REPLICATED])
    v_pack, _, _ = _pack([vo[n] for n in SMALL_REPLICATED])
    n_rep_rows = w_pack.shape[0]
    res = _reduce_adam([all_packs[:, :n_rep_rows]], w_pack[None], m_pack[None], v_pack[None], "adam_small")
    for dst, r in zip((out_g, out_d, out_m, out_v), res):
        for n, val in zip(SMALL_REPLICATED, _unpack(r[0], w_offs, w_sizes, rep_shapes)):
            dst[n] = val
    ln_rows0 = offs[len(SMALL_REPLICATED)] // PACK_LANES
    ln_rows = 2 * SGU_W // PACK_LANES
    ln_sum = _reduce_adam([all_packs[:, ln_rows0:ln_rows0 + ln_rows]], jnp.zeros((1, ln_rows, PACK_LANES), F32),
                          jnp.zeros((1, ln_rows, PACK_LANES), F32), jnp.zeros((1, ln_rows, PACK_LANES), F32),
                          "sum_ln_grads")[0]
    ln_grads = ln_sum.reshape(2, N_DEV, -1)
    ln_mine = lax.dynamic_index_in_dim(ln_grads, me, axis=1, keepdims=False)
    w_ln = jnp.concatenate([sgu_ln_g, sgu_ln_b], axis=0)[None]
    m_ln = jnp.concatenate([m_sgu_ln_g, m_sgu_ln_b], axis=0)[None]
    v_ln = jnp.concatenate([v_sgu_ln_g, v_sgu_ln_b], axis=0)[None]
    res = _reduce_adam([ln_mine[None]], w_ln, m_ln, v_ln, "adam_ln")
    for dst, r in zip((out_g, out_d, out_m, out_v), res):
        dst["sgu_ln_g"], dst["sgu_ln_b"] = r[0, 0:1], r[0, 1:2]

    return (loss, grad_x[None], *[out_g[n] for n in WEIGHT_ORDER], *[out_d[n] for n in WEIGHT_ORDER],
            *[out_m[n] for n in WEIGHT_ORDER], *[out_v[n] for n in WEIGHT_ORDER])
```

```python
import functools

import jax
import jax.numpy as jnp
from jax import lax
from jax.experimental import pallas as pl
from jax.experimental.pallas import tpu as pltpu

F32 = jnp.float32
BF16 = jnp.bfloat16

N_DEV = 8
HEAD = 128
HPG = 4
GROUP_W = HPG * HEAD
N_GROUPS = 3
DILATIONS = (1, 4, 16)
BLK = 128
SGU_GROUPS = 12
SGU_W = SGU_GROUPS * HEAD
ROT_HALF = 16
ROPE_THETA = 500000.0
NORM_EPS = 1e-6
LN_EPS = 1e-5
NEG_INF = -1e30
SCALE = HEAD ** -0.5

ADAM_LR = 0.001
ADAM_B1 = 0.9
ADAM_B2 = 0.999
ADAM_EPS = 1e-08
ADAM_WD = 0.01
ADAM_STEP = 10

VMEM_LIMIT_V7X = 56 * 1024 * 1024
MM_ROWS = 1024
MM_ROWS_WIDE = 2048
MM_TILE = 512
ROWS_F32 = 256
ROWS_BF16 = 512
MESH_AXES = ("x", "y", "c")

_DN = {
    "nn": (((1,), (0,)), ((), ())),
    "nt": (((1,), (1,)), ((), ())),
    "tn": (((0,), (0,)), ((), ())),
}


def _dot(a, b, kind="nn"):
    return lax.dot_general(a, b, _DN[kind], preferred_element_type=F32)


def _params():
    return pltpu.CompilerParams(vmem_limit_bytes=VMEM_LIMIT_V7X)


def _row_tile(rows, cap):
    if rows <= cap:
        return rows
    t = cap - cap % 16
    while t >= 16:
        if rows % t == 0:
            return t
        t -= 16
    return rows


def _gelu(x):
    c = 0.7978845608028654
    return 0.5 * x * (1.0 + jnp.tanh(c * (x + 0.044715 * x * x * x)))


def _gelu_and_grad(x):
    c = 0.7978845608028654
    x2 = x * x
    t = jnp.tanh(c * x * (1.0 + 0.044715 * x2))
    half = 0.5 * (1.0 + t)
    return x * half, half + 0.5 * x * (1.0 - t * t) * c * (1.0 + 3.0 * 0.044715 * x2)


def _sigmoid(x):
    return 1.0 / (1.0 + jnp.exp(-x))


def _matmul(name, grid, terms, n_acc, acc_shape, extras, outs, epilogue, side=None):
    nk = grid[-1]
    nt, ne, no = len(terms), len(extras), len(outs)
    kinds = [(t[4], t[5]) for t in terms]
    n_scratch_acc = 0 if nk == 1 else n_acc
    ns_in = len(side.ins) if side else 0
    ns_out = len(side.outs) if side else 0
    n_steps = 1
    for g in grid:
        n_steps *= g

    def body(*refs):
        pos = 0
        ab = refs[pos:pos + 2 * nt]
        pos += 2 * nt
        ex = refs[pos:pos + ne]
        pos += ne
        s_in = refs[pos:pos + ns_in]
        pos += ns_in
        out = refs[pos:pos + no]
        pos += no
        s_out = refs[pos:pos + ns_out]
        pos += ns_out
        accs = refs[pos:pos + n_scratch_acc]
        s_sems = refs[pos + n_scratch_acc:]
        if side:
            step = pl.program_id(0)
            for ax in range(1, len(grid)):
                step = step * grid[ax] + pl.program_id(ax)
            start, mid, finish = side.phases(s_in, s_out, *s_sems)
            pl.when(step == 0)(start)
        parts = [None] * n_acc
        for t, (kind, ai) in enumerate(kinds):
            a_ref, b_ref = ab[2 * t], ab[2 * t + 1]
            if len(b_ref.shape) == 2:
                pairs = [(a_ref[...], b_ref[...])]
            elif len(a_ref.shape) == 3:
                pairs = [(a_ref[q], b_ref[q]) for q in range(b_ref.shape[0])]
            else:
                bw = b_ref.shape[2]
                pairs = [(a_ref[:, q * bw:(q + 1) * bw], b_ref[q]) for q in range(b_ref.shape[0])]
            for a, b in pairs:
                p = _dot(a.astype(BF16), b.astype(BF16), kind)
                parts[ai] = p if parts[ai] is None else parts[ai] + p
        if nk == 1:
            epilogue(parts, ex, out)
        else:
            k = pl.program_id(len(grid) - 1)

            @pl.when(k == 0)
            def _():
                for ai in range(n_acc):
                    accs[ai][...] = parts[ai]

            @pl.when(k > 0)
            def _():
                for ai in range(n_acc):
                    accs[ai][...] += parts[ai]

            @pl.when(k == nk - 1)
            def _():
                epilogue([a[...] for a in accs], ex, out)

        if side:
            pl.when(step == (3 * n_steps) // 4)(mid)
            pl.when(step == n_steps - 1)(finish)

    hbm = pl.BlockSpec(memory_space=pltpu.HBM)
    in_specs, args = [], []
    for (a, a_spec, b, b_spec, _, _) in terms:
        in_specs += [a_spec, b_spec]
        args += [a, b]
    for (e, e_spec) in extras:
        in_specs.append(e_spec)
        args.append(e)
    scratch = [pltpu.VMEM(acc_shape, F32) for _ in range(n_scratch_acc)]
    out_specs = [o[1] for o in outs]
    out_shape = [o[0] for o in outs]
    if side:
        in_specs += [hbm] * ns_in
        args += list(side.ins)
        out_specs += [hbm] * ns_out
        out_shape += list(side.outs)
        scratch += side.scratch()
    res = pl.pallas_call(
        body, grid=grid, in_specs=in_specs, out_specs=out_specs, out_shape=out_shape,
        scratch_shapes=scratch, compiler_params=_params(), name=name,
    )(*args)
    return res if side is None else (res[:no], res[no:])


def _carrying(body, n_in, n_out, n_scratch, grid, side):
    if side is None:
        return body
    ns_in, ns_out = len(side.ins), len(side.outs)
    n_steps = 1
    for g in grid:
        n_steps *= g

    def wrapped(*refs):
        pos = n_in + ns_in
        ins, s_in = refs[:n_in], refs[n_in:pos]
        outs, s_out = refs[pos:pos + n_out], refs[pos + n_out:pos + n_out + ns_out]
        pos += n_out + ns_out
        scratch, sems = refs[pos:pos + n_scratch], refs[pos + n_scratch:]
        step = pl.program_id(0)
        for ax in range(1, len(grid)):
            step = step * grid[ax] + pl.program_id(ax)
        start, mid, finish = side.phases(s_in, s_out, *sems)
        pl.when(step == 0)(start)
        body(*ins, *outs, *scratch)
        pl.when(step == (3 * n_steps) // 4)(mid)
        pl.when(step == n_steps - 1)(finish)

    return wrapped


def _side_specs(side):
    if side is None:
        return [], [], [], [], []
    hbm = pl.BlockSpec(memory_space=pltpu.HBM)
    return ([hbm] * len(side.ins), list(side.ins), [hbm] * len(side.outs), list(side.outs), side.scratch())


def _one(res, side):
    return res[0] if side is None else (res[0][0], res[1])


def _store_epilogue(parts, ex, out):
    out[0][...] = parts[0].astype(out[0].dtype)


def _residual_epilogue(parts, ex, out):
    out[0][...] = (parts[0] + ex[0][...]).astype(out[0].dtype)


def _rmsnorm_fwd(x, g, name, side=None):
    rows, d = x.shape
    tm = _row_tile(rows, ROWS_BF16)
    grid = (rows // tm,)

    def body(x_ref, g_ref, o_ref):
        xf = x_ref[...]
        r = lax.rsqrt(jnp.mean(xf * xf, axis=-1, keepdims=True) + NORM_EPS)
        o_ref[...] = (xf * r * g_ref[...]).astype(o_ref.dtype)

    s_in, s_args, s_out, s_shapes, s_scratch = _side_specs(side)
    res = pl.pallas_call(
        _carrying(body, 2, 1, 0, grid, side), grid=grid,
        in_specs=[pl.BlockSpec((tm, d), lambda i: (i, 0)), pl.BlockSpec((1, d), lambda i: (0, 0))] + s_in,
        out_specs=[pl.BlockSpec((tm, d), lambda i: (i, 0))] + s_out,
        out_shape=[jax.ShapeDtypeStruct((rows, d), BF16)] + s_shapes,
        scratch_shapes=s_scratch, compiler_params=_params(), name=name,
    )(x, g, *s_args)
    return res[0] if side is None else (res[0], res[1:])


def _rmsnorm_bwd(x, g, dh, dres, name, out_dtype=BF16):
    rows, d = x.shape
    tm = _row_tile(rows, ROWS_F32)
    has_res = dres is not None

    def body(*refs):
        if has_res:
            x_ref, g_ref, dh_ref, dres_ref, dx_ref, dg_ref = refs
        else:
            x_ref, g_ref, dh_ref, dx_ref, dg_ref = refs
        i = pl.program_id(0)
        xf = x_ref[...]
        r = lax.rsqrt(jnp.mean(xf * xf, axis=-1, keepdims=True) + NORM_EPS)
        xhat = xf * r
        dy = dh_ref[...].astype(F32)
        gdy = dy * g_ref[...]
        c = jnp.mean(gdy * xhat, axis=-1, keepdims=True)
        dx = r * (gdy - xhat * c)
        if has_res:
            dx = dx + dres_ref[...].astype(F32)
        dx_ref[...] = dx.astype(out_dtype)

        @pl.when(i == 0)
        def _():
            dg_ref[...] = jnp.zeros_like(dg_ref)

        dg_ref[...] += jnp.sum(dy * xhat, axis=0, keepdims=True)

    row_spec = pl.BlockSpec((tm, d), lambda i: (i, 0))
    vec_spec = pl.BlockSpec((1, d), lambda i: (0, 0))
    in_specs = [row_spec, vec_spec, row_spec] + ([row_spec] if has_res else [])
    args = [x, g, dh] + ([dres] if has_res else [])
    return pl.pallas_call(
        body, grid=(rows // tm,), in_specs=in_specs,
        out_specs=[row_spec, vec_spec],
        out_shape=[jax.ShapeDtypeStruct((rows, d), out_dtype), jax.ShapeDtypeStruct((1, d), F32)],
        compiler_params=_params(), name=name,
    )(*args)


def _loss_head(x, g, target):
    rows, d = x.shape
    tm = _row_tile(rows, ROWS_F32)

    def body(x_ref, g_ref, t_ref, loss_ref, dxb_ref, dg_ref):
        i = pl.program_id(0)
        xf = x_ref[...]
        gv = g_ref[...]
        r = lax.rsqrt(jnp.mean(xf * xf, axis=-1, keepdims=True) + NORM_EPS)
        xhat = xf * r
        err = xhat * gv - t_ref[...]
        row_loss = jnp.mean(err * err, axis=-1, keepdims=True)
        dy = err * (1.0 / d)
        gdy = dy * gv
        c = jnp.mean(gdy * xhat, axis=-1, keepdims=True)
        dxb_ref[...] = (r * (gdy - xhat * c)).astype(BF16)

        @pl.when(i == 0)
        def _():
            dg_ref[...] = jnp.zeros_like(dg_ref)
            loss_ref[...] = jnp.zeros_like(loss_ref)

        dg_ref[...] += jnp.sum(dy * xhat, axis=0, keepdims=True)
        loss_ref[...] += 0.5 * jnp.sum(row_loss, axis=0, keepdims=True)

    row_spec = pl.BlockSpec((tm, d), lambda i: (i, 0))
    vec_spec = pl.BlockSpec((1, d), lambda i: (0, 0))
    return pl.pallas_call(
        body, grid=(rows // tm,), in_specs=[row_spec, vec_spec, row_spec],
        out_specs=[pl.BlockSpec((1, 1), lambda i: (0, 0)), row_spec, vec_spec],
        out_shape=[jax.ShapeDtypeStruct((1, 1), F32), jax.ShapeDtypeStruct((rows, d), BF16),
                   jax.ShapeDtypeStruct((1, d), F32)],
        compiler_params=_params(), name="loss_head",
    )(x, g, target)


def _rotary_tables(positions):
    inv_freq = ROPE_THETA ** (-jnp.arange(ROT_HALF, dtype=F32) / ROT_HALF)
    ang = positions.astype(F32)[:, None] * inv_freq
    cos, sin = jnp.cos(ang), jnp.sin(ang)
    s = positions.shape[0]
    z = jnp.zeros((s, HEAD - 2 * ROT_HALF), F32)
    z16 = jnp.zeros((s, ROT_HALF), F32)
    c = jnp.concatenate([cos, cos, jnp.ones_like(z)], axis=1)
    s1 = jnp.concatenate([z16, sin, z], axis=1)
    s2 = jnp.concatenate([-sin, z16, z], axis=1)
    return c, s1, s2


ATTN_PROJ_HEADS_PER_STEP = 10


def _attn_in_proj(h, w_t, tabs, side=None):
    s, d = h.shape
    n = w_t.shape[0]
    tn = ATTN_PROJ_HEADS_PER_STEP * HEAD
    n_rot_heads = 2 * N_GROUPS * HPG
    n_rot_steps = -(-n_rot_heads // ATTN_PROJ_HEADS_PER_STEP)
    tm = _row_tile(s, MM_ROWS)

    def epilogue(parts, ex, out):
        j = pl.program_id(0)
        acc = parts[0]

        @pl.when(j < n_rot_steps)
        def _():
            c, s1, s2 = ex[0][...], ex[1][...], ex[2][...]
            for t in range(ATTN_PROJ_HEADS_PER_STEP):
                seg = acc[:, t * HEAD:(t + 1) * HEAD]
                rot = seg * c + pltpu.roll(seg, ROT_HALF, 1) * s1 + pltpu.roll(seg, HEAD - ROT_HALF, 1) * s2
                is_rot = (j * ATTN_PROJ_HEADS_PER_STEP + t) < n_rot_heads
                out[0][:, t * HEAD:(t + 1) * HEAD] = jnp.where(is_rot, rot, seg).astype(BF16)

        @pl.when(j >= n_rot_steps)
        def _():
            out[0][...] = acc.astype(BF16)

    tab_spec = pl.BlockSpec((tm, HEAD), lambda j, m, k: (m, 0))
    return _one(_matmul(
        "attn_in_proj", (n // tn, s // tm, 1),
        [(h, pl.BlockSpec((tm, d), lambda j, m, k: (m, 0)),
          w_t, pl.BlockSpec((tn, d), lambda j, m, k: (j, 0)), "nt", 0)],
        1, None, [(tabs[0], tab_spec), (tabs[1], tab_spec), (tabs[2], tab_spec)],
        [(jax.ShapeDtypeStruct((s, n), BF16), pl.BlockSpec((tm, tn), lambda j, m, k: (m, j)))],
        epilogue, side), side)


ATT_TILE_BLOCKS = 4


def _att_blocks(seq_blocks):
    return min(ATT_TILE_BLOCKS, seq_blocks)


def _band_masks():
    qi = lax.broadcasted_iota(jnp.int32, (BLK, BLK), 0)
    ki = lax.broadcasted_iota(jnp.int32, (BLK, BLK), 1)
    return ki <= qi, ki >= qi


def _attn_fwd(q_arr, k_arr, v_arr, offs, seq_blocks, name, side=None):
    s = q_arr.shape[0]
    qo, ko, vo = offs
    nb = _att_blocks(seq_blocks)

    def body(q_ref, kc_ref, kp_ref, vc_ref, vp_ref, o_ref, lse_ref):
        n = pl.program_id(0)
        tile_starts_seq = (n * nb) % seq_blocks == 0
        mask_c, mask_p = _band_masks()
        pairs = [(b * BLK, h * HEAD) for b in range(nb) for h in range(HPG)]

        def keys_prev(ref_c, ref_p, r0, c0):
            return ref_p[:, c0:c0 + HEAD] if r0 == 0 else ref_c[r0 - BLK:r0, c0:c0 + HEAD]

        s_c, s_p = [], []
        for r0, c0 in pairs:
            q = q_ref[r0:r0 + BLK, c0:c0 + HEAD]
            s_c.append(jnp.where(mask_c, _dot(q, kc_ref[r0:r0 + BLK, c0:c0 + HEAD], "nt") * SCALE, NEG_INF))
            sp = jnp.where(mask_p, _dot(q, keys_prev(kc_ref, kp_ref, r0, c0), "nt") * SCALE, NEG_INF)
            s_p.append(jnp.where(tile_starts_seq, NEG_INF, sp) if r0 == 0 else sp)
        m = [jnp.maximum(jnp.max(a, axis=-1, keepdims=True), jnp.max(b, axis=-1, keepdims=True))
             for a, b in zip(s_c, s_p)]
        p_c = [jnp.exp(a - mm) for a, mm in zip(s_c, m)]
        p_p = [jnp.exp(a - mm) for a, mm in zip(s_p, m)]
        l = [jnp.sum(a, axis=-1, keepdims=True) + jnp.sum(b, axis=-1, keepdims=True) for a, b in zip(p_c, p_p)]
        inv = [1.0 / ll for ll in l]
        for i, (r0, c0) in enumerate(pairs):
            o = (_dot((p_c[i] * inv[i]).astype(BF16), vc_ref[r0:r0 + BLK, c0:c0 + HEAD])
                 + _dot((p_p[i] * inv[i]).astype(BF16), keys_prev(vc_ref, vp_ref, r0, c0)))
            o_ref[r0:r0 + BLK, c0:c0 + HEAD] = o.astype(BF16)
            lse_ref[r0:r0 + BLK, c0:c0 + HEAD] = jnp.broadcast_to(m[i] + jnp.log(l[i]), (BLK, HEAD))

    def cur(off):
        return pl.BlockSpec((nb * BLK, GROUP_W), lambda n: (n, off))

    def prev(off):
        return pl.BlockSpec((BLK, GROUP_W), lambda n: (jnp.maximum(n * nb - 1, 0), off))

    grid = (s // (nb * BLK),)
    s_in, s_args, s_out, s_shapes, s_scratch = _side_specs(side)
    return pl.pallas_call(
        _carrying(body, 5, 2, 0, grid, side), grid=grid,
        in_specs=[cur(qo), cur(ko), prev(ko), cur(vo), prev(vo)] + s_in,
        out_specs=[cur(0), cur(0)] + s_out,
        out_shape=[jax.ShapeDtypeStruct((s, GROUP_W), BF16), jax.ShapeDtypeStruct((s, GROUP_W), F32)] + s_shapes,
        scratch_shapes=s_scratch, compiler_params=_params(), name=name,
    )(q_arr, k_arr, k_arr, v_arr, v_arr, *s_args)


def _attn_bwd(q_arr, k_arr, v_arr, offs, do, lse, delta, seq_blocks, name):
    s = q_arr.shape[0]
    qo, ko, vo = offs
    nb = _att_blocks(seq_blocks)
    n_blocks = s // BLK

    def body(qc_ref, qn_ref, kc_ref, kp_ref, vc_ref, vp_ref, doc_ref, don_ref, lsec_ref, lsen_ref,
             dlc_ref, dln_ref, out_ref):
        n = pl.program_id(0)
        tile_starts_seq = (n * nb) % seq_blocks == 0
        next_in_seq = ((n + 1) * nb) % seq_blocks != 0
        mask_c, mask_p = _band_masks()

        def blk(ref_c, ref_edge, b, c0):
            if b < 0 or b >= nb:
                return ref_edge[:, c0:c0 + HEAD]
            return ref_c[b * BLK:(b + 1) * BLK, c0:c0 + HEAD]

        heads = [h * HEAD for h in range(HPG)]
        own = [(b, c0) for b in range(nb) for c0 in heads]
        cross = [(c, c0) for c in range(nb + 1) for c0 in heads]
        s_o, dp_o, s_x, dp_x = [], [], [], []
        for b, c0 in own:
            s_o.append(jnp.where(mask_c, _dot(blk(qc_ref, None, b, c0), blk(kc_ref, None, b, c0), "nt") * SCALE,
                                 NEG_INF))
            dp_o.append(_dot(blk(doc_ref, None, b, c0), blk(vc_ref, None, b, c0), "nt"))
        for c, c0 in cross:
            sx = jnp.where(mask_p, _dot(blk(qc_ref, qn_ref, c, c0), blk(kc_ref, kp_ref, c - 1, c0), "nt") * SCALE,
                           NEG_INF)
            if c == 0:
                sx = jnp.where(tile_starts_seq, NEG_INF, sx)
            if c == nb:
                sx = jnp.where(next_in_seq, sx, NEG_INF)
            s_x.append(sx)
            dp_x.append(_dot(blk(doc_ref, don_ref, c, c0), blk(vc_ref, vp_ref, c - 1, c0), "nt"))
        p_o, ds_o, p_x, ds_x = [], [], [], []
        for i, (b, c0) in enumerate(own):
            p = jnp.exp(s_o[i] - blk(lsec_ref, None, b, c0))
            ds_o.append((p * (dp_o[i] - blk(dlc_ref, None, b, c0)) * SCALE).astype(BF16))
            p_o.append(p.astype(BF16))
        for i, (c, c0) in enumerate(cross):
            p = jnp.exp(s_x[i] - blk(lsec_ref, lsen_ref, c, c0))
            ds_x.append((p * (dp_x[i] - blk(dlc_ref, dln_ref, c, c0)) * SCALE).astype(BF16))
            p_x.append(p.astype(BF16))
        for i, (b, c0) in enumerate(own):
            xq = b * HPG + i % HPG
            xk = (b + 1) * HPG + i % HPG
            dq = _dot(ds_o[i], blk(kc_ref, None, b, c0)) + _dot(ds_x[xq], blk(kc_ref, kp_ref, b - 1, c0))
            dk = (_dot(ds_o[i], blk(qc_ref, None, b, c0), "tn")
                  + _dot(ds_x[xk], blk(qc_ref, qn_ref, b + 1, c0), "tn"))
            dv = (_dot(p_o[i], blk(doc_ref, None, b, c0), "tn")
                  + _dot(p_x[xk], blk(doc_ref, don_ref, b + 1, c0), "tn"))
            r0 = b * BLK
            out_ref[r0:r0 + BLK, c0:c0 + HEAD] = dq.astype(BF16)
            out_ref[r0:r0 + BLK, GROUP_W + c0:GROUP_W + c0 + HEAD] = dk.astype(BF16)
            out_ref[r0:r0 + BLK, 2 * GROUP_W + c0:2 * GROUP_W + c0 + HEAD] = dv.astype(BF16)

    def cur(off):
        return pl.BlockSpec((nb * BLK, GROUP_W), lambda n: (n, off))

    def prev(off):
        return pl.BlockSpec((BLK, GROUP_W), lambda n: (jnp.maximum(n * nb - 1, 0), off))

    def nxt(off):
        return pl.BlockSpec((BLK, GROUP_W), lambda n: (jnp.minimum((n + 1) * nb, n_blocks - 1), off))

    return pl.pallas_call(
        body, grid=(s // (nb * BLK),),
        in_specs=[cur(qo), nxt(qo), cur(ko), prev(ko), cur(vo), prev(vo), cur(0), nxt(0), cur(0), nxt(0),
                  cur(0), nxt(0)],
        out_specs=pl.BlockSpec((nb * BLK, 3 * GROUP_W), lambda n: (n, 0)),
        out_shape=jax.ShapeDtypeStruct((s, 3 * GROUP_W), BF16),
        compiler_params=_params(), name=name,
    )(q_arr, q_arr, k_arr, k_arr, v_arr, v_arr, do, do, lse, lse, delta, delta)


def _merge_weights(lse_refs, c0):
    ls = [r[:, c0:c0 + HEAD] for r in lse_refs]
    m = jnp.maximum(jnp.maximum(ls[0], ls[1]), ls[2])
    es = [jnp.exp(l - m) for l in ls]
    inv = 1.0 / (es[0] + es[1] + es[2])
    return [e * inv for e in es]


def _merge_fwd(os_, lses):
    s = os_[0].shape[0]
    tm = _row_tile(s, ROWS_BF16)

    def body(o0, o1, o2, l0, l1, l2, out_ref):
        for h in range(HPG):
            c0 = h * HEAD
            w = _merge_weights((l0, l1, l2), c0)
            acc = None
            for wg, o in zip(w, (o0, o1, o2)):
                t = wg * o[:, c0:c0 + HEAD].astype(F32)
                acc = t if acc is None else acc + t
            out_ref[:, c0:c0 + HEAD] = acc.astype(BF16)

    spec = pl.BlockSpec((tm, GROUP_W), lambda i: (i, 0))
    return pl.pallas_call(
        body, grid=(s // tm,), in_specs=[spec] * 6, out_specs=spec,
        out_shape=jax.ShapeDtypeStruct((s, 2 * GROUP_W), BF16),
        compiler_params=_params(), name="merge_fwd",
    )(*os_, *lses)


def _merge_bwd(dcat, os_, lses):
    s = os_[0].shape[0]
    tm = _row_tile(s, ROWS_BF16)

    def body(d_ref, o0, o1, o2, l0, l1, l2, do0, do1, do2, dl0, dl1, dl2):
        for h in range(HPG):
            c0 = h * HEAD
            w = _merge_weights((l0, l1, l2), c0)
            dm = d_ref[:, c0:c0 + HEAD].astype(F32)
            merged = None
            for wg, o in zip(w, (o0, o1, o2)):
                t = wg * o[:, c0:c0 + HEAD].astype(F32)
                merged = t if merged is None else merged + t
            abar = jnp.sum(dm * merged, axis=-1, keepdims=True)
            for wg, do_ref, dl_ref in zip(w, (do0, do1, do2), (dl0, dl1, dl2)):
                do_ref[:, c0:c0 + HEAD] = (wg * dm).astype(BF16)
                dl_ref[:, c0:c0 + HEAD] = wg * abar

    spec = pl.BlockSpec((tm, GROUP_W), lambda i: (i, 0))
    return pl.pallas_call(
        body, grid=(s // tm,), in_specs=[spec] * 7, out_specs=[spec] * 6,
        out_shape=[jax.ShapeDtypeStruct((s, GROUP_W), BF16)] * 3 + [jax.ShapeDtypeStruct((s, GROUP_W), F32)] * 3,
        compiler_params=_params(), name="merge_bwd",
    )(dcat, *os_, *lses)


def _assemble_dproj(dqkv, dqm, tabs):
    s = dqm.shape[0]
    tm = _row_tile(s, ROWS_F32)
    width = 3 * N_GROUPS * GROUP_W + GROUP_W

    def body(d0, d1, d2, dm_ref, c_ref, s1_ref, s2_ref, out_ref):
        c, s1, s2 = c_ref[...], s1_ref[...], s2_ref[...]
        for g, d_ref in enumerate((d0, d1, d2)):
            for part in range(3):
                for h in range(HPG):
                    src = part * GROUP_W + h * HEAD
                    dst = part * N_GROUPS * GROUP_W + g * GROUP_W + h * HEAD
                    seg = d_ref[:, src:src + HEAD]
                    if part < 2:
                        t = seg.astype(F32)
                        t = t * c - pltpu.roll(t, HEAD - ROT_HALF, 1) * s2 - pltpu.roll(t, ROT_HALF, 1) * s1
                        seg = t.astype(BF16)
                    out_ref[:, dst:dst + HEAD] = seg
        out_ref[:, 3 * N_GROUPS * GROUP_W:] = dm_ref[...]

    g_spec = pl.BlockSpec((tm, 3 * GROUP_W), lambda i: (i, 0))
    m_spec = pl.BlockSpec((tm, GROUP_W), lambda i: (i, 0))
    t_spec = pl.BlockSpec((tm, HEAD), lambda i: (i, 0))
    return pl.pallas_call(
        body, grid=(s // tm,), in_specs=[g_spec] * 3 + [m_spec] + [t_spec] * 3,
        out_specs=pl.BlockSpec((tm, width), lambda i: (i, 0)),
        out_shape=jax.ShapeDtypeStruct((s, width), BF16),
        compiler_params=_params(), name="assemble_dproj",
    )(*dqkv, dqm, *tabs)


def _mem_softmax(q, k):
    s = _dot(q, k, "nt") * SCALE
    m = jnp.max(s, axis=-1, keepdims=True)
    p = jnp.exp(s - m)
    return p * (1.0 / jnp.sum(p, axis=-1, keepdims=True))


def _memattn_fwd(q_arr, q_off, kv, into, into_off, name):
    s = q_arr.shape[0]
    mlen = kv.shape[0]
    tq = _row_tile(s, ROWS_BF16)

    def body(q_ref, kv_ref, into_ref, o_ref):
        for h in range(HPG):
            c0 = h * HEAD
            p = _mem_softmax(q_ref[:, c0:c0 + HEAD], kv_ref[:, c0:c0 + HEAD])
            o = _dot(p.astype(BF16), kv_ref[:, GROUP_W + c0:GROUP_W + c0 + HEAD])
            o_ref[:, c0:c0 + HEAD] = o.astype(BF16)

    return pl.pallas_call(
        body, grid=(s // tq,),
        in_specs=[pl.BlockSpec((tq, GROUP_W), lambda i: (i, q_off)),
                  pl.BlockSpec((mlen, 2 * GROUP_W), lambda i: (0, 0)),
                  pl.BlockSpec(memory_space=pl.ANY)],
        out_specs=pl.BlockSpec((tq, GROUP_W), lambda i: (i, into_off)),
        out_shape=jax.ShapeDtypeStruct(into.shape, into.dtype),
        input_output_aliases={2: 0},
        compiler_params=_params(), name=name,
    )(q_arr, kv, into)


def _memattn_bwd(q_arr, q_off, kv, dcat, d_off, name, into=None, into_off=0):
    s = q_arr.shape[0]
    mlen = kv.shape[0]
    tq = _row_tile(s, ROWS_BF16)

    def body(*refs):
        q_ref, kv_ref, d_ref = refs[:3]
        dq_ref, dkv_ref = refs[-2:]
        i = pl.program_id(0)

        @pl.when(i == 0)
        def _():
            dkv_ref[...] = jnp.zeros_like(dkv_ref)

        heads = [h * HEAD for h in range(HPG)]
        p = [_mem_softmax(q_ref[:, c0:c0 + HEAD], kv_ref[:, c0:c0 + HEAD]) for c0 in heads]
        dp = [_dot(d_ref[:, c0:c0 + HEAD], kv_ref[:, GROUP_W + c0:GROUP_W + c0 + HEAD], "nt") for c0 in heads]
        ds = [(pp * (dd - jnp.sum(pp * dd, axis=-1, keepdims=True)) * SCALE).astype(BF16) for pp, dd in zip(p, dp)]
        for i, c0 in enumerate(heads):
            dq_ref[:, c0:c0 + HEAD] = _dot(ds[i], kv_ref[:, c0:c0 + HEAD]).astype(BF16)
            dkv_ref[:, c0:c0 + HEAD] += _dot(ds[i], q_ref[:, c0:c0 + HEAD], "tn")
            dkv_ref[:, GROUP_W + c0:GROUP_W + c0 + HEAD] += _dot(p[i].astype(BF16), d_ref[:, c0:c0 + HEAD], "tn")

    in_specs = [pl.BlockSpec((tq, GROUP_W), lambda i: (i, q_off)),
                pl.BlockSpec((mlen, 2 * GROUP_W), lambda i: (0, 0)),
                pl.BlockSpec((tq, GROUP_W), lambda i: (i, d_off))]
    args = [q_arr, kv, dcat]
    dq_shape = jax.ShapeDtypeStruct((s, GROUP_W), BF16)
    aliases = {}
    if into is not None:
        in_specs.append(pl.BlockSpec(memory_space=pl.ANY))
        args.append(into)
        dq_shape = jax.ShapeDtypeStruct(into.shape, into.dtype)
        aliases = {3: 0}
    return pl.pallas_call(
        body, grid=(s // tq,), in_specs=in_specs,
        out_specs=[pl.BlockSpec((tq, GROUP_W), lambda i: (i, into_off)),
                   pl.BlockSpec((mlen, 2 * GROUP_W), lambda i: (0, 0))],
        out_shape=[dq_shape, jax.ShapeDtypeStruct((mlen, 2 * GROUP_W), F32)],
        input_output_aliases=aliases, compiler_params=_params(), name=name,
    )(*args)


SGU_TILE = 256


def _sgu_norm(vg):
    mu = jnp.mean(vg, axis=-1, keepdims=True)
    xc = vg - mu
    var = jnp.mean(xc * xc, axis=-1, keepdims=True)
    rstd = lax.rsqrt(var + LN_EPS)
    return xc * rstd, rstd


def _tril_mask():
    r = lax.broadcasted_iota(jnp.int32, (BLK, BLK), 0)
    c = lax.broadcasted_iota(jnp.int32, (BLK, BLK), 1)
    return r >= c


def _sgu_fwd(proj, ln_g, ln_b, w_s, b_st):
    s = proj.shape[0]
    ts = _row_tile(s, SGU_TILE)

    def body(u_ref, v_ref, g_ref, b_ref, ws_ref, bst_ref, o_ref):
        ug = _gelu(u_ref[...].astype(F32))
        xhat, _ = _sgu_norm(_gelu(v_ref[...].astype(F32)))
        vn = (xhat * g_ref[...] + b_ref[...]).astype(BF16)
        tri = _tril_mask()
        for g in range(SGU_GROUPS):
            c0 = g * HEAD
            w = jnp.where(tri, ws_ref[g], 0.0).astype(BF16)
            bias = bst_ref[:, g:g + 1]
            for ch in range(ts // BLK):
                r0 = ch * BLK
                mixed = _dot(w, vn[r0:r0 + BLK, c0:c0 + HEAD]) + bias
                o_ref[r0:r0 + BLK, c0:c0 + HEAD] = (ug[r0:r0 + BLK, c0:c0 + HEAD] * mixed).astype(BF16)

    vec = pl.BlockSpec((1, SGU_W), lambda i: (0, 0))
    return pl.pallas_call(
        body, grid=(s // ts,),
        in_specs=[pl.BlockSpec((ts, SGU_W), lambda i: (i, 0)), pl.BlockSpec((ts, SGU_W), lambda i: (i, 1)),
                  vec, vec, pl.BlockSpec((SGU_GROUPS, BLK, BLK), lambda i: (0, 0, 0)),
                  pl.BlockSpec((BLK, SGU_GROUPS), lambda i: (0, 0))],
        out_specs=pl.BlockSpec((ts, SGU_W), lambda i: (i, 0)),
        out_shape=jax.ShapeDtypeStruct((s, SGU_W + GROUP_W), BF16),
        compiler_params=_params(), name="sgu_fwd",
    )(proj, proj, ln_g, ln_b, w_s, b_st)


def _sgu_bwd(proj, dcat, ln_g, ln_b, w_s, b_st):
    s = proj.shape[0]
    ts = _row_tile(s, SGU_TILE)

    def body(u_ref, v_ref, d_ref, g_ref, b_ref, ws_ref, bst_ref,
             duv_ref, dws_ref, dbs_ref, dg_ref, db_ref, dvn_ref):
        i = pl.program_id(0)

        @pl.when(i == 0)
        def _():
            dws_ref[...] = jnp.zeros_like(dws_ref)
            dbs_ref[...] = jnp.zeros_like(dbs_ref)
            dg_ref[...] = jnp.zeros_like(dg_ref)
            db_ref[...] = jnp.zeros_like(db_ref)

        u = u_ref[...].astype(F32)
        v = v_ref[...].astype(F32)
        ug, dug = _gelu_and_grad(u)
        vg, dvg_dv = _gelu_and_grad(v)
        xhat, rstd = _sgu_norm(vg)
        lng = g_ref[...]
        vn = (xhat * lng + b_ref[...]).astype(BF16)
        dout = d_ref[...].astype(F32)
        tri = _tril_mask()
        lane = lax.broadcasted_iota(jnp.int32, (BLK, BLK), 1)
        dbs = jnp.zeros((BLK, BLK), F32)
        for g in range(SGU_GROUPS):
            c0 = g * HEAD
            w = jnp.where(tri, ws_ref[g], 0.0).astype(BF16)
            bias = bst_ref[:, g:g + 1]
            dws = jnp.zeros((BLK, BLK), F32)
            for ch in range(ts // BLK):
                r0 = ch * BLK
                vn_gc = vn[r0:r0 + BLK, c0:c0 + HEAD]
                mixed = _dot(w, vn_gc) + bias
                do_gc = dout[r0:r0 + BLK, c0:c0 + HEAD]
                dmixed = do_gc * ug[r0:r0 + BLK, c0:c0 + HEAD]
                du = do_gc * mixed * dug[r0:r0 + BLK, c0:c0 + HEAD]
                duv_ref[r0:r0 + BLK, c0:c0 + HEAD] = du.astype(BF16)
                dmb = dmixed.astype(BF16)
                dws = dws + _dot(dmb, vn_gc, "nt")
                dbs = dbs + jnp.where(lane == g, jnp.sum(dmixed, axis=-1, keepdims=True), 0.0)
                dvn_ref[r0:r0 + BLK, c0:c0 + HEAD] = _dot(w, dmb, "tn")
            dws_ref[g] += jnp.where(tri, dws, 0.0)
        dbs_ref[...] += dbs
        dvn = dvn_ref[...]
        gd = dvn * lng
        c1 = jnp.mean(gd, axis=-1, keepdims=True)
        c2 = jnp.mean(gd * xhat, axis=-1, keepdims=True)
        dvg = rstd * (gd - c1 - xhat * c2)
        duv_ref[:, SGU_W:] = (dvg * dvg_dv).astype(BF16)
        dg_ref[...] += jnp.sum(dvn * xhat, axis=0, keepdims=True)
        db_ref[...] += jnp.sum(dvn, axis=0, keepdims=True)

    vec = pl.BlockSpec((1, SGU_W), lambda i: (0, 0))
    ws_spec = pl.BlockSpec((SGU_GROUPS, BLK, BLK), lambda i: (0, 0, 0))
    return pl.pallas_call(
        body, grid=(s // ts,),
        in_specs=[pl.BlockSpec((ts, SGU_W), lambda i: (i, 0)), pl.BlockSpec((ts, SGU_W), lambda i: (i, 1)),
                  pl.BlockSpec((ts, SGU_W), lambda i: (i, 0)),
                  vec, vec, ws_spec, pl.BlockSpec((BLK, SGU_GROUPS), lambda i: (0, 0))],
        out_specs=[pl.BlockSpec((ts, 2 * SGU_W), lambda i: (i, 0)), ws_spec,
                   pl.BlockSpec((BLK, BLK), lambda i: (0, 0)), vec, vec],
        out_shape=[jax.ShapeDtypeStruct((s, 2 * SGU_W + GROUP_W), BF16),
                   jax.ShapeDtypeStruct((SGU_GROUPS, BLK, BLK), F32),
                   jax.ShapeDtypeStruct((BLK, BLK), F32),
                   jax.ShapeDtypeStruct((1, SGU_W), F32), jax.ShapeDtypeStruct((1, SGU_W), F32)],
        scratch_shapes=[pltpu.VMEM((ts, SGU_W), F32)],
        compiler_params=_params(), name="sgu_bwd",
    )(proj, proj, dcat, ln_g, ln_b, w_s, b_st)


def _swiglu_fwd(h, wg_t, wu_t, name, side=None):
    s, d = h.shape
    f = wg_t.shape[0]
    tm, tn = _row_tile(s, MM_ROWS), _row_tile(f, MM_TILE)

    def epilogue(parts, ex, out):
        g, u = parts
        sg = _sigmoid(g)
        silu = g * sg
        out[0][...] = silu.astype(BF16)
        out[1][...] = (u * (sg + silu * (1.0 - sg))).astype(BF16)
        out[2][...] = (silu * u).astype(BF16)

    a_spec = pl.BlockSpec((tm, d), lambda n, m, k: (m, 0))
    w_spec = pl.BlockSpec((tn, d), lambda n, m, k: (n, 0))
    o_spec = pl.BlockSpec((tm, tn), lambda n, m, k: (m, n))
    o_shape = jax.ShapeDtypeStruct((s, f), BF16)
    return _matmul(name, (f // tn, s // tm, 1),
                   [(h, a_spec, wg_t, w_spec, "nt", 0), (h, a_spec, wu_t, w_spec, "nt", 1)],
                   2, None, [], [(o_shape, o_spec)] * 3, epilogue, side)


def _swiglu_down(hid, wd, res, name, side=None):
    s, f = hid.shape
    d = wd.shape[1]
    tm, tn = _row_tile(s, MM_ROWS), _row_tile(d, MM_TILE)
    o_spec = pl.BlockSpec((tm, tn), lambda n, m, k: (m, n))
    return _one(_matmul(
        name, (d // tn, s // tm, 1),
        [(hid, pl.BlockSpec((tm, f), lambda n, m, k: (m, 0)),
          wd, pl.BlockSpec((f, tn), lambda n, m, k: (0, n)), "nn", 0)],
        1, None, [(res, o_spec)], [(jax.ShapeDtypeStruct((s, d), F32), o_spec)],
        _residual_epilogue, side), side)


def _swiglu_bwd_hidden(dxb, wd, silu, up_dsilu, name):
    s, f = silu.shape
    d = dxb.shape[1]
    tm, tn = _row_tile(s, MM_ROWS_WIDE), _row_tile(f, MM_TILE)

    def epilogue(parts, ex, out):
        dh = parts[0]
        out[0][...] = (dh * ex[1][...].astype(F32)).astype(BF16)
        out[1][...] = (dh * ex[0][...].astype(F32)).astype(BF16)

    blk = pl.BlockSpec((tm, tn), lambda m, n, k: (m, n))
    o_shape = jax.ShapeDtypeStruct((s, f), BF16)
    return _matmul(
        name, (s // tm, f // tn, 1),
        [(dxb, pl.BlockSpec((tm, d), lambda m, n, k: (m, 0)),
          wd, pl.BlockSpec((tn, d), lambda m, n, k: (n, 0)), "nt", 0)],
        1, None, [(silu, blk), (up_dsilu, blk)], [(o_shape, blk)] * 2, epilogue)


def _swiglu_bwd_input(dgate, dup, wg_t, wu_t, name, side=None):
    s, f = dgate.shape
    d = wg_t.shape[1]
    tm, tn, tk = _row_tile(s, MM_ROWS), _row_tile(d, MM_TILE), f // 2
    a_spec = pl.BlockSpec((tm, tk), lambda n, m, k: (m, k))
    w_spec = pl.BlockSpec((tk, tn), lambda n, m, k: (k, n))
    return _one(_matmul(
        name, (d // tn, s // tm, f // tk),
        [(dgate, a_spec, wg_t, w_spec, "nn", 0), (dup, a_spec, wu_t, w_spec, "nn", 0)],
        1, (tm, tn), [],
        [(jax.ShapeDtypeStruct((s, d), BF16), pl.BlockSpec((tm, tn), lambda n, m, k: (m, n)))],
        _store_epilogue, side), side)


def _mm_tn_full(a, b, name, side=None):
    s, m = a.shape
    n = b.shape[1]
    tm, tn = _row_tile(m, MM_TILE), _row_tile(n, MM_TILE)
    return _one(_matmul(
        name, (m // tm, n // tn, 1),
        [(a, pl.BlockSpec((s, tm), lambda i, j, k: (0, i)),
          b, pl.BlockSpec((s, tn), lambda i, j, k: (0, j)), "tn", 0)],
        1, None, [],
        [(jax.ShapeDtypeStruct((m, n), BF16), pl.BlockSpec((tm, tn), lambda i, j, k: (i, j)))],
        _store_epilogue, side), side)


def _mm_nn(a, b, name, tn, out_dtype=BF16, res=None, side=None):
    m, k = a.shape
    n = b.shape[1]
    tm = _row_tile(m, MM_ROWS)
    extras = [] if res is None else [(res, pl.BlockSpec((tm, tn), lambda j, i, kk: (i, j)))]
    return _one(_matmul(
        name, (n // tn, m // tm, 1),
        [(a, pl.BlockSpec((tm, k), lambda j, i, kk: (i, 0)),
          b, pl.BlockSpec((k, tn), lambda j, i, kk: (0, j)), "nn", 0)],
        1, None, extras,
        [(jax.ShapeDtypeStruct((m, n), out_dtype), pl.BlockSpec((tm, tn), lambda j, i, kk: (i, j)))],
        _store_epilogue if res is None else _residual_epilogue, side), side)


def _mm_nn_colblocked(a, b_blk, name, res, side=None):
    m, k = a.shape
    nb, _, bw = b_blk.shape
    tm = _row_tile(m, MM_ROWS_WIDE)
    o_spec = pl.BlockSpec((tm, bw), lambda j, i, kk: (i, j))
    return _one(_matmul(
        name, (nb, m // tm, 1),
        [(a, pl.BlockSpec((tm, k), lambda j, i, kk: (i, 0)),
          b_blk, pl.BlockSpec((None, k, bw), lambda j, i, kk: (j, 0, 0)), "nn", 0)],
        1, None, [(res, o_spec)],
        [(jax.ShapeDtypeStruct((m, nb * bw), F32), o_spec)], _residual_epilogue, side), side)


def _mm_nt_colblocked(a, b_blk, name, out_dtype, jb, side=None):
    m = a.shape[0]
    nb, n, bw = b_blk.shape
    tm = _row_tile(m, MM_TILE)
    return _one(_matmul(
        name, (m // tm, nb // jb),
        [(a, pl.BlockSpec((tm, jb * bw), lambda i, j: (i, j)),
          b_blk, pl.BlockSpec((jb, n, bw), lambda i, j: (j, 0, 0)), "nt", 0)],
        1, (tm, n), [],
        [(jax.ShapeDtypeStruct((m, n), out_dtype), pl.BlockSpec((tm, n), lambda i, j: (i, 0)))],
        _store_epilogue, side), side)


def _mm_nt_rowblocked(a, b, name, tn, out_dtype, side=None):
    m, k = a.shape
    n = b.shape[0]
    tm = _row_tile(m, MM_ROWS)
    return _one(_matmul(
        name, (n // tn, m // tm, 1),
        [(a, pl.BlockSpec((tm, k), lambda j, i, kk: (i, 0)),
          b, pl.BlockSpec((tn, k), lambda j, i, kk: (j, 0)), "nt", 0)],
        1, None, [],
        [(jax.ShapeDtypeStruct((m, n), out_dtype), pl.BlockSpec((tm, tn), lambda j, i, kk: (i, j)))],
        _store_epilogue, side), side)


def _mm_tn_colblocked(a, b, name, bw, side=None):
    s, m = a.shape
    nb = b.shape[1] // bw
    tm = _row_tile(m, MM_TILE)
    return _one(_matmul(
        name, (nb, m // tm, 1),
        [(a, pl.BlockSpec((s, tm), lambda j, i, k: (0, i)),
          b, pl.BlockSpec((s, bw), lambda j, i, k: (0, j)), "tn", 0)],
        1, None, [],
        [(jax.ShapeDtypeStruct((nb, m, bw), BF16), pl.BlockSpec((None, tm, bw), lambda j, i, k: (j, i, 0)))],
        _store_epilogue, side), side)


def _mm_tn_rowblocked(a, b, name, bh):
    s, n = b.shape
    nb = a.shape[1] // bh
    tn = _row_tile(n, MM_TILE)
    return _matmul(
        name, (nb, n // tn, 1),
        [(a, pl.BlockSpec((s, bh), lambda j, i, k: (0, j)),
          b, pl.BlockSpec((s, tn), lambda j, i, k: (0, i)), "tn", 0)],
        1, None, [],
        [(jax.ShapeDtypeStruct((nb, bh, n), BF16), pl.BlockSpec((None, bh, tn), lambda j, i, k: (j, 0, i)))],
        _store_epilogue)[0]


def _as2d(a):
    return a.reshape(-1, a.shape[-1])


def _cast_bf16(w, name):
    w2 = _as2d(w)
    rows, cols = w2.shape
    tr = _row_tile(rows, ROWS_F32)

    def body(w_ref, o_ref):
        o_ref[...] = w_ref[...].astype(BF16)

    spec = pl.BlockSpec((tr, cols), lambda i: (i, 0))
    out = pl.pallas_call(
        body, grid=(rows // tr,), in_specs=[spec], out_specs=spec,
        out_shape=jax.ShapeDtypeStruct((rows, cols), BF16),
        compiler_params=_params(), name=name,
    )(w2)
    return out.reshape(w.shape)


def _cast_bf16_layer(w, layer, name):
    _, rows, cols = w.shape
    tr = _row_tile(rows, ROWS_F32)

    def body(w_ref, o_ref):
        o_ref[...] = w_ref[...].astype(BF16)

    return pl.pallas_call(
        body, grid=(rows // tr,),
        in_specs=[pl.BlockSpec((None, tr, cols), lambda i: (layer, i, 0))],
        out_specs=pl.BlockSpec((tr, cols), lambda i: (i, 0)),
        out_shape=jax.ShapeDtypeStruct((rows, cols), BF16),
        compiler_params=_params(), name=name,
    )(w)


ADAM_TILE_ELEMS = 384 * 1024


def _reduce_adam(recvs, w, m, v, name, side=None):
    n_layers, rows, cols = w.shape
    n_slots = recvs[0].shape[0]
    tr = _row_tile(rows, max(16, (ADAM_TILE_ELEMS // cols) // 16 * 16))
    nt = rows // tr
    c1 = 1.0 - ADAM_B1 ** ADAM_STEP
    c2 = 1.0 - ADAM_B2 ** ADAM_STEP

    def body(*refs):
        r_refs = refs[:n_layers]
        w_ref, m_ref, v_ref, g_out, d_out, m_out, v_out = refs[n_layers:]
        layer = pl.program_id(0)

        def update(r_ref):
            g = r_ref[0].astype(F32)
            for k in range(1, n_slots):
                g = g + r_ref[k].astype(F32)
            mm = ADAM_B1 * m_ref[...] + (1.0 - ADAM_B1) * g
            vv = ADAM_B2 * v_ref[...] + (1.0 - ADAM_B2) * (g * g)
            m_hat = mm / c1
            v_hat = vv / c2
            g_out[...] = g
            d_out[...] = -ADAM_LR * (m_hat / (jnp.sqrt(v_hat) + ADAM_EPS) + ADAM_WD * w_ref[...])
            m_out[...] = mm
            v_out[...] = vv

        for li in range(n_layers):
            if n_layers == 1:
                update(r_refs[li])
            else:
                pl.when(layer == li)(functools.partial(update, r_refs[li]))

    def recv_spec(li):
        def imap(layer, i):
            return (0, jnp.where(layer == li, i, jnp.where(layer < li, 0, nt - 1)), 0)
        return pl.BlockSpec((n_slots, tr, cols), imap)

    spec = pl.BlockSpec((None, tr, cols), lambda layer, i: (layer, i, 0))
    o_shape = jax.ShapeDtypeStruct(w.shape, F32)
    grid = (n_layers, nt)
    s_in, s_args, s_out, s_shapes, s_scratch = _side_specs(side)
    res = pl.pallas_call(
        _carrying(body, n_layers + 3, 4, 0, grid, side), grid=grid,
        in_specs=[recv_spec(li) for li in range(n_layers)] + [spec] * 3 + s_in,
        out_specs=[spec] * 4 + s_out, out_shape=[o_shape] * 4 + s_shapes,
        scratch_shapes=s_scratch, compiler_params=_params(), name=name,
    )(*recvs, w, m, v, *s_args)
    return res if side is None else (res[:4], res[4:])


def _my_place():
    return lax.axis_index("x"), lax.axis_index("y"), lax.axis_index("c")


class _GatherSide:
    def __init__(self, blocks):
        self.ins = list(blocks)
        self.outs = [jax.ShapeDtypeStruct((N_DEV,) + b.shape, b.dtype) for b in blocks]

    def scratch(self):
        n = len(self.ins)
        return [pltpu.SemaphoreType.DMA((7 * n,)), pltpu.SemaphoreType.DMA((7 * n,)),
                pltpu.SemaphoreType.DMA((n,))]

    def phases(self, x_refs, out_refs, send_sems, recv_sems, local_sems):
        n = len(self.ins)
        x, y, c = _my_place()
        me, sibling = (x, y, c), (x, y, 1 - c)
        chips = [(1 - x, y), (x, 1 - y), (1 - x, 1 - y)]

        def slot(t, px, py, pc):
            return out_refs[t].at[4 * px + 2 * py + pc]

        def copy(t, k, blk, to, src=None):
            return pltpu.make_async_remote_copy(
                src_ref=slot(t, *blk) if src is None else src, dst_ref=slot(t, *blk),
                send_sem=send_sems.at[7 * t + k], recv_sem=recv_sems.at[7 * t + k],
                device_id=to, device_id_type=pl.DeviceIdType.MESH)

        def own(t):
            return pltpu.make_async_copy(x_refs[t], slot(t, *me), local_sems.at[t])

        def first(t):
            return [copy(t, 0, me, sibling, src=x_refs[t])] + [
                copy(t, 1 + j, me, (*chip, c), src=x_refs[t]) for j, chip in enumerate(chips)]

        def passed(t):
            return [copy(t, 4 + j, (*chip, c), sibling) for j, chip in enumerate(chips)]

        def start():
            for t in range(n):
                own(t).start()
                for cp in first(t):
                    cp.start()

        def mid():
            for t in range(n):
                fwd = passed(t)
                for j, chip in enumerate(chips):
                    copy(t, 1 + j, (*chip, c), me).wait_recv()
                    fwd[j].start()

        def finish():
            for t in range(n):
                copy(t, 0, sibling, me).wait_recv()
                for j, chip in enumerate(chips):
                    copy(t, 4 + j, (*chip, 1 - c), me).wait_recv()
                for cp in first(t) + passed(t):
                    cp.wait_send()
                own(t).wait()

        return start, mid, finish


class _ExchangeSide:
    def __init__(self, blocked):
        self.ins = list(blocked)
        self.outs = [jax.ShapeDtypeStruct(b.shape, b.dtype) for b in blocked]

    def scratch(self):
        n = len(self.ins)
        return [pltpu.SemaphoreType.DMA((7 * n,)), pltpu.SemaphoreType.DMA((7 * n,)),
                pltpu.SemaphoreType.DMA((n,))]

    def phases(self, srcs, dsts, send_sems, recv_sems, local_sems):
        n = len(self.ins)
        x, y, c = _my_place()
        me = 4 * x + 2 * y + c

        def own(t):
            return pltpu.make_async_copy(srcs[t].at[me], dsts[t].at[me], local_sems.at[t])

        def copies(t, arriving):
            res = []
            for k in range(1, N_DEV):
                px, py, pc = x ^ (k >> 2), y ^ ((k >> 1) & 1), c ^ (k & 1)
                peer = 4 * px + 2 * py + pc
                sem = 7 * t + k - 1
                res.append(pltpu.make_async_remote_copy(
                    src_ref=srcs[t].at[peer], dst_ref=dsts[t].at[peer if arriving else me],
                    send_sem=send_sems.at[sem], recv_sem=recv_sems.at[sem],
                    device_id=(px, py, pc), device_id_type=pl.DeviceIdType.MESH))
            return res

        def start():
            for t in range(n):
                own(t).start()
                for send in copies(t, False):
                    send.start()

        def mid():
            pass

        def finish():
            for t in range(n):
                for arrival in copies(t, True):
                    arrival.wait_recv()
                for send in copies(t, False):
                    send.wait_send()
                own(t).wait()

        return start, mid, finish


def _to_residue_major(a, dilation):
    s, w = a.shape
    return a.reshape(s // dilation, dilation, w).transpose(1, 0, 2).reshape(s, w)


def _from_residue_major(a, dilation):
    s, w = a.shape
    return a.reshape(dilation, s // dilation, w).transpose(1, 0, 2).reshape(s, w)


def _mem_kv(mem, gain, wkv, layer, tag):
    mem_n = _rmsnorm_fwd(mem, gain, "mem_norm_" + tag)
    mlen, d = mem.shape
    nb, _, bh, n = wkv.shape
    kv = _matmul(
        "mem_kv_" + tag, (1, nb),
        [(mem_n, pl.BlockSpec((mlen, bh), lambda i, j: (0, j)),
          wkv, pl.BlockSpec((None, None, bh, n), lambda i, j: (j, layer, 0, 0)), "nn", 0)],
        1, (mlen, n), [],
        [(jax.ShapeDtypeStruct((mlen, n), BF16), pl.BlockSpec((mlen, n), lambda i, j: (0, 0)))],
        _store_epilogue)[0]
    return mem_n, kv


def _mem_kv_bwd(mem, gain, mem_n, wkv, layer, dkv, tag):
    mlen, d = mem.shape
    nb, _, bh, n = wkv.shape
    dkvb = dkv.astype(BF16)
    dw = _mm_tn_rowblocked(mem_n, dkvb, "mem_kv_dw_" + tag, bh)
    dmem_n = _matmul(
        "mem_kv_dx_" + tag, (nb, 1),
        [(dkvb, pl.BlockSpec((mlen, n), lambda j, k: (0, 0)),
          wkv, pl.BlockSpec((None, None, bh, n), lambda j, k: (j, layer, 0, 0)), "nt", 0)],
        1, None, [],
        [(jax.ShapeDtypeStruct((mlen, d), F32), pl.BlockSpec((mlen, bh), lambda j, k: (0, j)))],
        _store_epilogue)[0]
    _, dgain = _rmsnorm_bwd(mem, gain, dmem_n, None, "mem_norm_bwd_" + tag)
    return dw, dgain


def _row_blocks(a):
    return a.reshape(N_DEV, -1, a.shape[-1])


def _rows(a):
    return a.reshape(-1, a.shape[-1])


def _ffn_bwd(x, gain, w_gate, w_up, w_down, saved, dxb, tag):
    hf, silu, up_dsilu, hid = saved
    dgate, dup = _swiglu_bwd_hidden(dxb, w_down, silu, up_dsilu, "swiglu_bwd_hidden_" + tag)
    dwd = _mm_tn_full(hid, dxb, "swiglu_bwd_wdown_" + tag)
    dwg, (r_wd,) = _mm_tn_full(dgate, hf, "swiglu_bwd_wgate_" + tag, _ExchangeSide([_row_blocks(dwd)]))
    dwu, (r_wg,) = _mm_tn_full(dup, hf, "swiglu_bwd_wup_" + tag, _ExchangeSide([_row_blocks(dwg)]))
    dhf, (r_wu,) = _swiglu_bwd_input(dgate, dup, w_gate, w_up, "swiglu_bwd_input_" + tag,
                                     _ExchangeSide([_row_blocks(dwu)]))
    dxb_new, dgain = _rmsnorm_bwd(x, gain, dhf, dxb, "ffn_norm_bwd_" + tag)
    return dxb_new, dgain, r_wg, r_wu, r_wd


def _local_step(x, mem, positions, target, first_blocks, shards, small):
    s, d = x.shape
    tabs = _rotary_tables(positions)
    mix_norm, mem_norm, ffn_norm = small["mix_norm"], small["mem_norm"], small["ffn_norm"]

    h0, (w_attn_in, ln_all) = _rmsnorm_fwd(x, mix_norm[0:1], "mix_norm_0", _GatherSide(first_blocks))
    w_attn_in = _rows(w_attn_in)
    ln_full = ln_all.transpose(1, 0, 2).reshape(2, 1, -1)
    ln_g, ln_b = ln_full[0], ln_full[1]
    proj0, (w_mem_kv, w_attn_out, w_gate0) = _attn_in_proj(
        h0, w_attn_in, tabs, _GatherSide([shards["w_mem_kv"], shards["attn_w_out"], shards["w_gate"][0]]))
    up_shard = shards["w_up"][0]
    cut = -(-up_shard.shape[0] // N_GROUPS // 16) * 16
    up_pieces = [up_shard[g * cut:min((g + 1) * cut, up_shard.shape[0])] for g in range(N_GROUPS)]
    qkv, offs, outs, lses, up_gathered = [], [], [], [], []
    for g, dil in enumerate(DILATIONS):
        if dil == 1:
            arr, off = proj0, (g, N_GROUPS + g, 2 * N_GROUPS + g)
        else:
            cols = [proj0[:, (p * N_GROUPS + g) * GROUP_W:(p * N_GROUPS + g + 1) * GROUP_W] for p in range(3)]
            arr, off = _to_residue_major(jnp.concatenate(cols, axis=1), dil), (0, 1, 2)
        o, lse, piece = _attn_fwd(arr, arr, arr, off, s // dil // BLK, "attn_fwd_%d" % g,
                                  _GatherSide([up_pieces[g]]))
        up_gathered.append(piece)
        qkv.append(arr)
        offs.append(off)
        if dil > 1:
            o, lse = _from_residue_major(o, dil), _from_residue_major(lse, dil)
        outs.append(o)
        lses.append(lse)
    mix0 = _merge_fwd(outs, lses)
    qm_off0 = 3 * N_GROUPS
    mem_n0, kv0 = _mem_kv(mem, mem_norm[0:1], w_mem_kv, 0, "0")
    cat0 = _memattn_fwd(proj0, qm_off0, kv0, mix0, 1, "memattn_fwd_0")
    x1 = _mm_nn_colblocked(cat0, w_attn_out, "attn_out_proj", x)
    hf0 = _rmsnorm_fwd(x1, ffn_norm[0:1], "ffn_norm_0")
    w_gate0, w_up0 = _rows(w_gate0), _rows(jnp.concatenate(up_gathered, axis=1))
    (silu0, ud0, hid0), (w_down0, w_sgu_in, w_sgu_out) = _swiglu_fwd(
        hf0, w_gate0, w_up0, "swiglu_fwd_0",
        _GatherSide([shards["w_down"][0], shards["sgu_w_in"], shards["sgu_w_out"]]))
    w_down0 = _rows(w_down0)
    x2, (w_gate1,) = _swiglu_down(hid0, w_down0, x1, "swiglu_down_0", _GatherSide([shards["w_gate"][1]]))
    ffn_saved0 = (hf0, silu0, ud0, hid0)

    h1 = _rmsnorm_fwd(x2, mix_norm[1:2], "mix_norm_1")
    w_sgu_in = _rows(w_sgu_in)
    w_sgu_out = _rows(w_sgu_out)
    proj1, (w_up1,) = _mm_nt_rowblocked(h1, w_sgu_in, "sgu_in_proj", w_sgu_in.shape[0] // 7, BF16,
                                        _GatherSide([shards["w_up"][1]]))
    w_gate1, w_up1 = _rows(w_gate1), _rows(w_up1)
    b_st = small["sgu_b_spatial"].T
    mix1 = _sgu_fwd(proj1, ln_g, ln_b, small["sgu_w_spatial"], b_st)
    qm_off1 = 2 * SGU_W // GROUP_W
    mem_n1, kv1 = _mem_kv(mem, mem_norm[1:2], w_mem_kv, 1, "1")
    cat1 = _memattn_fwd(proj1, qm_off1, kv1, mix1, SGU_W // GROUP_W, "memattn_fwd_1")
    x3 = _mm_nn(cat1, w_sgu_out, "sgu_out_proj", d // 2, out_dtype=F32, res=x2)
    hf1 = _rmsnorm_fwd(x3, ffn_norm[1:2], "ffn_norm_1")
    (silu1, ud1, hid1), (w_down1,) = _swiglu_fwd(hf1, w_gate1, w_up1, "swiglu_fwd_1",
                                                 _GatherSide([shards["w_down"][1]]))
    w_down1 = _rows(w_down1)
    x4 = _swiglu_down(hid1, w_down1, x3, "swiglu_down_1")
    ffn_saved1 = (hf1, silu1, ud1, hid1)

    loss, dxb, d_final = _loss_head(x4, small["final_norm"], target)

    recvs, sgrads = {}, {}
    dxb, d_ffn1, r_wg1, r_wu1, r_wd1 = _ffn_bwd(x3, ffn_norm[1:2], w_gate1, w_up1, w_down1, ffn_saved1, dxb, "1")
    dcat1 = _mm_nt_rowblocked(dxb, w_sgu_out, "sgu_out_proj_dx", w_sgu_out.shape[0] // 2, BF16)
    dwsout = _row_blocks(_mm_tn_full(cat1, dxb, "sgu_out_proj_dw"))
    duv, dws, dbs, dlng, dlnb = _sgu_bwd(proj1, dcat1, ln_g, ln_b, small["sgu_w_spatial"], b_st)
    dproj1, dkv1 = _memattn_bwd(proj1, qm_off1, kv1, dcat1, SGU_W // GROUP_W, "memattn_bwd_1",
                                into=duv, into_off=2 * SGU_W // GROUP_W)
    dwkv1, d_memnorm1 = _mem_kv_bwd(mem, mem_norm[1:2], mem_n1, w_mem_kv, 1, dkv1, "1")
    dwsin, (r_wsout, r_wkv1) = _mm_tn_full(dproj1, h1, "sgu_in_proj_dw", _ExchangeSide([dwsout, dwkv1]))
    dh1, (r_wsin,) = _mm_nn(dproj1, w_sgu_in, "sgu_in_proj_dx", d // 4, out_dtype=BF16,
                            side=_ExchangeSide([_row_blocks(dwsin)]))
    dxb, d_mix1 = _rmsnorm_bwd(x2, mix_norm[1:2], dh1, dxb, "mix_norm_bwd_1")

    dxb, d_ffn0, r_wg0, r_wu0, r_wd0 = _ffn_bwd(x1, ffn_norm[0:1], w_gate0, w_up0, w_down0, ffn_saved0, dxb, "0")
    dcat0 = _mm_nt_colblocked(dxb, w_attn_out, "attn_out_proj_dx", BF16, 4)
    dwout0 = _mm_tn_colblocked(cat0, dxb, "attn_out_proj_dw", w_attn_out.shape[2])
    dos_and_deltas = _merge_bwd(dcat0, outs, lses)
    dqkv = []
    for g, dil in enumerate(DILATIONS):
        do_g, dl_g = dos_and_deltas[g], dos_and_deltas[N_GROUPS + g]
        lse_g = lses[g]
        if dil > 1:
            do_g, dl_g, lse_g = (_to_residue_major(t, dil) for t in (do_g, dl_g, lse_g))
        t = _attn_bwd(qkv[g], qkv[g], qkv[g], offs[g], do_g, lse_g, dl_g, s // dil // BLK, "attn_bwd_%d" % g)
        dqkv.append(_from_residue_major(t, dil) if dil > 1 else t)
    dqm0, dkv0 = _memattn_bwd(proj0, qm_off0, kv0, dcat0, 1, "memattn_bwd_0")
    dwkv0, d_memnorm0 = _mem_kv_bwd(mem, mem_norm[0:1], mem_n0, w_mem_kv, 0, dkv0, "0")
    dproj0 = _assemble_dproj(dqkv, dqm0, tabs)
    dwin0, (r_wout0, r_wkv0) = _mm_tn_full(dproj0, h0, "attn_in_proj_dw", _ExchangeSide([dwout0, dwkv0]))
    dh0, (r_win0,) = _mm_nn(dproj0, w_attn_in, "attn_in_proj_dx", d // 4, out_dtype=BF16,
                            side=_ExchangeSide([_row_blocks(dwin0)]))
    grad_x, d_mix0 = _rmsnorm_bwd(x, mix_norm[0:1], dh0, dxb, "mix_norm_bwd_0", out_dtype=F32)

    recvs["w_gate"] = [r_wg0, r_wg1]
    recvs["w_up"] = [r_wu0, r_wu1]
    recvs["w_down"] = [r_wd0, r_wd1]
    recvs["w_mem_kv"] = [r_wkv0, r_wkv1]
    recvs["attn_w_in"] = [r_win0]
    recvs["attn_w_out"] = [r_wout0]
    recvs["sgu_w_in"] = [r_wsin]
    recvs["sgu_w_out"] = [r_wsout]
    sgrads["mix_norm"] = jnp.concatenate([d_mix0, d_mix1], axis=0)
    sgrads["mem_norm"] = jnp.concatenate([d_memnorm0, d_memnorm1], axis=0)
    sgrads["ffn_norm"] = jnp.concatenate([d_ffn0, d_ffn1], axis=0)
    sgrads["final_norm"] = d_final
    sgrads["sgu_w_spatial"] = dws
    sgrads["sgu_b_spatial"] = dbs[:, :SGU_GROUPS].T
    sgrads["sgu_ln_g"] = dlng
    sgrads["sgu_ln_b"] = dlnb
    return loss, grad_x, recvs, sgrads


ADAM_ORDER = ("w_gate", "w_up", "w_down", "w_mem_kv", "sgu_w_in", "sgu_w_out", "attn_w_out", "attn_w_in")
TRANSPOSED = ("w_gate", "w_up", "sgu_w_in", "attn_w_in")
SMALL_REPLICATED = ("mix_norm", "mem_norm", "ffn_norm", "final_norm", "sgu_w_spatial", "sgu_b_spatial")
SMALL_SHARDED = ("sgu_ln_g", "sgu_ln_b")
WEIGHT_ORDER = ("mix_norm", "mem_norm", "w_mem_kv", "ffn_norm", "w_gate", "w_up", "w_down", "attn_w_in",
                "attn_w_out", "sgu_w_in", "sgu_ln_g", "sgu_ln_b", "sgu_w_spatial", "sgu_b_spatial",
                "sgu_w_out", "final_norm")
PACK_LANES = 128


def _pack(parts):
    flat = [p.reshape(-1) for p in parts]
    sizes = [f.shape[0] for f in flat]
    total = sum(sizes)
    rows = -(-total // PACK_LANES)
    rows = -(-rows // 8) * 8
    pad = rows * PACK_LANES - total
    packed = jnp.concatenate(flat + [jnp.zeros((pad,), F32)]).reshape(rows, PACK_LANES)
    offs, o = [], 0
    for sz in sizes:
        offs.append(o)
        o += sz
    return packed, offs, sizes


def _unpack(packed, offs, sizes, shapes):
    flat = packed.reshape(-1)
    return [flat[o:o + sz].reshape(shp) for o, sz, shp in zip(offs, sizes, shapes)]


def kernel(x, mem, positions, mix_norm, mem_norm, w_mem_kv, ffn_norm, w_gate, w_up, w_down, attn_w_in, attn_w_out, sgu_w_in, sgu_ln_g, sgu_ln_b, sgu_w_spatial, sgu_b_spatial, sgu_w_out, final_norm, loss_target, m_mix_norm, m_mem_norm, m_w_mem_kv, m_ffn_norm, m_w_gate, m_w_up, m_w_down, m_attn_w_in, m_attn_w_out, m_sgu_w_in, m_sgu_ln_g, m_sgu_ln_b, m_sgu_w_spatial, m_sgu_b_spatial, m_sgu_w_out, m_final_norm, v_mix_norm, v_mem_norm, v_w_mem_kv, v_ffn_norm, v_w_gate, v_w_up, v_w_down, v_attn_w_in, v_attn_w_out, v_sgu_w_in, v_sgu_ln_g, v_sgu_ln_b, v_sgu_w_spatial, v_sgu_b_spatial, v_sgu_w_out, v_final_norm):
    w = dict(mix_norm=mix_norm, mem_norm=mem_norm, w_mem_kv=w_mem_kv, ffn_norm=ffn_norm, w_gate=w_gate,
             w_up=w_up, w_down=w_down, attn_w_in=attn_w_in, attn_w_out=attn_w_out, sgu_w_in=sgu_w_in,
             sgu_ln_g=sgu_ln_g, sgu_ln_b=sgu_ln_b, sgu_w_spatial=sgu_w_spatial, sgu_b_spatial=sgu_b_spatial,
             sgu_w_out=sgu_w_out, final_norm=final_norm)
    mo = dict(mix_norm=m_mix_norm, mem_norm=m_mem_norm, w_mem_kv=m_w_mem_kv, ffn_norm=m_ffn_norm,
              w_gate=m_w_gate, w_up=m_w_up, w_down=m_w_down, attn_w_in=m_attn_w_in, attn_w_out=m_attn_w_out,
              sgu_w_in=m_sgu_w_in, sgu_ln_g=m_sgu_ln_g, sgu_ln_b=m_sgu_ln_b, sgu_w_spatial=m_sgu_w_spatial,
              sgu_b_spatial=m_sgu_b_spatial, sgu_w_out=m_sgu_w_out, final_norm=m_final_norm)
    vo = dict(mix_norm=v_mix_norm, mem_norm=v_mem_norm, w_mem_kv=v_w_mem_kv, ffn_norm=v_ffn_norm,
              w_gate=v_w_gate, w_up=v_w_up, w_down=v_w_down, attn_w_in=v_attn_w_in, attn_w_out=v_attn_w_out,
              sgu_w_in=v_sgu_w_in, sgu_ln_g=v_sgu_ln_g, sgu_ln_b=v_sgu_ln_b, sgu_w_spatial=v_sgu_w_spatial,
              sgu_b_spatial=v_sgu_b_spatial, sgu_w_out=v_sgu_w_out, final_norm=v_final_norm)
    me = 4 * lax.axis_index("x") + 2 * lax.axis_index("y") + lax.axis_index("c")
    d_model = x.shape[-1]

    for n in TRANSPOSED:
        w[n], mo[n], vo[n] = (jnp.swapaxes(t, 1, 2) for t in (w[n], mo[n], vo[n]))

    shards = {
        "w_mem_kv": _cast_bf16(w_mem_kv, "cast_w_mem_kv"),
        "attn_w_out": _cast_bf16(attn_w_out[0], "cast_attn_w_out"),
        "sgu_w_in": _cast_bf16(w["sgu_w_in"][0], "cast_sgu_w_in"),
        "sgu_w_out": _cast_bf16(sgu_w_out[0], "cast_sgu_w_out"),
    }
    for n in ("w_gate", "w_up", "w_down"):
        shards[n] = [_cast_bf16_layer(w[n], layer, "cast_%s_%d" % (n, layer)) for layer in range(w[n].shape[0])]
    ln_pack = jnp.concatenate([sgu_ln_g, sgu_ln_b], axis=0)
    first_blocks = [_cast_bf16(w["attn_w_in"][0], "cast_attn_w_in"), ln_pack]
    small = dict(mix_norm=mix_norm, mem_norm=mem_norm, ffn_norm=ffn_norm, final_norm=final_norm.reshape(1, -1),
                 sgu_w_spatial=sgu_w_spatial[0], sgu_b_spatial=sgu_b_spatial[0])

    loss, grad_x, recvs, sgrads = _local_step(x[0], mem[0], positions[0], loss_target[0], first_blocks, shards,
                                              small)
    loss = lax.psum(loss[0, 0], MESH_AXES)

    small_names = SMALL_REPLICATED + SMALL_SHARDED
    packed, offs, sizes = _pack([sgrads[n] for n in small_names])
    out_g, out_d, out_m, out_v = {}, {}, {}, {}
    for n in ADAM_ORDER:
        shard = w[n]
        w3 = shard.reshape(shard.shape[0], -1, shard.shape[-1])
        rs = [r.reshape(N_DEV, -1, shard.shape[-1]) for r in recvs[n]]
        operands = (rs, w3, mo[n].reshape(w3.shape), vo[n].reshape(w3.shape), "adam_" + n)
        if n == ADAM_ORDER[0]:
            res, (all_packs,) = _reduce_adam(*operands, _GatherSide([packed]))
        else:
            res = _reduce_adam(*operands)
        res = [r.reshape(shard.shape) for r in res]
        if n in TRANSPOSED:
            res = [jnp.swapaxes(r, 1, 2) for r in res]
        out_g[n], out_d[n], out_m[n], out_v[n] = res

    rep_shapes = [w[n].shape for n in SMALL_REPLICATED]
    w_pack, w_offs, w_sizes = _pack([w[n] for n in SMALL_REPLICATED])
    m_pack, _, _ = _pack([mo[n] for n in SMALL_REPLICATED])
    v_pack, _, _ = _pack([vo[n] for n in SMALL_REPLICATED])
    n_rep_rows = w_pack.shape[0]
    res = _reduce_adam([all_packs[:, :n_rep_rows]], w_pack[None], m_pack[None], v_pack[None], "adam_small")
    for dst, r in zip((out_g, out_d, out_m, out_v), res):
        for n, val in zip(SMALL_REPLICATED, _unpack(r[0], w_offs, w_sizes, rep_shapes)):
            dst[n] = val
    ln_rows0 = offs[len(SMALL_REPLICATED)] // PACK_LANES
    ln_rows = 2 * SGU_W // PACK_LANES
    ln_sum = _reduce_adam([all_packs[:, ln_rows0:ln_rows0 + ln_rows]], jnp.zeros((1, ln_rows, PACK_LANES), F32),
                          jnp.zeros((1, ln_rows, PACK_LANES), F32), jnp.zeros((1, ln_rows, PACK_LANES), F32),
                          "sum_ln_grads")[0]
    ln_grads = ln_sum.reshape(2, N_DEV, -1)
    ln_mine = lax.dynamic_index_in_dim(ln_grads, me, axis=1, keepdims=False)
    w_ln = jnp.concatenate([sgu_ln_g, sgu_ln_b], axis=0)[None]
    m_ln = jnp.concatenate([m_sgu_ln_g, m_sgu_ln_b], axis=0)[None]
    v_ln = jnp.concatenate([v_sgu_ln_g, v_sgu_ln_b], axis=0)[None]
    res = _reduce_adam([ln_mine[None]], w_ln, m_ln, v_ln, "adam_ln")
    for dst, r in zip((out_g, out_d, out_m, out_v), res):
        dst["sgu_ln_g"], dst["sgu_ln_b"] = r[0, 0:1], r[0, 1:2]

    return (loss, grad_x[None], *[out_g[n] for n in WEIGHT_ORDER], *[out_d[n] for n in WEIGHT_ORDER],
            *[out_m[n] for n in WEIGHT_ORDER], *[out_v[n] for n in WEIGHT_ORDER])
```
